```python
import math
import numpy as np
import jax
import jax.numpy as jnp
from jax import lax

D_MODEL = 1024
BATCH = 2
SEQ = 8192
DEPTH = 2

HEAD_DIM = 64
N_HEADS = 8
MIX_W = N_HEADS * HEAD_DIM
N_BRANCH = 4
RMS_EPS = 1e-6
CONV_K = 4
MOBA_BLOCK = 256
MOBA_TOPK = 3
MOBA_QBLOCK = 128
REL_BUCKETS = 32
REL_MAX_DIST = 128
GDN_CHUNK = 64
RWKV_W_RANK = 64
RWKV_A_RANK = 64
RWKV_G_RANK = 128
RWKV_V_RANK = 32
RWKV_LN_EPS = 64e-5
RWKV_COLS = 3 * MIX_W + RWKV_W_RANK + RWKV_A_RANK + RWKV_G_RANK
SSM_GROUPS = 2
SSM_STATE = 128
SSM_BC = SSM_GROUPS * SSM_STATE
SSD_CHUNK = 128
D_FF = ((8 * D_MODEL + 3 * 256 - 1) // (3 * 256)) * 256
IN_WIDTHS = (
    3 * MIX_W,
    3 * MIX_W,
    MIX_W,
    N_HEADS,
    N_HEADS,
    RWKV_COLS,
    MIX_W,
    MIX_W + 2 * SSM_BC,
    N_HEADS,
    N_BRANCH * D_MODEL,
)
N_IN = sum(IN_WIDTHS)

kernel_name = 'hybrid_moba_gdn_rwkv7_mamba2_block'


def split_cols(t, widths):
    idx = np.cumsum(widths)[:-1].tolist()
    return jnp.split(t, idx, axis=-1)


def rmsnorm(x, w):
    xf = x.astype(jnp.float32)
    y = xf * lax.rsqrt(jnp.mean(xf * xf, axis=-1, keepdims=True) + RMS_EPS)
    return (y * w.astype(jnp.float32)).astype(x.dtype)


def l2norm(x):
    xf = x.astype(jnp.float32)
    return xf * lax.rsqrt(jnp.sum(xf * xf, axis=-1, keepdims=True) + 1e-6)


def causal_dwconv(x, w, b=None):
    y = lax.conv_general_dilated(
        x, w[:, None, :].astype(x.dtype), window_strides=(1,),
        padding=[(w.shape[0] - 1, 0)], dimension_numbers=('NWC', 'WIO', 'NWC'),
        feature_group_count=x.shape[-1])
    if b is not None:
        y = y + b
    return y


def token_shift(x):
    return jnp.pad(x, ((0, 0), (1, 0), (0, 0)))[:, :-1]


def t5_bucket(dist):
    n = jnp.maximum(dist, 0)
    max_exact = REL_BUCKETS // 2
    nf = jnp.maximum(n, max_exact).astype(jnp.float32)
    large = max_exact + (jnp.log(nf / max_exact) / math.log(REL_MAX_DIST / max_exact)
                         * (REL_BUCKETS - max_exact)).astype(jnp.int32)
    large = jnp.minimum(large, REL_BUCKETS - 1)
    return jnp.where(n < max_exact, n, large)


def moba_attention(q, k, v, rel_bias):
    Bsz, S, H, Dh = q.shape
    nb = -(-S // MOBA_BLOCK)
    pad = nb * MOBA_BLOCK - S
    scale = Dh ** -0.5
    qh = q.transpose(0, 2, 1, 3)
    padk = ((0, 0), (0, pad), (0, 0), (0, 0))
    kp = jnp.pad(k, padk).transpose(0, 2, 1, 3).reshape(Bsz, H, nb, MOBA_BLOCK, Dh)
    vp = jnp.pad(v, padk).transpose(0, 2, 1, 3).reshape(Bsz, H, nb, MOBA_BLOCK, Dh)
    kmean = kp.mean(axis=3)
    q_blk = jnp.arange(S) // MOBA_BLOCK
    past = jnp.arange(nb)[None, :] < q_blk[:, None]
    gate = jnp.einsum('bhsd,bhnd->bhsn', qh, kmean).astype(jnp.float32)
    gate = jnp.where(past, gate, -jnp.inf)
    k_sel = min(MOBA_TOPK, nb)
    _, sel = lax.top_k(gate, k_sel)
    sel_valid = sel < q_blk[:, None]
    bi = jnp.arange(Bsz)[:, None, None, None]
    hi = jnp.arange(H)[None, :, None, None]
    offs = jnp.arange(MOBA_BLOCK)
    bias_t = rel_bias.T

    def one_chunk(c):
        q0 = c * MOBA_QBLOCK
        qpos = q0 + jnp.arange(MOBA_QBLOCK)
        qc = lax.dynamic_slice_in_dim(qh, q0, MOBA_QBLOCK, axis=2)
        sc = lax.dynamic_slice_in_dim(sel, q0, MOBA_QBLOCK, axis=2)
        vc = lax.dynamic_slice_in_dim(sel_valid, q0, MOBA_QBLOCK, axis=2)
        kg = kp[bi, hi, sc]
        vg = vp[bi, hi, sc]
        s_sel = jnp.einsum('bhqd,bhqjkd->bhqjk', qc, kg).astype(jnp.float32) * scale
        kpos_sel = sc[..., None] * MOBA_BLOCK + offs
        s_sel = s_sel + bias_t[hi[..., None], t5_bucket(qpos[:, None, None] - kpos_sel)]
        s_sel = jnp.where(vc[..., None], s_sel, -jnp.inf)
        own = q0 // MOBA_BLOCK
        ko = lax.dynamic_slice_in_dim(kp, own, 1, axis=2)[:, :, 0]
        vo = lax.dynamic_slice_in_dim(vp, own, 1, axis=2)[:, :, 0]
        d_own = qpos[:, None] - (own * MOBA_BLOCK + offs)[None, :]
        s_own = jnp.einsum('bhqd,bhkd->bhqk', qc, ko).astype(jnp.float32) * scale
        s_own = jnp.where(d_own >= 0, s_own + bias_t[:, t5_bucket(d_own)], -jnp.inf)
        n_sel = k_sel * MOBA_BLOCK
        s = jnp.concatenate([s_sel.reshape(Bsz, H, MOBA_QBLOCK, n_sel), s_own], axis=-1)
        p = jax.nn.softmax(s, axis=-1).astype(v.dtype)
        p_sel = p[..., :n_sel].reshape(Bsz, H, MOBA_QBLOCK, k_sel, MOBA_BLOCK)
        return (jnp.einsum('bhqjk,bhqjkd->bhqd', p_sel, vg)
                + jnp.einsum('bhqk,bhkd->bhqd', p[..., n_sel:], vo))

    o = lax.map(one_chunk, jnp.arange(S // MOBA_QBLOCK))
    return o.transpose(1, 0, 3, 2, 4).reshape(Bsz, S, H * Dh)


def gated_deltanet(q, k, v, beta, g):
    Bsz, S, H, Dk = q.shape
    Dv = v.shape[-1]
    C = GDN_CHUNK
    N = S // C
    ch = lambda t: t.reshape(Bsz, N, C, H, t.shape[-1]).transpose(0, 3, 1, 2, 4)
    q, k, v = ch(q), ch(k), ch(v)
    beta = beta.reshape(Bsz, N, C, H).transpose(0, 3, 1, 2)
    gc = jnp.cumsum(g.reshape(Bsz, N, C, H).transpose(0, 3, 1, 2), axis=-1)
    tri_incl = jnp.tril(jnp.ones((C, C), bool))
    tri_strict = jnp.tril(jnp.ones((C, C), bool), -1)
    decay = jnp.exp(jnp.where(tri_incl, gc[..., :, None] - gc[..., None, :], -jnp.inf))
    kb = k * beta[..., None]
    Lm = jnp.where(tri_strict, jnp.einsum('bhnid,bhnjd->bhnij', kb, k) * decay, 0.0)
    eye = jnp.eye(C, dtype=jnp.float32)
    T = lax.linalg.triangular_solve(Lm + eye, jnp.broadcast_to(eye, Lm.shape),
                                    left_side=True, lower=True, unit_diagonal=True)
    u = T @ (v * beta[..., None])
    w = T @ (kb * jnp.exp(gc)[..., None])
    a_in = jnp.where(tri_incl, jnp.einsum('bhnid,bhnjd->bhnij', q, k) * decay, 0.0)
    q_dec = q * jnp.exp(gc)[..., None]
    k_dec = k * jnp.exp(gc[..., -1:] - gc)[..., None]
    g_last = jnp.exp(gc[..., -1])

    def step(st, xs):
        u_c, w_c, qd_c, kd_c, a_c, gl_c = xs
        v_new = u_c - w_c @ st
        o_c = qd_c @ st + a_c @ v_new
        st = st * gl_c[..., None, None] + jnp.swapaxes(kd_c, -1, -2) @ v_new
        return st, o_c

    xs = tuple(jnp.moveaxis(t, 2, 0) for t in (u, w, q_dec, k_dec, a_in, g_last))
    _, o = lax.scan(step, jnp.zeros((Bsz, H, Dk, Dv), jnp.float32), xs)
    return o.transpose(1, 0, 3, 2, 4).reshape(Bsz, S, H, Dv)


def rwkv7_wkv(r, w, k, v, kk, a):
    Bsz, S, H, N = r.shape

    def step(st, xs):
        r_t, w_t, k_t, v_t, kk_t, a_t = xs
        sa = jnp.einsum('bhvk,bhk->bhv', st, -kk_t)
        st = (st * w_t[:, :, None, :] + sa[..., :, None] * (kk_t * a_t)[..., None, :]
              + v_t[..., :, None] * k_t[..., None, :])
        return st, jnp.einsum('bhvk,bhk->bhv', st, r_t)

    xs = tuple(t.transpose(1, 0, 2, 3) for t in (r, w, k, v, kk, a))
    _, y = lax.scan(step, jnp.zeros((Bsz, H, N, N), jnp.float32), xs)
    return y.transpose(1, 0, 2, 3)


def ssd_chunked(x, dt, A, Bm, Cm):
    Bsz, S, H, P = x.shape
    G, N = Bm.shape[2], Bm.shape[3]
    J = H // G
    L = SSD_CHUNK
    NC = S // L
    xg = (x * dt[..., None]).reshape(Bsz, NC, L, G, J, P)
    acs = jnp.cumsum((dt * A).reshape(Bsz, NC, L, G, J), axis=2)
    Bc = Bm.reshape(Bsz, NC, L, G, N)
    Cc = Cm.reshape(Bsz, NC, L, G, N)
    tri = jnp.tril(jnp.ones((L, L), bool))[:, :, None, None]
    seg = acs[:, :, :, None] - acs[:, :, None, :]
    Lmat = jnp.exp(jnp.where(tri, seg, -jnp.inf))
    cb = jnp.einsum('bctgn,bcsgn->bctsg', Cc, Bc)
    y_diag = jnp.einsum('bctsg,bctsgj,bcsgjp->bctgjp', cb, Lmat, xg)
    decay_states = jnp.exp(acs[:, :, -1:] - acs)
    states = jnp.einsum('bcsgn,bcsgj,bcsgjp->bcgjpn', Bc, decay_states, xg)
    chunk_decay = jnp.exp(acs[:, :, -1])

    def step(h, xs):
        st, cd = xs
        return h * cd[..., None, None] + st, h

    _, h_in = lax.scan(step, jnp.zeros((Bsz, G, J, P, N), jnp.float32),
                       (states.transpose(1, 0, 2, 3, 4, 5), chunk_decay.transpose(1, 0, 2, 3)))
    h_in = h_in.transpose(1, 0, 2, 3, 4, 5)
    y_off = jnp.einsum('bctgn,bcgjpn,bctgj->bctgjp', Cc, h_in, jnp.exp(acs))
    return (y_diag + y_off).reshape(Bsz, S, H, P)


def setup_inputs(seed: int = 0) -> dict:
    key = jax.random.key(seed)
    keys = jax.random.split(key, 40)
    counter = [0]
    L, f32 = DEPTH, jnp.float32

    def nxt():
        counter[0] += 1
        return keys[counter[0] - 1]

    def nrm(shape, scale):
        return jax.random.normal(nxt(), shape, f32) * scale

    def uni(shape, lo, hi):
        return jax.random.uniform(nxt(), shape, f32, lo, hi)

    def dt_bias(shape):
        dt = jnp.exp(uni(shape, math.log(1e-3), math.log(1e-1)))
        return dt + jnp.log(-jnp.expm1(-dt))

    return {
        'x': nrm((BATCH, SEQ, D_MODEL), 1.0),
        'rel_bias': nrm((REL_BUCKETS, N_HEADS), 0.5),
        'norm1_w': 1.0 + nrm((L, D_MODEL), 0.02),
        'w_in': nrm((L, D_MODEL, N_IN), D_MODEL ** -0.5),
        'moba_q_norm': 1.0 + nrm((L, HEAD_DIM), 0.02),
        'moba_k_norm': 1.0 + nrm((L, HEAD_DIM), 0.02),
        'gdn_conv_w': nrm((L, CONV_K, 3 * MIX_W), CONV_K ** -0.5),
        'gdn_A_log': jnp.log(uni((L, N_HEADS), 1.0, 16.0)),
        'gdn_dt_bias': dt_bias((L, N_HEADS)),
        'gdn_norm_w': 1.0 + nrm((L, HEAD_DIM), 0.02),
        'rwkv_mu': uni((L, RWKV_COLS), 0.0, 1.0),
        'rwkv_w0': uni((L, MIX_W), -6.0, -1.0),
        'rwkv_w_up': nrm((L, RWKV_W_RANK, MIX_W), 0.5 * RWKV_W_RANK ** -0.5),
        'rwkv_a0': nrm((L, MIX_W), 0.1),
        'rwkv_a_up': nrm((L, RWKV_A_RANK, MIX_W), 0.5 * RWKV_A_RANK ** -0.5),
        'rwkv_g_up': nrm((L, RWKV_G_RANK, MIX_W), RWKV_G_RANK ** -0.5),
        'rwkv_k_k': 0.85 + nrm((L, MIX_W), 0.02),
        'rwkv_k_a': 1.0 + nrm((L, MIX_W), 0.02),
        'rwkv_r_k': nrm((L, N_HEADS, HEAD_DIM), 0.1),
        'rwkv_v0': 0.5 + nrm((L - 1, MIX_W), 0.1),
        'rwkv_v_down': nrm((L - 1, MIX_W, RWKV_V_RANK), MIX_W ** -0.5),
        'rwkv_v_up': nrm((L - 1, RWKV_V_RANK, MIX_W), 0.5 * RWKV_V_RANK ** -0.5),
        'rwkv_ln_w': 1.0 + nrm((L, MIX_W), 0.02),
        'rwkv_ln_b': nrm((L, MIX_W), 0.02),
        'mamba_conv_w': nrm((L, CONV_K, MIX_W + 2 * SSM_BC), CONV_K ** -0.5),
        'mamba_conv_b': nrm((L, MIX_W + 2 * SSM_BC), 0.02),
        'mamba_dt_bias': dt_bias((L, N_HEADS)),
        'mamba_A_log': jnp.log(uni((L, N_HEADS), 1.0, 16.0)),
        'mamba_D': 1.0 + nrm((L, N_HEADS), 0.1),
        'mamba_norm_w': 1.0 + nrm((L, MIX_W), 0.02),
        'w_branch': nrm((L, N_BRANCH, MIX_W, D_MODEL), MIX_W ** -0.5),
        'w_out': nrm((L, D_MODEL, D_MODEL), D_MODEL ** -0.5),
        'norm2_w': 1.0 + nrm((L, D_MODEL), 0.02),
        'ffn_w_in': nrm((L, D_MODEL, 2 * D_FF), D_MODEL ** -0.5),
        'ffn_w_down': nrm((L, D_FF, D_MODEL), D_FF ** -0.5),
    }


def reference(x, rel_bias, norm1_w, w_in, moba_q_norm, moba_k_norm, gdn_conv_w, gdn_A_log,
              gdn_dt_bias, gdn_norm_w, rwkv_mu, rwkv_w0, rwkv_w_up, rwkv_a0, rwkv_a_up,
              rwkv_g_up, rwkv_k_k, rwkv_k_a, rwkv_r_k, rwkv_v0, rwkv_v_down, rwkv_v_up,
              rwkv_ln_w, rwkv_ln_b, mamba_conv_w, mamba_conv_b, mamba_dt_bias, mamba_A_log,
              mamba_D, mamba_norm_w, w_branch, w_out, norm2_w, ffn_w_in, ffn_w_down):
    Bsz, S, _ = x.shape
    H, Dh = N_HEADS, HEAD_DIM
    f32 = jnp.float32
    heads = lambda t: t.reshape(Bsz, S, H, -1)
    v_first = None
    for i in range(DEPTH):
        h = rmsnorm(x, norm1_w[i])
        proj = h @ w_in[i]
        (a_qkv, b_qkv, b_z, b_beta, b_a, c_all, d_z, d_xbc, d_dt,
         gate_logits) = split_cols(proj, IN_WIDTHS)

        a_q, a_k, a_v = jnp.split(a_qkv, 3, axis=-1)
        y_a = moba_attention(rmsnorm(heads(a_q), moba_q_norm[i]),
                             rmsnorm(heads(a_k), moba_k_norm[i]), heads(a_v), rel_bias)

        qkv = jax.nn.silu(causal_dwconv(b_qkv.astype(f32), gdn_conv_w[i]))
        g_q, g_k, g_v = jnp.split(qkv, 3, axis=-1)
        beta = jax.nn.sigmoid(b_beta.astype(f32))
        g_log = -jnp.exp(gdn_A_log[i]) * jax.nn.softplus(b_a.astype(f32) + gdn_dt_bias[i])
        o_b = gated_deltanet(l2norm(heads(g_q)) * (Dh ** -0.5), l2norm(heads(g_k)),
                             heads(g_v), beta, g_log)
        y_b = (rmsnorm(o_b, gdn_norm_w[i]) * jax.nn.silu(heads(b_z.astype(f32)))).reshape(Bsz, S, MIX_W)

        c = c_all.astype(f32)
        c = c + (token_shift(c) - c) * rwkv_mu[i]
        c_r, c_k, c_v, c_wd, c_ad, c_gd = split_cols(
            c, (MIX_W, MIX_W, MIX_W, RWKV_W_RANK, RWKV_A_RANK, RWKV_G_RANK))
        w_log = -jax.nn.softplus(-(rwkv_w0[i] + jnp.tanh(c_wd) @ rwkv_w_up[i])) - 0.5
        decay = jnp.exp(-jnp.exp(w_log))
        a_in = jax.nn.sigmoid(rwkv_a0[i] + c_ad @ rwkv_a_up[i])
        g_out = jax.nn.sigmoid(c_gd) @ rwkv_g_up[i]
        if i == 0:
            v_first = c_v
            v_r = c_v
        else:
            lam = jax.nn.sigmoid(rwkv_v0[i - 1] + (c_v @ rwkv_v_down[i - 1]) @ rwkv_v_up[i - 1])
            v_r = c_v + (v_first - c_v) * lam
        kk = l2norm(heads(c_k * rwkv_k_k[i]))
        k_r = c_k * (1.0 + (a_in - 1.0) * rwkv_k_a[i])
        r_h, k_h, v_h = heads(c_r), heads(k_r), heads(v_r)
        wkv = rwkv7_wkv(r_h, heads(decay), k_h, v_h, kk, heads(a_in))
        mu = jnp.mean(wkv, axis=-1, keepdims=True)
        var = jnp.mean(jnp.square(wkv - mu), axis=-1, keepdims=True)
        y_c = ((wkv - mu) * lax.rsqrt(var + RWKV_LN_EPS)).reshape(Bsz, S, MIX_W) * rwkv_ln_w[i] + rwkv_ln_b[i]
        bonus = jnp.sum(r_h * k_h * rwkv_r_k[i], axis=-1, keepdims=True) * v_h
        y_c = (y_c + bonus.reshape(Bsz, S, MIX_W)) * g_out

        xbc = jax.nn.silu(causal_dwconv(d_xbc.astype(f32), mamba_conv_w[i], mamba_conv_b[i]))
        m_x, m_B, m_C = split_cols(xbc, (MIX_W, SSM_BC, SSM_BC))
        dt = jax.nn.softplus(d_dt.astype(f32) + mamba_dt_bias[i])
        m_xh = heads(m_x)
        y_d = ssd_chunked(m_xh, dt, -jnp.exp(mamba_A_log[i].astype(f32)),
                          m_B.reshape(Bsz, S, SSM_GROUPS, SSM_STATE),
                          m_C.reshape(Bsz, S, SSM_GROUPS, SSM_STATE))
        y_d = y_d + m_xh * mamba_D[i][:, None]
        yz = y_d.reshape(Bsz, S, MIX_W) * jax.nn.silu(d_z.astype(f32))
        y_d = rmsnorm(yz.reshape(Bsz, S, SSM_GROUPS, -1),
                      mamba_norm_w[i].reshape(SSM_GROUPS, -1)).reshape(Bsz, S, MIX_W)

        ys = jnp.stack([y_a.astype(x.dtype), y_b.astype(x.dtype),
                        y_c.astype(x.dtype), y_d.astype(x.dtype)], axis=2)
        branch = jnp.einsum('bsnc,ncd->bsnd', ys, w_branch[i])
        gates = jax.nn.sigmoid(gate_logits.reshape(Bsz, S, N_BRANCH, D_MODEL))
        x = x + jnp.sum(gates * branch, axis=2) @ w_out[i]

        h2 = rmsnorm(x, norm2_w[i])
        ff_g, ff_u = jnp.split(h2 @ ffn_w_in[i], 2, axis=-1)
        x = x + (jax.nn.silu(ff_g) * ff_u) @ ffn_w_down[i]
    return x
```

```python
import functools
import math

import jax
import jax.numpy as jnp
from jax import lax
from jax.experimental import pallas as pl
from jax.experimental.pallas import tpu as pltpu

f32, bf16 = jnp.float32, jnp.bfloat16
HI = lax.Precision.HIGHEST

D_MODEL = 1024
N_HEADS = 8
HEAD_DIM = 64
MIX_W = N_HEADS * HEAD_DIM
RMS_EPS = 1e-6
L2_EPS = 1e-6
CONV_K = 4
MOBA_BLOCK = 256
MOBA_TOPK = 3
REL_BUCKETS = 32
REL_MAX_DIST = 128
GDN_CHUNK = 64
RWKV_CHUNK = 64
RWKV_LN_EPS = 64e-5
SSM_STATE = 128
SSD_CHUNK = 128
D_FF = 2816
NEG = -1e30

LANES = 128
SUBLANES = 8

OFF_A, OFF_BQKV, OFF_BZ, OFF_C, OFF_SMALL, OFF_DZ, OFF_DXBC, OFF_GATES = 0, 1536, 3072, 3584, 5376, 5632, 6144, 7168
N_PROJ = OFF_GATES + 4 * D_MODEL
SM_BETA, SM_BA, SM_DT = 0, 8, 16
W_A, W_BQKV, W_BZ, W_BBETA, W_BA, W_C, W_DZ, W_DXBC, W_DDT, W_GATES = 0, 1536, 3072, 3584, 3592, 3600, 5392, 5904, 6928, 6936

ROW_TILE = 256
MM_TILE_M = 512


def _cparams(sem, vmem_mb):
    return pltpu.CompilerParams(dimension_semantics=sem, vmem_limit_bytes=vmem_mb * 1024 * 1024)


def _mm(a, b):
    return jnp.dot(a.astype(bf16), b.astype(bf16), preferred_element_type=f32)


def _mm_nt(a, b):
    return lax.dot_general(a.astype(bf16), b.astype(bf16), (((1,), (1,)), ((), ())), preferred_element_type=f32)


def _mm_tn(a, b):
    return lax.dot_general(a.astype(bf16), b.astype(bf16), (((0,), (0,)), ((), ())), preferred_element_type=f32)


def _mm_hi(a, b):
    return jnp.dot(a, b, precision=HI, preferred_element_type=f32)


def _mm_nt_hi(a, b):
    return lax.dot_general(a, b, (((1,), (1,)), ((), ())), precision=HI, preferred_element_type=f32)


def _softplus(x):
    return jnp.maximum(x, 0.0) + jnp.log1p(jnp.exp(-jnp.abs(x)))


def _silu(x):
    return x * jax.nn.sigmoid(x)


def _iota2(shape):
    return lax.broadcasted_iota(jnp.int32, shape, 0), lax.broadcasted_iota(jnp.int32, shape, 1)


def _unit_lower_inverse(l_strict):
    n = l_strict.shape[0]
    r, c = _iota2((n, n))
    t = jnp.where(r == c, 1.0, 0.0)
    for ls in range(n.bit_length() - 1):
        m = ((r >> (ls + 1)) == (c >> (ls + 1))) & (((r >> ls) & 1) == 1) & (((c >> ls) & 1) == 0)
        lm = jnp.where(m, l_strict, 0.0)
        t = t - (lm if ls == 0 else _mm(_mm(t, lm), t))
    return t


def _norm_matmul_kernel(x_ref, nw_ref, w_ref, o_ref, h_scr):
    @pl.when(pl.program_id(1) == 0)
    def _():
        x = x_ref[...]
        y = x * lax.rsqrt(jnp.mean(x * x, axis=-1, keepdims=True) + RMS_EPS)
        h_scr[...] = (y * nw_ref[...]).astype(bf16)
    o_ref[...] = jnp.dot(h_scr[...], w_ref[...], preferred_element_type=f32).astype(o_ref.dtype)


def _norm_matmul(x2, nw, w, tn):
    t, d = x2.shape
    n = w.shape[1]
    tm = MM_TILE_M
    return pl.pallas_call(
        _norm_matmul_kernel, name="norm_matmul",
        grid=(t // tm, n // tn),
        in_specs=[pl.BlockSpec((tm, d), lambda i, j: (i, 0)),
                  pl.BlockSpec((1, d), lambda i, j: (0, 0)),
                  pl.BlockSpec((d, tn), lambda i, j: (0, j))],
        out_specs=pl.BlockSpec((tm, tn), lambda i, j: (i, j)),
        out_shape=jax.ShapeDtypeStruct((t, n), f32),
        scratch_shapes=[pltpu.VMEM((tm, d), bf16)],
        compiler_params=_cparams(("parallel", "arbitrary"), 40),
    )(x2, nw, w)


def _ffn_up_kernel(x_ref, nw_ref, wg_ref, wu_ref, o_ref, h_scr):
    @pl.when(pl.program_id(1) == 0)
    def _():
        x = x_ref[...]
        y = x * lax.rsqrt(jnp.mean(x * x, axis=-1, keepdims=True) + RMS_EPS)
        h_scr[...] = (y * nw_ref[...]).astype(bf16)
    h = h_scr[...]
    g = jnp.dot(h, wg_ref[...], preferred_element_type=f32)
    u = jnp.dot(h, wu_ref[...], preferred_element_type=f32)
    o_ref[...] = (_silu(g) * u).astype(o_ref.dtype)


def _ffn_up(x2, nw, wg, wu):
    t, d = x2.shape
    n = wg.shape[1]
    tm, tn = MM_TILE_M, n // 2
    return pl.pallas_call(
        _ffn_up_kernel, name="ffn_up",
        grid=(t // tm, n // tn),
        in_specs=[pl.BlockSpec((tm, d), lambda i, j: (i, 0)),
                  pl.BlockSpec((1, d), lambda i, j: (0, 0)),
                  pl.BlockSpec((d, tn), lambda i, j: (0, j)),
                  pl.BlockSpec((d, tn), lambda i, j: (0, j))],
        out_specs=pl.BlockSpec((tm, tn), lambda i, j: (i, j)),
        out_shape=jax.ShapeDtypeStruct((t, n), bf16),
        scratch_shapes=[pltpu.VMEM((tm, d), bf16)],
        compiler_params=_cparams(("parallel", "arbitrary"), 48),
    )(x2, nw, wg, wu)


def _ffn_down_kernel(x_ref, a_ref, w_ref, o_ref):
    o_ref[...] = x_ref[...] + jnp.dot(a_ref[...], w_ref[...], preferred_element_type=f32)


def _ffn_down(x2, act, wd):
    t, d = x2.shape
    n = act.shape[1]
    tm = MM_TILE_M
    return pl.pallas_call(
        _ffn_down_kernel, name="ffn_down",
        grid=(t // tm,),
        in_specs=[pl.BlockSpec((tm, d), lambda i: (i, 0)),
                  pl.BlockSpec((tm, n), lambda i: (i, 0)),
                  pl.BlockSpec((n, d), lambda i: (0, 0))],
        out_specs=pl.BlockSpec((tm, d), lambda i: (i, 0)),
        out_shape=jax.ShapeDtypeStruct((t, d), f32),
        compiler_params=_cparams(("parallel",), 40),
    )(x2, act, wd)


def _top3_bias(gate, n_past):
    col = lax.broadcasted_iota(jnp.int32, gate.shape, 1)
    g = jnp.where(col < n_past, gate, -jnp.inf)
    sel = jnp.zeros(gate.shape, jnp.bool_)
    for _ in range(MOBA_TOPK):
        m = jnp.max(g, axis=-1, keepdims=True)
        idx = jnp.min(jnp.where(g == m, col, gate.shape[1]), axis=-1, keepdims=True)
        pick = col == idx
        sel = sel | (pick & (m > -jnp.inf))
        g = jnp.where(pick, -jnp.inf, g)
    return jnp.where(sel, 0.0, NEG)


def _moba_prep_kernel(a_ref, qw_ref, kw_ref, bd_ref, qaug_ref, kaug_ref, v_ref, kmean_scr):
    i = pl.program_id(1)

    @pl.when(i == 0)
    def _():
        kmean_scr[...] = jnp.zeros_like(kmean_scr)

    a = a_ref[...]
    q, k, v = a[:, :MIX_W], a[:, MIX_W:2 * MIX_W], a[:, 2 * MIX_W:]
    bd = bd_ref[...]
    qn = q * lax.rsqrt(_mm_hi(q * q, bd) + RMS_EPS) * qw_ref[...]
    kn = k * lax.rsqrt(_mm_hi(k * k, bd) + RMS_EPS) * kw_ref[...]
    v_ref[0] = v.astype(bf16)
    lane = lax.broadcasted_iota(jnp.int32, (MOBA_BLOCK, LANES), 1)
    onehot = jnp.where(lane == i, 1.0, 0.0).astype(bf16)
    kmean = kmean_scr[...]
    for p in range(N_HEADS // 2):
        sl = slice(p * LANES, (p + 1) * LANES)
        kaug_ref[0, p] = jnp.concatenate([kn[:, sl].astype(bf16), onehot], axis=-1)
        for hh in range(2):
            keep = (lane < HEAD_DIM) if hh == 0 else (lane >= HEAD_DIM)
            q2m = jnp.where(keep, qn[:, sl], 0.0)
            gate = _mm_nt_hi(q2m, kmean[:, sl])
            selb = _top3_bias(gate, i)
            qaug_ref[0, 2 * p + hh] = jnp.concatenate(
                [(q2m * HEAD_DIM ** -0.5).astype(bf16), selb.astype(bf16)], axis=-1)
    kmean_scr[pl.ds(i, 1), :] = jnp.mean(kn, axis=0, keepdims=True)


def _moba_prep(proj, bsz, seq, qw, kw, bd_mean):
    nb = seq // MOBA_BLOCK
    assert nb <= LANES
    return pl.pallas_call(
        _moba_prep_kernel, name="moba_prep",
        grid=(bsz, nb),
        in_specs=[pl.BlockSpec((MOBA_BLOCK, 3 * MIX_W), lambda b, i: (b * nb + i, OFF_A // (3 * MIX_W))),
                  pl.BlockSpec((1, MIX_W), lambda b, i: (0, 0)),
                  pl.BlockSpec((1, MIX_W), lambda b, i: (0, 0)),
                  pl.BlockSpec((MIX_W, MIX_W), lambda b, i: (0, 0))],
        out_specs=[pl.BlockSpec((1, N_HEADS, MOBA_BLOCK, 2 * LANES), lambda b, i: (b, 0, i, 0)),
                   pl.BlockSpec((1, N_HEADS // 2, MOBA_BLOCK, 2 * LANES), lambda b, i: (b, 0, i, 0)),
                   pl.BlockSpec((1, MOBA_BLOCK, MIX_W), lambda b, i: (b, i, 0))],
        out_shape=[jax.ShapeDtypeStruct((bsz, N_HEADS, seq, 2 * LANES), bf16),
                   jax.ShapeDtypeStruct((bsz, N_HEADS // 2, seq, 2 * LANES), bf16),
                   jax.ShapeDtypeStruct((bsz, seq, MIX_W), bf16)],
        scratch_shapes=[pltpu.VMEM((LANES, MIX_W), f32)],
        compiler_params=_cparams(("parallel", "arbitrary"), 32),
    )(proj, qw, kw, bd_mean)


def _moba_attn_kernel(cfar_ref, qaug_ref, kaug_ref, v_ref, tab_ref, o_ref):
    p = pl.program_id(1)
    i = pl.program_id(2)
    blk = MOBA_BLOCK
    r, c = _iota2((blk, blk))
    own0 = pl.multiple_of(i * blk, blk)
    outs = []
    for hh in range(2):
        qa = qaug_ref[0, hh]
        k_own = kaug_ref[0, 0, pl.ds(own0, blk), :]
        v_own = v_ref[0, pl.ds(own0, blk), :]
        s = lax.dot_general(qa[:, :LANES], k_own[:, :LANES], (((1,), (1,)), ((), ())), preferred_element_type=f32)
        s = jnp.where(r >= c, s + tab_ref[0, hh, 1], NEG)
        m = jnp.max(s, axis=-1, keepdims=True)
        pe = jnp.exp(s - m)
        l = jnp.sum(pe, axis=-1, keepdims=True)
        acc = jnp.dot(pe.astype(bf16), v_own, preferred_element_type=f32)

        def step(j, carry, bias, qa=qa):
            m, l, acc = carry
            j0 = pl.multiple_of(j * blk, blk)
            kj = kaug_ref[0, 0, pl.ds(j0, blk), :]
            vj = v_ref[0, pl.ds(j0, blk), :]
            s = lax.dot_general(qa, kj, (((1,), (1,)), ((), ())), preferred_element_type=f32) + bias
            m_new = jnp.maximum(m, jnp.max(s, axis=-1, keepdims=True))
            alpha = jnp.exp(m - m_new)
            pe = jnp.exp(s - m_new)
            l = alpha * l + jnp.sum(pe, axis=-1, keepdims=True)
            acc = alpha * acc + jnp.dot(pe.astype(bf16), vj, preferred_element_type=f32)
            return m_new, l, acc

        n_far = jnp.maximum(i - 1, 0)
        carry = step(n_far, (m, l, acc), tab_ref[0, hh, 0])
        cf = cfar_ref[2 * p + hh]
        m, l, acc = lax.fori_loop(0, n_far, lambda j, cr: step(j, cr, cf), carry)
        outs.append(acc / l)
    lane = lax.broadcasted_iota(jnp.int32, (blk, LANES), 1)
    o_ref[0] = jnp.where(lane < HEAD_DIM, outs[0], outs[1])


def _moba_attn(qaug, kaug, v, tab, cfar):
    bsz, _, seq, _ = qaug.shape
    nb = seq // MOBA_BLOCK
    return pl.pallas_call(
        _moba_attn_kernel, name="moba_attn",
        grid=(bsz, N_HEADS // 2, nb),
        in_specs=[pl.BlockSpec(memory_space=pltpu.SMEM),
                  pl.BlockSpec((1, 2, MOBA_BLOCK, 2 * LANES), lambda b, p, i: (b, p, i, 0)),
                  pl.BlockSpec((1, 1, seq, 2 * LANES), lambda b, p, i: (b, p, 0, 0)),
                  pl.BlockSpec((1, seq, LANES), lambda b, p, i: (b, 0, p)),
                  pl.BlockSpec((1, 2, 2, MOBA_BLOCK, MOBA_BLOCK), lambda b, p, i: (p, 0, 0, 0, 0))],
        out_specs=pl.BlockSpec((1, MOBA_BLOCK, LANES), lambda b, p, i: (b, i, p)),
        out_shape=jax.ShapeDtypeStruct((bsz, seq, MIX_W), f32),
        compiler_params=_cparams(("parallel", "parallel", "arbitrary"), 40),
    )(cfar, qaug, kaug, v, tab)


def _t5_bucket(dist):
    n = jnp.maximum(dist, 0)
    max_exact = REL_BUCKETS // 2
    nf = jnp.maximum(n, max_exact).astype(f32)
    large = max_exact + (jnp.log(nf / max_exact) / math.log(REL_MAX_DIST / max_exact)
                         * (REL_BUCKETS - max_exact)).astype(jnp.int32)
    large = jnp.minimum(large, REL_BUCKETS - 1)
    return jnp.where(n < max_exact, n, large)


def _moba_bias_tables(rel_bias):
    r = jnp.arange(MOBA_BLOCK)[:, None]
    c = jnp.arange(MOBA_BLOCK)[None, :]
    bias_t = rel_bias.T
    adj = bias_t[:, _t5_bucket(MOBA_BLOCK + r - c)]
    own = bias_t[:, _t5_bucket(r - c)]
    tab = jnp.stack([adj, own], axis=1).reshape(N_HEADS // 2, 2, 2, MOBA_BLOCK, MOBA_BLOCK)
    cfar = bias_t[:, _t5_bucket(jnp.array(2 * MOBA_BLOCK))]
    return tab.astype(f32), cfar.astype(f32)


def _causal_conv(x, halo, w_ref):
    ts = x.shape[0]
    xe = jnp.concatenate([halo, x], axis=0)
    acc = x * w_ref[CONV_K - 1:CONV_K, :]
    for d in range(1, CONV_K):
        acc = acc + xe[SUBLANES - d:SUBLANES - d + ts] * w_ref[CONV_K - 1 - d:CONV_K - d, :]
    return acc


def _write_heads(o_ref, val):
    for h in range(N_HEADS):
        o_ref[0, h] = val[:, h * HEAD_DIM:(h + 1) * HEAD_DIM]


def _read_heads(ref):
    return jnp.concatenate([ref[0, h] for h in range(N_HEADS)], axis=-1)


def _hm_spec(rows):
    return pl.BlockSpec((1, N_HEADS, rows, HEAD_DIM), lambda b, i: (b, 0, i, 0))


def _hm_shape(bsz, seq):
    return jax.ShapeDtypeStruct((bsz, N_HEADS, seq, HEAD_DIM), f32)


def _row_spec(width, off, nt):
    return pl.BlockSpec((ROW_TILE, width), lambda b, i: (b * nt + i, off // width))


def _halo_spec(width, off, nt):
    per = ROW_TILE // SUBLANES
    return pl.BlockSpec((SUBLANES, width), lambda b, i: (jnp.maximum((b * nt + i) * per - 1, 0), off // width))


def _const_spec(shape):
    return pl.BlockSpec(shape, lambda b, i: (0,) * len(shape))


def _gdn_prep_kernel(x_ref, halo_ref, sm_ref, cw_ref, alog_ref, dtb_ref, bd_ref, eb_ref,
                     q_ref, k_ref, kb_ref, vb_ref, g_ref):
    i = pl.program_id(1)
    halo = jnp.where(i == 0, 0.0, halo_ref[...])
    qkv = _silu(_causal_conv(x_ref[...], halo, cw_ref))
    q, k, v = qkv[:, :MIX_W], qkv[:, MIX_W:2 * MIX_W], qkv[:, 2 * MIX_W:]
    bd = bd_ref[...]
    q = q * lax.rsqrt(_mm_hi(q * q, bd) + L2_EPS) * HEAD_DIM ** -0.5
    k = k * lax.rsqrt(_mm_hi(k * k, bd) + L2_EPS)
    sm = sm_ref[...]
    beta = _mm_hi(jax.nn.sigmoid(sm), eb_ref[...])
    g = -jnp.exp(alog_ref[...]) * _softplus(sm + dtb_ref[...])
    _write_heads(q_ref, q)
    _write_heads(k_ref, k)
    _write_heads(kb_ref, k * beta)
    _write_heads(vb_ref, v * beta)
    g_ref[...] = g[:, SM_BA:SM_BA + N_HEADS]


def _gdn_prep(proj, bsz, seq, conv_w, alog128, dtb128, bd_ones, e_beta):
    nt = seq // ROW_TILE
    w3 = 3 * MIX_W
    return pl.pallas_call(
        _gdn_prep_kernel, name="gdn_prep",
        grid=(bsz, nt),
        in_specs=[_row_spec(w3, OFF_BQKV, nt), _halo_spec(w3, OFF_BQKV, nt), _row_spec(LANES, OFF_SMALL, nt),
                  _const_spec((CONV_K, w3)), _const_spec((1, LANES)), _const_spec((1, LANES)),
                  _const_spec((MIX_W, MIX_W)), _const_spec((LANES, MIX_W))],
        out_specs=[_hm_spec(ROW_TILE)] * 4 + [pl.BlockSpec((ROW_TILE, N_HEADS), lambda b, i: (b * nt + i, 0))],
        out_shape=[_hm_shape(bsz, seq)] * 4 + [jax.ShapeDtypeStruct((bsz * seq, N_HEADS), f32)],
        compiler_params=_cparams(("parallel", "parallel"), 40),
    )(proj, proj, proj, conv_w, alog128, dtb128, bd_ones, e_beta)


def _gdn_chunk_kernel(q_ref, k_ref, kb_ref, vb_ref, g_ref, gt_ref, o_ref, st_scr):
    @pl.when(pl.program_id(1) == 0)
    def _():
        st_scr[...] = jnp.zeros_like(st_scr)

    n = GDN_CHUNK
    r, c = _iota2((n, n))
    tril = jnp.where(r >= c, 1.0, 0.0)
    gc_all = _mm_hi(tril, g_ref[0, 0])
    gct_all = _mm_nt_hi(gt_ref[0, 0], tril)
    for h in range(N_HEADS):
        q, k, kb, vb = q_ref[0, h], k_ref[0, h], kb_ref[0, h], vb_ref[0, h]
        gc = gc_all[:, h:h + 1]
        decay = jnp.exp(jnp.where(r >= c, gc - gct_all[h:h + 1, :], NEG))
        t = _unit_lower_inverse(jnp.where(r > c, _mm_nt(kb, k) * decay, 0.0))
        eg = jnp.exp(gc)
        u = _mm(t, vb)
        w = _mm(t, kb * eg)
        a_in = _mm_nt(q, k) * decay
        g_last = gc_all[n - 1:n, h:h + 1]
        st = st_scr[h]
        v_new = u - _mm(w, st)
        o_ref[0, h] = _mm(q * eg, st) + _mm(a_in, v_new)
        st_scr[h] = st * jnp.exp(g_last) + _mm_tn(k * jnp.exp(g_last - gc), v_new)


def _gdn_chunk(q, k, kb, vb, g):
    bsz, _, seq, _ = q.shape
    n = GDN_CHUNK
    nc = seq // n
    g4 = g.reshape(bsz, nc, n, N_HEADS)
    gt4 = jnp.swapaxes(g4, 2, 3)
    return pl.pallas_call(
        _gdn_chunk_kernel, name="gdn_chunk",
        grid=(bsz, nc),
        in_specs=[_hm_spec(n)] * 4 + [pl.BlockSpec((1, 1, n, N_HEADS), lambda b, i: (b, i, 0, 0)),
                                      pl.BlockSpec((1, 1, N_HEADS, n), lambda b, i: (b, i, 0, 0))],
        out_specs=_hm_spec(n),
        out_shape=_hm_shape(bsz, seq),
        scratch_shapes=[pltpu.VMEM((N_HEADS, HEAD_DIM, HEAD_DIM), f32)],
        compiler_params=_cparams(("parallel", "arbitrary"), 32),
    )(q, k, kb, vb, g4, gt4)


def _rwkv_prep_kernel(has_vres, *refs):
    if has_vres:
        (c_ref, halo_ref, mu_ref, w0_ref, wup_ref, a0_ref, aup_ref, gup_ref, kk_ref, ka_ref, bd_ref,
         vf_ref, v0_ref, vdn_ref, vup_ref,
         r_ref, lw_ref, k_ref, v_ref, na_ref, b_ref, gout_ref) = refs
    else:
        (c_ref, halo_ref, mu_ref, w0_ref, wup_ref, a0_ref, aup_ref, gup_ref, kk_ref, ka_ref, bd_ref,
         r_ref, lw_ref, k_ref, v_ref, na_ref, b_ref, gout_ref, cv_ref) = refs
    i = pl.program_id(1)
    c = c_ref[...]
    last = jnp.where(i == 0, 0.0, halo_ref[...])[SUBLANES - 1:SUBLANES]
    prev = jnp.concatenate([last, c[:-1]], axis=0)
    c = c + (prev - c) * mu_ref[...]
    c_r, c_k, c_v = c[:, :MIX_W], c[:, MIX_W:2 * MIX_W], c[:, 2 * MIX_W:3 * MIX_W]
    c_wd = c[:, 3 * MIX_W:3 * MIX_W + 64]
    c_ad = c[:, 3 * MIX_W + 64:3 * MIX_W + 128]
    c_gd = c[:, 3 * MIX_W + 128:]
    w_log = -_softplus(-(w0_ref[...] + _mm(jnp.tanh(c_wd), wup_ref[...]))) - 0.5
    a_in = jax.nn.sigmoid(a0_ref[...] + _mm(c_ad, aup_ref[...]))
    gout_ref[...] = _mm(jax.nn.sigmoid(c_gd), gup_ref[...])
    if has_vres:
        lam = jax.nn.sigmoid(v0_ref[...] + _mm(_mm(c_v, vdn_ref[...]), vup_ref[...]))
        v_r = c_v + (vf_ref[...] - c_v) * lam
    else:
        v_r = c_v
        cv_ref[...] = c_v
    kk = c_k * kk_ref[...]
    kk = kk * lax.rsqrt(_mm_hi(kk * kk, bd_ref[...]) + L2_EPS)
    _write_heads(r_ref, c_r)
    _write_heads(lw_ref, -jnp.exp(w_log))
    _write_heads(k_ref, c_k * (1.0 + (a_in - 1.0) * ka_ref[...]))
    _write_heads(v_ref, v_r)
    _write_heads(na_ref, -kk)
    _write_heads(b_ref, kk * a_in)


def _rwkv_prep(proj, bsz, seq, mu, w0, w_up, a0, a_up, g_up, k_k, k_a, bd_ones, vres):
    nt = seq // ROW_TILE
    wc = 3 * MIX_W + 256
    std = pl.BlockSpec((ROW_TILE, MIX_W), lambda b, i: (b * nt + i, 0))
    in_specs = [_row_spec(wc, OFF_C, nt), _halo_spec(wc, OFF_C, nt), _const_spec((1, wc)),
                _const_spec((1, MIX_W)), _const_spec((64, MIX_W)), _const_spec((1, MIX_W)), _const_spec((64, MIX_W)),
                _const_spec((128, MIX_W)), _const_spec((1, MIX_W)), _const_spec((1, MIX_W)), _const_spec((MIX_W, MIX_W))]
    args = [proj, proj, mu, w0, w_up, a0, a_up, g_up, k_k, k_a, bd_ones]
    out_specs = [_hm_spec(ROW_TILE)] * 6 + [std]
    out_shape = [_hm_shape(bsz, seq)] * 6 + [jax.ShapeDtypeStruct((bsz * seq, MIX_W), f32)]
    if vres is not None:
        v_first, v0, v_down, v_up = vres
        in_specs += [std, _const_spec((1, MIX_W)), _const_spec(v_down.shape), _const_spec(v_up.shape)]
        args += [v_first, v0, v_down, v_up]
    else:
        out_specs.append(std)
        out_shape.append(jax.ShapeDtypeStruct((bsz * seq, MIX_W), f32))
    return pl.pallas_call(
        functools.partial(_rwkv_prep_kernel, vres is not None), name="rwkv_prep",
        grid=(bsz, nt), in_specs=in_specs, out_specs=out_specs, out_shape=out_shape,
        compiler_params=_cparams(("parallel", "parallel"), 40),
    )(*args)


def _rwkv_chunk_kernel(r_ref, lw_ref, k_ref, v_ref, na_ref, b_ref, o_ref, st_scr):
    @pl.when(pl.program_id(1) == 0)
    def _():
        st_scr[...] = jnp.zeros_like(st_scr)

    n = RWKV_CHUNK
    row, col = _iota2((n, n))
    tril = jnp.where(row >= col, 1.0, 0.0)
    for h in range(N_HEADS):
        lw, v = lw_ref[0, h], v_ref[0, h]
        lc = _mm_hi(tril, lw)
        e_neg = jnp.exp(-lc)
        r_t = r_ref[0, h] * jnp.exp(lc)
        a_t = na_ref[0, h] * jnp.exp(lc - lw)
        b_t = b_ref[0, h] * e_neg
        k_t = k_ref[0, h] * e_neg
        m_ab = jnp.where(row > col, _mm_nt(a_t, b_t), 0.0)
        m_ak = jnp.where(row > col, _mm_nt(a_t, k_t), 0.0)
        a_rb = jnp.where(row >= col, _mm_nt(r_t, b_t), 0.0)
        a_rk = jnp.where(row >= col, _mm_nt(r_t, k_t), 0.0)
        t = _unit_lower_inverse(-m_ab)
        st = st_scr[h]
        u = _mm(t, _mm_nt(a_t, st) + _mm(m_ak, v))
        o_ref[0, h] = _mm_nt(r_t, st) + _mm(a_rb, u) + _mm(a_rk, v)
        p_end = jnp.exp(lc[n - 1:n, :])
        st_scr[h] = st * p_end + _mm_tn(u, b_t * p_end) + _mm_tn(v, k_t * p_end)


def _rwkv_chunk(r, lw, k, v, na, b):
    bsz, _, seq, _ = r.shape
    n = RWKV_CHUNK
    return pl.pallas_call(
        _rwkv_chunk_kernel, name="rwkv_chunk",
        grid=(bsz, seq // n),
        in_specs=[_hm_spec(n)] * 6,
        out_specs=_hm_spec(n),
        out_shape=_hm_shape(bsz, seq),
        scratch_shapes=[pltpu.VMEM((N_HEADS, HEAD_DIM, HEAD_DIM), f32)],
        compiler_params=_cparams(("parallel", "arbitrary"), 32),
    )(r, lw, k, v, na, b)


def _ssd_prep_kernel(x_ref, halo_ref, sm_ref, cw_ref, cb_ref, alog_ref, dtb_ref, edt_ref,
                     xdt_ref, x_out_ref, bc_ref, adt_ref):
    i = pl.program_id(1)
    halo = jnp.where(i == 0, 0.0, halo_ref[...])
    xbc = _silu(_causal_conv(x_ref[...], halo, cw_ref) + cb_ref[...])
    m_x = xbc[:, :MIX_W]
    dt = _softplus(sm_ref[...] + dtb_ref[...])
    _write_heads(xdt_ref, m_x * _mm_hi(dt, edt_ref[...]))
    _write_heads(x_out_ref, m_x)
    bc_ref[...] = xbc[:, MIX_W:]
    adt_ref[...] = (dt * -jnp.exp(alog_ref[...]))[:, SM_DT:SM_DT + N_HEADS]


def _ssd_prep(proj, bsz, seq, conv_w, conv_b, alog128, dtb128, e_dt):
    nt = seq // ROW_TILE
    wx = MIX_W + 4 * SSM_STATE
    return pl.pallas_call(
        _ssd_prep_kernel, name="ssd_prep",
        grid=(bsz, nt),
        in_specs=[_row_spec(wx, OFF_DXBC, nt), _halo_spec(wx, OFF_DXBC, nt), _row_spec(LANES, OFF_SMALL, nt),
                  _const_spec((CONV_K, wx)), _const_spec((1, wx)), _const_spec((1, LANES)), _const_spec((1, LANES)),
                  _const_spec((LANES, MIX_W))],
        out_specs=[_hm_spec(ROW_TILE)] * 2 + [pl.BlockSpec((ROW_TILE, 4 * SSM_STATE), lambda b, i: (b * nt + i, 0)),
                                              pl.BlockSpec((ROW_TILE, N_HEADS), lambda b, i: (b * nt + i, 0))],
        out_shape=[_hm_shape(bsz, seq)] * 2 + [jax.ShapeDtypeStruct((bsz * seq, 4 * SSM_STATE), f32),
                                               jax.ShapeDtypeStruct((bsz * seq, N_HEADS), f32)],
        compiler_params=_cparams(("parallel", "parallel"), 40),
    )(proj, proj, proj, conv_w, conv_b, alog128, dtb128, e_dt)


def _ssd_chunk_kernel(xdt_ref, x_ref, bc_ref, a_ref, at_ref, dvec_ref, o_ref, st_scr):
    @pl.when(pl.program_id(1) == 0)
    def _():
        st_scr[...] = jnp.zeros_like(st_scr)

    n = SSD_CHUNK
    r, c = _iota2((n, n))
    tril = jnp.where(r >= c, 1.0, 0.0)
    acs_all = _mm_hi(tril, a_ref[0, 0])
    acst_all = _mm_nt_hi(at_ref[0, 0], tril)
    bc = bc_ref[...]
    heads_per_group = N_HEADS // 2
    for g in range(2):
        b_g = bc[:, g * SSM_STATE:(g + 1) * SSM_STATE]
        c_g = bc[:, (2 + g) * SSM_STATE:(3 + g) * SSM_STATE]
        cb = _mm_nt(c_g, b_g)
        for j in range(heads_per_group):
            h = g * heads_per_group + j
            ac = acs_all[:, h:h + 1]
            lmat = jnp.exp(jnp.where(r >= c, ac - acst_all[h:h + 1, :], NEG))
            xg = xdt_ref[0, h]
            st = st_scr[h]
            a_last = acs_all[n - 1:n, h:h + 1]
            y = _mm(cb * lmat, xg) + _mm(c_g * jnp.exp(ac), st)
            o_ref[0, h] = y + x_ref[0, h] * dvec_ref[h:h + 1, :]
            st_scr[h] = st * jnp.exp(a_last) + _mm_tn(b_g * jnp.exp(a_last - ac), xg)


def _ssd_chunk(xdt, x, bc, adt, dvec):
    bsz, _, seq, _ = xdt.shape
    n = SSD_CHUNK
    nc = seq // n
    a4 = adt.reshape(bsz, nc, n, N_HEADS)
    at4 = jnp.swapaxes(a4, 2, 3)
    return pl.pallas_call(
        _ssd_chunk_kernel, name="ssd_chunk",
        grid=(bsz, nc),
        in_specs=[_hm_spec(n)] * 2 + [pl.BlockSpec((n, 4 * SSM_STATE), lambda b, i: (b * nc + i, 0)),
                                      pl.BlockSpec((1, 1, n, N_HEADS), lambda b, i: (b, i, 0, 0)),
                                      pl.BlockSpec((1, 1, N_HEADS, n), lambda b, i: (b, i, 0, 0)),
                                      _const_spec((N_HEADS, HEAD_DIM))],
        out_specs=_hm_spec(n),
        out_shape=_hm_shape(bsz, seq),
        scratch_shapes=[pltpu.VMEM((N_HEADS, SSM_STATE, HEAD_DIM), f32)],
        compiler_params=_cparams(("parallel", "arbitrary"), 32),
    )(xdt, x, bc, a4, at4, dvec)


def _merge_kernel(x_ref, ya_ref, ob_ref, bz_ref, gnw_ref,
                  wkv_ref, r_ref, k_ref, v_ref, gout_ref, lnw_ref, lnb_ref, rk_ref,
                  yd_ref, dz_ref, mnw_ref, g0_ref, g1_ref, g2_ref, g3_ref, wb_ref, wo_ref, o_ref):
    def per_head(fn):
        return jnp.concatenate([fn(h) for h in range(N_HEADS)], axis=-1)

    def gdn_head(h):
        o = ob_ref[0, h]
        return o * lax.rsqrt(jnp.mean(o * o, axis=-1, keepdims=True) + RMS_EPS) * gnw_ref[...]

    y_b = per_head(gdn_head) * _silu(bz_ref[...])

    def wkv_head(h):
        w = wkv_ref[0, h]
        mu = jnp.mean(w, axis=-1, keepdims=True)
        var = jnp.mean(jnp.square(w - mu), axis=-1, keepdims=True)
        return (w - mu) * lax.rsqrt(var + RWKV_LN_EPS)

    def bonus_head(h):
        return jnp.sum(r_ref[0, h] * k_ref[0, h] * rk_ref[h:h + 1, :], axis=-1, keepdims=True) * v_ref[0, h]

    y_c = (per_head(wkv_head) * lnw_ref[...] + lnb_ref[...] + per_head(bonus_head)) * gout_ref[...]

    yz = _read_heads(yd_ref) * _silu(dz_ref[...])
    half = MIX_W // 2
    y_d = jnp.concatenate(
        [yz[:, s:s + half] * lax.rsqrt(jnp.mean(jnp.square(yz[:, s:s + half]), axis=-1, keepdims=True) + RMS_EPS)
         for s in (0, half)], axis=-1) * mnw_ref[...]

    acc = jnp.zeros(x_ref.shape, f32)
    for n, (y, g_ref) in enumerate(((ya_ref[0], g0_ref), (y_b, g1_ref), (y_c, g2_ref), (y_d, g3_ref))):
        acc = acc + jax.nn.sigmoid(g_ref[...]) * _mm(y, wb_ref[n])
    o_ref[...] = x_ref[...] + _mm(acc, wo_ref[...])


def _merge(x2, proj, bsz, seq, ya, ob, gnw, wkv, r, k, v, gout, lnw, lnb, rk, yd, mnw, wb, wo):
    nt = seq // ROW_TILE
    std = lambda w: pl.BlockSpec((ROW_TILE, w), lambda b, i: (b * nt + i, 0))
    hm = _hm_spec(ROW_TILE)
    gate = lambda n: _row_spec(D_MODEL, OFF_GATES + n * D_MODEL, nt)
    return pl.pallas_call(
        _merge_kernel, name="merge",
        grid=(bsz, nt),
        in_specs=[std(D_MODEL), pl.BlockSpec((1, ROW_TILE, MIX_W), lambda b, i: (b, i, 0)),
                  hm, _row_spec(MIX_W, OFF_BZ, nt), _const_spec((1, HEAD_DIM)),
                  hm, hm, hm, hm, std(MIX_W), _const_spec((1, MIX_W)), _const_spec((1, MIX_W)),
                  _const_spec((N_HEADS, HEAD_DIM)),
                  hm, _row_spec(MIX_W, OFF_DZ, nt), _const_spec((1, MIX_W)),
                  gate(0), gate(1), gate(2), gate(3),
                  _const_spec((4, MIX_W, D_MODEL)), _const_spec((D_MODEL, D_MODEL))],
        out_specs=std(D_MODEL),
        out_shape=jax.ShapeDtypeStruct((bsz * seq, D_MODEL), f32),
        compiler_params=_cparams(("parallel", "parallel"), 48),
    )(x2, ya, ob, proj, gnw, wkv, r, k, v, gout, lnw, lnb, rk, yd, proj, mnw, proj, proj, proj, proj, wb, wo)


def _lane_vec(vals, off):
    return jnp.zeros((1, LANES), f32).at[0, off:off + vals.shape[0]].set(vals)


def _head_expand(off):
    n = jnp.arange(LANES)[:, None]
    c = jnp.arange(MIX_W)[None, :]
    return (n - off == c // HEAD_DIM).astype(f32)


def _pack_w_in(w):
    pad = lambda n: jnp.zeros((w.shape[0], n), w.dtype)
    cols = [w[:, W_A:W_BZ],
            w[:, W_BZ:W_BBETA],
            w[:, W_C:W_DZ],
            w[:, W_BBETA:W_C], w[:, W_DDT:W_GATES], pad(2 * LANES - 3 * N_HEADS),
            w[:, W_DZ:W_DXBC], w[:, W_DXBC:W_DDT], w[:, W_GATES:]]
    out = jnp.concatenate(cols, axis=1).astype(bf16)
    assert out.shape[1] == N_PROJ
    return out


def kernel(x, rel_bias, norm1_w, w_in, moba_q_norm, moba_k_norm, gdn_conv_w, gdn_A_log, gdn_dt_bias, gdn_norm_w, rwkv_mu, rwkv_w0, rwkv_w_up, rwkv_a0, rwkv_a_up, rwkv_g_up, rwkv_k_k, rwkv_k_a, rwkv_r_k, rwkv_v0, rwkv_v_down, rwkv_v_up, rwkv_ln_w, rwkv_ln_b, mamba_conv_w, mamba_conv_b, mamba_dt_bias, mamba_A_log, mamba_D, mamba_norm_w, w_branch, w_out, norm2_w, ffn_w_in, ffn_w_down):
    bsz, seq, d = x.shape
    depth = w_in.shape[0]
    assert d == D_MODEL and seq % MM_TILE_M == 0
    x2 = x.reshape(bsz * seq, d)
    row = lambda v: v.reshape(1, -1).astype(f32)

    hid = jnp.arange(MIX_W) // HEAD_DIM
    bd_ones = (hid[:, None] == hid[None, :]).astype(f32)
    bd_mean = bd_ones / HEAD_DIM
    e_beta, e_dt = _head_expand(SM_BETA), _head_expand(SM_DT)
    tab, cfar = _moba_bias_tables(rel_bias)
    v_first = None
    for i in range(depth):
        proj = _norm_matmul(x2, row(norm1_w[i]), _pack_w_in(w_in[i]), 1024)

        qaug, kaug, v_a = _moba_prep(proj, bsz, seq, row(jnp.tile(moba_q_norm[i], N_HEADS)),
                                     row(jnp.tile(moba_k_norm[i], N_HEADS)), bd_mean)
        y_a = _moba_attn(qaug, kaug, v_a, tab, cfar)

        q_b, k_b, kb_b, vb_b, g_b = _gdn_prep(proj, bsz, seq, gdn_conv_w[i], _lane_vec(gdn_A_log[i], SM_BA),
                                              _lane_vec(gdn_dt_bias[i], SM_BA), bd_ones, e_beta)
        o_b = _gdn_chunk(q_b, k_b, kb_b, vb_b, g_b)

        vres = None if i == 0 else (v_first, row(rwkv_v0[i - 1]), rwkv_v_down[i - 1].astype(bf16),
                                    rwkv_v_up[i - 1].astype(bf16))
        outs = _rwkv_prep(proj, bsz, seq, row(rwkv_mu[i]), row(rwkv_w0[i]), rwkv_w_up[i].astype(bf16),
                          row(rwkv_a0[i]), rwkv_a_up[i].astype(bf16), rwkv_g_up[i].astype(bf16),
                          row(rwkv_k_k[i]), row(rwkv_k_a[i]), bd_ones, vres)
        r_c, lw_c, k_c, v_c, na_c, b_c, g_out = outs[:7]
        if i == 0:
            v_first = outs[7]
        wkv = _rwkv_chunk(r_c, lw_c, k_c, v_c, na_c, b_c)

        xdt, x_d, bc, adt = _ssd_prep(proj, bsz, seq, mamba_conv_w[i], row(mamba_conv_b[i]),
                                      _lane_vec(mamba_A_log[i], SM_DT), _lane_vec(mamba_dt_bias[i], SM_DT), e_dt)
        y_d = _ssd_chunk(xdt, x_d, bc, adt, jnp.broadcast_to(mamba_D[i][:, None], (N_HEADS, HEAD_DIM)).astype(f32))

        x2 = _merge(x2, proj, bsz, seq, y_a, o_b, row(gdn_norm_w[i]), wkv, r_c, k_c, v_c, g_out,
                    row(rwkv_ln_w[i]), row(rwkv_ln_b[i]), rwkv_r_k[i].astype(f32), y_d, row(mamba_norm_w[i]),
                    w_branch[i].astype(bf16), w_out[i].astype(bf16))

        act = _ffn_up(x2, row(norm2_w[i]), ffn_w_in[i][:, :D_FF].astype(bf16), ffn_w_in[i][:, D_FF:].astype(bf16))
        x2 = _ffn_down(x2, act, ffn_w_down[i].astype(bf16))
    return x2.reshape(bsz, seq, d)
```

```python
import functools
import math

import jax
import jax.numpy as jnp
from jax import lax
from jax.experimental import pallas as pl
from jax.experimental.pallas import tpu as pltpu

f32, bf16 = jnp.float32, jnp.bfloat16
HI = lax.Precision.HIGHEST

D_MODEL = 1024
N_HEADS = 8
HEAD_DIM = 64
MIX_W = N_HEADS * HEAD_DIM
RMS_EPS = 1e-6
L2_EPS = 1e-6
CONV_K = 4
MOBA_BLOCK = 256
MOBA_TOPK = 3
REL_BUCKETS = 32
REL_MAX_DIST = 128
GDN_CHUNK = 64
RWKV_CHUNK = 64
RWKV_LN_EPS = 64e-5
SSM_STATE = 128
SSD_CHUNK = 128
D_FF = 2816
NEG = -1e30
LOG2E = math.log2(math.e)

LANES = 128
SUBLANES = 8

OFF_A, OFF_BQKV, OFF_BZ, OFF_C, OFF_SMALL, OFF_DZ, OFF_DXBC, OFF_GATES = 0, 1536, 3072, 3584, 5376, 5632, 6144, 7168
N_PROJ = OFF_GATES + 4 * D_MODEL
SM_BETA, SM_BA, SM_DT = 0, 8, 16
W_A, W_BQKV, W_BZ, W_BBETA, W_BA, W_C, W_DZ, W_DXBC, W_DDT, W_GATES = 0, 1536, 3072, 3584, 3592, 3600, 5392, 5904, 6928, 6936

ROW_TILE = 256
MM_TILE_M = 512


def _cparams(sem, vmem_mb):
    return pltpu.CompilerParams(dimension_semantics=sem, vmem_limit_bytes=vmem_mb * 1024 * 1024)


def _mm(a, b):
    return jnp.dot(a.astype(bf16), b.astype(bf16), preferred_element_type=f32)


def _mm_nt(a, b):
    return lax.dot_general(a.astype(bf16), b.astype(bf16), (((1,), (1,)), ((), ())), preferred_element_type=f32)


def _mm_tn(a, b):
    return lax.dot_general(a.astype(bf16), b.astype(bf16), (((0,), (0,)), ((), ())), preferred_element_type=f32)


def _mm_hi(a, b):
    return jnp.dot(a, b, precision=HI, preferred_element_type=f32)


def _mm_nt_hi(a, b):
    return lax.dot_general(a, b, (((1,), (1,)), ((), ())), precision=HI, preferred_element_type=f32)


def _softplus(x):
    return jnp.maximum(x, 0.0) + jnp.log1p(jnp.exp(-jnp.abs(x)))


def _silu(x):
    return x * jax.nn.sigmoid(x)


def _iota2(shape):
    return lax.broadcasted_iota(jnp.int32, shape, 0), lax.broadcasted_iota(jnp.int32, shape, 1)


def _split3(x):
    hi = x.astype(bf16)
    r1 = x - hi.astype(f32)
    mid = r1.astype(bf16)
    return hi, mid, (r1 - mid.astype(f32)).astype(bf16)


def _sel_left(m01, x):
    return sum(jnp.dot(m01, p, preferred_element_type=f32) for p in _split3(x))


def _sel_right(x, m01):
    return sum(jnp.dot(p, m01, preferred_element_type=f32) for p in _split3(x))


def _unit_lower_inverses(l_list):
    n = l_list[0].shape[0]
    r, c = _iota2((n, n))
    eye = jnp.where(r == c, 1.0, 0.0)
    ts = [eye for _ in l_list]
    for ls in range(n.bit_length() - 1):
        m = ((r >> (ls + 1)) == (c >> (ls + 1))) & (((r >> ls) & 1) == 1) & (((c >> ls) & 1) == 0)
        lms = [jnp.where(m, l, 0.0) for l in l_list]
        if ls == 0:
            ts = [t - lm for t, lm in zip(ts, lms)]
        else:
            tl = [_mm(t, lm) for t, lm in zip(ts, lms)]
            ts = [t - _mm(x, t) for t, x in zip(ts, tl)]
    return ts


def _norm_matmul_kernel(x_ref, nw_ref, w_ref, o_ref, h_scr):
    @pl.when(pl.program_id(1) == 0)
    def _():
        x = x_ref[...]
        y = x * lax.rsqrt(jnp.mean(x * x, axis=-1, keepdims=True) + RMS_EPS)
        h_scr[...] = (y * nw_ref[...]).astype(bf16)
    o_ref[...] = jnp.dot(h_scr[...], w_ref[...], preferred_element_type=f32).astype(o_ref.dtype)


def _norm_matmul(x2, nw, w, tn):
    t, d = x2.shape
    n = w.shape[1]
    tm = MM_TILE_M
    return pl.pallas_call(
        _norm_matmul_kernel, name="norm_matmul",
        grid=(t // tm, n // tn),
        in_specs=[pl.BlockSpec((tm, d), lambda i, j: (i, 0)),
                  pl.BlockSpec((1, d), lambda i, j: (0, 0)),
                  pl.BlockSpec((d, tn), lambda i, j: (0, j))],
        out_specs=pl.BlockSpec((tm, tn), lambda i, j: (i, j)),
        out_shape=jax.ShapeDtypeStruct((t, n), f32),
        scratch_shapes=[pltpu.VMEM((tm, d), bf16)],
        compiler_params=_cparams(("parallel", "arbitrary"), 40),
    )(x2, nw, w)


def _ffn_up_kernel(x_ref, nw_ref, wg_ref, wu_ref, o_ref, h_scr):
    @pl.when(pl.program_id(1) == 0)
    def _():
        x = x_ref[...]
        y = x * lax.rsqrt(jnp.mean(x * x, axis=-1, keepdims=True) + RMS_EPS)
        h_scr[...] = (y * nw_ref[...]).astype(bf16)
    h = h_scr[...]
    g = jnp.dot(h, wg_ref[...], preferred_element_type=f32)
    u = jnp.dot(h, wu_ref[...], preferred_element_type=f32)
    o_ref[...] = (_silu(g) * u).astype(o_ref.dtype)


def _ffn_up(x2, nw, wg, wu):
    t, d = x2.shape
    n = wg.shape[1]
    tm, tn = MM_TILE_M, n // 2
    return pl.pallas_call(
        _ffn_up_kernel, name="ffn_up",
        grid=(t // tm, n // tn),
        in_specs=[pl.BlockSpec((tm, d), lambda i, j: (i, 0)),
                  pl.BlockSpec((1, d), lambda i, j: (0, 0)),
                  pl.BlockSpec((d, tn), lambda i, j: (0, j)),
                  pl.BlockSpec((d, tn), lambda i, j: (0, j))],
        out_specs=pl.BlockSpec((tm, tn), lambda i, j: (i, j)),
        out_shape=jax.ShapeDtypeStruct((t, n), bf16),
        scratch_shapes=[pltpu.VMEM((tm, d), bf16)],
        compiler_params=_cparams(("parallel", "arbitrary"), 48),
    )(x2, nw, wg, wu)


def _ffn_down_kernel(x_ref, a_ref, w_ref, o_ref):
    o_ref[...] = x_ref[...] + jnp.dot(a_ref[...], w_ref[...], preferred_element_type=f32)


def _ffn_down(x2, act, wd):
    t, d = x2.shape
    n = act.shape[1]
    tm = MM_TILE_M
    return pl.pallas_call(
        _ffn_down_kernel, name="ffn_down",
        grid=(t // tm,),
        in_specs=[pl.BlockSpec((tm, d), lambda i: (i, 0)),
                  pl.BlockSpec((tm, n), lambda i: (i, 0)),
                  pl.BlockSpec((n, d), lambda i: (0, 0))],
        out_specs=pl.BlockSpec((tm, d), lambda i: (i, 0)),
        out_shape=jax.ShapeDtypeStruct((t, d), f32),
        compiler_params=_cparams(("parallel",), 40),
    )(x2, act, wd)


def _top3_bias(gate, n_past):
    col = lax.broadcasted_iota(jnp.int32, gate.shape, 1)
    g = jnp.where(col < n_past, gate, -jnp.inf)
    sel = jnp.zeros(gate.shape, jnp.bool_)
    for _ in range(MOBA_TOPK):
        m = jnp.max(g, axis=-1, keepdims=True)
        idx = jnp.min(jnp.where(g == m, col, gate.shape[1]), axis=-1, keepdims=True)
        pick = col == idx
        sel = sel | (pick & (m > -jnp.inf))
        g = jnp.where(pick, -jnp.inf, g)
    return jnp.where(sel, 0.0, NEG)


def _moba_prep_kernel(a_ref, qw_ref, kw_ref, bd_ref, qaug_ref, kaug_ref, vaug_ref, kmean_scr):
    i = pl.program_id(1)

    @pl.when(i == 0)
    def _():
        kmean_scr[...] = jnp.zeros_like(kmean_scr)

    a = a_ref[...]
    q, k, v = a[:, :MIX_W], a[:, MIX_W:2 * MIX_W], a[:, 2 * MIX_W:]
    bd = bd_ref[...]
    qn = q * lax.rsqrt(_sel_right(q * q, bd) + RMS_EPS) * qw_ref[...]
    kn = k * lax.rsqrt(_sel_right(k * k, bd) + RMS_EPS) * kw_ref[...]
    lane = lax.broadcasted_iota(jnp.int32, (MOBA_BLOCK, LANES), 1)
    onehot = jnp.where(lane == i, 1.0, 0.0).astype(bf16)
    ones_lane = jnp.where(lax.broadcasted_iota(jnp.int32, (MOBA_BLOCK, HEAD_DIM), 1) == 0, 1.0, 0.0).astype(bf16)
    for h in range(N_HEADS):
        vaug_ref[0, h] = jnp.concatenate([v[:, h * HEAD_DIM:(h + 1) * HEAD_DIM].astype(bf16), ones_lane], axis=-1)
    kmean = kmean_scr[...]
    for p in range(N_HEADS // 2):
        sl = slice(p * LANES, (p + 1) * LANES)
        kaug_ref[0, p] = jnp.concatenate([kn[:, sl].astype(bf16), onehot], axis=-1)
        for hh in range(2):
            keep = (lane < HEAD_DIM) if hh == 0 else (lane >= HEAD_DIM)
            q2m = jnp.where(keep, qn[:, sl], 0.0)
            gate = _mm_nt_hi(q2m, kmean[:, sl])
            selb = _top3_bias(gate, i)
            qaug_ref[0, 2 * p + hh] = jnp.concatenate(
                [(q2m * (HEAD_DIM ** -0.5 * LOG2E)).astype(bf16), selb.astype(bf16)], axis=-1)
    kmean_scr[pl.ds(i, 1), :] = jnp.mean(kn, axis=0, keepdims=True)


def _moba_prep(proj, bsz, seq, qw, kw, bd_mean):
    nb = seq // MOBA_BLOCK
    assert nb <= LANES
    return pl.pallas_call(
        _moba_prep_kernel, name="moba_prep",
        grid=(bsz, nb),
        in_specs=[pl.BlockSpec((MOBA_BLOCK, 3 * MIX_W), lambda b, i: (b * nb + i, OFF_A // (3 * MIX_W))),
                  pl.BlockSpec((1, MIX_W), lambda b, i: (0, 0)),
                  pl.BlockSpec((1, MIX_W), lambda b, i: (0, 0)),
                  pl.BlockSpec((MIX_W, MIX_W), lambda b, i: (0, 0))],
        out_specs=[pl.BlockSpec((1, N_HEADS, MOBA_BLOCK, 2 * LANES), lambda b, i: (b, 0, i, 0)),
                   pl.BlockSpec((1, N_HEADS // 2, MOBA_BLOCK, 2 * LANES), lambda b, i: (b, 0, i, 0)),
                   pl.BlockSpec((1, N_HEADS, MOBA_BLOCK, LANES), lambda b, i: (b, 0, i, 0))],
        out_shape=[jax.ShapeDtypeStruct((bsz, N_HEADS, seq, 2 * LANES), bf16),
                   jax.ShapeDtypeStruct((bsz, N_HEADS // 2, seq, 2 * LANES), bf16),
                   jax.ShapeDtypeStruct((bsz, N_HEADS, seq, LANES), bf16)],
        scratch_shapes=[pltpu.VMEM((LANES, MIX_W), f32)],
        compiler_params=_cparams(("parallel", "arbitrary"), 32),
    )(proj, qw, kw, bd_mean)


def _moba_attn_kernel(qaug_ref, kaug_ref, vaug_ref, tab_ref, o_ref):
    i = pl.program_id(2)
    blk = MOBA_BLOCK
    hs = (0, 1)
    r, c = _iota2((blk, blk))
    nt = lambda a, b: lax.dot_general(a, b, (((1,), (1,)), ((), ())), preferred_element_type=f32)
    rmax = lambda s: jnp.max(s, axis=-1, keepdims=True)
    pv = lambda pe, v: jnp.dot(pe.astype(bf16), v, preferred_element_type=f32)
    kblk = lambda j: kaug_ref[0, 0, pl.ds(pl.multiple_of(j * blk, blk), blk), :]
    vblk = lambda hh, j: vaug_ref[0, hh, pl.ds(pl.multiple_of(j * blk, blk), blk), :]

    n_far = jnp.maximum(i - 1, 0)
    k_own, k_adj = kblk(i), kblk(n_far)
    qa = [qaug_ref[0, hh] for hh in hs]
    s_own = [jnp.where(r >= c, nt(qa[hh][:, :LANES], k_own[:, :LANES]) + tab_ref[0, hh, 1], NEG) for hh in hs]
    s_adj = [nt(qa[hh], k_adj) + tab_ref[0, hh, 0] for hh in hs]
    m = [jnp.maximum(rmax(s_own[hh]), rmax(s_adj[hh])) for hh in hs]
    acc = [pv(jnp.exp2(s_own[hh] - m[hh]), vblk(hh, i)) + pv(jnp.exp2(s_adj[hh] - m[hh]), vblk(hh, n_far))
           for hh in hs]

    def body(t, carry):
        m, acc = list(carry[:2]), list(carry[2:])
        j0 = 2 * t
        j1 = jnp.where(j0 + 1 < n_far, j0 + 1, i)
        k0, k1 = kblk(j0), kblk(j1)
        qa = [qaug_ref[0, hh] for hh in hs]
        s0 = [nt(qa[hh], k0) for hh in hs]
        s1 = [nt(qa[hh], k1) for hh in hs]
        m_new = [jnp.maximum(m[hh], jnp.maximum(rmax(s0[hh]), rmax(s1[hh]))) for hh in hs]
        acc = [jnp.exp2(m[hh] - m_new[hh]) * acc[hh]
               + pv(jnp.exp2(s0[hh] - m_new[hh]), vblk(hh, j0)) + pv(jnp.exp2(s1[hh] - m_new[hh]), vblk(hh, j1))
               for hh in hs]
        return (*m_new, *acc)

    carry = lax.fori_loop(0, (n_far + 1) // 2, body, (*m, *acc))
    acc = carry[2:]
    o_ref[0] = jnp.concatenate([acc[hh][:, :HEAD_DIM] / acc[hh][:, HEAD_DIM:HEAD_DIM + 1] for hh in hs], axis=-1)


def _moba_attn(qaug, kaug, vaug, tab):
    bsz, _, seq, _ = qaug.shape
    nb = seq // MOBA_BLOCK
    return pl.pallas_call(
        _moba_attn_kernel, name="moba_attn",
        grid=(bsz, N_HEADS // 2, nb),
        in_specs=[pl.BlockSpec((1, 2, MOBA_BLOCK, 2 * LANES), lambda b, p, i: (b, p, i, 0)),
                  pl.BlockSpec((1, 1, seq, 2 * LANES), lambda b, p, i: (b, p, 0, 0)),
                  pl.BlockSpec((1, 2, seq, LANES), lambda b, p, i: (b, p, 0, 0)),
                  pl.BlockSpec((1, 2, 2, MOBA_BLOCK, MOBA_BLOCK), lambda b, p, i: (p, 0, 0, 0, 0))],
        out_specs=pl.BlockSpec((1, MOBA_BLOCK, LANES), lambda b, p, i: (b, i, p)),
        out_shape=jax.ShapeDtypeStruct((bsz, seq, MIX_W), f32),
        compiler_params=_cparams(("parallel", "parallel", "arbitrary"), 40),
    )(qaug, kaug, vaug, tab)


def _t5_bucket(dist):
    n = jnp.maximum(dist, 0)
    max_exact = REL_BUCKETS // 2
    nf = jnp.maximum(n, max_exact).astype(f32)
    large = max_exact + (jnp.log(nf / max_exact) / math.log(REL_MAX_DIST / max_exact)
                         * (REL_BUCKETS - max_exact)).astype(jnp.int32)
    large = jnp.minimum(large, REL_BUCKETS - 1)
    return jnp.where(n < max_exact, n, large)


def _moba_bias_tables(rel_bias):
    assert MOBA_BLOCK >= REL_MAX_DIST
    r = jnp.arange(MOBA_BLOCK)[:, None]
    c = jnp.arange(MOBA_BLOCK)[None, :]
    bias_t = rel_bias.T.astype(f32)
    adj = bias_t[:, _t5_bucket(MOBA_BLOCK + r - c)]
    own = bias_t[:, _t5_bucket(r - c)]
    far = bias_t[:, _t5_bucket(jnp.array(2 * MOBA_BLOCK))]
    tab = (jnp.stack([adj, own], axis=1) - far[:, None, None, None]) * LOG2E
    return tab.reshape(N_HEADS // 2, 2, 2, MOBA_BLOCK, MOBA_BLOCK)


def _causal_conv(x, halo, w_ref):
    ts = x.shape[0]
    xe = jnp.concatenate([halo, x], axis=0)
    acc = x * w_ref[CONV_K - 1:CONV_K, :]
    for d in range(1, CONV_K):
        acc = acc + xe[SUBLANES - d:SUBLANES - d + ts] * w_ref[CONV_K - 1 - d:CONV_K - d, :]
    return acc


def _write_heads(o_ref, val):
    for h in range(N_HEADS):
        o_ref[0, h] = val[:, h * HEAD_DIM:(h + 1) * HEAD_DIM].astype(o_ref.dtype)


def _read_heads(ref):
    return jnp.concatenate([ref[0, h] for h in range(N_HEADS)], axis=-1)


def _hm_spec(rows):
    return pl.BlockSpec((1, N_HEADS, rows, HEAD_DIM), lambda b, i: (b, 0, i, 0))


def _hm_shape(bsz, seq, dtype=f32):
    return jax.ShapeDtypeStruct((bsz, N_HEADS, seq, HEAD_DIM), dtype)


def _chunk_sum_matrix(chunk):
    r = jnp.arange(ROW_TILE)[:, None]
    c = jnp.arange(ROW_TILE)[None, :]
    same = (r // chunk) == (c // chunk)
    return jnp.concatenate([same & (r >= c), same], axis=0).astype(bf16)


def _row_spec(width, off, nt):
    return pl.BlockSpec((ROW_TILE, width), lambda b, i: (b * nt + i, off // width))


def _halo_spec(width, off, nt):
    per = ROW_TILE // SUBLANES
    return pl.BlockSpec((SUBLANES, width), lambda b, i: (jnp.maximum((b * nt + i) * per - 1, 0), off // width))


def _const_spec(shape):
    return pl.BlockSpec(shape, lambda b, i: (0,) * len(shape))


def _gdn_prep_kernel(x_ref, halo_ref, sm_ref, cw_ref, alog_ref, dtb_ref, bd_ref, eb_ref, ea_ref, cm_ref,
                     q_ref, k_ref, kb_ref, vb_ref, qd_ref, kbe_ref, kd_ref, gc_ref):
    i = pl.program_id(1)
    halo = jnp.where(i == 0, 0.0, halo_ref[...])
    qkv = _silu(_causal_conv(x_ref[...], halo, cw_ref))
    q, k, v = qkv[:, :MIX_W], qkv[:, MIX_W:2 * MIX_W], qkv[:, 2 * MIX_W:]
    bd = bd_ref[...]
    q = q * lax.rsqrt(_sel_right(q * q, bd) + L2_EPS) * HEAD_DIM ** -0.5
    k = k * lax.rsqrt(_sel_right(k * k, bd) + L2_EPS)
    sm = sm_ref[...]
    beta = _sel_right(jax.nn.sigmoid(sm), eb_ref[...])
    g = -jnp.exp(alog_ref[...]) * _softplus(sm + dtb_ref[...])
    sums = _sel_left(cm_ref[...], g)
    gc, g_end = sums[:ROW_TILE], sums[ROW_TILE:]
    eg = jnp.exp(_sel_right(gc, ea_ref[...]))
    e_rest = jnp.exp(_sel_right(g_end - gc, ea_ref[...]))
    kb = k * beta
    _write_heads(q_ref, q)
    _write_heads(k_ref, k)
    _write_heads(kb_ref, kb)
    _write_heads(vb_ref, v * beta)
    _write_heads(qd_ref, q * eg)
    _write_heads(kbe_ref, kb * eg)
    _write_heads(kd_ref, k * e_rest)
    gc_ref[...] = gc[:, SM_BA:SM_BA + N_HEADS]


def _gdn_prep(proj, bsz, seq, conv_w, alog128, dtb128, bd_ones, e_beta, e_ba, cm):
    nt = seq // ROW_TILE
    w3 = 3 * MIX_W
    return pl.pallas_call(
        _gdn_prep_kernel, name="gdn_prep",
        grid=(bsz, nt),
        in_specs=[_row_spec(w3, OFF_BQKV, nt), _halo_spec(w3, OFF_BQKV, nt), _row_spec(LANES, OFF_SMALL, nt),
                  _const_spec((CONV_K, w3)), _const_spec((1, LANES)), _const_spec((1, LANES)),
                  _const_spec((MIX_W, MIX_W)), _const_spec((LANES, MIX_W)), _const_spec((LANES, MIX_W)),
                  _const_spec((2 * ROW_TILE, ROW_TILE))],
        out_specs=[_hm_spec(ROW_TILE)] * 7 + [pl.BlockSpec((ROW_TILE, N_HEADS), lambda b, i: (b * nt + i, 0))],
        out_shape=[_hm_shape(bsz, seq, bf16)] * 7 + [jax.ShapeDtypeStruct((bsz * seq, N_HEADS), f32)],
        compiler_params=_cparams(("parallel", "parallel"), 40),
    )(proj, proj, proj, conv_w, alog128, dtb128, bd_ones, e_beta, e_ba, cm)


def _gdn_chunk_kernel(q_ref, k_ref, kb_ref, vb_ref, qd_ref, kbe_ref, kd_ref, gc_ref, gct_ref, o_ref, st_scr):
    @pl.when(pl.program_id(1) == 0)
    def _():
        st_scr[...] = jnp.zeros_like(st_scr)

    n = GDN_CHUNK
    hs = range(N_HEADS)
    r, c = _iota2((n, n))
    gc_all, gct_all = gc_ref[0, 0], gct_ref[0, 0]
    decay = [jnp.exp(jnp.where(r >= c, gc_all[:, h:h + 1] - gct_all[h:h + 1, :], NEG)) for h in hs]
    gram = [_mm_nt(jnp.concatenate([kb_ref[0, h], q_ref[0, h]], axis=0), k_ref[0, h]) for h in hs]
    t = _unit_lower_inverses([jnp.where(r > c, gram[h][:n] * decay[h], 0.0) for h in hs])
    u = [_mm(t[h], vb_ref[0, h]) for h in hs]
    w = [_mm(t[h], kbe_ref[0, h]) for h in hs]
    st = [st_scr[h] for h in hs]
    ws = [_mm(jnp.concatenate([w[h].astype(bf16), qd_ref[0, h]], axis=0), st[h]) for h in hs]
    v_new = [u[h] - ws[h][:n] for h in hs]
    o = [ws[h][n:] + _mm(gram[h][n:] * decay[h], v_new[h]) for h in hs]
    upd = [_mm_tn(kd_ref[0, h], v_new[h]) for h in hs]
    for h in hs:
        o_ref[0, h] = o[h]
        st_scr[h] = st[h] * jnp.exp(gc_all[n - 1:n, h:h + 1]) + upd[h]


def _gdn_chunk(q, k, kb, vb, qd, kbe, kd, gc):
    bsz, _, seq, _ = q.shape
    n = GDN_CHUNK
    nc = seq // n
    gc4 = gc.reshape(bsz, nc, n, N_HEADS)
    gct4 = jnp.swapaxes(gc4, 2, 3)
    return pl.pallas_call(
        _gdn_chunk_kernel, name="gdn_chunk",
        grid=(bsz, nc),
        in_specs=[_hm_spec(n)] * 7 + [pl.BlockSpec((1, 1, n, N_HEADS), lambda b, i: (b, i, 0, 0)),
                                      pl.BlockSpec((1, 1, N_HEADS, n), lambda b, i: (b, i, 0, 0))],
        out_specs=_hm_spec(n),
        out_shape=_hm_shape(bsz, seq),
        scratch_shapes=[pltpu.VMEM((N_HEADS, HEAD_DIM, HEAD_DIM), f32)],
        compiler_params=_cparams(("parallel", "arbitrary"), 32),
    )(q, k, kb, vb, qd, kbe, kd, gc4, gct4)


def _rwkv_prep_kernel(has_vres, *refs):
    if has_vres:
        (c_ref, halo_ref, mu_ref, w0_ref, wup_ref, a0_ref, aup_ref, gup_ref, kk_ref, ka_ref, bd_ref,
         rk_ref, cm_ref, vf_ref, v0_ref, vdn_ref, vup_ref,
         rt_ref, at_ref, bt_ref, kt_ref, bp_ref, kp_ref, v_ref, pe_ref, gout_ref, bonus_ref) = refs
    else:
        (c_ref, halo_ref, mu_ref, w0_ref, wup_ref, a0_ref, aup_ref, gup_ref, kk_ref, ka_ref, bd_ref,
         rk_ref, cm_ref,
         rt_ref, at_ref, bt_ref, kt_ref, bp_ref, kp_ref, v_ref, pe_ref, gout_ref, bonus_ref, cv_ref) = refs
    i = pl.program_id(1)
    c = c_ref[...]
    last = jnp.where(i == 0, 0.0, halo_ref[...])[SUBLANES - 1:SUBLANES]
    prev = jnp.concatenate([last, c[:-1]], axis=0)
    c = c + (prev - c) * mu_ref[...]
    c_r, c_k, c_v = c[:, :MIX_W], c[:, MIX_W:2 * MIX_W], c[:, 2 * MIX_W:3 * MIX_W]
    c_wd = c[:, 3 * MIX_W:3 * MIX_W + 64]
    c_ad = c[:, 3 * MIX_W + 64:3 * MIX_W + 128]
    c_gd = c[:, 3 * MIX_W + 128:]
    w_log = -_softplus(-(w0_ref[...] + _mm(jnp.tanh(c_wd), wup_ref[...]))) - 0.5
    a_in = jax.nn.sigmoid(a0_ref[...] + _mm(c_ad, aup_ref[...]))
    gout_ref[...] = _mm(jax.nn.sigmoid(c_gd), gup_ref[...])
    if has_vres:
        lam = jax.nn.sigmoid(v0_ref[...] + _mm(_mm(c_v, vdn_ref[...]), vup_ref[...]))
        v_r = c_v + (vf_ref[...] - c_v) * lam
    else:
        v_r = c_v
        cv_ref[...] = c_v
    bd = bd_ref[...]
    kk = c_k * kk_ref[...]
    kk = kk * lax.rsqrt(_sel_right(kk * kk, bd) + L2_EPS)
    k_r = c_k * (1.0 + (a_in - 1.0) * ka_ref[...])
    b = kk * a_in
    bonus_ref[...] = _sel_right(c_r * k_r * rk_ref[...], bd) * v_r
    sums = _sel_left(cm_ref[...], -jnp.exp(w_log))
    lc, lc_end = sums[:ROW_TILE], sums[ROW_TILE:]
    e_neg = jnp.exp(-lc)
    e_rest = jnp.exp(lc_end - lc)
    _write_heads(rt_ref, c_r * jnp.exp(lc))
    _write_heads(at_ref, -kk * jnp.exp(lc + jnp.exp(w_log)))
    _write_heads(bt_ref, b * e_neg)
    _write_heads(kt_ref, k_r * e_neg)
    _write_heads(bp_ref, b * e_rest)
    _write_heads(kp_ref, k_r * e_rest)
    _write_heads(v_ref, v_r)
    _write_heads(pe_ref, jnp.exp(lc_end))


def _rwkv_prep(proj, bsz, seq, mu, w0, w_up, a0, a_up, g_up, k_k, k_a, r_k, bd_ones, cm, vres):
    nt = seq // ROW_TILE
    wc = 3 * MIX_W + 256
    std = pl.BlockSpec((ROW_TILE, MIX_W), lambda b, i: (b * nt + i, 0))
    std_shape = jax.ShapeDtypeStruct((bsz * seq, MIX_W), f32)
    in_specs = [_row_spec(wc, OFF_C, nt), _halo_spec(wc, OFF_C, nt), _const_spec((1, wc)),
                _const_spec((1, MIX_W)), _const_spec((64, MIX_W)), _const_spec((1, MIX_W)), _const_spec((64, MIX_W)),
                _const_spec((128, MIX_W)), _const_spec((1, MIX_W)), _const_spec((1, MIX_W)), _const_spec((MIX_W, MIX_W)),
                _const_spec((1, MIX_W)), _const_spec((2 * ROW_TILE, ROW_TILE))]
    args = [proj, proj, mu, w0, w_up, a0, a_up, g_up, k_k, k_a, bd_ones, r_k, cm]
    out_specs = [_hm_spec(ROW_TILE)] * 8 + [std, std]
    out_shape = [_hm_shape(bsz, seq, bf16)] * 7 + [_hm_shape(bsz, seq), std_shape, std_shape]
    if vres is not None:
        v_first, v0, v_down, v_up = vres
        in_specs += [std, _const_spec((1, MIX_W)), _const_spec(v_down.shape), _const_spec(v_up.shape)]
        args += [v_first, v0, v_down, v_up]
    else:
        out_specs.append(std)
        out_shape.append(std_shape)
    return pl.pallas_call(
        functools.partial(_rwkv_prep_kernel, vres is not None), name="rwkv_prep",
        grid=(bsz, nt), in_specs=in_specs, out_specs=out_specs, out_shape=out_shape,
        compiler_params=_cparams(("parallel", "parallel"), 40),
    )(*args)


def _rwkv_chunk_kernel(rt_ref, at_ref, bt_ref, kt_ref, bp_ref, kp_ref, v_ref, pe_ref, o_ref, st_scr):
    @pl.when(pl.program_id(1) == 0)
    def _():
        st_scr[...] = jnp.zeros_like(st_scr)

    n = RWKV_CHUNK
    hs = range(N_HEADS)
    row, col = _iota2((2 * n, 2 * n))
    rr, cc = row & (n - 1), col & (n - 1)
    mask = rr + jnp.where(row < n, 0, 1) > cc
    lhs = [jnp.concatenate([at_ref[0, h], rt_ref[0, h]], axis=0) for h in hs]
    gm = [jnp.where(mask, _mm_nt(lhs[h], jnp.concatenate([bt_ref[0, h], kt_ref[0, h]], axis=0)), 0.0) for h in hs]
    t = _unit_lower_inverses([-gm[h][:n, :n] for h in hs])
    st = [st_scr[h] for h in hs]
    ah = [_mm_nt(lhs[h], st[h]) for h in hs]
    v = [v_ref[0, h] for h in hs]
    makv = [_mm(gm[h][:n], jnp.concatenate([jnp.zeros_like(v[h]), v[h]], axis=0)) for h in hs]
    u = [_mm(t[h], ah[h][:n] + makv[h]) for h in hs]
    uv = [jnp.concatenate([u[h].astype(bf16), v[h]], axis=0) for h in hs]
    o = [ah[h][n:] + _mm(gm[h][n:], uv[h]) for h in hs]
    upd = [_mm_tn(uv[h], jnp.concatenate([bp_ref[0, h], kp_ref[0, h]], axis=0)) for h in hs]
    for h in hs:
        o_ref[0, h] = o[h]
        st_scr[h] = st[h] * pe_ref[0, h][0:1, :] + upd[h]


def _rwkv_chunk(rt, at, bt, kt, bp, kp, v, pe):
    bsz, _, seq, _ = rt.shape
    n = RWKV_CHUNK
    return pl.pallas_call(
        _rwkv_chunk_kernel, name="rwkv_chunk",
        grid=(bsz, seq // n),
        in_specs=[_hm_spec(n)] * 8,
        out_specs=_hm_spec(n),
        out_shape=_hm_shape(bsz, seq),
        scratch_shapes=[pltpu.VMEM((N_HEADS, HEAD_DIM, HEAD_DIM), f32)],
        compiler_params=_cparams(("parallel", "arbitrary"), 32),
    )(rt, at, bt, kt, bp, kp, v, pe)


def _ssd_prep_kernel(x_ref, halo_ref, sm_ref, cw_ref, cb_ref, alog_ref, dtb_ref, edt_ref,
                     xdt_ref, x_out_ref, bc_ref, adt_ref):
    i = pl.program_id(1)
    halo = jnp.where(i == 0, 0.0, halo_ref[...])
    xbc = _silu(_causal_conv(x_ref[...], halo, cw_ref) + cb_ref[...])
    m_x = xbc[:, :MIX_W]
    dt = _softplus(sm_ref[...] + dtb_ref[...])
    _write_heads(xdt_ref, m_x * _sel_right(dt, edt_ref[...]))
    _write_heads(x_out_ref, m_x)
    bc_ref[...] = xbc[:, MIX_W:]
    adt_ref[...] = (dt * -jnp.exp(alog_ref[...]))[:, SM_DT:SM_DT + N_HEADS]


def _ssd_prep(proj, bsz, seq, conv_w, conv_b, alog128, dtb128, e_dt):
    nt = seq // ROW_TILE
    wx = MIX_W + 4 * SSM_STATE
    return pl.pallas_call(
        _ssd_prep_kernel, name="ssd_prep",
        grid=(bsz, nt),
        in_specs=[_row_spec(wx, OFF_DXBC, nt), _halo_spec(wx, OFF_DXBC, nt), _row_spec(LANES, OFF_SMALL, nt),
                  _const_spec((CONV_K, wx)), _const_spec((1, wx)), _const_spec((1, LANES)), _const_spec((1, LANES)),
                  _const_spec((LANES, MIX_W))],
        out_specs=[_hm_spec(ROW_TILE)] * 2 + [pl.BlockSpec((ROW_TILE, 4 * SSM_STATE), lambda b, i: (b * nt + i, 0)),
                                              pl.BlockSpec((ROW_TILE, N_HEADS), lambda b, i: (b * nt + i, 0))],
        out_shape=[_hm_shape(bsz, seq)] * 2 + [jax.ShapeDtypeStruct((bsz * seq, 4 * SSM_STATE), f32),
                                               jax.ShapeDtypeStruct((bsz * seq, N_HEADS), f32)],
        compiler_params=_cparams(("parallel", "parallel"), 40),
    )(proj, proj, proj, conv_w, conv_b, alog128, dtb128, e_dt)


def _ssd_chunk_kernel(xdt_ref, x_ref, bc_ref, a_ref, at_ref, dvec_ref, o_ref, st_scr):
    @pl.when(pl.program_id(1) == 0)
    def _():
        st_scr[...] = jnp.zeros_like(st_scr)

    n = SSD_CHUNK
    r, c = _iota2((n, n))
    tril = jnp.where(r >= c, 1.0, 0.0)
    acs_all = _mm_hi(tril, a_ref[0, 0])
    acst_all = _mm_nt_hi(at_ref[0, 0], tril)
    bc = bc_ref[...]
    heads_per_group = N_HEADS // 2
    for g in range(2):
        b_g = bc[:, g * SSM_STATE:(g + 1) * SSM_STATE]
        c_g = bc[:, (2 + g) * SSM_STATE:(3 + g) * SSM_STATE]
        cb = _mm_nt(c_g, b_g)
        for j in range(heads_per_group):
            h = g * heads_per_group + j
            ac = acs_all[:, h:h + 1]
            lmat = jnp.exp(jnp.where(r >= c, ac - acst_all[h:h + 1, :], NEG))
            xg = xdt_ref[0, h]
            st = st_scr[h]
            a_last = acs_all[n - 1:n, h:h + 1]
            y = _mm(cb * lmat, xg) + _mm(c_g * jnp.exp(ac), st)
            o_ref[0, h] = y + x_ref[0, h] * dvec_ref[h:h + 1, :]
            st_scr[h] = st * jnp.exp(a_last) + _mm_tn(b_g * jnp.exp(a_last - ac), xg)


def _ssd_chunk(xdt, x, bc, adt, dvec):
    bsz, _, seq, _ = xdt.shape
    n = SSD_CHUNK
    nc = seq // n
    a4 = adt.reshape(bsz, nc, n, N_HEADS)
    at4 = jnp.swapaxes(a4, 2, 3)
    return pl.pallas_call(
        _ssd_chunk_kernel, name="ssd_chunk",
        grid=(bsz, nc),
        in_specs=[_hm_spec(n)] * 2 + [pl.BlockSpec((n, 4 * SSM_STATE), lambda b, i: (b * nc + i, 0)),
                                      pl.BlockSpec((1, 1, n, N_HEADS), lambda b, i: (b, i, 0, 0)),
                                      pl.BlockSpec((1, 1, N_HEADS, n), lambda b, i: (b, i, 0, 0)),
                                      _const_spec((N_HEADS, HEAD_DIM))],
        out_specs=_hm_spec(n),
        out_shape=_hm_shape(bsz, seq),
        scratch_shapes=[pltpu.VMEM((N_HEADS, SSM_STATE, HEAD_DIM), f32)],
        compiler_params=_cparams(("parallel", "arbitrary"), 32),
    )(xdt, x, bc, a4, at4, dvec)


def _merge_kernel(x_ref, ya_ref, ob_ref, bz_ref, gnw_ref,
                  wkv_ref, bonus_ref, gout_ref, lnw_ref, lnb_ref,
                  yd_ref, dz_ref, mnw_ref, g0_ref, g1_ref, g2_ref, g3_ref, wb_ref, wo_ref, o_ref):
    def per_head(fn):
        return jnp.concatenate([fn(h) for h in range(N_HEADS)], axis=-1)

    def gdn_head(h):
        o = ob_ref[0, h]
        return o * lax.rsqrt(jnp.mean(o * o, axis=-1, keepdims=True) + RMS_EPS) * gnw_ref[...]

    y_b = per_head(gdn_head) * _silu(bz_ref[...])

    def wkv_head(h):
        w = wkv_ref[0, h]
        mu = jnp.mean(w, axis=-1, keepdims=True)
        var = jnp.mean(jnp.square(w - mu), axis=-1, keepdims=True)
        return (w - mu) * lax.rsqrt(var + RWKV_LN_EPS)

    y_c = (per_head(wkv_head) * lnw_ref[...] + lnb_ref[...] + bonus_ref[...]) * gout_ref[...]

    yz = _read_heads(yd_ref) * _silu(dz_ref[...])
    half = MIX_W // 2
    y_d = jnp.concatenate(
        [yz[:, s:s + half] * lax.rsqrt(jnp.mean(jnp.square(yz[:, s:s + half]), axis=-1, keepdims=True) + RMS_EPS)
         for s in (0, half)], axis=-1) * mnw_ref[...]

    acc = jnp.zeros(x_ref.shape, f32)
    for n, (y, g_ref) in enumerate(((ya_ref[0], g0_ref), (y_b, g1_ref), (y_c, g2_ref), (y_d, g3_ref))):
        acc = acc + jax.nn.sigmoid(g_ref[...]) * _mm(y, wb_ref[n])
    o_ref[...] = x_ref[...] + _mm(acc, wo_ref[...])


def _merge(x2, proj, bsz, seq, ya, ob, gnw, wkv, bonus, gout, lnw, lnb, yd, mnw, wb, wo):
    nt = seq // ROW_TILE
    std = lambda w: pl.BlockSpec((ROW_TILE, w), lambda b, i: (b * nt + i, 0))
    hm = _hm_spec(ROW_TILE)
    gate = lambda n: _row_spec(D_MODEL, OFF_GATES + n * D_MODEL, nt)
    return pl.pallas_call(
        _merge_kernel, name="merge",
        grid=(bsz, nt),
        in_specs=[std(D_MODEL), pl.BlockSpec((1, ROW_TILE, MIX_W), lambda b, i: (b, i, 0)),
                  hm, _row_spec(MIX_W, OFF_BZ, nt), _const_spec((1, HEAD_DIM)),
                  hm, std(MIX_W), std(MIX_W), _const_spec((1, MIX_W)), _const_spec((1, MIX_W)),
                  hm, _row_spec(MIX_W, OFF_DZ, nt), _const_spec((1, MIX_W)),
                  gate(0), gate(1), gate(2), gate(3),
                  _const_spec((4, MIX_W, D_MODEL)), _const_spec((D_MODEL, D_MODEL))],
        out_specs=std(D_MODEL),
        out_shape=jax.ShapeDtypeStruct((bsz * seq, D_MODEL), f32),
        compiler_params=_cparams(("parallel", "parallel"), 48),
    )(x2, ya, ob, proj, gnw, wkv, bonus, gout, lnw, lnb, yd, proj, mnw, proj, proj, proj, proj, wb, wo)


def _lane_vec(vals, off):
    return jnp.zeros((1, LANES), f32).at[0, off:off + vals.shape[0]].set(vals)


def _head_expand(off):
    n = jnp.arange(LANES)[:, None]
    c = jnp.arange(MIX_W)[None, :]
    return (n - off == c // HEAD_DIM).astype(bf16)


def _pack_w_in(w):
    pad = lambda n: jnp.zeros((w.shape[0], n), w.dtype)
    cols = [w[:, W_A:W_BZ],
            w[:, W_BZ:W_BBETA],
            w[:, W_C:W_DZ],
            w[:, W_BBETA:W_C], w[:, W_DDT:W_GATES], pad(2 * LANES - 3 * N_HEADS),
            w[:, W_DZ:W_DXBC], w[:, W_DXBC:W_DDT], w[:, W_GATES:]]
    out = jnp.concatenate(cols, axis=1).astype(bf16)
    assert out.shape[1] == N_PROJ
    return out


def kernel(x, rel_bias, norm1_w, w_in, moba_q_norm, moba_k_norm, gdn_conv_w, gdn_A_log, gdn_dt_bias, gdn_norm_w, rwkv_mu, rwkv_w0, rwkv_w_up, rwkv_a0, rwkv_a_up, rwkv_g_up, rwkv_k_k, rwkv_k_a, rwkv_r_k, rwkv_v0, rwkv_v_down, rwkv_v_up, rwkv_ln_w, rwkv_ln_b, mamba_conv_w, mamba_conv_b, mamba_dt_bias, mamba_A_log, mamba_D, mamba_norm_w, w_branch, w_out, norm2_w, ffn_w_in, ffn_w_down):
    bsz, seq, d = x.shape
    depth = w_in.shape[0]
    assert d == D_MODEL and seq % MM_TILE_M == 0
    x2 = x.reshape(bsz * seq, d)
    row = lambda v: v.reshape(1, -1).astype(f32)

    hid = jnp.arange(MIX_W) // HEAD_DIM
    bd_ones = (hid[:, None] == hid[None, :]).astype(bf16)
    bd_mean = (bd_ones.astype(f32) / HEAD_DIM).astype(bf16)
    e_beta, e_ba, e_dt = _head_expand(SM_BETA), _head_expand(SM_BA), _head_expand(SM_DT)
    cm64 = _chunk_sum_matrix(GDN_CHUNK)
    assert GDN_CHUNK == RWKV_CHUNK
    tab = _moba_bias_tables(rel_bias)
    v_first = None
    for i in range(depth):
        proj = _norm_matmul(x2, row(norm1_w[i]), _pack_w_in(w_in[i]), 1024)

        qaug, kaug, v_a = _moba_prep(proj, bsz, seq, row(jnp.tile(moba_q_norm[i], N_HEADS)),
                                     row(jnp.tile(moba_k_norm[i], N_HEADS)), bd_mean)
        y_a = _moba_attn(qaug, kaug, v_a, tab)

        gdn_in = _gdn_prep(proj, bsz, seq, gdn_conv_w[i], _lane_vec(gdn_A_log[i], SM_BA),
                           _lane_vec(gdn_dt_bias[i], SM_BA), bd_ones, e_beta, e_ba, cm64)
        o_b = _gdn_chunk(*gdn_in)

        vres = None if i == 0 else (v_first, row(rwkv_v0[i - 1]), rwkv_v_down[i - 1].astype(bf16),
                                    rwkv_v_up[i - 1].astype(bf16))
        outs = _rwkv_prep(proj, bsz, seq, row(rwkv_mu[i]), row(rwkv_w0[i]), rwkv_w_up[i].astype(bf16),
                          row(rwkv_a0[i]), rwkv_a_up[i].astype(bf16), rwkv_g_up[i].astype(bf16),
                          row(rwkv_k_k[i]), row(rwkv_k_a[i]), row(rwkv_r_k[i]), bd_ones, cm64, vres)
        g_out, bonus = outs[8], outs[9]
        if i == 0:
            v_first = outs[10]
        wkv = _rwkv_chunk(*outs[:8])

        xdt, x_d, bc, adt = _ssd_prep(proj, bsz, seq, mamba_conv_w[i], row(mamba_conv_b[i]),
                                      _lane_vec(mamba_A_log[i], SM_DT), _lane_vec(mamba_dt_bias[i], SM_DT), e_dt)
        y_d = _ssd_chunk(xdt, x_d, bc, adt, jnp.broadcast_to(mamba_D[i][:, None], (N_HEADS, HEAD_DIM)).astype(f32))

        x2 = _merge(x2, proj, bsz, seq, y_a, o_b, row(gdn_norm_w[i]), wkv, bonus, g_out,
                    row(rwkv_ln_w[i]), row(rwkv_ln_b[i]), y_d, row(mamba_norm_w[i]),
                    w_branch[i].astype(bf16), w_out[i].astype(bf16))

        act = _ffn_up(x2, row(norm2_w[i]), ffn_w_in[i][:, :D_FF].astype(bf16), ffn_w_in[i][:, D_FF:].astype(bf16))
        x2 = _ffn_down(x2, act, ffn_w_down[i].astype(bf16))
    return x2.reshape(bsz, seq, d)
```

```python
import functools
import math

import jax
import jax.numpy as jnp
from jax import lax
from jax.experimental import pallas as pl
from jax.experimental.pallas import tpu as pltpu

f32, bf16 = jnp.float32, jnp.bfloat16
HI = lax.Precision.HIGHEST

D_MODEL = 1024
N_HEADS = 8
HEAD_DIM = 64
MIX_W = N_HEADS * HEAD_DIM
RMS_EPS = 1e-6
L2_EPS = 1e-6
CONV_K = 4
MOBA_BLOCK = 256
MOBA_TOPK = 3
MOBA_KV_PER_TRIP = 4
REL_BUCKETS = 32
REL_MAX_DIST = 128
GDN_CHUNK = 64
RWKV_CHUNK = 64
RWKV_LN_EPS = 64e-5
SSM_STATE = 128
SSD_CHUNK = 128
D_FF = 2816
NEG = -1e30
LOG2E = math.log2(math.e)

LANES = 128
BF16_SUBLANES = 16
VT_ROWS = HEAD_DIM + BF16_SUBLANES

OFF_A, OFF_BQKV, OFF_BZ, OFF_C, OFF_SMALL, OFF_DZ, OFF_DXBC, OFF_GATES = 0, 1536, 3072, 3584, 5376, 5632, 6144, 7168
N_PROJ = OFF_GATES + 4 * D_MODEL
SM_BETA, SM_BA, SM_DT = 0, 8, 16
W_A, W_BQKV, W_BZ, W_BBETA, W_BA, W_C, W_DZ, W_DXBC, W_DDT, W_GATES = 0, 1536, 3072, 3584, 3592, 3600, 5392, 5904, 6928, 6936

ROW_TILE = 256
MM_TILE_M = 512
IN_PROJ_TILE_M, IN_PROJ_TILE_N = 1024, 1024
HALO_ROWS = BF16_SUBLANES


def _cparams(sem, vmem_mb):
    return pltpu.CompilerParams(dimension_semantics=sem, vmem_limit_bytes=vmem_mb * 1024 * 1024)


def _mm(a, b):
    return jnp.dot(a.astype(bf16), b.astype(bf16), preferred_element_type=f32)


def _mm_nt(a, b):
    return lax.dot_general(a.astype(bf16), b.astype(bf16), (((1,), (1,)), ((), ())), preferred_element_type=f32)


def _mm_tn(a, b):
    return lax.dot_general(a.astype(bf16), b.astype(bf16), (((0,), (0,)), ((), ())), preferred_element_type=f32)


def _mm_hi(a, b):
    return jnp.dot(a, b, precision=HI, preferred_element_type=f32)


def _mm_nt_hi(a, b):
    return lax.dot_general(a, b, (((1,), (1,)), ((), ())), precision=HI, preferred_element_type=f32)


def _softplus(x):
    return jnp.maximum(x, 0.0) + jnp.log1p(jnp.exp(-jnp.abs(x)))


def _silu(x):
    return x * jax.nn.sigmoid(x)


def _iota2(shape):
    return lax.broadcasted_iota(jnp.int32, shape, 0), lax.broadcasted_iota(jnp.int32, shape, 1)


def _split3(x):
    hi = x.astype(bf16)
    r1 = x - hi.astype(f32)
    mid = r1.astype(bf16)
    return hi, mid, (r1 - mid.astype(f32)).astype(bf16)


def _sel_left(m01, x):
    return sum(jnp.dot(m01, p, preferred_element_type=f32) for p in _split3(x))


def _sel_right(x, m01):
    return sum(jnp.dot(p, m01, preferred_element_type=f32) for p in _split3(x))


def _unit_lower_inverses(l_list):
    n = l_list[0].shape[0]
    r, c = _iota2((n, n))
    eye = jnp.where(r == c, 1.0, 0.0)
    ts = [eye for _ in l_list]
    for ls in range(n.bit_length() - 1):
        m = ((r >> (ls + 1)) == (c >> (ls + 1))) & (((r >> ls) & 1) == 1) & (((c >> ls) & 1) == 0)
        lms = [jnp.where(m, l, 0.0) for l in l_list]
        if ls == 0:
            ts = [t - lm for t, lm in zip(ts, lms)]
        else:
            tl = [_mm(t, lm) for t, lm in zip(ts, lms)]
            ts = [t - _mm(x, t) for t, x in zip(ts, tl)]
    return ts


def _in_proj_kernel(x_ref, nw_ref, w_ref, o_ref, sm_ref, h_scr):
    j = pl.program_id(1)

    @pl.when(j == 0)
    def _():
        x = x_ref[...]
        y = x * lax.rsqrt(jnp.mean(x * x, axis=-1, keepdims=True) + RMS_EPS)
        h_scr[...] = (y * nw_ref[...]).astype(bf16)

    acc = jnp.dot(h_scr[...], w_ref[...], preferred_element_type=f32)
    o_ref[...] = acc.astype(o_ref.dtype)
    tn = o_ref.shape[1]

    @pl.when(j == OFF_SMALL // tn)
    def _():
        sm_ref[...] = acc[:, OFF_SMALL % tn:OFF_SMALL % tn + LANES]


def _in_proj(x2, nw, w):
    t, d = x2.shape
    n = w.shape[1]
    tm, tn = IN_PROJ_TILE_M, IN_PROJ_TILE_N
    return pl.pallas_call(
        _in_proj_kernel, name="in_proj",
        grid=(t // tm, n // tn),
        in_specs=[pl.BlockSpec((tm, d), lambda i, j: (i, 0)),
                  pl.BlockSpec((1, d), lambda i, j: (0, 0)),
                  pl.BlockSpec((d, tn), lambda i, j: (0, j))],
        out_specs=[pl.BlockSpec((tm, tn), lambda i, j: (i, j)),
                   pl.BlockSpec((tm, LANES), lambda i, j: (i, 0))],
        out_shape=[jax.ShapeDtypeStruct((t, n), bf16), jax.ShapeDtypeStruct((t, LANES), f32)],
        scratch_shapes=[pltpu.VMEM((tm, d), bf16)],
        compiler_params=_cparams(("parallel", "arbitrary"), 48),
    )(x2, nw, w)


def _ffn_up_kernel(x_ref, nw_ref, wg_ref, wu_ref, o_ref, h_scr):
    @pl.when(pl.program_id(1) == 0)
    def _():
        x = x_ref[...]
        y = x * lax.rsqrt(jnp.mean(x * x, axis=-1, keepdims=True) + RMS_EPS)
        h_scr[...] = (y * nw_ref[...]).astype(bf16)
    h = h_scr[...]
    g = jnp.dot(h, wg_ref[...], preferred_element_type=f32)
    u = jnp.dot(h, wu_ref[...], preferred_element_type=f32)
    o_ref[...] = (_silu(g) * u).astype(o_ref.dtype)


def _ffn_up(x2, nw, wg, wu):
    t, d = x2.shape
    n = wg.shape[1]
    tm, tn = MM_TILE_M, n // 2
    return pl.pallas_call(
        _ffn_up_kernel, name="ffn_up",
        grid=(t // tm, n // tn),
        in_specs=[pl.BlockSpec((tm, d), lambda i, j: (i, 0)),
                  pl.BlockSpec((1, d), lambda i, j: (0, 0)),
                  pl.BlockSpec((d, tn), lambda i, j: (0, j)),
                  pl.BlockSpec((d, tn), lambda i, j: (0, j))],
        out_specs=pl.BlockSpec((tm, tn), lambda i, j: (i, j)),
        out_shape=jax.ShapeDtypeStruct((t, n), bf16),
        scratch_shapes=[pltpu.VMEM((tm, d), bf16)],
        compiler_params=_cparams(("parallel", "arbitrary"), 48),
    )(x2, nw, wg, wu)


def _ffn_down_kernel(x_ref, a_ref, w_ref, o_ref):
    o_ref[...] = x_ref[...] + jnp.dot(a_ref[...], w_ref[...], preferred_element_type=f32)


def _ffn_down(x2, act, wd):
    t, d = x2.shape
    n = act.shape[1]
    tm = MM_TILE_M
    return pl.pallas_call(
        _ffn_down_kernel, name="ffn_down",
        grid=(t // tm,),
        in_specs=[pl.BlockSpec((tm, d), lambda i: (i, 0)),
                  pl.BlockSpec((tm, n), lambda i: (i, 0)),
                  pl.BlockSpec((n, d), lambda i: (0, 0))],
        out_specs=pl.BlockSpec((tm, d), lambda i: (i, 0)),
        out_shape=jax.ShapeDtypeStruct((t, d), f32),
        compiler_params=_cparams(("parallel",), 40),
    )(x2, act, wd)


def _top3_bias(gate, n_past):
    col = lax.broadcasted_iota(jnp.int32, gate.shape, 1)
    g = jnp.where(col < n_past, gate, -jnp.inf)
    sel = jnp.zeros(gate.shape, jnp.bool_)
    for _ in range(MOBA_TOPK):
        m = jnp.max(g, axis=-1, keepdims=True)
        idx = jnp.min(jnp.where(g == m, col, gate.shape[1]), axis=-1, keepdims=True)
        pick = col == idx
        sel = sel | (pick & (m > -jnp.inf))
        g = jnp.where(pick, -jnp.inf, g)
    return jnp.where(sel, 0.0, NEG)


def _moba_prep_kernel(a_ref, qw_ref, kw_ref, bd_ref, qaugt_ref, kaug_ref, vaugt_ref, kmean_scr):
    i = pl.program_id(1)

    @pl.when(i == 0)
    def _():
        kmean_scr[...] = jnp.zeros_like(kmean_scr)

    a = a_ref[...].astype(f32)
    q, k, v = a[:, :MIX_W], a[:, MIX_W:2 * MIX_W], a[:, 2 * MIX_W:]
    bd = bd_ref[...]
    qn = q * lax.rsqrt(_sel_right(q * q, bd) + RMS_EPS) * qw_ref[...]
    kn = k * lax.rsqrt(_sel_right(k * k, bd) + RMS_EPS) * kw_ref[...]
    lane = lax.broadcasted_iota(jnp.int32, (MOBA_BLOCK, LANES), 1)
    onehot = jnp.where(lane == i, 1.0, 0.0).astype(bf16)
    ones_row = jnp.where(lax.broadcasted_iota(jnp.int32, (VT_ROWS - HEAD_DIM, MOBA_BLOCK), 0) == 0, 1.0, 0.0)
    kmean = kmean_scr[...]
    for p in range(N_HEADS // 2):
        sl = slice(p * LANES, (p + 1) * LANES)
        kaug_ref[0, p] = jnp.concatenate([kn[:, sl].astype(bf16), onehot], axis=-1)
        vt = v[:, sl].T
        for hh in range(2):
            vaugt_ref[0, 2 * p + hh] = jnp.concatenate(
                [vt[hh * HEAD_DIM:(hh + 1) * HEAD_DIM], ones_row], axis=0).astype(bf16)
            keep = (lane < HEAD_DIM) if hh == 0 else (lane >= HEAD_DIM)
            q2m = jnp.where(keep, qn[:, sl], 0.0)
            gate = _mm_nt_hi(q2m, kmean[:, sl])
            selb = _top3_bias(gate, i)
            qaugt_ref[0, 2 * p + hh] = jnp.concatenate(
                [(q2m * (HEAD_DIM ** -0.5 * LOG2E)).T, selb.T], axis=0).astype(bf16)
    kmean_scr[pl.ds(i, 1), :] = jnp.mean(kn, axis=0, keepdims=True)


def _moba_prep(proj, bsz, seq, qw, kw, bd_mean):
    nb = seq // MOBA_BLOCK
    assert nb <= LANES
    return pl.pallas_call(
        _moba_prep_kernel, name="moba_prep",
        grid=(bsz, nb),
        in_specs=[pl.BlockSpec((MOBA_BLOCK, 3 * MIX_W), lambda b, i: (b * nb + i, OFF_A // (3 * MIX_W))),
                  pl.BlockSpec((1, MIX_W), lambda b, i: (0, 0)),
                  pl.BlockSpec((1, MIX_W), lambda b, i: (0, 0)),
                  pl.BlockSpec((MIX_W, MIX_W), lambda b, i: (0, 0))],
        out_specs=[pl.BlockSpec((1, N_HEADS, 2 * LANES, MOBA_BLOCK), lambda b, i: (b, 0, 0, i)),
                   pl.BlockSpec((1, N_HEADS // 2, MOBA_BLOCK, 2 * LANES), lambda b, i: (b, 0, i, 0)),
                   pl.BlockSpec((1, N_HEADS, VT_ROWS, MOBA_BLOCK), lambda b, i: (b, 0, 0, i))],
        out_shape=[jax.ShapeDtypeStruct((bsz, N_HEADS, 2 * LANES, seq), bf16),
                   jax.ShapeDtypeStruct((bsz, N_HEADS // 2, seq, 2 * LANES), bf16),
                   jax.ShapeDtypeStruct((bsz, N_HEADS, VT_ROWS, seq), bf16)],
        scratch_shapes=[pltpu.VMEM((LANES, MIX_W), f32)],
        compiler_params=_cparams(("parallel", "arbitrary"), 32),
    )(proj, qw, kw, bd_mean)


def _moba_attn_kernel(qaugt_ref, kaug_ref, vaugt_ref, tabt_ref, o_ref):
    i = pl.program_id(2)
    blk = MOBA_BLOCK
    hs = (0, 1)
    key, qry = _iota2((blk, blk))
    mm = lambda a, b: jnp.dot(a, b, preferred_element_type=f32)
    cmax = lambda s: jnp.max(s, axis=0, keepdims=True)
    pv = lambda v, pe: jnp.dot(v, pe.astype(bf16), preferred_element_type=f32)
    kblk = lambda j: kaug_ref[0, 0, pl.ds(pl.multiple_of(j * blk, blk), blk), :]
    vblk = lambda hh, j: vaugt_ref[0, hh, :, pl.ds(pl.multiple_of(j * blk, blk), blk)]

    n_far = jnp.maximum(i - 1, 0)
    k_own, k_adj = kblk(i), kblk(n_far)
    qt = [qaugt_ref[0, hh] for hh in hs]
    s_own = [jnp.where(qry >= key, mm(k_own[:, :LANES], qt[hh][:LANES]) + tabt_ref[0, hh, 1], NEG) for hh in hs]
    s_adj = [mm(k_adj, qt[hh]) + tabt_ref[0, hh, 0] for hh in hs]
    m = [jnp.maximum(cmax(s_own[hh]), cmax(s_adj[hh])) for hh in hs]
    acc = [pv(vblk(hh, i), jnp.exp2(s_own[hh] - m[hh])) + pv(vblk(hh, n_far), jnp.exp2(s_adj[hh] - m[hh]))
           for hh in hs]

    nk = MOBA_KV_PER_TRIP

    def body(t, carry):
        m, acc = list(carry[:2]), list(carry[2:])
        js = [jnp.where(nk * t + a < n_far, nk * t + a, i) for a in range(nk)]
        ks = [kblk(j) for j in js]
        qt = [qaugt_ref[0, hh] for hh in hs]
        s = [[mm(k, qt[hh]) for k in ks] for hh in hs]
        m_new = [functools.reduce(jnp.maximum, [cmax(x) for x in s[hh]], m[hh]) for hh in hs]
        acc = [functools.reduce(lambda x, y: x + y,
                                [pv(vblk(hh, js[a]), jnp.exp2(s[hh][a] - m_new[hh])) for a in range(nk)],
                                jnp.exp2(m[hh] - m_new[hh]) * acc[hh])
               for hh in hs]
        return (*m_new, *acc)

    carry = lax.fori_loop(0, (n_far + nk - 1) // nk, body, (*m, *acc))
    acc = carry[2:]
    o_ref[0] = jnp.concatenate([(acc[hh][:HEAD_DIM] / acc[hh][HEAD_DIM:HEAD_DIM + 1]).T for hh in hs], axis=-1)


def _moba_attn(qaugt, kaug, vaugt, tabt):
    bsz, _, _, seq = qaugt.shape
    nb = seq // MOBA_BLOCK
    return pl.pallas_call(
        _moba_attn_kernel, name="moba_attn",
        grid=(bsz, N_HEADS // 2, nb),
        in_specs=[pl.BlockSpec((1, 2, 2 * LANES, MOBA_BLOCK), lambda b, p, i: (b, p, 0, i)),
                  pl.BlockSpec((1, 1, seq, 2 * LANES), lambda b, p, i: (b, p, 0, 0)),
                  pl.BlockSpec((1, 2, VT_ROWS, seq), lambda b, p, i: (b, p, 0, 0)),
                  pl.BlockSpec((1, 2, 2, MOBA_BLOCK, MOBA_BLOCK), lambda b, p, i: (p, 0, 0, 0, 0))],
        out_specs=pl.BlockSpec((1, MOBA_BLOCK, LANES), lambda b, p, i: (b, i, p)),
        out_shape=jax.ShapeDtypeStruct((bsz, seq, MIX_W), f32),
        compiler_params=_cparams(("parallel", "parallel", "arbitrary"), 40),
    )(qaugt, kaug, vaugt, tabt)


def _t5_bucket(dist):
    n = jnp.maximum(dist, 0)
    max_exact = REL_BUCKETS // 2
    nf = jnp.maximum(n, max_exact).astype(f32)
    large = max_exact + (jnp.log(nf / max_exact) / math.log(REL_MAX_DIST / max_exact)
                         * (REL_BUCKETS - max_exact)).astype(jnp.int32)
    large = jnp.minimum(large, REL_BUCKETS - 1)
    return jnp.where(n < max_exact, n, large)


def _moba_bias_kernel(vec_ref, o_ref):
    blk = MOBA_BLOCK
    t = pltpu.roll(jnp.broadcast_to(vec_ref[0] * LOG2E, (blk, 2 * blk)), 0, 1, stride=1, stride_axis=0)
    o_ref[0, 0] = t[:, blk:]
    o_ref[0, 1] = t[:, :blk]


def _moba_bias_tables(rel_bias):
    assert MOBA_BLOCK >= REL_MAX_DIST
    by_dist = rel_bias.astype(f32)[_t5_bucket(jnp.arange(2 * MOBA_BLOCK))]
    far = rel_bias.astype(f32)[_t5_bucket(jnp.array(2 * MOBA_BLOCK))]
    vec = (by_dist - far).T.reshape(N_HEADS, 1, 2 * MOBA_BLOCK)
    tab = pl.pallas_call(
        _moba_bias_kernel, name="moba_bias",
        grid=(N_HEADS,),
        in_specs=[pl.BlockSpec((1, 1, 2 * MOBA_BLOCK), lambda h: (h, 0, 0))],
        out_specs=pl.BlockSpec((1, 2, MOBA_BLOCK, MOBA_BLOCK), lambda h: (h, 0, 0, 0)),
        out_shape=jax.ShapeDtypeStruct((N_HEADS, 2, MOBA_BLOCK, MOBA_BLOCK), f32),
        compiler_params=_cparams(("parallel",), 16),
    )(vec)
    return tab.reshape(N_HEADS // 2, 2, 2, MOBA_BLOCK, MOBA_BLOCK)


def _causal_conv(x, halo, w_ref):
    ts, nh = x.shape[0], halo.shape[0]
    xe = jnp.concatenate([halo, x], axis=0)
    acc = x * w_ref[CONV_K - 1:CONV_K, :]
    for d in range(1, CONV_K):
        acc = acc + xe[nh - d:nh - d + ts] * w_ref[CONV_K - 1 - d:CONV_K - d, :]
    return acc


def _tile_and_halo(x_ref, halo_ref):
    halo = jnp.where(pl.program_id(1) == 0, 0.0, halo_ref[...].astype(f32))
    return x_ref[...].astype(f32), halo


def _write_heads(o_ref, val):
    for h in range(N_HEADS):
        o_ref[0, h] = val[:, h * HEAD_DIM:(h + 1) * HEAD_DIM].astype(o_ref.dtype)


def _read_heads(ref):
    return jnp.concatenate([ref[0, h] for h in range(N_HEADS)], axis=-1)


def _hm_spec(rows):
    return pl.BlockSpec((1, N_HEADS, rows, HEAD_DIM), lambda b, i: (b, 0, i, 0))


def _hm_shape(bsz, seq, dtype=f32):
    return jax.ShapeDtypeStruct((bsz, N_HEADS, seq, HEAD_DIM), dtype)


def _chunk_sum_matrix(chunk):
    r = jnp.arange(ROW_TILE)[:, None]
    c = jnp.arange(ROW_TILE)[None, :]
    same = (r // chunk) == (c // chunk)
    return jnp.concatenate([same & (r >= c), same], axis=0).astype(bf16)


def _row_spec(width, off, nt):
    return pl.BlockSpec((ROW_TILE, width), lambda b, i: (b * nt + i, off // width))


def _halo_spec(width, off, nt):
    per = ROW_TILE // HALO_ROWS
    return pl.BlockSpec((HALO_ROWS, width), lambda b, i: (jnp.maximum((b * nt + i) * per - 1, 0), off // width))


def _small_spec(nt):
    return pl.BlockSpec((ROW_TILE, LANES), lambda b, i: (b * nt + i, 0))


def _const_spec(shape):
    return pl.BlockSpec(shape, lambda b, i: (0,) * len(shape))


def _gdn_prep_kernel(x_ref, halo_ref, sm_ref, cw_ref, alog_ref, dtb_ref, bd_ref, eb_ref, ea_ref, cm_ref,
                     q_ref, k_ref, kb_ref, vb_ref, qd_ref, kbe_ref, kd_ref, gc_ref):
    qkv = _silu(_causal_conv(*_tile_and_halo(x_ref, halo_ref), cw_ref))
    q, k, v = qkv[:, :MIX_W], qkv[:, MIX_W:2 * MIX_W], qkv[:, 2 * MIX_W:]
    bd = bd_ref[...]
    q = q * lax.rsqrt(_sel_right(q * q, bd) + L2_EPS) * HEAD_DIM ** -0.5
    k = k * lax.rsqrt(_sel_right(k * k, bd) + L2_EPS)
    sm = sm_ref[...]
    beta = _sel_right(jax.nn.sigmoid(sm), eb_ref[...])
    g = -jnp.exp(alog_ref[...]) * _softplus(sm + dtb_ref[...])
    sums = _sel_left(cm_ref[...], g)
    gc, g_end = sums[:ROW_TILE], sums[ROW_TILE:]
    eg = jnp.exp(_sel_right(gc, ea_ref[...]))
    e_rest = jnp.exp(_sel_right(g_end - gc, ea_ref[...]))
    kb = k * beta
    _write_heads(q_ref, q)
    _write_heads(k_ref, k)
    _write_heads(kb_ref, kb)
    _write_heads(vb_ref, v * beta)
    _write_heads(qd_ref, q * eg)
    _write_heads(kbe_ref, kb * eg)
    _write_heads(kd_ref, k * e_rest)
    gc_ref[...] = gc[:, SM_BA:SM_BA + N_HEADS]


def _gdn_prep(proj, small, bsz, seq, conv_w, alog128, dtb128, bd_ones, e_beta, e_ba, cm):
    nt = seq // ROW_TILE
    w3 = 3 * MIX_W
    return pl.pallas_call(
        _gdn_prep_kernel, name="gdn_prep",
        grid=(bsz, nt),
        in_specs=[_row_spec(w3, OFF_BQKV, nt), _halo_spec(w3, OFF_BQKV, nt), _small_spec(nt),
                  _const_spec((CONV_K, w3)), _const_spec((1, LANES)), _const_spec((1, LANES)),
                  _const_spec((MIX_W, MIX_W)), _const_spec((LANES, MIX_W)), _const_spec((LANES, MIX_W)),
                  _const_spec((2 * ROW_TILE, ROW_TILE))],
        out_specs=[_hm_spec(ROW_TILE)] * 7 + [pl.BlockSpec((ROW_TILE, N_HEADS), lambda b, i: (b * nt + i, 0))],
        out_shape=[_hm_shape(bsz, seq, bf16)] * 7 + [jax.ShapeDtypeStruct((bsz * seq, N_HEADS), f32)],
        compiler_params=_cparams(("parallel", "parallel"), 40),
    )(proj, proj, small, conv_w, alog128, dtb128, bd_ones, e_beta, e_ba, cm)


def _gdn_chunk_kernel(q_ref, k_ref, kb_ref, vb_ref, qd_ref, kbe_ref, kd_ref, gc_ref, gct_ref, o_ref, st_scr):
    @pl.when(pl.program_id(1) == 0)
    def _():
        st_scr[...] = jnp.zeros_like(st_scr)

    n = GDN_CHUNK
    hs = range(N_HEADS)
    r, c = _iota2((n, n))
    gc_all, gct_all = gc_ref[0, 0], gct_ref[0, 0]
    decay = [jnp.exp(jnp.where(r >= c, gc_all[:, h:h + 1] - gct_all[h:h + 1, :], NEG)) for h in hs]
    gram = [_mm_nt(jnp.concatenate([kb_ref[0, h], q_ref[0, h]], axis=0), k_ref[0, h]) for h in hs]
    t = _unit_lower_inverses([jnp.where(r > c, gram[h][:n] * decay[h], 0.0) for h in hs])
    u = [_mm(t[h], vb_ref[0, h]) for h in hs]
    w = [_mm(t[h], kbe_ref[0, h]) for h in hs]
    st = [st_scr[h] for h in hs]
    ws = [_mm(jnp.concatenate([w[h].astype(bf16), qd_ref[0, h]], axis=0), st[h]) for h in hs]
    v_new = [u[h] - ws[h][:n] for h in hs]
    o = [ws[h][n:] + _mm(gram[h][n:] * decay[h], v_new[h]) for h in hs]
    upd = [_mm_tn(kd_ref[0, h], v_new[h]) for h in hs]
    for h in hs:
        o_ref[0, h] = o[h]
        st_scr[h] = st[h] * jnp.exp(gc_all[n - 1:n, h:h + 1]) + upd[h]


def _gdn_chunk(q, k, kb, vb, qd, kbe, kd, gc):
    bsz, _, seq, _ = q.shape
    n = GDN_CHUNK
    nc = seq // n
    gc4 = gc.reshape(bsz, nc, n, N_HEADS)
    gct4 = jnp.swapaxes(gc4, 2, 3)
    return pl.pallas_call(
        _gdn_chunk_kernel, name="gdn_chunk",
        grid=(bsz, nc),
        in_specs=[_hm_spec(n)] * 7 + [pl.BlockSpec((1, 1, n, N_HEADS), lambda b, i: (b, i, 0, 0)),
                                      pl.BlockSpec((1, 1, N_HEADS, n), lambda b, i: (b, i, 0, 0))],
        out_specs=_hm_spec(n),
        out_shape=_hm_shape(bsz, seq),
        scratch_shapes=[pltpu.VMEM((N_HEADS, HEAD_DIM, HEAD_DIM), f32)],
        compiler_params=_cparams(("parallel", "arbitrary"), 32),
    )(q, k, kb, vb, qd, kbe, kd, gc4, gct4)


def _rwkv_prep_kernel(has_vres, *refs):
    if has_vres:
        (c_ref, halo_ref, mu_ref, w0_ref, wup_ref, a0_ref, aup_ref, gup_ref, kk_ref, ka_ref, bd_ref,
         rk_ref, cm_ref, vf_ref, v0_ref, vdn_ref, vup_ref,
         rt_ref, at_ref, bt_ref, kt_ref, bp_ref, kp_ref, v_ref, pe_ref, gout_ref, bonus_ref) = refs
    else:
        (c_ref, halo_ref, mu_ref, w0_ref, wup_ref, a0_ref, aup_ref, gup_ref, kk_ref, ka_ref, bd_ref,
         rk_ref, cm_ref,
         rt_ref, at_ref, bt_ref, kt_ref, bp_ref, kp_ref, v_ref, pe_ref, gout_ref, bonus_ref, cv_ref) = refs
    c, halo = _tile_and_halo(c_ref, halo_ref)
    prev = jnp.concatenate([halo[HALO_ROWS - 1:], c[:-1]], axis=0)
    c = c + (prev - c) * mu_ref[...]
    c_r, c_k, c_v = c[:, :MIX_W], c[:, MIX_W:2 * MIX_W], c[:, 2 * MIX_W:3 * MIX_W]
    c_wd = c[:, 3 * MIX_W:3 * MIX_W + 64]
    c_ad = c[:, 3 * MIX_W + 64:3 * MIX_W + 128]
    c_gd = c[:, 3 * MIX_W + 128:]
    w_log = -_softplus(-(w0_ref[...] + _mm(jnp.tanh(c_wd), wup_ref[...]))) - 0.5
    a_in = jax.nn.sigmoid(a0_ref[...] + _mm(c_ad, aup_ref[...]))
    gout_ref[...] = _mm(jax.nn.sigmoid(c_gd), gup_ref[...])
    if has_vres:
        lam = jax.nn.sigmoid(v0_ref[...] + _mm(_mm(c_v, vdn_ref[...]), vup_ref[...]))
        v_r = c_v + (vf_ref[...] - c_v) * lam
    else:
        v_r = c_v
        cv_ref[...] = c_v
    bd = bd_ref[...]
    kk = c_k * kk_ref[...]
    kk = kk * lax.rsqrt(_sel_right(kk * kk, bd) + L2_EPS)
    k_r = c_k * (1.0 + (a_in - 1.0) * ka_ref[...])
    b = kk * a_in
    bonus_ref[...] = _sel_right(c_r * k_r * rk_ref[...], bd) * v_r
    sums = _sel_left(cm_ref[...], -jnp.exp(w_log))
    lc, lc_end = sums[:ROW_TILE], sums[ROW_TILE:]
    e_neg = jnp.exp(-lc)
    e_rest = jnp.exp(lc_end - lc)
    _write_heads(rt_ref, c_r * jnp.exp(lc))
    _write_heads(at_ref, -kk * jnp.exp(lc + jnp.exp(w_log)))
    _write_heads(bt_ref, b * e_neg)
    _write_heads(kt_ref, k_r * e_neg)
    _write_heads(bp_ref, b * e_rest)
    _write_heads(kp_ref, k_r * e_rest)
    _write_heads(v_ref, v_r)
    _write_heads(pe_ref, jnp.exp(lc_end))


def _rwkv_prep(proj, bsz, seq, mu, w0, w_up, a0, a_up, g_up, k_k, k_a, r_k, bd_ones, cm, vres):
    nt = seq // ROW_TILE
    wc = 3 * MIX_W + 256
    std = pl.BlockSpec((ROW_TILE, MIX_W), lambda b, i: (b * nt + i, 0))
    std_shape = jax.ShapeDtypeStruct((bsz * seq, MIX_W), f32)
    in_specs = [_row_spec(wc, OFF_C, nt), _halo_spec(wc, OFF_C, nt), _const_spec((1, wc)),
                _const_spec((1, MIX_W)), _const_spec((64, MIX_W)), _const_spec((1, MIX_W)), _const_spec((64, MIX_W)),
                _const_spec((128, MIX_W)), _const_spec((1, MIX_W)), _const_spec((1, MIX_W)), _const_spec((MIX_W, MIX_W)),
                _const_spec((1, MIX_W)), _const_spec((2 * ROW_TILE, ROW_TILE))]
    args = [proj, proj, mu, w0, w_up, a0, a_up, g_up, k_k, k_a, bd_ones, r_k, cm]
    out_specs = [_hm_spec(ROW_TILE)] * 8 + [std, std]
    out_shape = [_hm_shape(bsz, seq, bf16)] * 7 + [_hm_shape(bsz, seq), std_shape, std_shape]
    if vres is not None:
        v_first, v0, v_down, v_up = vres
        in_specs += [std, _const_spec((1, MIX_W)), _const_spec(v_down.shape), _const_spec(v_up.shape)]
        args += [v_first, v0, v_down, v_up]
    else:
        out_specs.append(std)
        out_shape.append(std_shape)
    return pl.pallas_call(
        functools.partial(_rwkv_prep_kernel, vres is not None), name="rwkv_prep",
        grid=(bsz, nt), in_specs=in_specs, out_specs=out_specs, out_shape=out_shape,
        compiler_params=_cparams(("parallel", "parallel"), 40),
    )(*args)


def _rwkv_chunk_kernel(rt_ref, at_ref, bt_ref, kt_ref, bp_ref, kp_ref, v_ref, pe_ref, o_ref, st_scr):
    @pl.when(pl.program_id(1) == 0)
    def _():
        st_scr[...] = jnp.zeros_like(st_scr)

    n = RWKV_CHUNK
    hs = range(N_HEADS)
    row, col = _iota2((2 * n, 2 * n))
    rr, cc = row & (n - 1), col & (n - 1)
    mask = rr + jnp.where(row < n, 0, 1) > cc
    lhs = [jnp.concatenate([at_ref[0, h], rt_ref[0, h]], axis=0) for h in hs]
    gm = [jnp.where(mask, _mm_nt(lhs[h], jnp.concatenate([bt_ref[0, h], kt_ref[0, h]], axis=0)), 0.0) for h in hs]
    t = _unit_lower_inverses([-gm[h][:n, :n] for h in hs])
    st = [st_scr[h] for h in hs]
    ah = [_mm_nt(lhs[h], st[h]) for h in hs]
    v = [v_ref[0, h] for h in hs]
    makv = [_mm(gm[h][:n], jnp.concatenate([jnp.zeros_like(v[h]), v[h]], axis=0)) for h in hs]
    u = [_mm(t[h], ah[h][:n] + makv[h]) for h in hs]
    uv = [jnp.concatenate([u[h].astype(bf16), v[h]], axis=0) for h in hs]
    o = [ah[h][n:] + _mm(gm[h][n:], uv[h]) for h in hs]
    upd = [_mm_tn(uv[h], jnp.concatenate([bp_ref[0, h], kp_ref[0, h]], axis=0)) for h in hs]
    for h in hs:
        o_ref[0, h] = o[h]
        st_scr[h] = st[h] * pe_ref[0, h][0:1, :] + upd[h]


def _rwkv_chunk(rt, at, bt, kt, bp, kp, v, pe):
    bsz, _, seq, _ = rt.shape
    n = RWKV_CHUNK
    return pl.pallas_call(
        _rwkv_chunk_kernel, name="rwkv_chunk",
        grid=(bsz, seq // n),
        in_specs=[_hm_spec(n)] * 8,
        out_specs=_hm_spec(n),
        out_shape=_hm_shape(bsz, seq),
        scratch_shapes=[pltpu.VMEM((N_HEADS, HEAD_DIM, HEAD_DIM), f32)],
        compiler_params=_cparams(("parallel", "arbitrary"), 32),
    )(rt, at, bt, kt, bp, kp, v, pe)


def _ssd_prep_kernel(x_ref, halo_ref, sm_ref, cw_ref, cb_ref, alog_ref, dtb_ref, edt_ref,
                     xdt_ref, x_out_ref, bc_ref, adt_ref):
    xbc = _silu(_causal_conv(*_tile_and_halo(x_ref, halo_ref), cw_ref) + cb_ref[...])
    m_x = xbc[:, :MIX_W]
    dt = _softplus(sm_ref[...] + dtb_ref[...])
    _write_heads(xdt_ref, m_x * _sel_right(dt, edt_ref[...]))
    _write_heads(x_out_ref, m_x)
    bc_ref[...] = xbc[:, MIX_W:]
    adt_ref[...] = (dt * -jnp.exp(alog_ref[...]))[:, SM_DT:SM_DT + N_HEADS]


def _ssd_prep(proj, small, bsz, seq, conv_w, conv_b, alog128, dtb128, e_dt):
    nt = seq // ROW_TILE
    wx = MIX_W + 4 * SSM_STATE
    return pl.pallas_call(
        _ssd_prep_kernel, name="ssd_prep",
        grid=(bsz, nt),
        in_specs=[_row_spec(wx, OFF_DXBC, nt), _halo_spec(wx, OFF_DXBC, nt), _small_spec(nt),
                  _const_spec((CONV_K, wx)), _const_spec((1, wx)), _const_spec((1, LANES)), _const_spec((1, LANES)),
                  _const_spec((LANES, MIX_W))],
        out_specs=[_hm_spec(ROW_TILE)] * 2 + [pl.BlockSpec((ROW_TILE, 4 * SSM_STATE), lambda b, i: (b * nt + i, 0)),
                                              pl.BlockSpec((ROW_TILE, N_HEADS), lambda b, i: (b * nt + i, 0))],
        out_shape=[_hm_shape(bsz, seq)] * 2 + [jax.ShapeDtypeStruct((bsz * seq, 4 * SSM_STATE), f32),
                                               jax.ShapeDtypeStruct((bsz * seq, N_HEADS), f32)],
        compiler_params=_cparams(("parallel", "parallel"), 40),
    )(proj, proj, small, conv_w, conv_b, alog128, dtb128, e_dt)


def _ssd_chunk_kernel(xdt_ref, x_ref, bc_ref, a_ref, at_ref, dvec_ref, o_ref, st_scr):
    @pl.when(pl.program_id(1) == 0)
    def _():
        st_scr[...] = jnp.zeros_like(st_scr)

    n = SSD_CHUNK
    r, c = _iota2((n, n))
    tril = jnp.where(r >= c, 1.0, 0.0)
    acs_all = _mm_hi(tril, a_ref[0, 0])
    acst_all = _mm_nt_hi(at_ref[0, 0], tril)
    bc = bc_ref[...]
    heads_per_group = N_HEADS // 2
    for g in range(2):
        b_g = bc[:, g * SSM_STATE:(g + 1) * SSM_STATE]
        c_g = bc[:, (2 + g) * SSM_STATE:(3 + g) * SSM_STATE]
        cb = _mm_nt(c_g, b_g)
        for j in range(heads_per_group):
            h = g * heads_per_group + j
            ac = acs_all[:, h:h + 1]
            lmat = jnp.exp(jnp.where(r >= c, ac - acst_all[h:h + 1, :], NEG))
            xg = xdt_ref[0, h]
            st = st_scr[h]
            a_last = acs_all[n - 1:n, h:h + 1]
            y = _mm(cb * lmat, xg) + _mm(c_g * jnp.exp(ac), st)
            o_ref[0, h] = y + x_ref[0, h] * dvec_ref[h:h + 1, :]
            st_scr[h] = st * jnp.exp(a_last) + _mm_tn(b_g * jnp.exp(a_last - ac), xg)


def _ssd_chunk(xdt, x, bc, adt, dvec):
    bsz, _, seq, _ = xdt.shape
    n = SSD_CHUNK
    nc = seq // n
    a4 = adt.reshape(bsz, nc, n, N_HEADS)
    at4 = jnp.swapaxes(a4, 2, 3)
    return pl.pallas_call(
        _ssd_chunk_kernel, name="ssd_chunk",
        grid=(bsz, nc),
        in_specs=[_hm_spec(n)] * 2 + [pl.BlockSpec((n, 4 * SSM_STATE), lambda b, i: (b * nc + i, 0)),
                                      pl.BlockSpec((1, 1, n, N_HEADS), lambda b, i: (b, i, 0, 0)),
                                      pl.BlockSpec((1, 1, N_HEADS, n), lambda b, i: (b, i, 0, 0)),
                                      _const_spec((N_HEADS, HEAD_DIM))],
        out_specs=_hm_spec(n),
        out_shape=_hm_shape(bsz, seq),
        scratch_shapes=[pltpu.VMEM((N_HEADS, SSM_STATE, HEAD_DIM), f32)],
        compiler_params=_cparams(("parallel", "arbitrary"), 32),
    )(xdt, x, bc, a4, at4, dvec)


def _merge_kernel(x_ref, ya_ref, ob_ref, bz_ref, gnw_ref,
                  wkv_ref, bonus_ref, gout_ref, lnw_ref, lnb_ref,
                  yd_ref, dz_ref, mnw_ref, g0_ref, g1_ref, g2_ref, g3_ref, wb_ref, wo_ref, o_ref):
    def per_head(fn):
        return jnp.concatenate([fn(h) for h in range(N_HEADS)], axis=-1)

    def gdn_head(h):
        o = ob_ref[0, h]
        return o * lax.rsqrt(jnp.mean(o * o, axis=-1, keepdims=True) + RMS_EPS) * gnw_ref[...]

    y_b = per_head(gdn_head) * _silu(bz_ref[...].astype(f32))

    def wkv_head(h):
        w = wkv_ref[0, h]
        mu = jnp.mean(w, axis=-1, keepdims=True)
        var = jnp.mean(jnp.square(w - mu), axis=-1, keepdims=True)
        return (w - mu) * lax.rsqrt(var + RWKV_LN_EPS)

    y_c = (per_head(wkv_head) * lnw_ref[...] + lnb_ref[...] + bonus_ref[...]) * gout_ref[...]

    yz = _read_heads(yd_ref) * _silu(dz_ref[...].astype(f32))
    half = MIX_W // 2
    y_d = jnp.concatenate(
        [yz[:, s:s + half] * lax.rsqrt(jnp.mean(jnp.square(yz[:, s:s + half]), axis=-1, keepdims=True) + RMS_EPS)
         for s in (0, half)], axis=-1) * mnw_ref[...]

    acc = jnp.zeros(x_ref.shape, f32)
    for n, (y, g_ref) in enumerate(((ya_ref[0], g0_ref), (y_b, g1_ref), (y_c, g2_ref), (y_d, g3_ref))):
        acc = acc + jax.nn.sigmoid(g_ref[...].astype(f32)) * _mm(y, wb_ref[n])
    o_ref[...] = x_ref[...] + _mm(acc, wo_ref[...])


def _merge(x2, proj, bsz, seq, ya, ob, gnw, wkv, bonus, gout, lnw, lnb, yd, mnw, wb, wo):
    nt = seq // ROW_TILE
    std = lambda w: pl.BlockSpec((ROW_TILE, w), lambda b, i: (b * nt + i, 0))
    hm = _hm_spec(ROW_TILE)
    gate = lambda n: _row_spec(D_MODEL, OFF_GATES + n * D_MODEL, nt)
    return pl.pallas_call(
        _merge_kernel, name="merge",
        grid=(bsz, nt),
        in_specs=[std(D_MODEL), pl.BlockSpec((1, ROW_TILE, MIX_W), lambda b, i: (b, i, 0)),
                  hm, _row_spec(MIX_W, OFF_BZ, nt), _const_spec((1, HEAD_DIM)),
                  hm, std(MIX_W), std(MIX_W), _const_spec((1, MIX_W)), _const_spec((1, MIX_W)),
                  hm, _row_spec(MIX_W, OFF_DZ, nt), _const_spec((1, MIX_W)),
                  gate(0), gate(1), gate(2), gate(3),
                  _const_spec((4, MIX_W, D_MODEL)), _const_spec((D_MODEL, D_MODEL))],
        out_specs=std(D_MODEL),
        out_shape=jax.ShapeDtypeStruct((bsz * seq, D_MODEL), f32),
        compiler_params=_cparams(("parallel", "parallel"), 48),
    )(x2, ya, ob, proj, gnw, wkv, bonus, gout, lnw, lnb, yd, proj, mnw, proj, proj, proj, proj, wb, wo)


def _lane_vec(vals, off):
    return jnp.zeros((1, LANES), f32).at[0, off:off + vals.shape[0]].set(vals)


def _head_expand(off):
    n = jnp.arange(LANES)[:, None]
    c = jnp.arange(MIX_W)[None, :]
    return (n - off == c // HEAD_DIM).astype(bf16)


def _pack_w_in(w):
    pad = lambda n: jnp.zeros((w.shape[0], n), w.dtype)
    cols = [w[:, W_A:W_BZ],
            w[:, W_BZ:W_BBETA],
            w[:, W_C:W_DZ],
            w[:, W_BBETA:W_C], w[:, W_DDT:W_GATES], pad(2 * LANES - 3 * N_HEADS),
            w[:, W_DZ:W_DXBC], w[:, W_DXBC:W_DDT], w[:, W_GATES:]]
    out = jnp.concatenate(cols, axis=1).astype(bf16)
    assert out.shape[1] == N_PROJ
    return out


def kernel(x, rel_bias, norm1_w, w_in, moba_q_norm, moba_k_norm, gdn_conv_w, gdn_A_log, gdn_dt_bias, gdn_norm_w, rwkv_mu, rwkv_w0, rwkv_w_up, rwkv_a0, rwkv_a_up, rwkv_g_up, rwkv_k_k, rwkv_k_a, rwkv_r_k, rwkv_v0, rwkv_v_down, rwkv_v_up, rwkv_ln_w, rwkv_ln_b, mamba_conv_w, mamba_conv_b, mamba_dt_bias, mamba_A_log, mamba_D, mamba_norm_w, w_branch, w_out, norm2_w, ffn_w_in, ffn_w_down):
    bsz, seq, d = x.shape
    depth = w_in.shape[0]
    assert d == D_MODEL and (bsz * seq) % IN_PROJ_TILE_M == 0 and seq % MM_TILE_M == 0
    x2 = x.reshape(bsz * seq, d)
    row = lambda v: v.reshape(1, -1).astype(f32)

    hid = jnp.arange(MIX_W) // HEAD_DIM
    bd_ones = (hid[:, None] == hid[None, :]).astype(bf16)
    bd_mean = (bd_ones.astype(f32) / HEAD_DIM).astype(bf16)
    e_beta, e_ba, e_dt = _head_expand(SM_BETA), _head_expand(SM_BA), _head_expand(SM_DT)
    cm64 = _chunk_sum_matrix(GDN_CHUNK)
    assert GDN_CHUNK == RWKV_CHUNK
    tab = _moba_bias_tables(rel_bias)
    v_first = None
    for i in range(depth):
        proj, small = _in_proj(x2, row(norm1_w[i]), _pack_w_in(w_in[i]))

        qaug, kaug, v_a = _moba_prep(proj, bsz, seq, row(jnp.tile(moba_q_norm[i], N_HEADS)),
                                     row(jnp.tile(moba_k_norm[i], N_HEADS)), bd_mean)
        y_a = _moba_attn(qaug, kaug, v_a, tab)

        gdn_in = _gdn_prep(proj, small, bsz, seq, gdn_conv_w[i], _lane_vec(gdn_A_log[i], SM_BA),
                           _lane_vec(gdn_dt_bias[i], SM_BA), bd_ones, e_beta, e_ba, cm64)
        o_b = _gdn_chunk(*gdn_in)

        vres = None if i == 0 else (v_first, row(rwkv_v0[i - 1]), rwkv_v_down[i - 1].astype(bf16),
                                    rwkv_v_up[i - 1].astype(bf16))
        outs = _rwkv_prep(proj, bsz, seq, row(rwkv_mu[i]), row(rwkv_w0[i]), rwkv_w_up[i].astype(bf16),
                          row(rwkv_a0[i]), rwkv_a_up[i].astype(bf16), rwkv_g_up[i].astype(bf16),
                          row(rwkv_k_k[i]), row(rwkv_k_a[i]), row(rwkv_r_k[i]), bd_ones, cm64, vres)
        g_out, bonus = outs[8], outs[9]
        if i == 0:
            v_first = outs[10]
        wkv = _rwkv_chunk(*outs[:8])

        xdt, x_d, bc, adt = _ssd_prep(proj, small, bsz, seq, mamba_conv_w[i], row(mamba_conv_b[i]),
                                      _lane_vec(mamba_A_log[i], SM_DT), _lane_vec(mamba_dt_bias[i], SM_DT), e_dt)
        y_d = _ssd_chunk(xdt, x_d, bc, adt, jnp.broadcast_to(mamba_D[i][:, None], (N_HEADS, HEAD_DIM)).astype(f32))

        x2 = _merge(x2, proj, bsz, seq, y_a, o_b, row(gdn_norm_w[i]), wkv, bonus, g_out,
                    row(rwkv_ln_w[i]), row(rwkv_ln_b[i]), y_d, row(mamba_norm_w[i]),
                    w_branch[i].astype(bf16), w_out[i].astype(bf16))

        act = _ffn_up(x2, row(norm2_w[i]), ffn_w_in[i][:, :D_FF].astype(bf16), ffn_w_in[i][:, D_FF:].astype(bf16))
        x2 = _ffn_down(x2, act, ffn_w_down[i].astype(bf16))
    return x2.reshape(bsz, seq, d)
```

```python
import functools
import math

import jax
import jax.numpy as jnp
from jax import lax
from jax.experimental import pallas as pl
from jax.experimental.pallas import tpu as pltpu

f32, bf16 = jnp.float32, jnp.bfloat16
HI = lax.Precision.HIGHEST

D_MODEL = 1024
N_HEADS = 8
HEAD_DIM = 64
MIX_W = N_HEADS * HEAD_DIM
RMS_EPS = 1e-6
L2_EPS = 1e-6
CONV_K = 4
MOBA_BLOCK = 256
MOBA_TOPK = 3
MOBA_KV_PER_GROUP = 2
REL_BUCKETS = 32
REL_MAX_DIST = 128
GDN_CHUNK = 64
RWKV_CHUNK = 64
CHUNKS_PER_STEP = 4
RWKV_LN_EPS = 64e-5
SSM_STATE = 128
SSD_CHUNK = 128
D_FF = 2816
NEG = -1e30
LOG2E = math.log2(math.e)

LANES = 128
BF16_SUBLANES = 16
VT_ROWS = HEAD_DIM + BF16_SUBLANES

OFF_A, OFF_BQKV, OFF_BZ, OFF_C, OFF_SMALL, OFF_DZ, OFF_DXBC, OFF_GATES = 0, 1536, 3072, 3584, 5376, 5632, 6144, 7168
N_PROJ = OFF_GATES + 4 * D_MODEL
SM_BETA, SM_BA, SM_DT = 0, 8, 16
W_A, W_BQKV, W_BZ, W_BBETA, W_BA, W_C, W_DZ, W_DXBC, W_DDT, W_GATES = 0, 1536, 3072, 3584, 3592, 3600, 5392, 5904, 6928, 6936

ROW_TILE = 256
MM_TILE_M = 512
IN_PROJ_TILE_M, IN_PROJ_TILE_N = 1024, 1024
HALO_ROWS = BF16_SUBLANES


def _cparams(sem, vmem_mb):
    return pltpu.CompilerParams(dimension_semantics=sem, vmem_limit_bytes=vmem_mb * 1024 * 1024)


def _mm(a, b):
    return jnp.dot(a.astype(bf16), b.astype(bf16), preferred_element_type=f32)


def _mm_nt(a, b):
    return lax.dot_general(a.astype(bf16), b.astype(bf16), (((1,), (1,)), ((), ())), preferred_element_type=f32)


def _mm_tn(a, b):
    return lax.dot_general(a.astype(bf16), b.astype(bf16), (((0,), (0,)), ((), ())), preferred_element_type=f32)


def _mm_hi(a, b):
    return jnp.dot(a, b, precision=HI, preferred_element_type=f32)


def _mm_nt_hi(a, b):
    return lax.dot_general(a, b, (((1,), (1,)), ((), ())), precision=HI, preferred_element_type=f32)


def _softplus(x):
    return jnp.maximum(x, 0.0) + jnp.log1p(jnp.exp(-jnp.abs(x)))


def _silu(x):
    return x * jax.nn.sigmoid(x)


def _iota2(shape):
    return lax.broadcasted_iota(jnp.int32, shape, 0), lax.broadcasted_iota(jnp.int32, shape, 1)


def _split3(x):
    hi = x.astype(bf16)
    r1 = x - hi.astype(f32)
    mid = r1.astype(bf16)
    return hi, mid, (r1 - mid.astype(f32)).astype(bf16)


def _sel_left(m01, x):
    return sum(jnp.dot(m01, p, preferred_element_type=f32) for p in _split3(x))


def _sel_right(x, m01):
    return sum(jnp.dot(p, m01, preferred_element_type=f32) for p in _split3(x))


def _unit_lower_inverses(l_list):
    n = l_list[0].shape[0]
    r, c = _iota2((n, n))
    eye = jnp.where(r == c, 1.0, 0.0)
    ts = [eye for _ in l_list]
    for ls in range(n.bit_length() - 1):
        m = ((r >> (ls + 1)) == (c >> (ls + 1))) & (((r >> ls) & 1) == 1) & (((c >> ls) & 1) == 0)
        lms = [jnp.where(m, l, 0.0) for l in l_list]
        if ls == 0:
            ts = [t - lm for t, lm in zip(ts, lms)]
        else:
            tl = [_mm(t, lm) for t, lm in zip(ts, lms)]
            ts = [t - _mm(x, t) for t, x in zip(ts, tl)]
    return ts


def _in_proj_kernel(x_ref, nw_ref, w_ref, o_ref, sm_ref, h_scr):
    j = pl.program_id(1)

    @pl.when(j == 0)
    def _():
        x = x_ref[...]
        y = x * lax.rsqrt(jnp.mean(x * x, axis=-1, keepdims=True) + RMS_EPS)
        h_scr[...] = (y * nw_ref[...]).astype(bf16)

    acc = jnp.dot(h_scr[...], w_ref[...], preferred_element_type=f32)
    o_ref[...] = acc.astype(o_ref.dtype)
    tn = o_ref.shape[1]

    @pl.when(j == OFF_SMALL // tn)
    def _():
        sm_ref[...] = acc[:, OFF_SMALL % tn:OFF_SMALL % tn + LANES]


def _in_proj(x2, nw, w):
    t, d = x2.shape
    n = w.shape[1]
    tm, tn = IN_PROJ_TILE_M, IN_PROJ_TILE_N
    return pl.pallas_call(
        _in_proj_kernel, name="in_proj",
        grid=(t // tm, n // tn),
        in_specs=[pl.BlockSpec((tm, d), lambda i, j: (i, 0)),
                  pl.BlockSpec((1, d), lambda i, j: (0, 0)),
                  pl.BlockSpec((d, tn), lambda i, j: (0, j))],
        out_specs=[pl.BlockSpec((tm, tn), lambda i, j: (i, j)),
                   pl.BlockSpec((tm, LANES), lambda i, j: (i, 0))],
        out_shape=[jax.ShapeDtypeStruct((t, n), bf16), jax.ShapeDtypeStruct((t, LANES), f32)],
        scratch_shapes=[pltpu.VMEM((tm, d), bf16)],
        compiler_params=_cparams(("parallel", "arbitrary"), 48),
    )(x2, nw, w)


def _ffn_up_kernel(x_ref, nw_ref, wg_ref, wu_ref, o_ref, h_scr):
    @pl.when(pl.program_id(1) == 0)
    def _():
        x = x_ref[...]
        y = x * lax.rsqrt(jnp.mean(x * x, axis=-1, keepdims=True) + RMS_EPS)
        h_scr[...] = (y * nw_ref[...]).astype(bf16)
    h = h_scr[...]
    g = jnp.dot(h, wg_ref[...], preferred_element_type=f32)
    u = jnp.dot(h, wu_ref[...], preferred_element_type=f32)
    o_ref[...] = (_silu(g) * u).astype(o_ref.dtype)


def _ffn_up(x2, nw, wg, wu):
    t, d = x2.shape
    n = wg.shape[1]
    tm, tn = MM_TILE_M, n // 2
    return pl.pallas_call(
        _ffn_up_kernel, name="ffn_up",
        grid=(t // tm, n // tn),
        in_specs=[pl.BlockSpec((tm, d), lambda i, j: (i, 0)),
                  pl.BlockSpec((1, d), lambda i, j: (0, 0)),
                  pl.BlockSpec((d, tn), lambda i, j: (0, j)),
                  pl.BlockSpec((d, tn), lambda i, j: (0, j))],
        out_specs=pl.BlockSpec((tm, tn), lambda i, j: (i, j)),
        out_shape=jax.ShapeDtypeStruct((t, n), bf16),
        scratch_shapes=[pltpu.VMEM((tm, d), bf16)],
        compiler_params=_cparams(("parallel", "arbitrary"), 48),
    )(x2, nw, wg, wu)


def _ffn_down_kernel(x_ref, a_ref, w_ref, o_ref):
    o_ref[...] = x_ref[...] + jnp.dot(a_ref[...], w_ref[...], preferred_element_type=f32)


def _ffn_down(x2, act, wd):
    t, d = x2.shape
    n = act.shape[1]
    tm = MM_TILE_M
    return pl.pallas_call(
        _ffn_down_kernel, name="ffn_down",
        grid=(t // tm,),
        in_specs=[pl.BlockSpec((tm, d), lambda i: (i, 0)),
                  pl.BlockSpec((tm, n), lambda i: (i, 0)),
                  pl.BlockSpec((n, d), lambda i: (0, 0))],
        out_specs=pl.BlockSpec((tm, d), lambda i: (i, 0)),
        out_shape=jax.ShapeDtypeStruct((t, d), f32),
        compiler_params=_cparams(("parallel",), 40),
    )(x2, act, wd)


def _top3_bias(gate_t, n_past):
    row = lax.broadcasted_iota(jnp.int32, gate_t.shape, 0)
    g = jnp.where(row < n_past, gate_t, -jnp.inf)
    sel = jnp.zeros(gate_t.shape, jnp.bool_)
    for _ in range(MOBA_TOPK):
        m = jnp.max(g, axis=0, keepdims=True)
        idx = jnp.min(jnp.where(g == m, row, gate_t.shape[0]), axis=0, keepdims=True)
        pick = row == idx
        sel = sel | (pick & (m > -jnp.inf))
        g = jnp.where(pick, -jnp.inf, g)
    return jnp.where(sel, 0.0, NEG)


def _moba_prep_kernel(a_ref, qw_ref, kw_ref, bd_ref, qaugt_ref, kaug_ref, vaugt_ref, kmean_scr):
    i = pl.program_id(1)

    @pl.when(i == 0)
    def _():
        kmean_scr[...] = jnp.zeros_like(kmean_scr)

    a = a_ref[...].astype(f32)
    q, k, v = a[:, :MIX_W], a[:, MIX_W:2 * MIX_W], a[:, 2 * MIX_W:]
    bd = bd_ref[...]
    qn = q * lax.rsqrt(_sel_right(q * q, bd) + RMS_EPS) * qw_ref[...]
    kn = k * lax.rsqrt(_sel_right(k * k, bd) + RMS_EPS) * kw_ref[...]
    lane = lax.broadcasted_iota(jnp.int32, (MOBA_BLOCK, LANES), 1)
    onehot = jnp.where(lane == i, 1.0, 0.0).astype(bf16)
    ones_row = jnp.where(lax.broadcasted_iota(jnp.int32, (VT_ROWS - HEAD_DIM, MOBA_BLOCK), 0) == 0, 1.0, 0.0)
    kmean = kmean_scr[...]
    nbp = kmean.shape[0]
    dim = lax.broadcasted_iota(jnp.int32, (LANES, MOBA_BLOCK), 0)
    sel_pad = jnp.zeros((LANES - nbp, MOBA_BLOCK), f32)
    for p in range(N_HEADS // 2):
        sl = slice(p * LANES, (p + 1) * LANES)
        kaug_ref[0, p] = jnp.concatenate([kn[:, sl].astype(bf16), onehot], axis=-1)
        vt = v[:, sl].T
        qt = (qn[:, sl] * (HEAD_DIM ** -0.5 * LOG2E)).T
        for hh in range(2):
            vaugt_ref[0, 2 * p + hh] = jnp.concatenate(
                [vt[hh * HEAD_DIM:(hh + 1) * HEAD_DIM], ones_row], axis=0).astype(bf16)
            keep = (dim < HEAD_DIM) if hh == 0 else (dim >= HEAD_DIM)
            q2t = jnp.where(keep, qt, 0.0)
            gate_t = _mm_hi(kmean[:, sl], q2t)
            qaugt_ref[0, 2 * p + hh] = jnp.concatenate([q2t, _top3_bias(gate_t, i), sel_pad], axis=0).astype(bf16)
    kmean_scr[pl.ds(i, 1), :] = jnp.mean(kn, axis=0, keepdims=True)


def _moba_prep(proj, bsz, seq, qw, kw, bd_mean):
    nb = seq // MOBA_BLOCK
    nbp = -(-nb // 8) * 8
    assert nbp <= LANES
    return pl.pallas_call(
        _moba_prep_kernel, name="moba_prep",
        grid=(bsz, nb),
        in_specs=[pl.BlockSpec((MOBA_BLOCK, 3 * MIX_W), lambda b, i: (b * nb + i, OFF_A // (3 * MIX_W))),
                  pl.BlockSpec((1, MIX_W), lambda b, i: (0, 0)),
                  pl.BlockSpec((1, MIX_W), lambda b, i: (0, 0)),
                  pl.BlockSpec((MIX_W, MIX_W), lambda b, i: (0, 0))],
        out_specs=[pl.BlockSpec((1, N_HEADS, 2 * LANES, MOBA_BLOCK), lambda b, i: (b, 0, 0, i)),
                   pl.BlockSpec((1, N_HEADS // 2, MOBA_BLOCK, 2 * LANES), lambda b, i: (b, 0, i, 0)),
                   pl.BlockSpec((1, N_HEADS, VT_ROWS, MOBA_BLOCK), lambda b, i: (b, 0, 0, i))],
        out_shape=[jax.ShapeDtypeStruct((bsz, N_HEADS, 2 * LANES, seq), bf16),
                   jax.ShapeDtypeStruct((bsz, N_HEADS // 2, seq, 2 * LANES), bf16),
                   jax.ShapeDtypeStruct((bsz, N_HEADS, VT_ROWS, seq), bf16)],
        scratch_shapes=[pltpu.VMEM((nbp, MIX_W), f32)],
        compiler_params=_cparams(("parallel", "arbitrary"), 32),
    )(proj, qw, kw, bd_mean)


def _moba_attn_kernel(qaugt_ref, kaug_ref, vaugt_ref, tabt_ref, o_ref, sa_scr, sb_scr):
    i = pl.program_id(2)
    blk = MOBA_BLOCK
    hs = (0, 1)
    key, qry = _iota2((blk, blk))
    mm = lambda a, b: jnp.dot(a, b, preferred_element_type=f32)
    cmax = lambda s: jnp.max(s, axis=0, keepdims=True)
    pv = lambda v, pe: jnp.dot(v, pe.astype(bf16), preferred_element_type=f32)
    kblk = lambda j: kaug_ref[0, 0, pl.ds(pl.multiple_of(j * blk, blk), blk), :]
    vblk = lambda hh, j: vaugt_ref[0, hh, :, pl.ds(pl.multiple_of(j * blk, blk), blk)]

    n_far = jnp.maximum(i - 1, 0)
    nk = MOBA_KV_PER_GROUP
    n_groups = (n_far + nk - 1) // nk
    blocks_of = lambda g: [jnp.where(nk * g + a < n_far, nk * g + a, i) for a in range(nk)]

    def scores(g, buf):
        qt = [qaugt_ref[0, hh] for hh in hs]
        for a, j in enumerate(blocks_of(g)):
            k = kblk(j)
            for hh in hs:
                buf[hh, a] = mm(k, qt[hh])

    scores(0, sa_scr)

    k_own, k_adj = kblk(i), kblk(n_far)
    qt = [qaugt_ref[0, hh] for hh in hs]
    s_own = [jnp.where(qry >= key, mm(k_own[:, :LANES], qt[hh][:LANES]) + tabt_ref[0, hh, 1], NEG) for hh in hs]
    s_adj = [mm(k_adj, qt[hh]) + tabt_ref[0, hh, 0] for hh in hs]
    m = [jnp.maximum(cmax(s_own[hh]), cmax(s_adj[hh])) for hh in hs]
    acc = [pv(vblk(hh, i), jnp.exp2(s_own[hh] - m[hh])) + pv(vblk(hh, n_far), jnp.exp2(s_adj[hh] - m[hh]))
           for hh in hs]

    def consume(g, buf, m, acc):
        js = blocks_of(g)
        s = [[buf[hh, a] for a in range(nk)] for hh in hs]
        m_new = [functools.reduce(jnp.maximum, [cmax(x) for x in s[hh]], m[hh]) for hh in hs]
        acc = [functools.reduce(lambda x, y: x + y,
                                [pv(vblk(hh, js[a]), jnp.exp2(s[hh][a] - m_new[hh])) for a in range(nk)],
                                jnp.exp2(m[hh] - m_new[hh]) * acc[hh])
               for hh in hs]
        return m_new, acc

    def body(u, carry):
        m, acc = list(carry[:2]), list(carry[2:])
        scores(2 * u + 1, sb_scr)
        m, acc = consume(2 * u, sa_scr, m, acc)
        scores(2 * u + 2, sa_scr)
        m, acc = consume(2 * u + 1, sb_scr, m, acc)
        return (*m, *acc)

    carry = lax.fori_loop(0, n_groups // 2, body, (*m, *acc))
    last = n_groups - 1
    carry = lax.cond(n_groups % 2 == 1,
                     lambda cr: (lambda r: (*r[0], *r[1]))(consume(last, sa_scr, list(cr[:2]), list(cr[2:]))),
                     lambda cr: cr, carry)
    acc = carry[2:]
    o_ref[0] = jnp.concatenate([(acc[hh][:HEAD_DIM] / acc[hh][HEAD_DIM:HEAD_DIM + 1]).T for hh in hs], axis=-1)


def _moba_attn(qaugt, kaug, vaugt, tabt):
    bsz, _, _, seq = qaugt.shape
    nb = seq // MOBA_BLOCK
    return pl.pallas_call(
        _moba_attn_kernel, name="moba_attn",
        grid=(bsz, N_HEADS // 2, nb),
        in_specs=[pl.BlockSpec((1, 2, 2 * LANES, MOBA_BLOCK), lambda b, p, i: (b, p, 0, i)),
                  pl.BlockSpec((1, 1, seq, 2 * LANES), lambda b, p, i: (b, p, 0, 0)),
                  pl.BlockSpec((1, 2, VT_ROWS, seq), lambda b, p, i: (b, p, 0, 0)),
                  pl.BlockSpec((1, 2, 2, MOBA_BLOCK, MOBA_BLOCK), lambda b, p, i: (p, 0, 0, 0, 0))],
        out_specs=pl.BlockSpec((1, MOBA_BLOCK, LANES), lambda b, p, i: (b, i, p)),
        out_shape=jax.ShapeDtypeStruct((bsz, seq, MIX_W), f32),
        scratch_shapes=[pltpu.VMEM((2, MOBA_KV_PER_GROUP, MOBA_BLOCK, MOBA_BLOCK), f32)] * 2,
        compiler_params=_cparams(("parallel", "parallel", "arbitrary"), 40),
    )(qaugt, kaug, vaugt, tabt)


def _t5_bucket(dist):
    n = jnp.maximum(dist, 0)
    max_exact = REL_BUCKETS // 2
    nf = jnp.maximum(n, max_exact).astype(f32)
    large = max_exact + (jnp.log(nf / max_exact) / math.log(REL_MAX_DIST / max_exact)
                         * (REL_BUCKETS - max_exact)).astype(jnp.int32)
    large = jnp.minimum(large, REL_BUCKETS - 1)
    return jnp.where(n < max_exact, n, large)


def _moba_bias_kernel(vec_ref, o_ref):
    blk = MOBA_BLOCK
    t = pltpu.roll(jnp.broadcast_to(vec_ref[0] * LOG2E, (blk, 2 * blk)), 0, 1, stride=1, stride_axis=0)
    o_ref[0, 0] = t[:, blk:]
    o_ref[0, 1] = t[:, :blk]


def _moba_bias_tables(rel_bias):
    assert MOBA_BLOCK >= REL_MAX_DIST
    by_dist = rel_bias.astype(f32)[_t5_bucket(jnp.arange(2 * MOBA_BLOCK))]
    far = rel_bias.astype(f32)[_t5_bucket(jnp.array(2 * MOBA_BLOCK))]
    vec = (by_dist - far).T.reshape(N_HEADS, 1, 2 * MOBA_BLOCK)
    tab = pl.pallas_call(
        _moba_bias_kernel, name="moba_bias",
        grid=(N_HEADS,),
        in_specs=[pl.BlockSpec((1, 1, 2 * MOBA_BLOCK), lambda h: (h, 0, 0))],
        out_specs=pl.BlockSpec((1, 2, MOBA_BLOCK, MOBA_BLOCK), lambda h: (h, 0, 0, 0)),
        out_shape=jax.ShapeDtypeStruct((N_HEADS, 2, MOBA_BLOCK, MOBA_BLOCK), f32),
        compiler_params=_cparams(("parallel",), 16),
    )(vec)
    return tab.reshape(N_HEADS // 2, 2, 2, MOBA_BLOCK, MOBA_BLOCK)


def _causal_conv(x, halo, w_ref):
    ts, nh = x.shape[0], halo.shape[0]
    xe = jnp.concatenate([halo, x], axis=0)
    acc = x * w_ref[CONV_K - 1:CONV_K, :]
    for d in range(1, CONV_K):
        acc = acc + xe[nh - d:nh - d + ts] * w_ref[CONV_K - 1 - d:CONV_K - d, :]
    return acc


def _tile_and_halo(x_ref, halo_ref):
    halo = jnp.where(pl.program_id(1) == 0, 0.0, halo_ref[...].astype(f32))
    return x_ref[...].astype(f32), halo


def _write_heads(o_ref, val):
    for h in range(N_HEADS):
        o_ref[0, h] = val[:, h * HEAD_DIM:(h + 1) * HEAD_DIM].astype(o_ref.dtype)


def _read_heads(ref):
    return jnp.concatenate([ref[0, h] for h in range(N_HEADS)], axis=-1)


def _hm_spec(rows):
    return pl.BlockSpec((1, N_HEADS, rows, HEAD_DIM), lambda b, i: (b, 0, i, 0))


def _hm_shape(bsz, seq, dtype=f32):
    return jax.ShapeDtypeStruct((bsz, N_HEADS, seq, HEAD_DIM), dtype)


def _chunk_sum_matrix(chunk):
    r = jnp.arange(ROW_TILE)[:, None]
    c = jnp.arange(ROW_TILE)[None, :]
    same = (r // chunk) == (c // chunk)
    return jnp.concatenate([same & (r >= c), same], axis=0).astype(bf16)


def _row_spec(width, off, nt):
    return pl.BlockSpec((ROW_TILE, width), lambda b, i: (b * nt + i, off // width))


def _halo_spec(width, off, nt):
    per = ROW_TILE // HALO_ROWS
    return pl.BlockSpec((HALO_ROWS, width), lambda b, i: (jnp.maximum((b * nt + i) * per - 1, 0), off // width))


def _small_spec(nt):
    return pl.BlockSpec((ROW_TILE, LANES), lambda b, i: (b * nt + i, 0))


def _const_spec(shape):
    return pl.BlockSpec(shape, lambda b, i: (0,) * len(shape))


def _gdn_prep_kernel(x_ref, halo_ref, sm_ref, cw_ref, alog_ref, dtb_ref, bd_ref, eb_ref, ea_ref, cm_ref,
                     q_ref, k_ref, kb_ref, vb_ref, qd_ref, kbe_ref, kd_ref, gc_ref):
    qkv = _silu(_causal_conv(*_tile_and_halo(x_ref, halo_ref), cw_ref))
    q, k, v = qkv[:, :MIX_W], qkv[:, MIX_W:2 * MIX_W], qkv[:, 2 * MIX_W:]
    bd = bd_ref[...]
    q = q * lax.rsqrt(_sel_right(q * q, bd) + L2_EPS) * HEAD_DIM ** -0.5
    k = k * lax.rsqrt(_sel_right(k * k, bd) + L2_EPS)
    sm = sm_ref[...]
    beta = _sel_right(jax.nn.sigmoid(sm), eb_ref[...])
    g = -jnp.exp(alog_ref[...]) * _softplus(sm + dtb_ref[...])
    sums = _sel_left(cm_ref[...], g)
    gc, g_end = sums[:ROW_TILE], sums[ROW_TILE:]
    eg = jnp.exp(_sel_right(gc, ea_ref[...]))
    e_rest = jnp.exp(_sel_right(g_end - gc, ea_ref[...]))
    kb = k * beta
    _write_heads(q_ref, q)
    _write_heads(k_ref, k)
    _write_heads(kb_ref, kb)
    _write_heads(vb_ref, v * beta)
    _write_heads(qd_ref, q * eg)
    _write_heads(kbe_ref, kb * eg)
    _write_heads(kd_ref, k * e_rest)
    gc_ref[...] = gc[:, SM_BA:SM_BA + N_HEADS]


def _gdn_prep(proj, small, bsz, seq, conv_w, alog128, dtb128, bd_ones, e_beta, e_ba, cm):
    nt = seq // ROW_TILE
    w3 = 3 * MIX_W
    return pl.pallas_call(
        _gdn_prep_kernel, name="gdn_prep",
        grid=(bsz, nt),
        in_specs=[_row_spec(w3, OFF_BQKV, nt), _halo_spec(w3, OFF_BQKV, nt), _small_spec(nt),
                  _const_spec((CONV_K, w3)), _const_spec((1, LANES)), _const_spec((1, LANES)),
                  _const_spec((MIX_W, MIX_W)), _const_spec((LANES, MIX_W)), _const_spec((LANES, MIX_W)),
                  _const_spec((2 * ROW_TILE, ROW_TILE))],
        out_specs=[_hm_spec(ROW_TILE)] * 7 + [pl.BlockSpec((ROW_TILE, N_HEADS), lambda b, i: (b * nt + i, 0))],
        out_shape=[_hm_shape(bsz, seq, bf16)] * 7 + [jax.ShapeDtypeStruct((bsz * seq, N_HEADS), f32)],
        compiler_params=_cparams(("parallel", "parallel"), 40),
    )(proj, proj, small, conv_w, alog128, dtb128, bd_ones, e_beta, e_ba, cm)


def _gdn_chunk_kernel(q_ref, k_ref, kb_ref, vb_ref, qd_ref, kbe_ref, kd_ref, gc_ref, gct_ref, o_ref, st_scr):
    @pl.when(pl.program_id(1) == 0)
    def _():
        st_scr[...] = jnp.zeros_like(st_scr)

    n = GDN_CHUNK
    hs = range(N_HEADS)
    ch = [(ci, h) for ci in range(CHUNKS_PER_STEP) for h in hs]
    blk = lambda ref, ci, h: ref[0, h, ci * n:(ci + 1) * n, :]
    r, c = _iota2((n, n))
    gc_all = [gc_ref[0, ci] for ci in range(CHUNKS_PER_STEP)]
    gct_all = [gct_ref[0, ci] for ci in range(CHUNKS_PER_STEP)]
    decay = {x: jnp.exp(jnp.where(r >= c, gc_all[x[0]][:, x[1]:x[1] + 1] - gct_all[x[0]][x[1]:x[1] + 1, :], NEG))
             for x in ch}
    gram = {x: _mm_nt(jnp.concatenate([blk(kb_ref, *x), blk(q_ref, *x)], axis=0), blk(k_ref, *x)) for x in ch}
    t = dict(zip(ch, _unit_lower_inverses([jnp.where(r > c, gram[x][:n] * decay[x], 0.0) for x in ch])))
    u = {x: _mm(t[x], blk(vb_ref, *x)) for x in ch}
    w = {x: _mm(t[x], blk(kbe_ref, *x)) for x in ch}
    a_in = {x: gram[x][n:] * decay[x] for x in ch}
    st = [st_scr[h] for h in hs]
    for ci in range(CHUNKS_PER_STEP):
        ws = [_mm(jnp.concatenate([w[ci, h].astype(bf16), blk(qd_ref, ci, h)], axis=0), st[h]) for h in hs]
        v_new = [u[ci, h] - ws[h][:n] for h in hs]
        o = [ws[h][n:] + _mm(a_in[ci, h], v_new[h]) for h in hs]
        upd = [_mm_tn(blk(kd_ref, ci, h), v_new[h]) for h in hs]
        for h in hs:
            o_ref[0, h, ci * n:(ci + 1) * n, :] = o[h]
        st = [st[h] * jnp.exp(gc_all[ci][n - 1:n, h:h + 1]) + upd[h] for h in hs]
    for h in hs:
        st_scr[h] = st[h]


def _gdn_chunk(q, k, kb, vb, qd, kbe, kd, gc):
    bsz, _, seq, _ = q.shape
    n = GDN_CHUNK
    nc = seq // n
    per = CHUNKS_PER_STEP
    gc4 = gc.reshape(bsz, nc, n, N_HEADS)
    gct4 = jnp.swapaxes(gc4, 2, 3)
    return pl.pallas_call(
        _gdn_chunk_kernel, name="gdn_chunk",
        grid=(bsz, nc // per),
        in_specs=[_hm_spec(per * n)] * 7 + [pl.BlockSpec((1, per, n, N_HEADS), lambda b, i: (b, i, 0, 0)),
                                            pl.BlockSpec((1, per, N_HEADS, n), lambda b, i: (b, i, 0, 0))],
        out_specs=_hm_spec(per * n),
        out_shape=_hm_shape(bsz, seq),
        scratch_shapes=[pltpu.VMEM((N_HEADS, HEAD_DIM, HEAD_DIM), f32)],
        compiler_params=_cparams(("parallel", "arbitrary"), 32),
    )(q, k, kb, vb, qd, kbe, kd, gc4, gct4)


def _rwkv_prep_kernel(has_vres, *refs):
    if has_vres:
        (c_ref, halo_ref, mu_ref, w0_ref, wup_ref, a0_ref, aup_ref, gup_ref, kk_ref, ka_ref, bd_ref,
         rk_ref, cm_ref, vf_ref, v0_ref, vdn_ref, vup_ref,
         rt_ref, at_ref, bt_ref, kt_ref, bp_ref, kp_ref, v_ref, pe_ref, gout_ref, bonus_ref) = refs
    else:
        (c_ref, halo_ref, mu_ref, w0_ref, wup_ref, a0_ref, aup_ref, gup_ref, kk_ref, ka_ref, bd_ref,
         rk_ref, cm_ref,
         rt_ref, at_ref, bt_ref, kt_ref, bp_ref, kp_ref, v_ref, pe_ref, gout_ref, bonus_ref, cv_ref) = refs
    c, halo = _tile_and_halo(c_ref, halo_ref)
    prev = jnp.concatenate([halo[HALO_ROWS - 1:], c[:-1]], axis=0)
    c = c + (prev - c) * mu_ref[...]
    c_r, c_k, c_v = c[:, :MIX_W], c[:, MIX_W:2 * MIX_W], c[:, 2 * MIX_W:3 * MIX_W]
    c_wd = c[:, 3 * MIX_W:3 * MIX_W + 64]
    c_ad = c[:, 3 * MIX_W + 64:3 * MIX_W + 128]
    c_gd = c[:, 3 * MIX_W + 128:]
    w_log = -_softplus(-(w0_ref[...] + _mm(jnp.tanh(c_wd), wup_ref[...]))) - 0.5
    a_in = jax.nn.sigmoid(a0_ref[...] + _mm(c_ad, aup_ref[...]))
    gout_ref[...] = _mm(jax.nn.sigmoid(c_gd), gup_ref[...])
    if has_vres:
        lam = jax.nn.sigmoid(v0_ref[...] + _mm(_mm(c_v, vdn_ref[...]), vup_ref[...]))
        v_r = c_v + (vf_ref[...] - c_v) * lam
    else:
        v_r = c_v
        cv_ref[...] = c_v
    bd = bd_ref[...]
    kk = c_k * kk_ref[...]
    kk = kk * lax.rsqrt(_sel_right(kk * kk, bd) + L2_EPS)
    k_r = c_k * (1.0 + (a_in - 1.0) * ka_ref[...])
    b = kk * a_in
    bonus_ref[...] = _sel_right(c_r * k_r * rk_ref[...], bd) * v_r
    sums = _sel_left(cm_ref[...], -jnp.exp(w_log))
    lc, lc_end = sums[:ROW_TILE], sums[ROW_TILE:]
    e_neg = jnp.exp(-lc)
    e_rest = jnp.exp(lc_end - lc)
    _write_heads(rt_ref, c_r * jnp.exp(lc))
    _write_heads(at_ref, -kk * jnp.exp(lc + jnp.exp(w_log)))
    _write_heads(bt_ref, b * e_neg)
    _write_heads(kt_ref, k_r * e_neg)
    _write_heads(bp_ref, b * e_rest)
    _write_heads(kp_ref, k_r * e_rest)
    _write_heads(v_ref, v_r)
    _write_heads(pe_ref, jnp.exp(lc_end))


def _rwkv_prep(proj, bsz, seq, mu, w0, w_up, a0, a_up, g_up, k_k, k_a, r_k, bd_ones, cm, vres):
    nt = seq // ROW_TILE
    wc = 3 * MIX_W + 256
    std = pl.BlockSpec((ROW_TILE, MIX_W), lambda b, i: (b * nt + i, 0))
    std_shape = jax.ShapeDtypeStruct((bsz * seq, MIX_W), f32)
    in_specs = [_row_spec(wc, OFF_C, nt), _halo_spec(wc, OFF_C, nt), _const_spec((1, wc)),
                _const_spec((1, MIX_W)), _const_spec((64, MIX_W)), _const_spec((1, MIX_W)), _const_spec((64, MIX_W)),
                _const_spec((128, MIX_W)), _const_spec((1, MIX_W)), _const_spec((1, MIX_W)), _const_spec((MIX_W, MIX_W)),
                _const_spec((1, MIX_W)), _const_spec((2 * ROW_TILE, ROW_TILE))]
    args = [proj, proj, mu, w0, w_up, a0, a_up, g_up, k_k, k_a, bd_ones, r_k, cm]
    out_specs = [_hm_spec(ROW_TILE)] * 8 + [std, std]
    out_shape = [_hm_shape(bsz, seq, bf16)] * 7 + [_hm_shape(bsz, seq), std_shape, std_shape]
    if vres is not None:
        v_first, v0, v_down, v_up = vres
        in_specs += [std, _const_spec((1, MIX_W)), _const_spec(v_down.shape), _const_spec(v_up.shape)]
        args += [v_first, v0, v_down, v_up]
    else:
        out_specs.append(std)
        out_shape.append(std_shape)
    return pl.pallas_call(
        functools.partial(_rwkv_prep_kernel, vres is not None), name="rwkv_prep",
        grid=(bsz, nt), in_specs=in_specs, out_specs=out_specs, out_shape=out_shape,
        compiler_params=_cparams(("parallel", "parallel"), 40),
    )(*args)


def _rwkv_chunk_kernel(rt_ref, at_ref, bt_ref, kt_ref, bp_ref, kp_ref, v_ref, pe_ref, o_ref, st_scr):
    @pl.when(pl.program_id(1) == 0)
    def _():
        st_scr[...] = jnp.zeros_like(st_scr)

    n = RWKV_CHUNK
    hs = range(N_HEADS)
    ch = [(ci, h) for ci in range(CHUNKS_PER_STEP) for h in hs]
    blk = lambda ref, ci, h: ref[0, h, ci * n:(ci + 1) * n, :]
    row, col = _iota2((2 * n, 2 * n))
    rr, cc = row & (n - 1), col & (n - 1)
    mask = rr + jnp.where(row < n, 0, 1) > cc
    lhs = {x: jnp.concatenate([blk(at_ref, *x), blk(rt_ref, *x)], axis=0) for x in ch}
    gm = {x: jnp.where(mask, _mm_nt(lhs[x], jnp.concatenate([blk(bt_ref, *x), blk(kt_ref, *x)], axis=0)), 0.0)
          for x in ch}
    t = dict(zip(ch, _unit_lower_inverses([-gm[x][:n, :n] for x in ch])))
    v = {x: blk(v_ref, *x) for x in ch}
    makv = {x: _mm(gm[x][:n], jnp.concatenate([jnp.zeros_like(v[x]), v[x]], axis=0)) for x in ch}
    st = [st_scr[h] for h in hs]
    for ci in range(CHUNKS_PER_STEP):
        ah = [_mm_nt(lhs[ci, h], st[h]) for h in hs]
        u = [_mm(t[ci, h], ah[h][:n] + makv[ci, h]) for h in hs]
        uv = [jnp.concatenate([u[h].astype(bf16), v[ci, h]], axis=0) for h in hs]
        o = [ah[h][n:] + _mm(gm[ci, h][n:], uv[h]) for h in hs]
        upd = [_mm_tn(uv[h], jnp.concatenate([blk(bp_ref, ci, h), blk(kp_ref, ci, h)], axis=0)) for h in hs]
        for h in hs:
            o_ref[0, h, ci * n:(ci + 1) * n, :] = o[h]
        st = [st[h] * pe_ref[0, h, ci * n:ci * n + 1, :] + upd[h] for h in hs]
    for h in hs:
        st_scr[h] = st[h]


def _rwkv_chunk(rt, at, bt, kt, bp, kp, v, pe):
    bsz, _, seq, _ = rt.shape
    n = RWKV_CHUNK * CHUNKS_PER_STEP
    return pl.pallas_call(
        _rwkv_chunk_kernel, name="rwkv_chunk",
        grid=(bsz, seq // n),
        in_specs=[_hm_spec(n)] * 8,
        out_specs=_hm_spec(n),
        out_shape=_hm_shape(bsz, seq),
        scratch_shapes=[pltpu.VMEM((N_HEADS, HEAD_DIM, HEAD_DIM), f32)],
        compiler_params=_cparams(("parallel", "arbitrary"), 32),
    )(rt, at, bt, kt, bp, kp, v, pe)


def _ssd_prep_kernel(x_ref, halo_ref, sm_ref, cw_ref, cb_ref, alog_ref, dtb_ref, edt_ref,
                     xdt_ref, x_out_ref, bc_ref, adt_ref):
    xbc = _silu(_causal_conv(*_tile_and_halo(x_ref, halo_ref), cw_ref) + cb_ref[...])
    m_x = xbc[:, :MIX_W]
    dt = _softplus(sm_ref[...] + dtb_ref[...])
    _write_heads(xdt_ref, m_x * _sel_right(dt, edt_ref[...]))
    _write_heads(x_out_ref, m_x)
    bc_ref[...] = xbc[:, MIX_W:]
    adt_ref[...] = (dt * -jnp.exp(alog_ref[...]))[:, SM_DT:SM_DT + N_HEADS]


def _ssd_prep(proj, small, bsz, seq, conv_w, conv_b, alog128, dtb128, e_dt):
    nt = seq // ROW_TILE
    wx = MIX_W + 4 * SSM_STATE
    return pl.pallas_call(
        _ssd_prep_kernel, name="ssd_prep",
        grid=(bsz, nt),
        in_specs=[_row_spec(wx, OFF_DXBC, nt), _halo_spec(wx, OFF_DXBC, nt), _small_spec(nt),
                  _const_spec((CONV_K, wx)), _const_spec((1, wx)), _const_spec((1, LANES)), _const_spec((1, LANES)),
                  _const_spec((LANES, MIX_W))],
        out_specs=[_hm_spec(ROW_TILE)] * 2 + [pl.BlockSpec((ROW_TILE, 4 * SSM_STATE), lambda b, i: (b * nt + i, 0)),
                                              pl.BlockSpec((ROW_TILE, N_HEADS), lambda b, i: (b * nt + i, 0))],
        out_shape=[_hm_shape(bsz, seq)] * 2 + [jax.ShapeDtypeStruct((bsz * seq, 4 * SSM_STATE), f32),
                                               jax.ShapeDtypeStruct((bsz * seq, N_HEADS), f32)],
        compiler_params=_cparams(("parallel", "parallel"), 40),
    )(proj, proj, small, conv_w, conv_b, alog128, dtb128, e_dt)


def _ssd_chunk_kernel(xdt_ref, x_ref, bc_ref, a_ref, at_ref, dvec_ref, o_ref, st_scr):
    @pl.when(pl.program_id(1) == 0)
    def _():
        st_scr[...] = jnp.zeros_like(st_scr)

    n = SSD_CHUNK
    r, c = _iota2((n, n))
    tril = jnp.where(r >= c, 1.0, 0.0)
    acs_all = _mm_hi(tril, a_ref[0, 0])
    acst_all = _mm_nt_hi(at_ref[0, 0], tril)
    bc = bc_ref[...]
    heads_per_group = N_HEADS // 2
    for g in range(2):
        b_g = bc[:, g * SSM_STATE:(g + 1) * SSM_STATE]
        c_g = bc[:, (2 + g) * SSM_STATE:(3 + g) * SSM_STATE]
        cb = _mm_nt(c_g, b_g)
        for j in range(heads_per_group):
            h = g * heads_per_group + j
            ac = acs_all[:, h:h + 1]
            lmat = jnp.exp(jnp.where(r >= c, ac - acst_all[h:h + 1, :], NEG))
            xg = xdt_ref[0, h]
            st = st_scr[h]
            a_last = acs_all[n - 1:n, h:h + 1]
            y = _mm(cb * lmat, xg) + _mm(c_g * jnp.exp(ac), st)
            o_ref[0, h] = y + x_ref[0, h] * dvec_ref[h:h + 1, :]
            st_scr[h] = st * jnp.exp(a_last) + _mm_tn(b_g * jnp.exp(a_last - ac), xg)


def _ssd_chunk(xdt, x, bc, adt, dvec):
    bsz, _, seq, _ = xdt.shape
    n = SSD_CHUNK
    nc = seq // n
    a4 = adt.reshape(bsz, nc, n, N_HEADS)
    at4 = jnp.swapaxes(a4, 2, 3)
    return pl.pallas_call(
        _ssd_chunk_kernel, name="ssd_chunk",
        grid=(bsz, nc),
        in_specs=[_hm_spec(n)] * 2 + [pl.BlockSpec((n, 4 * SSM_STATE), lambda b, i: (b * nc + i, 0)),
                                      pl.BlockSpec((1, 1, n, N_HEADS), lambda b, i: (b, i, 0, 0)),
                                      pl.BlockSpec((1, 1, N_HEADS, n), lambda b, i: (b, i, 0, 0)),
                                      _const_spec((N_HEADS, HEAD_DIM))],
        out_specs=_hm_spec(n),
        out_shape=_hm_shape(bsz, seq),
        scratch_shapes=[pltpu.VMEM((N_HEADS, SSM_STATE, HEAD_DIM), f32)],
        compiler_params=_cparams(("parallel", "arbitrary"), 32),
    )(xdt, x, bc, a4, at4, dvec)


def _merge_kernel(x_ref, ya_ref, ob_ref, bz_ref, gnw_ref,
                  wkv_ref, bonus_ref, gout_ref, lnw_ref, lnb_ref,
                  yd_ref, dz_ref, mnw_ref, g0_ref, g1_ref, g2_ref, g3_ref, wb_ref, wo_ref, o_ref):
    def per_head(fn):
        return jnp.concatenate([fn(h) for h in range(N_HEADS)], axis=-1)

    def gdn_head(h):
        o = ob_ref[0, h]
        return o * lax.rsqrt(jnp.mean(o * o, axis=-1, keepdims=True) + RMS_EPS) * gnw_ref[...]

    y_b = per_head(gdn_head) * _silu(bz_ref[...].astype(f32))

    def wkv_head(h):
        w = wkv_ref[0, h]
        mu = jnp.mean(w, axis=-1, keepdims=True)
        var = jnp.mean(jnp.square(w - mu), axis=-1, keepdims=True)
        return (w - mu) * lax.rsqrt(var + RWKV_LN_EPS)

    y_c = (per_head(wkv_head) * lnw_ref[...] + lnb_ref[...] + bonus_ref[...]) * gout_ref[...]

    yz = _read_heads(yd_ref) * _silu(dz_ref[...].astype(f32))
    half = MIX_W // 2
    y_d = jnp.concatenate(
        [yz[:, s:s + half] * lax.rsqrt(jnp.mean(jnp.square(yz[:, s:s + half]), axis=-1, keepdims=True) + RMS_EPS)
         for s in (0, half)], axis=-1) * mnw_ref[...]

    acc = jnp.zeros(x_ref.shape, f32)
    for n, (y, g_ref) in enumerate(((ya_ref[0], g0_ref), (y_b, g1_ref), (y_c, g2_ref), (y_d, g3_ref))):
        acc = acc + jax.nn.sigmoid(g_ref[...].astype(f32)) * _mm(y, wb_ref[n])
    o_ref[...] = x_ref[...] + _mm(acc, wo_ref[...])


def _merge(x2, proj, bsz, seq, ya, ob, gnw, wkv, bonus, gout, lnw, lnb, yd, mnw, wb, wo):
    nt = seq // ROW_TILE
    std = lambda w: pl.BlockSpec((ROW_TILE, w), lambda b, i: (b * nt + i, 0))
    hm = _hm_spec(ROW_TILE)
    gate = lambda n: _row_spec(D_MODEL, OFF_GATES + n * D_MODEL, nt)
    return pl.pallas_call(
        _merge_kernel, name="merge",
        grid=(bsz, nt),
        in_specs=[std(D_MODEL), pl.BlockSpec((1, ROW_TILE, MIX_W), lambda b, i: (b, i, 0)),
                  hm, _row_spec(MIX_W, OFF_BZ, nt), _const_spec((1, HEAD_DIM)),
                  hm, std(MIX_W), std(MIX_W), _const_spec((1, MIX_W)), _const_spec((1, MIX_W)),
                  hm, _row_spec(MIX_W, OFF_DZ, nt), _const_spec((1, MIX_W)),
                  gate(0), gate(1), gate(2), gate(3),
                  _const_spec((4, MIX_W, D_MODEL)), _const_spec((D_MODEL, D_MODEL))],
        out_specs=std(D_MODEL),
        out_shape=jax.ShapeDtypeStruct((bsz * seq, D_MODEL), f32),
        compiler_params=_cparams(("parallel", "parallel"), 48),
    )(x2, ya, ob, proj, gnw, wkv, bonus, gout, lnw, lnb, yd, proj, mnw, proj, proj, proj, proj, wb, wo)


def _lane_vec(vals, off):
    return jnp.zeros((1, LANES), f32).at[0, off:off + vals.shape[0]].set(vals)


def _head_expand(off):
    n = jnp.arange(LANES)[:, None]
    c = jnp.arange(MIX_W)[None, :]
    return (n - off == c // HEAD_DIM).astype(bf16)


def _pack_w_in(w):
    pad = lambda n: jnp.zeros((w.shape[0], n), w.dtype)
    cols = [w[:, W_A:W_BZ],
            w[:, W_BZ:W_BBETA],
            w[:, W_C:W_DZ],
            w[:, W_BBETA:W_C], w[:, W_DDT:W_GATES], pad(2 * LANES - 3 * N_HEADS),
            w[:, W_DZ:W_DXBC], w[:, W_DXBC:W_DDT], w[:, W_GATES:]]
    out = jnp.concatenate(cols, axis=1).astype(bf16)
    assert out.shape[1] == N_PROJ
    return out


def kernel(x, rel_bias, norm1_w, w_in, moba_q_norm, moba_k_norm, gdn_conv_w, gdn_A_log, gdn_dt_bias, gdn_norm_w, rwkv_mu, rwkv_w0, rwkv_w_up, rwkv_a0, rwkv_a_up, rwkv_g_up, rwkv_k_k, rwkv_k_a, rwkv_r_k, rwkv_v0, rwkv_v_down, rwkv_v_up, rwkv_ln_w, rwkv_ln_b, mamba_conv_w, mamba_conv_b, mamba_dt_bias, mamba_A_log, mamba_D, mamba_norm_w, w_branch, w_out, norm2_w, ffn_w_in, ffn_w_down):
    bsz, seq, d = x.shape
    depth = w_in.shape[0]
    assert d == D_MODEL and (bsz * seq) % IN_PROJ_TILE_M == 0 and seq % MM_TILE_M == 0
    x2 = x.reshape(bsz * seq, d)
    row = lambda v: v.reshape(1, -1).astype(f32)

    hid = jnp.arange(MIX_W) // HEAD_DIM
    bd_ones = (hid[:, None] == hid[None, :]).astype(bf16)
    bd_mean = (bd_ones.astype(f32) / HEAD_DIM).astype(bf16)
    e_beta, e_ba, e_dt = _head_expand(SM_BETA), _head_expand(SM_BA), _head_expand(SM_DT)
    cm64 = _chunk_sum_matrix(GDN_CHUNK)
    assert GDN_CHUNK == RWKV_CHUNK
    tab = _moba_bias_tables(rel_bias)
    v_first = None
    for i in range(depth):
        proj, small = _in_proj(x2, row(norm1_w[i]), _pack_w_in(w_in[i]))

        qaug, kaug, v_a = _moba_prep(proj, bsz, seq, row(jnp.tile(moba_q_norm[i], N_HEADS)),
                                     row(jnp.tile(moba_k_norm[i], N_HEADS)), bd_mean)
        y_a = _moba_attn(qaug, kaug, v_a, tab)

        gdn_in = _gdn_prep(proj, small, bsz, seq, gdn_conv_w[i], _lane_vec(gdn_A_log[i], SM_BA),
                           _lane_vec(gdn_dt_bias[i], SM_BA), bd_ones, e_beta, e_ba, cm64)
        o_b = _gdn_chunk(*gdn_in)

        vres = None if i == 0 else (v_first, row(rwkv_v0[i - 1]), rwkv_v_down[i - 1].astype(bf16),
                                    rwkv_v_up[i - 1].astype(bf16))
        outs = _rwkv_prep(proj, bsz, seq, row(rwkv_mu[i]), row(rwkv_w0[i]), rwkv_w_up[i].astype(bf16),
                          row(rwkv_a0[i]), rwkv_a_up[i].astype(bf16), rwkv_g_up[i].astype(bf16),
                          row(rwkv_k_k[i]), row(rwkv_k_a[i]), row(rwkv_r_k[i]), bd_ones, cm64, vres)
        g_out, bonus = outs[8], outs[9]
        if i == 0:
            v_first = outs[10]
        wkv = _rwkv_chunk(*outs[:8])

        xdt, x_d, bc, adt = _ssd_prep(proj, small, bsz, seq, mamba_conv_w[i], row(mamba_conv_b[i]),
                                      _lane_vec(mamba_A_log[i], SM_DT), _lane_vec(mamba_dt_bias[i], SM_DT), e_dt)
        y_d = _ssd_chunk(xdt, x_d, bc, adt, jnp.broadcast_to(mamba_D[i][:, None], (N_HEADS, HEAD_DIM)).astype(f32))

        x2 = _merge(x2, proj, bsz, seq, y_a, o_b, row(gdn_norm_w[i]), wkv, bonus, g_out,
                    row(rwkv_ln_w[i]), row(rwkv_ln_b[i]), y_d, row(mamba_norm_w[i]),
                    w_branch[i].astype(bf16), w_out[i].astype(bf16))

        act = _ffn_up(x2, row(norm2_w[i]), ffn_w_in[i][:, :D_FF].astype(bf16), ffn_w_in[i][:, D_FF:].astype(bf16))
        x2 = _ffn_down(x2, act, ffn_w_down[i].astype(bf16))
    return x2.reshape(bsz, seq, d)
```

```python
import functools
import math

import jax
import jax.numpy as jnp
from jax import lax
from jax.experimental import pallas as pl
from jax.experimental.pallas import tpu as pltpu

f32, bf16 = jnp.float32, jnp.bfloat16
HI = lax.Precision.HIGHEST

D_MODEL = 1024
N_HEADS = 8
HEAD_DIM = 64
MIX_W = N_HEADS * HEAD_DIM
RMS_EPS = 1e-6
L2_EPS = 1e-6
CONV_K = 4
MOBA_BLOCK = 256
MOBA_TOPK = 3
MOBA_KV_PER_GROUP = 2
REL_BUCKETS = 32
REL_MAX_DIST = 128
GDN_CHUNK = 64
RWKV_CHUNK = 64
CHUNKS_PER_STEP = 4
RWKV_LN_EPS = 64e-5
SSM_STATE = 128
SSD_CHUNK = 128
SSD_CHUNKS_PER_STEP = 2
D_FF = 2816
NEG = -1e30
LOG2E = math.log2(math.e)

LANES = 128
BF16_SUBLANES = 16
VT_ROWS = HEAD_DIM + BF16_SUBLANES

OFF_A, OFF_BQKV, OFF_BZ, OFF_C, OFF_SMALL, OFF_DZ, OFF_DXBC, OFF_GATES = 0, 1536, 3072, 3584, 5376, 5632, 6144, 7168
N_PROJ = OFF_GATES + 4 * D_MODEL
SM_BETA, SM_BA, SM_DT = 0, 8, 16
W_A, W_BQKV, W_BZ, W_BBETA, W_BA, W_C, W_DZ, W_DXBC, W_DDT, W_GATES = 0, 1536, 3072, 3584, 3592, 3600, 5392, 5904, 6928, 6936

ROW_TILE = 256
MM_TILE_M = 512
IN_PROJ_TILE_M, IN_PROJ_TILE_N = 1024, 1024
HALO_ROWS = BF16_SUBLANES


def _cparams(sem, vmem_mb):
    return pltpu.CompilerParams(dimension_semantics=sem, vmem_limit_bytes=vmem_mb * 1024 * 1024)


def _mm(a, b):
    return jnp.dot(a.astype(bf16), b.astype(bf16), preferred_element_type=f32)


def _mm_nt(a, b):
    return lax.dot_general(a.astype(bf16), b.astype(bf16), (((1,), (1,)), ((), ())), preferred_element_type=f32)


def _mm_tn(a, b):
    return lax.dot_general(a.astype(bf16), b.astype(bf16), (((0,), (0,)), ((), ())), preferred_element_type=f32)


def _mm_hi(a, b):
    return jnp.dot(a, b, precision=HI, preferred_element_type=f32)


def _softplus(x):
    return jnp.maximum(x, 0.0) + jnp.log1p(jnp.exp(-jnp.abs(x)))


def _sigmoid(x):
    return jax.nn.sigmoid(x)


def _silu(x):
    return x * _sigmoid(x)


def _iota2(shape):
    return lax.broadcasted_iota(jnp.int32, shape, 0), lax.broadcasted_iota(jnp.int32, shape, 1)


def _split3(x):
    hi = x.astype(bf16)
    r1 = x - hi.astype(f32)
    mid = r1.astype(bf16)
    return hi, mid, (r1 - mid.astype(f32)).astype(bf16)


def _sel_left(m01, x):
    return sum(jnp.dot(m01, p, preferred_element_type=f32) for p in _split3(x))


def _sel_right(x, m01):
    return sum(jnp.dot(p, m01, preferred_element_type=f32) for p in _split3(x))


def _sel_right1(x, m01):
    return jnp.dot(x.astype(bf16), m01, preferred_element_type=f32)


def _unit_lower_inverses(l_list):
    n = l_list[0].shape[0]
    r, c = _iota2((n, n))
    eye = jnp.where(r == c, 1.0, 0.0)
    ts = [eye for _ in l_list]
    for ls in range(n.bit_length() - 1):
        m = ((r >> (ls + 1)) == (c >> (ls + 1))) & (((r >> ls) & 1) == 1) & (((c >> ls) & 1) == 0)
        lms = [jnp.where(m, l, 0.0) for l in l_list]
        if ls == 0:
            ts = [t - lm for t, lm in zip(ts, lms)]
        else:
            tl = [_mm(t, lm) for t, lm in zip(ts, lms)]
            ts = [t - _mm(x, t) for t, x in zip(ts, tl)]
    return ts


def _in_proj_kernel(x_ref, nw_ref, w_ref, o_ref, sm_ref, h_scr):
    j = pl.program_id(1)

    @pl.when(j == 0)
    def _():
        x = x_ref[...]
        y = x * lax.rsqrt(jnp.mean(x * x, axis=-1, keepdims=True) + RMS_EPS)
        h_scr[...] = (y * nw_ref[...]).astype(bf16)

    tn = o_ref.shape[1]
    w = w_ref[:, pl.ds(pl.multiple_of(j * tn, tn), tn)]
    acc = jnp.dot(h_scr[...], w, preferred_element_type=f32)
    o_ref[...] = acc.astype(o_ref.dtype)

    @pl.when(j == OFF_SMALL // tn)
    def _():
        sm_ref[...] = acc[:, OFF_SMALL % tn:OFF_SMALL % tn + LANES]


def _in_proj(x2, nw, w):
    t, d = x2.shape
    n = w.shape[1]
    tm, tn = IN_PROJ_TILE_M, IN_PROJ_TILE_N
    return pl.pallas_call(
        _in_proj_kernel, name="in_proj",
        grid=(t // tm, n // tn),
        in_specs=[pl.BlockSpec((tm, d), lambda i, j: (i, 0)),
                  pl.BlockSpec((1, d), lambda i, j: (0, 0)),
                  pl.BlockSpec((d, n), lambda i, j: (0, 0), pipeline_mode=pl.Buffered(1))],
        out_specs=[pl.BlockSpec((tm, tn), lambda i, j: (i, j)),
                   pl.BlockSpec((tm, LANES), lambda i, j: (i, 0))],
        out_shape=[jax.ShapeDtypeStruct((t, n), bf16), jax.ShapeDtypeStruct((t, LANES), f32)],
        scratch_shapes=[pltpu.VMEM((tm, d), bf16)],
        compiler_params=_cparams(("parallel", "arbitrary"), 48),
    )(x2, nw, w)


def _ffn_up_kernel(x_ref, nw_ref, wg_ref, wu_ref, o_ref, h_scr):
    @pl.when(pl.program_id(1) == 0)
    def _():
        x = x_ref[...]
        y = x * lax.rsqrt(jnp.mean(x * x, axis=-1, keepdims=True) + RMS_EPS)
        h_scr[...] = (y * nw_ref[...]).astype(bf16)
    h = h_scr[...]
    tn = o_ref.shape[1]
    cols = pl.ds(pl.multiple_of(pl.program_id(1) * tn, LANES), tn)
    g = jnp.dot(h, wg_ref[:, cols], preferred_element_type=f32)
    u = jnp.dot(h, wu_ref[:, cols], preferred_element_type=f32)
    o_ref[...] = (_silu(g) * u).astype(o_ref.dtype)


def _ffn_up(x2, nw, wg, wu):
    t, d = x2.shape
    n = wg.shape[1]
    tm, tn = MM_TILE_M, n // 2
    return pl.pallas_call(
        _ffn_up_kernel, name="ffn_up",
        grid=(t // tm, n // tn),
        in_specs=[pl.BlockSpec((tm, d), lambda i, j: (i, 0)),
                  pl.BlockSpec((1, d), lambda i, j: (0, 0)),
                  pl.BlockSpec((d, n), lambda i, j: (0, 0), pipeline_mode=pl.Buffered(1)),
                  pl.BlockSpec((d, n), lambda i, j: (0, 0), pipeline_mode=pl.Buffered(1))],
        out_specs=pl.BlockSpec((tm, tn), lambda i, j: (i, j)),
        out_shape=jax.ShapeDtypeStruct((t, n), bf16),
        scratch_shapes=[pltpu.VMEM((tm, d), bf16)],
        compiler_params=_cparams(("parallel", "arbitrary"), 48),
    )(x2, nw, wg, wu)


def _ffn_down_kernel(x_ref, a_ref, w_ref, o_ref):
    o_ref[...] = x_ref[...] + jnp.dot(a_ref[...], w_ref[...], preferred_element_type=f32)


def _ffn_down(x2, act, wd):
    t, d = x2.shape
    n = act.shape[1]
    tm = MM_TILE_M
    return pl.pallas_call(
        _ffn_down_kernel, name="ffn_down",
        grid=(t // tm,),
        in_specs=[pl.BlockSpec((tm, d), lambda i: (i, 0)),
                  pl.BlockSpec((tm, n), lambda i: (i, 0)),
                  pl.BlockSpec((n, d), lambda i: (0, 0))],
        out_specs=pl.BlockSpec((tm, d), lambda i: (i, 0)),
        out_shape=jax.ShapeDtypeStruct((t, d), f32),
        compiler_params=_cparams(("parallel",), 40),
    )(x2, act, wd)


def _top3_bias(gate_t, n_past):
    row = lax.broadcasted_iota(jnp.int32, gate_t.shape, 0)
    g = jnp.where(row < n_past, gate_t, -jnp.inf)
    sel = jnp.zeros(gate_t.shape, jnp.bool_)
    for _ in range(MOBA_TOPK):
        m = jnp.max(g, axis=0, keepdims=True)
        idx = jnp.min(jnp.where(g == m, row, gate_t.shape[0]), axis=0, keepdims=True)
        pick = row == idx
        sel = sel | (pick & (m > -jnp.inf))
        g = jnp.where(pick, -jnp.inf, g)
    return jnp.where(sel, 0.0, NEG)


def _moba_prep_kernel(a_ref, qw_ref, kw_ref, bd_ref, qaugt_ref, kaug_ref, vaugt_ref, kmean_scr):
    i = pl.program_id(1)

    @pl.when(i == 0)
    def _():
        kmean_scr[...] = jnp.zeros_like(kmean_scr)

    a = a_ref[...].astype(f32)
    q, k, v = a[:, :MIX_W], a[:, MIX_W:2 * MIX_W], a[:, 2 * MIX_W:]
    bd = bd_ref[...]
    qn = q * lax.rsqrt(_sel_right1(q * q, bd) + RMS_EPS) * qw_ref[...]
    kn = k * lax.rsqrt(_sel_right1(k * k, bd) + RMS_EPS) * kw_ref[...]
    lane = lax.broadcasted_iota(jnp.int32, (MOBA_BLOCK, LANES), 1)
    onehot = jnp.where(lane == i, 1.0, 0.0).astype(bf16)
    ones_row = jnp.where(lax.broadcasted_iota(jnp.int32, (VT_ROWS - HEAD_DIM, MOBA_BLOCK), 0) == 0, 1.0, 0.0)
    kmean = kmean_scr[...]
    nbp = kmean.shape[0]
    dim = lax.broadcasted_iota(jnp.int32, (LANES, MOBA_BLOCK), 0)
    sel_pad = jnp.zeros((LANES - nbp, MOBA_BLOCK), f32)
    for p in range(N_HEADS // 2):
        sl = slice(p * LANES, (p + 1) * LANES)
        kaug_ref[0, p] = jnp.concatenate([kn[:, sl].astype(bf16), onehot], axis=-1)
        vt = v[:, sl].T
        qt = (qn[:, sl] * (HEAD_DIM ** -0.5 * LOG2E)).T
        for hh in range(2):
            vaugt_ref[0, 2 * p + hh] = jnp.concatenate(
                [vt[hh * HEAD_DIM:(hh + 1) * HEAD_DIM], ones_row], axis=0).astype(bf16)
            keep = (dim < HEAD_DIM) if hh == 0 else (dim >= HEAD_DIM)
            q2t = jnp.where(keep, qt, 0.0)
            gate_t = _mm_hi(kmean[:, sl], q2t)
            qaugt_ref[0, 2 * p + hh] = jnp.concatenate([q2t, _top3_bias(gate_t, i), sel_pad], axis=0).astype(bf16)
    kmean_scr[pl.ds(i, 1), :] = jnp.mean(kn, axis=0, keepdims=True)


def _moba_prep(proj, bsz, seq, qw, kw, bd_mean):
    nb = seq // MOBA_BLOCK
    nbp = -(-nb // 8) * 8
    assert nbp <= LANES
    return pl.pallas_call(
        _moba_prep_kernel, name="moba_prep",
        grid=(bsz, nb),
        in_specs=[pl.BlockSpec((MOBA_BLOCK, 3 * MIX_W), lambda b, i: (b * nb + i, OFF_A // (3 * MIX_W))),
                  pl.BlockSpec((1, MIX_W), lambda b, i: (0, 0)),
                  pl.BlockSpec((1, MIX_W), lambda b, i: (0, 0)),
                  pl.BlockSpec((MIX_W, MIX_W), lambda b, i: (0, 0))],
        out_specs=[pl.BlockSpec((1, N_HEADS, 2 * LANES, MOBA_BLOCK), lambda b, i: (b, 0, 0, i)),
                   pl.BlockSpec((1, N_HEADS // 2, MOBA_BLOCK, 2 * LANES), lambda b, i: (b, 0, i, 0)),
                   pl.BlockSpec((1, N_HEADS, VT_ROWS, MOBA_BLOCK), lambda b, i: (b, 0, 0, i))],
        out_shape=[jax.ShapeDtypeStruct((bsz, N_HEADS, 2 * LANES, seq), bf16),
                   jax.ShapeDtypeStruct((bsz, N_HEADS // 2, seq, 2 * LANES), bf16),
                   jax.ShapeDtypeStruct((bsz, N_HEADS, VT_ROWS, seq), bf16)],
        scratch_shapes=[pltpu.VMEM((nbp, MIX_W), f32)],
        compiler_params=_cparams(("parallel", "arbitrary"), 32),
    )(proj, qw, kw, bd_mean)


def _moba_attn_kernel(qaugt_ref, kaug_ref, vaugt_ref, tabt_ref, o_ref, sa_scr, sb_scr):
    i = pl.program_id(2)
    blk = MOBA_BLOCK
    hs = (0, 1)
    key, qry = _iota2((blk, blk))
    mm = lambda a, b: jnp.dot(a, b, preferred_element_type=f32)
    cmax = lambda s: jnp.max(s, axis=0, keepdims=True)
    pv = lambda v, pe: jnp.dot(v, pe.astype(bf16), preferred_element_type=f32)
    kblk = lambda j: kaug_ref[0, 0, pl.ds(pl.multiple_of(j * blk, blk), blk), :]
    vblk = lambda hh, j: vaugt_ref[0, hh, :, pl.ds(pl.multiple_of(j * blk, blk), blk)]

    n_far = jnp.maximum(i - 1, 0)
    nk = MOBA_KV_PER_GROUP
    n_groups = (n_far + nk - 1) // nk
    blocks_of = lambda g: [jnp.where(nk * g + a < n_far, nk * g + a, i) for a in range(nk)]

    def scores(g, buf):
        qt = [qaugt_ref[0, hh] for hh in hs]
        for a, j in enumerate(blocks_of(g)):
            k = kblk(j)
            for hh in hs:
                buf[hh, a] = mm(k, qt[hh])

    scores(0, sa_scr)

    k_own, k_adj = kblk(i), kblk(n_far)
    qt = [qaugt_ref[0, hh] for hh in hs]
    s_own = [jnp.where(qry >= key, mm(k_own[:, :LANES], qt[hh][:LANES]) + tabt_ref[0, hh, 1], NEG) for hh in hs]
    s_adj = [mm(k_adj, qt[hh]) + tabt_ref[0, hh, 0] for hh in hs]
    m = [jnp.maximum(cmax(s_own[hh]), cmax(s_adj[hh])) for hh in hs]
    acc = [pv(vblk(hh, i), jnp.exp2(s_own[hh] - m[hh])) + pv(vblk(hh, n_far), jnp.exp2(s_adj[hh] - m[hh]))
           for hh in hs]

    def consume(g, buf, m, acc):
        js = blocks_of(g)
        s = [[buf[hh, a] for a in range(nk)] for hh in hs]
        m_new = [functools.reduce(jnp.maximum, [cmax(x) for x in s[hh]], m[hh]) for hh in hs]
        acc = [functools.reduce(lambda x, y: x + y,
                                [pv(vblk(hh, js[a]), jnp.exp2(s[hh][a] - m_new[hh])) for a in range(nk)],
                                jnp.exp2(m[hh] - m_new[hh]) * acc[hh])
               for hh in hs]
        return m_new, acc

    def body(u, carry):
        m, acc = list(carry[:2]), list(carry[2:])
        scores(2 * u + 1, sb_scr)
        m, acc = consume(2 * u, sa_scr, m, acc)
        scores(2 * u + 2, sa_scr)
        m, acc = consume(2 * u + 1, sb_scr, m, acc)
        return (*m, *acc)

    carry = lax.fori_loop(0, n_groups // 2, body, (*m, *acc))
    last = n_groups - 1
    carry = lax.cond(n_groups % 2 == 1,
                     lambda cr: (lambda r: (*r[0], *r[1]))(consume(last, sa_scr, list(cr[:2]), list(cr[2:]))),
                     lambda cr: cr, carry)
    acc = carry[2:]
    o_ref[0] = jnp.concatenate([(acc[hh][:HEAD_DIM] / acc[hh][HEAD_DIM:HEAD_DIM + 1]).T for hh in hs], axis=-1)


def _moba_attn(qaugt, kaug, vaugt, tabt):
    bsz, _, _, seq = qaugt.shape
    nb = seq // MOBA_BLOCK
    return pl.pallas_call(
        _moba_attn_kernel, name="moba_attn",
        grid=(bsz, N_HEADS // 2, nb),
        in_specs=[pl.BlockSpec((1, 2, 2 * LANES, MOBA_BLOCK), lambda b, p, i: (b, p, 0, i)),
                  pl.BlockSpec((1, 1, seq, 2 * LANES), lambda b, p, i: (b, p, 0, 0)),
                  pl.BlockSpec((1, 2, VT_ROWS, seq), lambda b, p, i: (b, p, 0, 0)),
                  pl.BlockSpec((1, 2, 2, MOBA_BLOCK, MOBA_BLOCK), lambda b, p, i: (p, 0, 0, 0, 0))],
        out_specs=pl.BlockSpec((1, MOBA_BLOCK, LANES), lambda b, p, i: (b, i, p)),
        out_shape=jax.ShapeDtypeStruct((bsz, seq, MIX_W), f32),
        scratch_shapes=[pltpu.VMEM((2, MOBA_KV_PER_GROUP, MOBA_BLOCK, MOBA_BLOCK), f32)] * 2,
        compiler_params=_cparams(("parallel", "parallel", "arbitrary"), 40),
    )(qaugt, kaug, vaugt, tabt)


def _t5_bucket(dist):
    n = jnp.maximum(dist, 0)
    max_exact = REL_BUCKETS // 2
    nf = jnp.maximum(n, max_exact).astype(f32)
    large = max_exact + (jnp.log(nf / max_exact) / math.log(REL_MAX_DIST / max_exact)
                         * (REL_BUCKETS - max_exact)).astype(jnp.int32)
    large = jnp.minimum(large, REL_BUCKETS - 1)
    return jnp.where(n < max_exact, n, large)


def _moba_bias_kernel(vec_ref, o_ref):
    blk = MOBA_BLOCK
    t = pltpu.roll(jnp.broadcast_to(vec_ref[0] * LOG2E, (blk, 2 * blk)), 0, 1, stride=1, stride_axis=0)
    o_ref[0, 0] = t[:, blk:]
    o_ref[0, 1] = t[:, :blk]


def _moba_bias_tables(rel_bias):
    assert MOBA_BLOCK >= REL_MAX_DIST
    by_dist = rel_bias.astype(f32)[_t5_bucket(jnp.arange(2 * MOBA_BLOCK))]
    far = rel_bias.astype(f32)[_t5_bucket(jnp.array(2 * MOBA_BLOCK))]
    vec = (by_dist - far).T.reshape(N_HEADS, 1, 2 * MOBA_BLOCK)
    tab = pl.pallas_call(
        _moba_bias_kernel, name="moba_bias",
        grid=(N_HEADS,),
        in_specs=[pl.BlockSpec((1, 1, 2 * MOBA_BLOCK), lambda h: (h, 0, 0))],
        out_specs=pl.BlockSpec((1, 2, MOBA_BLOCK, MOBA_BLOCK), lambda h: (h, 0, 0, 0)),
        out_shape=jax.ShapeDtypeStruct((N_HEADS, 2, MOBA_BLOCK, MOBA_BLOCK), f32),
        compiler_params=_cparams(("parallel",), 16),
    )(vec)
    return tab.reshape(N_HEADS // 2, 2, 2, MOBA_BLOCK, MOBA_BLOCK)


def _causal_conv(x, halo, w_ref):
    ts, nh = x.shape[0], halo.shape[0]
    xe = jnp.concatenate([halo, x], axis=0)
    acc = x * w_ref[CONV_K - 1:CONV_K, :]
    for d in range(1, CONV_K):
        acc = acc + xe[nh - d:nh - d + ts] * w_ref[CONV_K - 1 - d:CONV_K - d, :]
    return acc


def _tile_and_halo(x_ref, halo_ref):
    halo = jnp.where(pl.program_id(1) == 0, 0.0, halo_ref[...].astype(f32))
    return x_ref[...].astype(f32), halo


def _write_heads(o_ref, val):
    for h in range(N_HEADS):
        o_ref[0, h] = val[:, h * HEAD_DIM:(h + 1) * HEAD_DIM].astype(o_ref.dtype)


def _read_heads(ref):
    return jnp.concatenate([ref[0, h] for h in range(N_HEADS)], axis=-1)


def _hm_spec(rows):
    return pl.BlockSpec((1, N_HEADS, rows, HEAD_DIM), lambda b, i: (b, 0, i, 0))


def _hm_shape(bsz, seq, dtype=f32):
    return jax.ShapeDtypeStruct((bsz, N_HEADS, seq, HEAD_DIM), dtype)


def _chunk_sum_matrix(chunk):
    r = jnp.arange(ROW_TILE)[:, None]
    c = jnp.arange(ROW_TILE)[None, :]
    same = (r // chunk) == (c // chunk)
    return jnp.concatenate([same & (r >= c), same], axis=0).astype(bf16)


def _row_spec(width, off, nt):
    return pl.BlockSpec((ROW_TILE, width), lambda b, i: (b * nt + i, off // width))


def _halo_spec(width, off, nt):
    per = ROW_TILE // HALO_ROWS
    return pl.BlockSpec((HALO_ROWS, width), lambda b, i: (jnp.maximum((b * nt + i) * per - 1, 0), off // width))


def _small_spec(nt):
    return pl.BlockSpec((ROW_TILE, LANES), lambda b, i: (b * nt + i, 0))


def _const_spec(shape):
    return pl.BlockSpec(shape, lambda b, i: (0,) * len(shape))


def _gdn_prep_kernel(x_ref, halo_ref, sm_ref, cw_ref, alog_ref, dtb_ref, bd_ref, eb_ref, ea_ref, cm_ref,
                     q_ref, k_ref, kb_ref, vb_ref, qd_ref, kbe_ref, kd_ref, gc_ref):
    qkv = _silu(_causal_conv(*_tile_and_halo(x_ref, halo_ref), cw_ref))
    q, k, v = qkv[:, :MIX_W], qkv[:, MIX_W:2 * MIX_W], qkv[:, 2 * MIX_W:]
    bd = bd_ref[...]
    q = q * lax.rsqrt(_sel_right1(q * q, bd) + L2_EPS) * HEAD_DIM ** -0.5
    k = k * lax.rsqrt(_sel_right1(k * k, bd) + L2_EPS)
    sm = sm_ref[...]
    beta = _sel_right1(_sigmoid(sm), eb_ref[...])
    g = -jnp.exp(alog_ref[...]) * _softplus(sm + dtb_ref[...])
    sums = _sel_left(cm_ref[...], g)
    gc, g_end = sums[:ROW_TILE], sums[ROW_TILE:]
    eg = jnp.exp(_sel_right(gc, ea_ref[...]))
    e_rest = jnp.exp(_sel_right(g_end - gc, ea_ref[...]))
    kb = k * beta
    _write_heads(q_ref, q)
    _write_heads(k_ref, k)
    _write_heads(kb_ref, kb)
    _write_heads(vb_ref, v * beta)
    _write_heads(qd_ref, q * eg)
    _write_heads(kbe_ref, kb * eg)
    _write_heads(kd_ref, k * e_rest)
    gc_ref[...] = gc[:, SM_BA:SM_BA + N_HEADS]


def _gdn_prep(proj, small, bsz, seq, conv_w, alog128, dtb128, bd_ones, e_beta, e_ba, cm):
    nt = seq // ROW_TILE
    w3 = 3 * MIX_W
    return pl.pallas_call(
        _gdn_prep_kernel, name="gdn_prep",
        grid=(bsz, nt),
        in_specs=[_row_spec(w3, OFF_BQKV, nt), _halo_spec(w3, OFF_BQKV, nt), _small_spec(nt),
                  _const_spec((CONV_K, w3)), _const_spec((1, LANES)), _const_spec((1, LANES)),
                  _const_spec((MIX_W, MIX_W)), _const_spec((LANES, MIX_W)), _const_spec((LANES, MIX_W)),
                  _const_spec((2 * ROW_TILE, ROW_TILE))],
        out_specs=[_hm_spec(ROW_TILE)] * 7 + [pl.BlockSpec((ROW_TILE, N_HEADS), lambda b, i: (b * nt + i, 0))],
        out_shape=[_hm_shape(bsz, seq, bf16)] * 7 + [jax.ShapeDtypeStruct((bsz * seq, N_HEADS), f32)],
        compiler_params=_cparams(("parallel", "parallel"), 40),
    )(proj, proj, small, conv_w, alog128, dtb128, bd_ones, e_beta, e_ba, cm)


def _gdn_chunk_kernel(q_ref, k_ref, kb_ref, vb_ref, qd_ref, kbe_ref, kd_ref, gc_ref, gct_ref, nw_ref, o_ref, st_scr):
    @pl.when(pl.program_id(1) == 0)
    def _():
        st_scr[...] = jnp.zeros_like(st_scr)

    n = GDN_CHUNK
    hs = range(N_HEADS)
    ch = [(ci, h) for ci in range(CHUNKS_PER_STEP) for h in hs]
    blk = lambda ref, ci, h: ref[0, h, ci * n:(ci + 1) * n, :]
    r, c = _iota2((n, n))
    gc_all = [gc_ref[0, ci] for ci in range(CHUNKS_PER_STEP)]
    gct_all = [gct_ref[0, ci] for ci in range(CHUNKS_PER_STEP)]
    decay = {x: jnp.exp(jnp.where(r >= c, gc_all[x[0]][:, x[1]:x[1] + 1] - gct_all[x[0]][x[1]:x[1] + 1, :], NEG))
             for x in ch}
    gram = {x: _mm_nt(jnp.concatenate([blk(kb_ref, *x), blk(q_ref, *x)], axis=0), blk(k_ref, *x)) for x in ch}
    t = dict(zip(ch, _unit_lower_inverses([jnp.where(r > c, gram[x][:n] * decay[x], 0.0) for x in ch])))
    u = {x: _mm(t[x], blk(vb_ref, *x)) for x in ch}
    w = {x: _mm(t[x], blk(kbe_ref, *x)) for x in ch}
    a_in = {x: gram[x][n:] * decay[x] for x in ch}
    st = [st_scr[h] for h in hs]
    for ci in range(CHUNKS_PER_STEP):
        ws = [_mm(jnp.concatenate([w[ci, h].astype(bf16), blk(qd_ref, ci, h)], axis=0), st[h]) for h in hs]
        v_new = [u[ci, h] - ws[h][:n] for h in hs]
        o = [ws[h][n:] + _mm(a_in[ci, h], v_new[h]) for h in hs]
        upd = [_mm_tn(blk(kd_ref, ci, h), v_new[h]) for h in hs]
        for h in hs:
            on = o[h] * lax.rsqrt(jnp.mean(o[h] * o[h], axis=-1, keepdims=True) + RMS_EPS) * nw_ref[...]
            o_ref[ci * n:(ci + 1) * n, h * HEAD_DIM:(h + 1) * HEAD_DIM] = on
        st = [st[h] * jnp.exp(gc_all[ci][n - 1:n, h:h + 1]) + upd[h] for h in hs]
    for h in hs:
        st_scr[h] = st[h]


def _tok_spec(rows, nsteps):
    return pl.BlockSpec((rows, MIX_W), lambda b, i: (b * nsteps + i, 0))


def _gdn_chunk(q, k, kb, vb, qd, kbe, kd, gc, norm_w):
    bsz, _, seq, _ = q.shape
    n = GDN_CHUNK
    nc = seq // n
    per = CHUNKS_PER_STEP
    gc4 = gc.reshape(bsz, nc, n, N_HEADS)
    gct4 = jnp.swapaxes(gc4, 2, 3)
    return pl.pallas_call(
        _gdn_chunk_kernel, name="gdn_chunk",
        grid=(bsz, nc // per),
        in_specs=[_hm_spec(per * n)] * 7 + [pl.BlockSpec((1, per, n, N_HEADS), lambda b, i: (b, i, 0, 0)),
                                            pl.BlockSpec((1, per, N_HEADS, n), lambda b, i: (b, i, 0, 0)),
                                            _const_spec((1, HEAD_DIM))],
        out_specs=_tok_spec(per * n, nc // per),
        out_shape=jax.ShapeDtypeStruct((bsz * seq, MIX_W), f32),
        scratch_shapes=[pltpu.VMEM((N_HEADS, HEAD_DIM, HEAD_DIM), f32)],
        compiler_params=_cparams(("parallel", "arbitrary"), 32),
    )(q, k, kb, vb, qd, kbe, kd, gc4, gct4, norm_w)


def _rwkv_prep_kernel(has_vres, *refs):
    if has_vres:
        (c_ref, halo_ref, mu_ref, w0_ref, wup_ref, a0_ref, aup_ref, gup_ref, kk_ref, ka_ref, bd_ref,
         rk_ref, cm_ref, vf_ref, v0_ref, vdn_ref, vup_ref,
         rt_ref, at_ref, bt_ref, kt_ref, bp_ref, kp_ref, v_ref, pe_ref, gout_ref, bonus_ref) = refs
    else:
        (c_ref, halo_ref, mu_ref, w0_ref, wup_ref, a0_ref, aup_ref, gup_ref, kk_ref, ka_ref, bd_ref,
         rk_ref, cm_ref,
         rt_ref, at_ref, bt_ref, kt_ref, bp_ref, kp_ref, v_ref, pe_ref, gout_ref, bonus_ref, cv_ref) = refs
    c, halo = _tile_and_halo(c_ref, halo_ref)
    prev = jnp.concatenate([halo[HALO_ROWS - 1:], c[:-1]], axis=0)
    c = c + (prev - c) * mu_ref[...]
    c_r, c_k, c_v = c[:, :MIX_W], c[:, MIX_W:2 * MIX_W], c[:, 2 * MIX_W:3 * MIX_W]
    c_wd = c[:, 3 * MIX_W:3 * MIX_W + 64]
    c_ad = c[:, 3 * MIX_W + 64:3 * MIX_W + 128]
    c_gd = c[:, 3 * MIX_W + 128:]
    w_log = -_softplus(-(w0_ref[...] + _mm(jnp.tanh(c_wd), wup_ref[...]))) - 0.5
    a_in = _sigmoid(a0_ref[...] + _mm(c_ad, aup_ref[...]))
    gout_ref[...] = _mm(_sigmoid(c_gd), gup_ref[...])
    if has_vres:
        lam = _sigmoid(v0_ref[...] + _mm(_mm(c_v, vdn_ref[...]), vup_ref[...]))
        v_r = c_v + (vf_ref[...] - c_v) * lam
    else:
        v_r = c_v
        cv_ref[...] = c_v
    bd = bd_ref[...]
    kk = c_k * kk_ref[...]
    kk = kk * lax.rsqrt(_sel_right1(kk * kk, bd) + L2_EPS)
    k_r = c_k * (1.0 + (a_in - 1.0) * ka_ref[...])
    b = kk * a_in
    bonus_ref[...] = _sel_right1(c_r * k_r * rk_ref[...], bd) * v_r
    sums = _sel_left(cm_ref[...], -jnp.exp(w_log))
    lc, lc_end = sums[:ROW_TILE], sums[ROW_TILE:]
    e_neg = jnp.exp(-lc)
    e_rest = jnp.exp(lc_end - lc)
    _write_heads(rt_ref, c_r * jnp.exp(lc))
    _write_heads(at_ref, -kk * jnp.exp(lc + jnp.exp(w_log)))
    _write_heads(bt_ref, b * e_neg)
    _write_heads(kt_ref, k_r * e_neg)
    _write_heads(bp_ref, b * e_rest)
    _write_heads(kp_ref, k_r * e_rest)
    _write_heads(v_ref, v_r)
    _write_heads(pe_ref, jnp.exp(lc_end))


def _rwkv_prep(proj, bsz, seq, mu, w0, w_up, a0, a_up, g_up, k_k, k_a, r_k, bd_ones, cm, vres):
    nt = seq // ROW_TILE
    wc = 3 * MIX_W + 256
    std = pl.BlockSpec((ROW_TILE, MIX_W), lambda b, i: (b * nt + i, 0))
    std_shape = jax.ShapeDtypeStruct((bsz * seq, MIX_W), f32)
    in_specs = [_row_spec(wc, OFF_C, nt), _halo_spec(wc, OFF_C, nt), _const_spec((1, wc)),
                _const_spec((1, MIX_W)), _const_spec((64, MIX_W)), _const_spec((1, MIX_W)), _const_spec((64, MIX_W)),
                _const_spec((128, MIX_W)), _const_spec((1, MIX_W)), _const_spec((1, MIX_W)), _const_spec((MIX_W, MIX_W)),
                _const_spec((1, MIX_W)), _const_spec((2 * ROW_TILE, ROW_TILE))]
    args = [proj, proj, mu, w0, w_up, a0, a_up, g_up, k_k, k_a, bd_ones, r_k, cm]
    out_specs = [_hm_spec(ROW_TILE)] * 8 + [std, std]
    out_shape = [_hm_shape(bsz, seq, bf16)] * 7 + [_hm_shape(bsz, seq), std_shape, std_shape]
    if vres is not None:
        v_first, v0, v_down, v_up = vres
        in_specs += [std, _const_spec((1, MIX_W)), _const_spec(v_down.shape), _const_spec(v_up.shape)]
        args += [v_first, v0, v_down, v_up]
    else:
        out_specs.append(std)
        out_shape.append(std_shape)
    return pl.pallas_call(
        functools.partial(_rwkv_prep_kernel, vres is not None), name="rwkv_prep",
        grid=(bsz, nt), in_specs=in_specs, out_specs=out_specs, out_shape=out_shape,
        compiler_params=_cparams(("parallel", "parallel"), 40),
    )(*args)


def _rwkv_chunk_kernel(rt_ref, at_ref, bt_ref, kt_ref, bp_ref, kp_ref, v_ref, pe_ref, o_ref, st_scr):
    @pl.when(pl.program_id(1) == 0)
    def _():
        st_scr[...] = jnp.zeros_like(st_scr)

    n = RWKV_CHUNK
    hs = range(N_HEADS)
    ch = [(ci, h) for ci in range(CHUNKS_PER_STEP) for h in hs]
    blk = lambda ref, ci, h: ref[0, h, ci * n:(ci + 1) * n, :]
    row, col = _iota2((2 * n, 2 * n))
    rr, cc = row & (n - 1), col & (n - 1)
    mask = rr + jnp.where(row < n, 0, 1) > cc
    lhs = {x: jnp.concatenate([blk(at_ref, *x), blk(rt_ref, *x)], axis=0) for x in ch}
    gm = {x: jnp.where(mask, _mm_nt(lhs[x], jnp.concatenate([blk(bt_ref, *x), blk(kt_ref, *x)], axis=0)), 0.0)
          for x in ch}
    t = dict(zip(ch, _unit_lower_inverses([-gm[x][:n, :n] for x in ch])))
    v = {x: blk(v_ref, *x) for x in ch}
    makv = {x: _mm(gm[x][:n], jnp.concatenate([jnp.zeros_like(v[x]), v[x]], axis=0)) for x in ch}
    st = [st_scr[h] for h in hs]
    for ci in range(CHUNKS_PER_STEP):
        ah = [_mm_nt(lhs[ci, h], st[h]) for h in hs]
        u = [_mm(t[ci, h], ah[h][:n] + makv[ci, h]) for h in hs]
        uv = [jnp.concatenate([u[h].astype(bf16), v[ci, h]], axis=0) for h in hs]
        o = [ah[h][n:] + _mm(gm[ci, h][n:], uv[h]) for h in hs]
        upd = [_mm_tn(uv[h], jnp.concatenate([blk(bp_ref, ci, h), blk(kp_ref, ci, h)], axis=0)) for h in hs]
        for h in hs:
            o_ref[0, h, ci * n:(ci + 1) * n, :] = o[h]
        st = [st[h] * pe_ref[0, h, ci * n:ci * n + 1, :] + upd[h] for h in hs]
    for h in hs:
        st_scr[h] = st[h]


def _rwkv_chunk(rt, at, bt, kt, bp, kp, v, pe):
    bsz, _, seq, _ = rt.shape
    n = RWKV_CHUNK * CHUNKS_PER_STEP
    return pl.pallas_call(
        _rwkv_chunk_kernel, name="rwkv_chunk",
        grid=(bsz, seq // n),
        in_specs=[_hm_spec(n)] * 8,
        out_specs=_hm_spec(n),
        out_shape=_hm_shape(bsz, seq),
        scratch_shapes=[pltpu.VMEM((N_HEADS, HEAD_DIM, HEAD_DIM), f32)],
        compiler_params=_cparams(("parallel", "arbitrary"), 32),
    )(rt, at, bt, kt, bp, kp, v, pe)


def _ssd_prep_kernel(x_ref, halo_ref, sm_ref, cw_ref, cb_ref, alog_ref, dtb_ref, edt_ref, cm_ref,
                     xdt_ref, x_out_ref, bc_ref, acs_ref):
    xbc = _silu(_causal_conv(*_tile_and_halo(x_ref, halo_ref), cw_ref) + cb_ref[...])
    m_x = xbc[:, :MIX_W]
    dt = _softplus(sm_ref[...] + dtb_ref[...])
    _write_heads(xdt_ref, m_x * _sel_right1(dt, edt_ref[...]))
    _write_heads(x_out_ref, m_x)
    bc_ref[...] = xbc[:, MIX_W:].astype(bc_ref.dtype)
    acs = _sel_left(cm_ref[...], dt * -jnp.exp(alog_ref[...]))[:ROW_TILE]
    acs_ref[...] = acs[:, SM_DT:SM_DT + N_HEADS]


def _ssd_prep(proj, small, bsz, seq, conv_w, conv_b, alog128, dtb128, e_dt, cm):
    nt = seq // ROW_TILE
    wx = MIX_W + 4 * SSM_STATE
    return pl.pallas_call(
        _ssd_prep_kernel, name="ssd_prep",
        grid=(bsz, nt),
        in_specs=[_row_spec(wx, OFF_DXBC, nt), _halo_spec(wx, OFF_DXBC, nt), _small_spec(nt),
                  _const_spec((CONV_K, wx)), _const_spec((1, wx)), _const_spec((1, LANES)), _const_spec((1, LANES)),
                  _const_spec((LANES, MIX_W)), _const_spec((2 * ROW_TILE, ROW_TILE))],
        out_specs=[_hm_spec(ROW_TILE)] * 2 + [pl.BlockSpec((ROW_TILE, 4 * SSM_STATE), lambda b, i: (b * nt + i, 0)),
                                              pl.BlockSpec((ROW_TILE, N_HEADS), lambda b, i: (b * nt + i, 0))],
        out_shape=[_hm_shape(bsz, seq, bf16), _hm_shape(bsz, seq),
                   jax.ShapeDtypeStruct((bsz * seq, 4 * SSM_STATE), bf16),
                   jax.ShapeDtypeStruct((bsz * seq, N_HEADS), f32)],
        compiler_params=_cparams(("parallel", "parallel"), 40),
    )(proj, proj, small, conv_w, conv_b, alog128, dtb128, e_dt, cm)


def _ssd_chunk_kernel(xdt_ref, x_ref, bc_ref, a_ref, at_ref, dvec_ref, o_ref, st_scr):
    @pl.when(pl.program_id(1) == 0)
    def _():
        st_scr[...] = jnp.zeros_like(st_scr)

    n = SSD_CHUNK
    hs = range(N_HEADS)
    cis = range(SSD_CHUNKS_PER_STEP)
    heads_per_group = N_HEADS // 2
    grp = lambda h: h // heads_per_group
    ch = [(ci, h) for ci in cis for h in hs]
    rows = lambda ci: slice(ci * n, (ci + 1) * n)
    r, c = _iota2((n, n))
    b_g = {(ci, g): bc_ref[rows(ci), g * SSM_STATE:(g + 1) * SSM_STATE] for ci in cis for g in range(2)}
    c_g = {(ci, g): bc_ref[rows(ci), (2 + g) * SSM_STATE:(3 + g) * SSM_STATE] for ci in cis for g in range(2)}
    cb = {x: _mm_nt(c_g[x], b_g[x]) for x in b_g}
    ac = {(ci, h): a_ref[0, ci][:, h:h + 1] for ci, h in ch}
    a_last = {(ci, h): a_ref[0, ci][n - 1:n, h:h + 1] for ci, h in ch}
    lmat = {(ci, h): jnp.exp(jnp.where(r >= c, ac[ci, h] - at_ref[0, ci][h:h + 1, :], NEG)) for ci, h in ch}
    xg = {(ci, h): xdt_ref[0, h, rows(ci), :] for ci, h in ch}
    y_diag = {(ci, h): _mm(cb[ci, grp(h)] * lmat[ci, h], xg[ci, h]) for ci, h in ch}
    upd = {(ci, h): _mm_tn(b_g[ci, grp(h)].astype(f32) * jnp.exp(a_last[ci, h] - ac[ci, h]), xg[ci, h]) for ci, h in ch}
    c_in = {(ci, h): c_g[ci, grp(h)].astype(f32) * jnp.exp(ac[ci, h]) for ci, h in ch}
    st = [st_scr[h] for h in hs]
    for ci in cis:
        y_off = [_mm(c_in[ci, h], st[h]) for h in hs]
        for h in hs:
            o_ref[rows(ci), h * HEAD_DIM:(h + 1) * HEAD_DIM] = (
                y_diag[ci, h] + y_off[h] + x_ref[0, h, rows(ci), :] * dvec_ref[h:h + 1, :])
        st = [st[h] * jnp.exp(a_last[ci, h]) + upd[ci, h] for h in hs]
    for h in hs:
        st_scr[h] = st[h]


def _ssd_chunk(xdt, x, bc, acs, dvec):
    bsz, _, seq, _ = xdt.shape
    n = SSD_CHUNK
    nc = seq // n
    per = SSD_CHUNKS_PER_STEP
    a4 = acs.reshape(bsz, nc, n, N_HEADS)
    at4 = jnp.swapaxes(a4, 2, 3)
    return pl.pallas_call(
        _ssd_chunk_kernel, name="ssd_chunk",
        grid=(bsz, nc // per),
        in_specs=[_hm_spec(per * n)] * 2 + [pl.BlockSpec((per * n, 4 * SSM_STATE), lambda b, i: (b * (nc // per) + i, 0)),
                                            pl.BlockSpec((1, per, n, N_HEADS), lambda b, i: (b, i, 0, 0)),
                                            pl.BlockSpec((1, per, N_HEADS, n), lambda b, i: (b, i, 0, 0)),
                                            _const_spec((N_HEADS, HEAD_DIM))],
        out_specs=_tok_spec(per * n, nc // per),
        out_shape=jax.ShapeDtypeStruct((bsz * seq, MIX_W), f32),
        scratch_shapes=[pltpu.VMEM((N_HEADS, SSM_STATE, HEAD_DIM), f32)],
        compiler_params=_cparams(("parallel", "arbitrary"), 32),
    )(xdt, x, bc, a4, at4, dvec)


def _merge_kernel(x_ref, ya_ref, ob_ref, bz_ref, wkv_ref, bonus_ref, gout_ref, lnw_ref, lnb_ref,
                  yd_ref, dz_ref, mnw_ref, g0_ref, g1_ref, g2_ref, g3_ref, wb_ref, wo_ref, o_ref):
    def gated(n, y, g_ref):
        return _sigmoid(g_ref[...].astype(f32)) * _mm(y, wb_ref[n])

    acc = gated(0, ya_ref[0], g0_ref)
    acc = acc + gated(1, ob_ref[...] * _silu(bz_ref[...].astype(f32)), g1_ref)

    def wkv_head(h):
        w = wkv_ref[0, h]
        mu = jnp.mean(w, axis=-1, keepdims=True)
        var = jnp.mean(jnp.square(w - mu), axis=-1, keepdims=True)
        return (w - mu) * lax.rsqrt(var + RWKV_LN_EPS)

    wkv_ln = jnp.concatenate([wkv_head(h) for h in range(N_HEADS)], axis=-1)
    y_c = (wkv_ln * lnw_ref[...] + lnb_ref[...] + bonus_ref[...]) * gout_ref[...]
    acc = acc + gated(2, y_c, g2_ref)

    yz = yd_ref[...] * _silu(dz_ref[...].astype(f32))
    half = MIX_W // 2
    y_d = jnp.concatenate(
        [yz[:, s:s + half] * lax.rsqrt(jnp.mean(jnp.square(yz[:, s:s + half]), axis=-1, keepdims=True) + RMS_EPS)
         for s in (0, half)], axis=-1) * mnw_ref[...]
    acc = acc + gated(3, y_d, g3_ref)
    o_ref[...] = x_ref[...] + _mm(acc, wo_ref[...])


def _merge(x2, proj, bsz, seq, ya, ob, wkv, bonus, gout, lnw, lnb, yd, mnw, wb, wo):
    nt = seq // ROW_TILE
    std = lambda w: pl.BlockSpec((ROW_TILE, w), lambda b, i: (b * nt + i, 0))
    gate = lambda n: _row_spec(D_MODEL, OFF_GATES + n * D_MODEL, nt)
    return pl.pallas_call(
        _merge_kernel, name="merge",
        grid=(bsz, nt),
        in_specs=[std(D_MODEL), pl.BlockSpec((1, ROW_TILE, MIX_W), lambda b, i: (b, i, 0)),
                  std(MIX_W), _row_spec(MIX_W, OFF_BZ, nt),
                  _hm_spec(ROW_TILE), std(MIX_W), std(MIX_W), _const_spec((1, MIX_W)), _const_spec((1, MIX_W)),
                  std(MIX_W), _row_spec(MIX_W, OFF_DZ, nt), _const_spec((1, MIX_W)),
                  gate(0), gate(1), gate(2), gate(3),
                  _const_spec((4, MIX_W, D_MODEL)), _const_spec((D_MODEL, D_MODEL))],
        out_specs=std(D_MODEL),
        out_shape=jax.ShapeDtypeStruct((bsz * seq, D_MODEL), f32),
        compiler_params=_cparams(("parallel", "parallel"), 48),
    )(x2, ya, ob, proj, wkv, bonus, gout, lnw, lnb, yd, proj, mnw, proj, proj, proj, proj, wb, wo)


def _lane_vec(vals, off):
    return jnp.zeros((1, LANES), f32).at[0, off:off + vals.shape[0]].set(vals)


def _head_expand(off):
    n = jnp.arange(LANES)[:, None]
    c = jnp.arange(MIX_W)[None, :]
    return (n - off == c // HEAD_DIM).astype(bf16)


def _pack_w_in(w):
    pad = lambda n: jnp.zeros((w.shape[0], n), w.dtype)
    cols = [w[:, W_A:W_BZ],
            w[:, W_BZ:W_BBETA],
            w[:, W_C:W_DZ],
            w[:, W_BBETA:W_C], w[:, W_DDT:W_GATES], pad(2 * LANES - 3 * N_HEADS),
            w[:, W_DZ:W_DXBC], w[:, W_DXBC:W_DDT], w[:, W_GATES:]]
    out = jnp.concatenate(cols, axis=1).astype(bf16)
    assert out.shape[1] == N_PROJ
    return out


def kernel(x, rel_bias, norm1_w, w_in, moba_q_norm, moba_k_norm, gdn_conv_w, gdn_A_log, gdn_dt_bias, gdn_norm_w, rwkv_mu, rwkv_w0, rwkv_w_up, rwkv_a0, rwkv_a_up, rwkv_g_up, rwkv_k_k, rwkv_k_a, rwkv_r_k, rwkv_v0, rwkv_v_down, rwkv_v_up, rwkv_ln_w, rwkv_ln_b, mamba_conv_w, mamba_conv_b, mamba_dt_bias, mamba_A_log, mamba_D, mamba_norm_w, w_branch, w_out, norm2_w, ffn_w_in, ffn_w_down):
    bsz, seq, d = x.shape
    depth = w_in.shape[0]
    assert d == D_MODEL and (bsz * seq) % IN_PROJ_TILE_M == 0 and seq % MM_TILE_M == 0
    x2 = x.reshape(bsz * seq, d)
    row = lambda v: v.reshape(1, -1).astype(f32)

    hid = jnp.arange(MIX_W) // HEAD_DIM
    bd_ones = (hid[:, None] == hid[None, :]).astype(bf16)
    bd_mean = (bd_ones.astype(f32) / HEAD_DIM).astype(bf16)
    e_beta, e_ba, e_dt = _head_expand(SM_BETA), _head_expand(SM_BA), _head_expand(SM_DT)
    cm64, cm128 = _chunk_sum_matrix(GDN_CHUNK), _chunk_sum_matrix(SSD_CHUNK)
    assert GDN_CHUNK == RWKV_CHUNK
    tab = _moba_bias_tables(rel_bias)
    v_first = None
    for i in range(depth):
        proj, small = _in_proj(x2, row(norm1_w[i]), _pack_w_in(w_in[i]))

        qaug, kaug, v_a = _moba_prep(proj, bsz, seq, row(jnp.tile(moba_q_norm[i], N_HEADS)),
                                     row(jnp.tile(moba_k_norm[i], N_HEADS)), bd_mean)
        y_a = _moba_attn(qaug, kaug, v_a, tab)

        gdn_in = _gdn_prep(proj, small, bsz, seq, gdn_conv_w[i], _lane_vec(gdn_A_log[i], SM_BA),
                           _lane_vec(gdn_dt_bias[i], SM_BA), bd_ones, e_beta, e_ba, cm64)
        o_b = _gdn_chunk(*gdn_in, row(gdn_norm_w[i]))

        vres = None if i == 0 else (v_first, row(rwkv_v0[i - 1]), rwkv_v_down[i - 1].astype(bf16),
                                    rwkv_v_up[i - 1].astype(bf16))
        outs = _rwkv_prep(proj, bsz, seq, row(rwkv_mu[i]), row(rwkv_w0[i]), rwkv_w_up[i].astype(bf16),
                          row(rwkv_a0[i]), rwkv_a_up[i].astype(bf16), rwkv_g_up[i].astype(bf16),
                          row(rwkv_k_k[i]), row(rwkv_k_a[i]), row(rwkv_r_k[i]), bd_ones, cm64, vres)
        g_out, bonus = outs[8], outs[9]
        if i == 0:
            v_first = outs[10]
        wkv = _rwkv_chunk(*outs[:8])

        xdt, x_d, bc, acs = _ssd_prep(proj, small, bsz, seq, mamba_conv_w[i], row(mamba_conv_b[i]),
                                      _lane_vec(mamba_A_log[i], SM_DT), _lane_vec(mamba_dt_bias[i], SM_DT), e_dt, cm128)
        y_d = _ssd_chunk(xdt, x_d, bc, acs, jnp.broadcast_to(mamba_D[i][:, None], (N_HEADS, HEAD_DIM)).astype(f32))

        x2 = _merge(x2, proj, bsz, seq, y_a, o_b, wkv, bonus, g_out,
                    row(rwkv_ln_w[i]), row(rwkv_ln_b[i]), y_d, row(mamba_norm_w[i]),
                    w_branch[i].astype(bf16), w_out[i].astype(bf16))

        act = _ffn_up(x2, row(norm2_w[i]), ffn_w_in[i][:, :D_FF].astype(bf16), ffn_w_in[i][:, D_FF:].astype(bf16))
        x2 = _ffn_down(x2, act, ffn_w_down[i].astype(bf16))
    return x2.reshape(bsz, seq, d)
```

```python
import functools
import math

import jax
import jax.numpy as jnp
from jax import lax
from jax.experimental import pallas as pl
from jax.experimental.pallas import tpu as pltpu

f32, bf16 = jnp.float32, jnp.bfloat16
HI = lax.Precision.HIGHEST

D_MODEL = 1024
N_HEADS = 8
HEAD_DIM = 64
MIX_W = N_HEADS * HEAD_DIM
RMS_EPS = 1e-6
L2_EPS = 1e-6
CONV_K = 4
MOBA_BLOCK = 256
MOBA_TOPK = 3
MOBA_PAIRS_PER_STEP = 2
MOBA_KV_PER_GROUP = 2
REL_BUCKETS = 32
REL_MAX_DIST = 128
GDN_CHUNK = 64
RWKV_CHUNK = 64
CHUNKS_PER_STEP = 4
RWKV_LN_EPS = 64e-5
SSM_STATE = 128
SSD_CHUNK = 128
SSD_CHUNKS_PER_STEP = 2
D_FF = 2816
FFN_HIDDEN_SPLITS = 2
NEG = -1e30
LOG2E = math.log2(math.e)

LANES = 128
BF16_SUBLANES = 16
VT_ROWS = HEAD_DIM + BF16_SUBLANES

OFF_A, OFF_BQKV, OFF_BZ, OFF_C, OFF_SMALL, OFF_DZ, OFF_DXBC, OFF_GATES = 0, 1536, 3072, 3584, 5376, 5632, 6144, 7168
N_PROJ = OFF_GATES + 4 * D_MODEL
SM_BETA, SM_BA, SM_DT = 0, 8, 16
W_A, W_BQKV, W_BZ, W_BBETA, W_BA, W_C, W_DZ, W_DXBC, W_DDT, W_GATES = 0, 1536, 3072, 3584, 3592, 3600, 5392, 5904, 6928, 6936

ROW_TILE = 512
CUMSUM_TILE = 256
MM_TILE_M = 512
IN_PROJ_TILE_M, IN_PROJ_TILE_N = 1024, 1024
HALO_ROWS = BF16_SUBLANES


def _cparams(sem, vmem_mb):
    return pltpu.CompilerParams(dimension_semantics=sem, vmem_limit_bytes=vmem_mb * 1024 * 1024)


def _mm(a, b):
    return jnp.dot(a.astype(bf16), b.astype(bf16), preferred_element_type=f32)


def _mm_nt(a, b):
    return lax.dot_general(a.astype(bf16), b.astype(bf16), (((1,), (1,)), ((), ())), preferred_element_type=f32)


def _mm_tn(a, b):
    return lax.dot_general(a.astype(bf16), b.astype(bf16), (((0,), (0,)), ((), ())), preferred_element_type=f32)


def _mm_hi(a, b):
    return jnp.dot(a, b, precision=HI, preferred_element_type=f32)


def _softplus(x):
    return jnp.maximum(x, 0.0) + jnp.log1p(jnp.exp(-jnp.abs(x)))


def _sigmoid(x):
    return jax.nn.sigmoid(x)


def _silu(x):
    return x * _sigmoid(x)


def _iota2(shape):
    return lax.broadcasted_iota(jnp.int32, shape, 0), lax.broadcasted_iota(jnp.int32, shape, 1)


def _split3(x):
    hi = x.astype(bf16)
    r1 = x - hi.astype(f32)
    mid = r1.astype(bf16)
    return hi, mid, (r1 - mid.astype(f32)).astype(bf16)


def _sel_left(m01, x):
    return sum(jnp.dot(m01, p, preferred_element_type=f32) for p in _split3(x))


def _sel_right(x, m01):
    return sum(jnp.dot(p, m01, preferred_element_type=f32) for p in _split3(x))


def _sel_right1(x, m01):
    return jnp.dot(x.astype(bf16), m01, preferred_element_type=f32)


def _unit_lower_inverses(l_list):
    n = l_list[0].shape[0]
    r, c = _iota2((n, n))
    eye = jnp.where(r == c, 1.0, 0.0)
    ts = [eye for _ in l_list]
    for ls in range(n.bit_length() - 1):
        m = ((r >> (ls + 1)) == (c >> (ls + 1))) & (((r >> ls) & 1) == 1) & (((c >> ls) & 1) == 0)
        lms = [jnp.where(m, l, 0.0) for l in l_list]
        if ls == 0:
            ts = [t - lm for t, lm in zip(ts, lms)]
        else:
            tl = [_mm(t, lm) for t, lm in zip(ts, lms)]
            ts = [t - _mm(x, t) for t, x in zip(ts, tl)]
    return ts


def _in_proj_kernel(x_ref, nw_ref, w_ref, o_ref, sm_ref, h_scr):
    j = pl.program_id(1)

    @pl.when(j == 0)
    def _():
        x = x_ref[...]
        y = x * lax.rsqrt(jnp.mean(x * x, axis=-1, keepdims=True) + RMS_EPS)
        h_scr[...] = (y * nw_ref[...]).astype(bf16)

    tn = o_ref.shape[1]
    w = w_ref[:, pl.ds(pl.multiple_of(j * tn, tn), tn)]
    acc = jnp.dot(h_scr[...], w, preferred_element_type=f32)
    o_ref[...] = acc.astype(o_ref.dtype)

    @pl.when(j == OFF_SMALL // tn)
    def _():
        sm_ref[...] = acc[:, OFF_SMALL % tn:OFF_SMALL % tn + LANES]


def _in_proj(x2, nw, w):
    t, d = x2.shape
    n = w.shape[1]
    tm, tn = IN_PROJ_TILE_M, IN_PROJ_TILE_N
    return pl.pallas_call(
        _in_proj_kernel, name="in_proj",
        grid=(t // tm, n // tn),
        in_specs=[pl.BlockSpec((tm, d), lambda i, j: (i, 0)),
                  pl.BlockSpec((1, d), lambda i, j: (0, 0)),
                  pl.BlockSpec((d, n), lambda i, j: (0, 0), pipeline_mode=pl.Buffered(1))],
        out_specs=[pl.BlockSpec((tm, tn), lambda i, j: (i, j)),
                   pl.BlockSpec((tm, LANES), lambda i, j: (i, 0))],
        out_shape=[jax.ShapeDtypeStruct((t, n), bf16), jax.ShapeDtypeStruct((t, LANES), f32)],
        scratch_shapes=[pltpu.VMEM((tm, d), bf16)],
        compiler_params=_cparams(("parallel", "arbitrary"), 48),
    )(x2, nw, w)


def _ffn_kernel(x_ref, nw_ref, wg_ref, wu_ref, wd_ref, o_ref):
    x = x_ref[...]
    h = (x * lax.rsqrt(jnp.mean(x * x, axis=-1, keepdims=True) + RMS_EPS) * nw_ref[...]).astype(bf16)
    n = wg_ref.shape[1]
    step = n // FFN_HIDDEN_SPLITS
    acc = x
    for c0 in range(0, n, step):
        g = jnp.dot(h, wg_ref[:, c0:c0 + step], preferred_element_type=f32)
        u = jnp.dot(h, wu_ref[:, c0:c0 + step], preferred_element_type=f32)
        acc = acc + jnp.dot((_silu(g) * u).astype(bf16), wd_ref[c0:c0 + step, :], preferred_element_type=f32)
    o_ref[...] = acc


def _ffn(x2, nw, wg, wu, wd):
    t, d = x2.shape
    n = wg.shape[1]
    tm = MM_TILE_M
    resident = lambda shape: pl.BlockSpec(shape, lambda i: (0, 0), pipeline_mode=pl.Buffered(1))
    return pl.pallas_call(
        _ffn_kernel, name="ffn",
        grid=(t // tm,),
        in_specs=[pl.BlockSpec((tm, d), lambda i: (i, 0)),
                  pl.BlockSpec((1, d), lambda i: (0, 0)),
                  resident((d, n)), resident((d, n)), resident((n, d))],
        out_specs=pl.BlockSpec((tm, d), lambda i: (i, 0)),
        out_shape=jax.ShapeDtypeStruct((t, d), f32),
        compiler_params=_cparams(("parallel",), 48),
    )(x2, nw, wg, wu, wd)


def _top3_bias(gate_t, n_past):
    row = lax.broadcasted_iota(jnp.int32, gate_t.shape, 0)
    g = jnp.where(row < n_past, gate_t, -jnp.inf)
    sel = jnp.zeros(gate_t.shape, jnp.bool_)
    for _ in range(MOBA_TOPK):
        m = jnp.max(g, axis=0, keepdims=True)
        idx = jnp.min(jnp.where(g == m, row, gate_t.shape[0]), axis=0, keepdims=True)
        pick = row == idx
        sel = sel | (pick & (m > -jnp.inf))
        g = jnp.where(pick, -jnp.inf, g)
    return jnp.where(sel, 0.0, NEG)


def _moba_prep_kernel(a_ref, qw_ref, kw_ref, bd_ref, qaugt_ref, kaug_ref, vaugt_ref, kmean_scr):
    i = pl.program_id(1)

    @pl.when(i == 0)
    def _():
        kmean_scr[...] = jnp.zeros_like(kmean_scr)

    a = a_ref[...].astype(f32)
    q, k, v = a[:, :MIX_W], a[:, MIX_W:2 * MIX_W], a[:, 2 * MIX_W:]
    bd = bd_ref[...]
    qn = q * lax.rsqrt(_sel_right1(q * q, bd) + RMS_EPS) * qw_ref[...]
    kn = k * lax.rsqrt(_sel_right1(k * k, bd) + RMS_EPS) * kw_ref[...]
    lane = lax.broadcasted_iota(jnp.int32, (MOBA_BLOCK, LANES), 1)
    onehot = jnp.where(lane == i, 1.0, 0.0).astype(bf16)
    ones_row = jnp.where(lax.broadcasted_iota(jnp.int32, (VT_ROWS - HEAD_DIM, MOBA_BLOCK), 0) == 0, 1.0, 0.0)
    kmean = kmean_scr[...]
    nbp = kmean.shape[0]
    dim = lax.broadcasted_iota(jnp.int32, (LANES, MOBA_BLOCK), 0)
    sel_pad = jnp.zeros((LANES - nbp, MOBA_BLOCK), f32)
    for p in range(N_HEADS // 2):
        sl = slice(p * LANES, (p + 1) * LANES)
        kaug_ref[0, p] = jnp.concatenate([kn[:, sl].astype(bf16), onehot], axis=-1)
        vt = v[:, sl].T
        qt = (qn[:, sl] * (HEAD_DIM ** -0.5 * LOG2E)).T
        for hh in range(2):
            vaugt_ref[0, 2 * p + hh] = jnp.concatenate(
                [vt[hh * HEAD_DIM:(hh + 1) * HEAD_DIM], ones_row], axis=0).astype(bf16)
            keep = (dim < HEAD_DIM) if hh == 0 else (dim >= HEAD_DIM)
            q2t = jnp.where(keep, qt, 0.0)
            gate_t = _mm_hi(kmean[:, sl], q2t)
            qaugt_ref[0, 2 * p + hh] = jnp.concatenate([q2t, _top3_bias(gate_t, i), sel_pad], axis=0).astype(bf16)
    kmean_scr[pl.ds(i, 1), :] = jnp.mean(kn, axis=0, keepdims=True)


def _moba_prep(proj, bsz, seq, qw, kw, bd_mean):
    nb = seq // MOBA_BLOCK
    nbp = -(-nb // 8) * 8
    assert nbp <= LANES
    return pl.pallas_call(
        _moba_prep_kernel, name="moba_prep",
        grid=(bsz, nb),
        in_specs=[pl.BlockSpec((MOBA_BLOCK, 3 * MIX_W), lambda b, i: (b * nb + i, OFF_A // (3 * MIX_W))),
                  pl.BlockSpec((1, MIX_W), lambda b, i: (0, 0)),
                  pl.BlockSpec((1, MIX_W), lambda b, i: (0, 0)),
                  pl.BlockSpec((MIX_W, MIX_W), lambda b, i: (0, 0))],
        out_specs=[pl.BlockSpec((1, N_HEADS, 2 * LANES, MOBA_BLOCK), lambda b, i: (b, 0, 0, i)),
                   pl.BlockSpec((1, N_HEADS // 2, MOBA_BLOCK, 2 * LANES), lambda b, i: (b, 0, i, 0)),
                   pl.BlockSpec((1, N_HEADS, VT_ROWS, MOBA_BLOCK), lambda b, i: (b, 0, 0, i))],
        out_shape=[jax.ShapeDtypeStruct((bsz, N_HEADS, 2 * LANES, seq), bf16),
                   jax.ShapeDtypeStruct((bsz, N_HEADS // 2, seq, 2 * LANES), bf16),
                   jax.ShapeDtypeStruct((bsz, N_HEADS, VT_ROWS, seq), bf16)],
        scratch_shapes=[pltpu.VMEM((nbp, MIX_W), f32)],
        compiler_params=_cparams(("parallel", "arbitrary"), 32),
    )(proj, qw, kw, bd_mean)


def _moba_attn_kernel(qaugt_ref, kaug_ref, vaugt_ref, tabt_ref, o_ref, sa_scr, sb_scr):
    i = pl.program_id(2)
    blk = MOBA_BLOCK
    pairs = range(MOBA_PAIRS_PER_STEP)
    hs = range(2 * MOBA_PAIRS_PER_STEP)
    key, qry = _iota2((blk, blk))
    mm = lambda a, b: jnp.dot(a, b, preferred_element_type=f32)
    cmax = lambda s: jnp.max(s, axis=0, keepdims=True)
    pv = lambda v, pe: jnp.dot(v, pe.astype(bf16), preferred_element_type=f32)
    kblk = lambda pp, j: kaug_ref[0, pp, pl.ds(pl.multiple_of(j * blk, blk), blk), :]
    vblk = lambda hh, j: vaugt_ref[0, hh, :, pl.ds(pl.multiple_of(j * blk, blk), blk)]

    n_far = jnp.maximum(i - 1, 0)
    nk = MOBA_KV_PER_GROUP
    n_groups = (n_far + nk - 1) // nk
    blocks_of = lambda g: [jnp.where(nk * g + a < n_far, nk * g + a, i) for a in range(nk)]

    def scores(g, buf):
        qt = [qaugt_ref[0, hh] for hh in hs]
        for a, j in enumerate(blocks_of(g)):
            ks = [kblk(pp, j) for pp in pairs]
            for hh in hs:
                buf[hh, a] = mm(ks[hh // 2], qt[hh])

    scores(0, sa_scr)

    k_own = [kblk(pp, i) for pp in pairs]
    k_adj = [kblk(pp, n_far) for pp in pairs]
    qt = [qaugt_ref[0, hh] for hh in hs]
    s_own = [jnp.where(qry >= key, mm(k_own[hh // 2][:, :LANES], qt[hh][:LANES]) + tabt_ref[hh // 2, hh % 2, 1], NEG)
             for hh in hs]
    s_adj = [mm(k_adj[hh // 2], qt[hh]) + tabt_ref[hh // 2, hh % 2, 0] for hh in hs]
    m = [jnp.maximum(cmax(s_own[hh]), cmax(s_adj[hh])) for hh in hs]
    acc = [pv(vblk(hh, i), jnp.exp2(s_own[hh] - m[hh])) + pv(vblk(hh, n_far), jnp.exp2(s_adj[hh] - m[hh]))
           for hh in hs]

    def consume(g, buf, m, acc):
        js = blocks_of(g)
        m_new, acc_new = [], []
        for hh in hs:
            s = [buf[hh, a] for a in range(nk)]
            mh = functools.reduce(jnp.maximum, [cmax(x) for x in s], m[hh])
            ah = jnp.exp2(m[hh] - mh) * acc[hh]
            for a in range(nk):
                ah = ah + pv(vblk(hh, js[a]), jnp.exp2(s[a] - mh))
            m_new.append(mh)
            acc_new.append(ah)
        return m_new, acc_new

    nh = len(hs)

    def body(u, carry):
        m, acc = list(carry[:nh]), list(carry[nh:])
        scores(2 * u + 1, sb_scr)
        m, acc = consume(2 * u, sa_scr, m, acc)
        scores(2 * u + 2, sa_scr)
        m, acc = consume(2 * u + 1, sb_scr, m, acc)
        return (*m, *acc)

    carry = lax.fori_loop(0, n_groups // 2, body, (*m, *acc))
    last = n_groups - 1
    carry = lax.cond(n_groups % 2 == 1,
                     lambda cr: (lambda r: (*r[0], *r[1]))(consume(last, sa_scr, list(cr[:nh]), list(cr[nh:]))),
                     lambda cr: cr, carry)
    acc = carry[nh:]
    o_ref[0] = jnp.concatenate([(acc[hh][:HEAD_DIM] / acc[hh][HEAD_DIM:HEAD_DIM + 1]).T for hh in hs], axis=-1)


def _moba_attn(qaugt, kaug, vaugt, tabt):
    bsz, _, _, seq = qaugt.shape
    nb = seq // MOBA_BLOCK
    pp = MOBA_PAIRS_PER_STEP
    return pl.pallas_call(
        _moba_attn_kernel, name="moba_attn",
        grid=(bsz, N_HEADS // (2 * pp), nb),
        in_specs=[pl.BlockSpec((1, 2 * pp, 2 * LANES, MOBA_BLOCK), lambda b, p, i: (b, p, 0, i)),
                  pl.BlockSpec((1, pp, seq, 2 * LANES), lambda b, p, i: (b, p, 0, 0)),
                  pl.BlockSpec((1, 2 * pp, VT_ROWS, seq), lambda b, p, i: (b, p, 0, 0)),
                  pl.BlockSpec((pp, 2, 2, MOBA_BLOCK, MOBA_BLOCK), lambda b, p, i: (p, 0, 0, 0, 0))],
        out_specs=pl.BlockSpec((1, MOBA_BLOCK, pp * LANES), lambda b, p, i: (b, i, p)),
        out_shape=jax.ShapeDtypeStruct((bsz, seq, MIX_W), f32),
        scratch_shapes=[pltpu.VMEM((2 * pp, MOBA_KV_PER_GROUP, MOBA_BLOCK, MOBA_BLOCK), f32)] * 2,
        compiler_params=_cparams(("parallel", "parallel", "arbitrary"), 56),
    )(qaugt, kaug, vaugt, tabt)


def _t5_bucket(dist):
    n = jnp.maximum(dist, 0)
    max_exact = REL_BUCKETS // 2
    nf = jnp.maximum(n, max_exact).astype(f32)
    large = max_exact + (jnp.log(nf / max_exact) / math.log(REL_MAX_DIST / max_exact)
                         * (REL_BUCKETS - max_exact)).astype(jnp.int32)
    large = jnp.minimum(large, REL_BUCKETS - 1)
    return jnp.where(n < max_exact, n, large)


def _moba_bias_kernel(vec_ref, o_ref):
    blk = MOBA_BLOCK
    t = pltpu.roll(jnp.broadcast_to(vec_ref[0] * LOG2E, (blk, 2 * blk)), 0, 1, stride=1, stride_axis=0)
    o_ref[0, 0] = t[:, blk:]
    o_ref[0, 1] = t[:, :blk]


def _moba_bias_tables(rel_bias):
    assert MOBA_BLOCK >= REL_MAX_DIST
    by_dist = rel_bias.astype(f32)[_t5_bucket(jnp.arange(2 * MOBA_BLOCK))]
    far = rel_bias.astype(f32)[_t5_bucket(jnp.array(2 * MOBA_BLOCK))]
    vec = (by_dist - far).T.reshape(N_HEADS, 1, 2 * MOBA_BLOCK)
    tab = pl.pallas_call(
        _moba_bias_kernel, name="moba_bias",
        grid=(N_HEADS,),
        in_specs=[pl.BlockSpec((1, 1, 2 * MOBA_BLOCK), lambda h: (h, 0, 0))],
        out_specs=pl.BlockSpec((1, 2, MOBA_BLOCK, MOBA_BLOCK), lambda h: (h, 0, 0, 0)),
        out_shape=jax.ShapeDtypeStruct((N_HEADS, 2, MOBA_BLOCK, MOBA_BLOCK), f32),
        compiler_params=_cparams(("parallel",), 16),
    )(vec)
    return tab.reshape(N_HEADS // 2, 2, 2, MOBA_BLOCK, MOBA_BLOCK)


def _causal_conv(x, halo, w_ref):
    ts, nh = x.shape[0], halo.shape[0]
    xe = jnp.concatenate([halo, x], axis=0)
    acc = x * w_ref[CONV_K - 1:CONV_K, :]
    for d in range(1, CONV_K):
        acc = acc + xe[nh - d:nh - d + ts] * w_ref[CONV_K - 1 - d:CONV_K - d, :]
    return acc


def _tile_and_halo(x_ref, halo_ref):
    halo = jnp.where(pl.program_id(1) == 0, 0.0, halo_ref[...].astype(f32))
    return x_ref[...].astype(f32), halo


def _write_heads(o_ref, val):
    for h in range(N_HEADS):
        o_ref[0, h] = val[:, h * HEAD_DIM:(h + 1) * HEAD_DIM].astype(o_ref.dtype)


def _read_heads(ref):
    return jnp.concatenate([ref[0, h] for h in range(N_HEADS)], axis=-1)


def _hm_spec(rows):
    return pl.BlockSpec((1, N_HEADS, rows, HEAD_DIM), lambda b, i: (b, 0, i, 0))


def _hm_shape(bsz, seq, dtype=f32):
    return jax.ShapeDtypeStruct((bsz, N_HEADS, seq, HEAD_DIM), dtype)


def _chunk_sum_matrix(chunk):
    r = jnp.arange(CUMSUM_TILE)[:, None]
    c = jnp.arange(CUMSUM_TILE)[None, :]
    same = (r // chunk) == (c // chunk)
    return jnp.concatenate([same & (r >= c), same], axis=0).astype(bf16)


def _chunk_sums(cm_ref, x):
    parts = [_sel_left(cm_ref[...], x[r:r + CUMSUM_TILE]) for r in range(0, x.shape[0], CUMSUM_TILE)]
    return (jnp.concatenate([p[:CUMSUM_TILE] for p in parts], axis=0),
            jnp.concatenate([p[CUMSUM_TILE:] for p in parts], axis=0))


def _row_spec(width, off, nt):
    return pl.BlockSpec((ROW_TILE, width), lambda b, i: (b * nt + i, off // width))


def _halo_spec(width, off, nt):
    per = ROW_TILE // HALO_ROWS
    return pl.BlockSpec((HALO_ROWS, width), lambda b, i: (jnp.maximum((b * nt + i) * per - 1, 0), off // width))


def _small_spec(nt):
    return pl.BlockSpec((ROW_TILE, LANES), lambda b, i: (b * nt + i, 0))


def _const_spec(shape):
    return pl.BlockSpec(shape, lambda b, i: (0,) * len(shape))


def _gdn_prep_kernel(x_ref, halo_ref, sm_ref, cw_ref, alog_ref, dtb_ref, bd_ref, eb_ref, ea_ref, cm_ref,
                     q_ref, k_ref, kb_ref, vb_ref, qd_ref, kbe_ref, kd_ref, gc_ref):
    qkv = _silu(_causal_conv(*_tile_and_halo(x_ref, halo_ref), cw_ref))
    q, k, v = qkv[:, :MIX_W], qkv[:, MIX_W:2 * MIX_W], qkv[:, 2 * MIX_W:]
    bd = bd_ref[...]
    q = q * lax.rsqrt(_sel_right1(q * q, bd) + L2_EPS) * HEAD_DIM ** -0.5
    k = k * lax.rsqrt(_sel_right1(k * k, bd) + L2_EPS)
    sm = sm_ref[...]
    beta = _sel_right1(_sigmoid(sm), eb_ref[...])
    g = -jnp.exp(alog_ref[...]) * _softplus(sm + dtb_ref[...])
    gc, g_end = _chunk_sums(cm_ref, g)
    eg = jnp.exp(_sel_right(gc, ea_ref[...]))
    e_rest = jnp.exp(_sel_right(g_end - gc, ea_ref[...]))
    kb = k * beta
    _write_heads(q_ref, q)
    _write_heads(k_ref, k)
    _write_heads(kb_ref, kb)
    _write_heads(vb_ref, v * beta)
    _write_heads(qd_ref, q * eg)
    _write_heads(kbe_ref, kb * eg)
    _write_heads(kd_ref, k * e_rest)
    gc_ref[...] = gc[:, SM_BA:SM_BA + N_HEADS]


def _gdn_prep(proj, small, bsz, seq, conv_w, alog128, dtb128, bd_ones, e_beta, e_ba, cm):
    nt = seq // ROW_TILE
    w3 = 3 * MIX_W
    return pl.pallas_call(
        _gdn_prep_kernel, name="gdn_prep",
        grid=(bsz, nt),
        in_specs=[_row_spec(w3, OFF_BQKV, nt), _halo_spec(w3, OFF_BQKV, nt), _small_spec(nt),
                  _const_spec((CONV_K, w3)), _const_spec((1, LANES)), _const_spec((1, LANES)),
                  _const_spec((MIX_W, MIX_W)), _const_spec((LANES, MIX_W)), _const_spec((LANES, MIX_W)),
                  _const_spec((2 * CUMSUM_TILE, CUMSUM_TILE))],
        out_specs=[_hm_spec(ROW_TILE)] * 7 + [pl.BlockSpec((ROW_TILE, N_HEADS), lambda b, i: (b * nt + i, 0))],
        out_shape=[_hm_shape(bsz, seq, bf16)] * 7 + [jax.ShapeDtypeStruct((bsz * seq, N_HEADS), f32)],
        compiler_params=_cparams(("parallel", "parallel"), 40),
    )(proj, proj, small, conv_w, alog128, dtb128, bd_ones, e_beta, e_ba, cm)


def _gdn_chunk_kernel(q_ref, k_ref, kb_ref, vb_ref, qd_ref, kbe_ref, kd_ref, gc_ref, gct_ref, nw_ref, o_ref, st_scr):
    @pl.when(pl.program_id(1) == 0)
    def _():
        st_scr[...] = jnp.zeros_like(st_scr)

    n = GDN_CHUNK
    hs = range(N_HEADS)
    ch = [(ci, h) for ci in range(CHUNKS_PER_STEP) for h in hs]
    blk = lambda ref, ci, h: ref[0, h, ci * n:(ci + 1) * n, :]
    r, c = _iota2((n, n))
    gc_all = [gc_ref[0, ci] for ci in range(CHUNKS_PER_STEP)]
    gct_all = [gct_ref[0, ci] for ci in range(CHUNKS_PER_STEP)]
    decay = {x: jnp.exp(jnp.where(r >= c, gc_all[x[0]][:, x[1]:x[1] + 1] - gct_all[x[0]][x[1]:x[1] + 1, :], NEG))
             for x in ch}
    gram = {x: _mm_nt(jnp.concatenate([blk(kb_ref, *x), blk(q_ref, *x)], axis=0), blk(k_ref, *x)) for x in ch}
    t = dict(zip(ch, _unit_lower_inverses([jnp.where(r > c, gram[x][:n] * decay[x], 0.0) for x in ch])))
    u = {x: _mm(t[x], blk(vb_ref, *x)) for x in ch}
    w = {x: _mm(t[x], blk(kbe_ref, *x)) for x in ch}
    a_in = {x: gram[x][n:] * decay[x] for x in ch}
    st = [st_scr[h] for h in hs]
    for ci in range(CHUNKS_PER_STEP):
        ws = [_mm(jnp.concatenate([w[ci, h].astype(bf16), blk(qd_ref, ci, h)], axis=0), st[h]) for h in hs]
        v_new = [u[ci, h] - ws[h][:n] for h in hs]
        o = [ws[h][n:] + _mm(a_in[ci, h], v_new[h]) for h in hs]
        upd = [_mm_tn(blk(kd_ref, ci, h), v_new[h]) for h in hs]
        for h in hs:
            on = o[h] * lax.rsqrt(jnp.mean(o[h] * o[h], axis=-1, keepdims=True) + RMS_EPS) * nw_ref[...]
            o_ref[ci * n:(ci + 1) * n, h * HEAD_DIM:(h + 1) * HEAD_DIM] = on
        st = [st[h] * jnp.exp(gc_all[ci][n - 1:n, h:h + 1]) + upd[h] for h in hs]
    for h in hs:
        st_scr[h] = st[h]


def _tok_spec(rows, nsteps):
    return pl.BlockSpec((rows, MIX_W), lambda b, i: (b * nsteps + i, 0))


def _gdn_chunk(q, k, kb, vb, qd, kbe, kd, gc, norm_w):
    bsz, _, seq, _ = q.shape
    n = GDN_CHUNK
    nc = seq // n
    per = CHUNKS_PER_STEP
    gc4 = gc.reshape(bsz, nc, n, N_HEADS)
    gct4 = jnp.swapaxes(gc4, 2, 3)
    return pl.pallas_call(
        _gdn_chunk_kernel, name="gdn_chunk",
        grid=(bsz, nc // per),
        in_specs=[_hm_spec(per * n)] * 7 + [pl.BlockSpec((1, per, n, N_HEADS), lambda b, i: (b, i, 0, 0)),
                                            pl.BlockSpec((1, per, N_HEADS, n), lambda b, i: (b, i, 0, 0)),
                                            _const_spec((1, HEAD_DIM))],
        out_specs=_tok_spec(per * n, nc // per),
        out_shape=jax.ShapeDtypeStruct((bsz * seq, MIX_W), f32),
        scratch_shapes=[pltpu.VMEM((N_HEADS, HEAD_DIM, HEAD_DIM), f32)],
        compiler_params=_cparams(("parallel", "arbitrary"), 32),
    )(q, k, kb, vb, qd, kbe, kd, gc4, gct4, norm_w)


def _rwkv_prep_kernel(has_vres, *refs):
    if has_vres:
        (c_ref, halo_ref, mu_ref, w0_ref, wup_ref, a0_ref, aup_ref, gup_ref, kk_ref, ka_ref, bd_ref,
         rk_ref, cm_ref, vf_ref, v0_ref, vdn_ref, vup_ref,
         rt_ref, at_ref, bt_ref, kt_ref, bp_ref, kp_ref, v_ref, pe_ref, gout_ref, bonus_ref) = refs
    else:
        (c_ref, halo_ref, mu_ref, w0_ref, wup_ref, a0_ref, aup_ref, gup_ref, kk_ref, ka_ref, bd_ref,
         rk_ref, cm_ref,
         rt_ref, at_ref, bt_ref, kt_ref, bp_ref, kp_ref, v_ref, pe_ref, gout_ref, bonus_ref, cv_ref) = refs
    c, halo = _tile_and_halo(c_ref, halo_ref)
    prev = jnp.concatenate([halo[HALO_ROWS - 1:], c[:-1]], axis=0)
    c = c + (prev - c) * mu_ref[...]
    c_r, c_k, c_v = c[:, :MIX_W], c[:, MIX_W:2 * MIX_W], c[:, 2 * MIX_W:3 * MIX_W]
    c_wd = c[:, 3 * MIX_W:3 * MIX_W + 64]
    c_ad = c[:, 3 * MIX_W + 64:3 * MIX_W + 128]
    c_gd = c[:, 3 * MIX_W + 128:]
    w_log = -_softplus(-(w0_ref[...] + _mm(jnp.tanh(c_wd), wup_ref[...]))) - 0.5
    a_in = _sigmoid(a0_ref[...] + _mm(c_ad, aup_ref[...]))
    gout_ref[...] = _mm(_sigmoid(c_gd), gup_ref[...])
    if has_vres:
        lam = _sigmoid(v0_ref[...] + _mm(_mm(c_v, vdn_ref[...]), vup_ref[...]))
        v_r = c_v + (vf_ref[...] - c_v) * lam
    else:
        v_r = c_v
        cv_ref[...] = c_v
    bd = bd_ref[...]
    kk = c_k * kk_ref[...]
    kk = kk * lax.rsqrt(_sel_right1(kk * kk, bd) + L2_EPS)
    k_r = c_k * (1.0 + (a_in - 1.0) * ka_ref[...])
    b = kk * a_in
    bonus_ref[...] = _sel_right1(c_r * k_r * rk_ref[...], bd) * v_r
    lc, lc_end = _chunk_sums(cm_ref, -jnp.exp(w_log))
    e_neg = jnp.exp(-lc)
    e_rest = jnp.exp(lc_end - lc)
    _write_heads(rt_ref, c_r * jnp.exp(lc))
    _write_heads(at_ref, -kk * jnp.exp(lc + jnp.exp(w_log)))
    _write_heads(bt_ref, b * e_neg)
    _write_heads(kt_ref, k_r * e_neg)
    _write_heads(bp_ref, b * e_rest)
    _write_heads(kp_ref, k_r * e_rest)
    _write_heads(v_ref, v_r)
    _write_heads(pe_ref, jnp.exp(lc_end))


def _rwkv_prep(proj, bsz, seq, mu, w0, w_up, a0, a_up, g_up, k_k, k_a, r_k, bd_ones, cm, vres):
    nt = seq // ROW_TILE
    wc = 3 * MIX_W + 256
    std = pl.BlockSpec((ROW_TILE, MIX_W), lambda b, i: (b * nt + i, 0))
    std_shape = jax.ShapeDtypeStruct((bsz * seq, MIX_W), f32)
    in_specs = [_row_spec(wc, OFF_C, nt), _halo_spec(wc, OFF_C, nt), _const_spec((1, wc)),
                _const_spec((1, MIX_W)), _const_spec((64, MIX_W)), _const_spec((1, MIX_W)), _const_spec((64, MIX_W)),
                _const_spec((128, MIX_W)), _const_spec((1, MIX_W)), _const_spec((1, MIX_W)), _const_spec((MIX_W, MIX_W)),
                _const_spec((1, MIX_W)), _const_spec((2 * CUMSUM_TILE, CUMSUM_TILE))]
    args = [proj, proj, mu, w0, w_up, a0, a_up, g_up, k_k, k_a, bd_ones, r_k, cm]
    out_specs = [_hm_spec(ROW_TILE)] * 8 + [std, std]
    out_shape = [_hm_shape(bsz, seq, bf16)] * 7 + [_hm_shape(bsz, seq), std_shape, std_shape]
    if vres is not None:
        v_first, v0, v_down, v_up = vres
        in_specs += [std, _const_spec((1, MIX_W)), _const_spec(v_down.shape), _const_spec(v_up.shape)]
        args += [v_first, v0, v_down, v_up]
    else:
        out_specs.append(std)
        out_shape.append(std_shape)
    return pl.pallas_call(
        functools.partial(_rwkv_prep_kernel, vres is not None), name="rwkv_prep",
        grid=(bsz, nt), in_specs=in_specs, out_specs=out_specs, out_shape=out_shape,
        compiler_params=_cparams(("parallel", "parallel"), 40),
    )(*args)


def _rwkv_chunk_kernel(rt_ref, at_ref, bt_ref, kt_ref, bp_ref, kp_ref, v_ref, pe_ref, o_ref, st_scr):
    @pl.when(pl.program_id(1) == 0)
    def _():
        st_scr[...] = jnp.zeros_like(st_scr)

    n = RWKV_CHUNK
    hs = range(N_HEADS)
    ch = [(ci, h) for ci in range(CHUNKS_PER_STEP) for h in hs]
    blk = lambda ref, ci, h: ref[0, h, ci * n:(ci + 1) * n, :]
    row, col = _iota2((2 * n, 2 * n))
    rr, cc = row & (n - 1), col & (n - 1)
    mask = rr + jnp.where(row < n, 0, 1) > cc
    lhs = {x: jnp.concatenate([blk(at_ref, *x), blk(rt_ref, *x)], axis=0) for x in ch}
    gm = {x: jnp.where(mask, _mm_nt(lhs[x], jnp.concatenate([blk(bt_ref, *x), blk(kt_ref, *x)], axis=0)), 0.0)
          for x in ch}
    t = dict(zip(ch, _unit_lower_inverses([-gm[x][:n, :n] for x in ch])))
    v = {x: blk(v_ref, *x) for x in ch}
    makv = {x: _mm(gm[x][:n], jnp.concatenate([jnp.zeros_like(v[x]), v[x]], axis=0)) for x in ch}
    st = [st_scr[h] for h in hs]
    for ci in range(CHUNKS_PER_STEP):
        ah = [_mm_nt(lhs[ci, h], st[h]) for h in hs]
        u = [_mm(t[ci, h], ah[h][:n] + makv[ci, h]) for h in hs]
        uv = [jnp.concatenate([u[h].astype(bf16), v[ci, h]], axis=0) for h in hs]
        o = [ah[h][n:] + _mm(gm[ci, h][n:], uv[h]) for h in hs]
        upd = [_mm_tn(uv[h], jnp.concatenate([blk(bp_ref, ci, h), blk(kp_ref, ci, h)], axis=0)) for h in hs]
        for h in hs:
            o_ref[0, h, ci * n:(ci + 1) * n, :] = o[h]
        st = [st[h] * pe_ref[0, h, ci * n:ci * n + 1, :] + upd[h] for h in hs]
    for h in hs:
        st_scr[h] = st[h]


def _rwkv_chunk(rt, at, bt, kt, bp, kp, v, pe):
    bsz, _, seq, _ = rt.shape
    n = RWKV_CHUNK * CHUNKS_PER_STEP
    return pl.pallas_call(
        _rwkv_chunk_kernel, name="rwkv_chunk",
        grid=(bsz, seq // n),
        in_specs=[_hm_spec(n)] * 8,
        out_specs=_hm_spec(n),
        out_shape=_hm_shape(bsz, seq),
        scratch_shapes=[pltpu.VMEM((N_HEADS, HEAD_DIM, HEAD_DIM), f32)],
        compiler_params=_cparams(("parallel", "arbitrary"), 32),
    )(rt, at, bt, kt, bp, kp, v, pe)


def _ssd_prep_kernel(x_ref, halo_ref, sm_ref, cw_ref, cb_ref, alog_ref, dtb_ref, edt_ref, cm_ref,
                     xdt_ref, x_out_ref, bc_ref, acs_ref):
    xbc = _silu(_causal_conv(*_tile_and_halo(x_ref, halo_ref), cw_ref) + cb_ref[...])
    m_x = xbc[:, :MIX_W]
    dt = _softplus(sm_ref[...] + dtb_ref[...])
    _write_heads(xdt_ref, m_x * _sel_right1(dt, edt_ref[...]))
    _write_heads(x_out_ref, m_x)
    bc_ref[...] = xbc[:, MIX_W:].astype(bc_ref.dtype)
    acs, _ = _chunk_sums(cm_ref, dt * -jnp.exp(alog_ref[...]))
    acs_ref[...] = acs[:, SM_DT:SM_DT + N_HEADS]


def _ssd_prep(proj, small, bsz, seq, conv_w, conv_b, alog128, dtb128, e_dt, cm):
    nt = seq // ROW_TILE
    wx = MIX_W + 4 * SSM_STATE
    return pl.pallas_call(
        _ssd_prep_kernel, name="ssd_prep",
        grid=(bsz, nt),
        in_specs=[_row_spec(wx, OFF_DXBC, nt), _halo_spec(wx, OFF_DXBC, nt), _small_spec(nt),
                  _const_spec((CONV_K, wx)), _const_spec((1, wx)), _const_spec((1, LANES)), _const_spec((1, LANES)),
                  _const_spec((LANES, MIX_W)), _const_spec((2 * CUMSUM_TILE, CUMSUM_TILE))],
        out_specs=[_hm_spec(ROW_TILE)] * 2 + [pl.BlockSpec((ROW_TILE, 4 * SSM_STATE), lambda b, i: (b * nt + i, 0)),
                                              pl.BlockSpec((ROW_TILE, N_HEADS), lambda b, i: (b * nt + i, 0))],
        out_shape=[_hm_shape(bsz, seq, bf16), _hm_shape(bsz, seq),
                   jax.ShapeDtypeStruct((bsz * seq, 4 * SSM_STATE), bf16),
                   jax.ShapeDtypeStruct((bsz * seq, N_HEADS), f32)],
        compiler_params=_cparams(("parallel", "parallel"), 40),
    )(proj, proj, small, conv_w, conv_b, alog128, dtb128, e_dt, cm)


def _ssd_chunk_kernel(xdt_ref, x_ref, bc_ref, a_ref, at_ref, dvec_ref, o_ref, st_scr):
    @pl.when(pl.program_id(1) == 0)
    def _():
        st_scr[...] = jnp.zeros_like(st_scr)

    n = SSD_CHUNK
    hs = range(N_HEADS)
    cis = range(SSD_CHUNKS_PER_STEP)
    heads_per_group = N_HEADS // 2
    grp = lambda h: h // heads_per_group
    ch = [(ci, h) for ci in cis for h in hs]
    rows = lambda ci: slice(ci * n, (ci + 1) * n)
    r, c = _iota2((n, n))
    b_g = {(ci, g): bc_ref[rows(ci), g * SSM_STATE:(g + 1) * SSM_STATE] for ci in cis for g in range(2)}
    c_g = {(ci, g): bc_ref[rows(ci), (2 + g) * SSM_STATE:(3 + g) * SSM_STATE] for ci in cis for g in range(2)}
    cb = {x: _mm_nt(c_g[x], b_g[x]) for x in b_g}
    ac = {(ci, h): a_ref[0, ci][:, h:h + 1] for ci, h in ch}
    a_last = {(ci, h): a_ref[0, ci][n - 1:n, h:h + 1] for ci, h in ch}
    lmat = {(ci, h): jnp.exp(jnp.where(r >= c, ac[ci, h] - at_ref[0, ci][h:h + 1, :], NEG)) for ci, h in ch}
    xg = {(ci, h): xdt_ref[0, h, rows(ci), :] for ci, h in ch}
    y_diag = {(ci, h): _mm(cb[ci, grp(h)] * lmat[ci, h], xg[ci, h]) for ci, h in ch}
    upd = {(ci, h): _mm_tn(b_g[ci, grp(h)].astype(f32) * jnp.exp(a_last[ci, h] - ac[ci, h]), xg[ci, h]) for ci, h in ch}
    c_in = {(ci, h): c_g[ci, grp(h)].astype(f32) * jnp.exp(ac[ci, h]) for ci, h in ch}
    st = [st_scr[h] for h in hs]
    for ci in cis:
        y_off = [_mm(c_in[ci, h], st[h]) for h in hs]
        for h in hs:
            o_ref[rows(ci), h * HEAD_DIM:(h + 1) * HEAD_DIM] = (
                y_diag[ci, h] + y_off[h] + x_ref[0, h, rows(ci), :] * dvec_ref[h:h + 1, :])
        st = [st[h] * jnp.exp(a_last[ci, h]) + upd[ci, h] for h in hs]
    for h in hs:
        st_scr[h] = st[h]


def _ssd_chunk(xdt, x, bc, acs, dvec):
    bsz, _, seq, _ = xdt.shape
    n = SSD_CHUNK
    nc = seq // n
    per = SSD_CHUNKS_PER_STEP
    a4 = acs.reshape(bsz, nc, n, N_HEADS)
    at4 = jnp.swapaxes(a4, 2, 3)
    return pl.pallas_call(
        _ssd_chunk_kernel, name="ssd_chunk",
        grid=(bsz, nc // per),
        in_specs=[_hm_spec(per * n)] * 2 + [pl.BlockSpec((per * n, 4 * SSM_STATE), lambda b, i: (b * (nc // per) + i, 0)),
                                            pl.BlockSpec((1, per, n, N_HEADS), lambda b, i: (b, i, 0, 0)),
                                            pl.BlockSpec((1, per, N_HEADS, n), lambda b, i: (b, i, 0, 0)),
                                            _const_spec((N_HEADS, HEAD_DIM))],
        out_specs=_tok_spec(per * n, nc // per),
        out_shape=jax.ShapeDtypeStruct((bsz * seq, MIX_W), f32),
        scratch_shapes=[pltpu.VMEM((N_HEADS, SSM_STATE, HEAD_DIM), f32)],
        compiler_params=_cparams(("parallel", "arbitrary"), 32),
    )(xdt, x, bc, a4, at4, dvec)


def _merge_kernel(x_ref, ya_ref, ob_ref, bz_ref, wkv_ref, bonus_ref, gout_ref, lnw_ref, lnb_ref,
                  yd_ref, dz_ref, mnw_ref, g0_ref, g1_ref, g2_ref, g3_ref, wb_ref, wo_ref, o_ref):
    def gated(n, y, g_ref):
        return _sigmoid(g_ref[...].astype(f32)) * _mm(y, wb_ref[n])

    acc = gated(0, ya_ref[0], g0_ref)
    acc = acc + gated(1, ob_ref[...] * _silu(bz_ref[...].astype(f32)), g1_ref)

    def wkv_head(h):
        w = wkv_ref[0, h]
        mu = jnp.mean(w, axis=-1, keepdims=True)
        var = jnp.mean(jnp.square(w - mu), axis=-1, keepdims=True)
        return (w - mu) * lax.rsqrt(var + RWKV_LN_EPS)

    wkv_ln = jnp.concatenate([wkv_head(h) for h in range(N_HEADS)], axis=-1)
    y_c = (wkv_ln * lnw_ref[...] + lnb_ref[...] + bonus_ref[...]) * gout_ref[...]
    acc = acc + gated(2, y_c, g2_ref)

    yz = yd_ref[...] * _silu(dz_ref[...].astype(f32))
    half = MIX_W // 2
    y_d = jnp.concatenate(
        [yz[:, s:s + half] * lax.rsqrt(jnp.mean(jnp.square(yz[:, s:s + half]), axis=-1, keepdims=True) + RMS_EPS)
         for s in (0, half)], axis=-1) * mnw_ref[...]
    acc = acc + gated(3, y_d, g3_ref)
    o_ref[...] = x_ref[...] + _mm(acc, wo_ref[...])


def _merge(x2, proj, bsz, seq, ya, ob, wkv, bonus, gout, lnw, lnb, yd, mnw, wb, wo):
    nt = seq // ROW_TILE
    std = lambda w: pl.BlockSpec((ROW_TILE, w), lambda b, i: (b * nt + i, 0))
    gate = lambda n: _row_spec(D_MODEL, OFF_GATES + n * D_MODEL, nt)
    return pl.pallas_call(
        _merge_kernel, name="merge",
        grid=(bsz, nt),
        in_specs=[std(D_MODEL), pl.BlockSpec((1, ROW_TILE, MIX_W), lambda b, i: (b, i, 0)),
                  std(MIX_W), _row_spec(MIX_W, OFF_BZ, nt),
                  _hm_spec(ROW_TILE), std(MIX_W), std(MIX_W), _const_spec((1, MIX_W)), _const_spec((1, MIX_W)),
                  std(MIX_W), _row_spec(MIX_W, OFF_DZ, nt), _const_spec((1, MIX_W)),
                  gate(0), gate(1), gate(2), gate(3),
                  _const_spec((4, MIX_W, D_MODEL)), _const_spec((D_MODEL, D_MODEL))],
        out_specs=std(D_MODEL),
        out_shape=jax.ShapeDtypeStruct((bsz * seq, D_MODEL), f32),
        compiler_params=_cparams(("parallel", "parallel"), 48),
    )(x2, ya, ob, proj, wkv, bonus, gout, lnw, lnb, yd, proj, mnw, proj, proj, proj, proj, wb, wo)


def _lane_vec(vals, off):
    return jnp.zeros((1, LANES), f32).at[0, off:off + vals.shape[0]].set(vals)


def _head_expand(off):
    n = jnp.arange(LANES)[:, None]
    c = jnp.arange(MIX_W)[None, :]
    return (n - off == c // HEAD_DIM).astype(bf16)


def _pack_w_in(w):
    pad = lambda n: jnp.zeros((w.shape[0], n), w.dtype)
    cols = [w[:, W_A:W_BZ],
            w[:, W_BZ:W_BBETA],
            w[:, W_C:W_DZ],
            w[:, W_BBETA:W_C], w[:, W_DDT:W_GATES], pad(2 * LANES - 3 * N_HEADS),
            w[:, W_DZ:W_DXBC], w[:, W_DXBC:W_DDT], w[:, W_GATES:]]
    out = jnp.concatenate(cols, axis=1).astype(bf16)
    assert out.shape[1] == N_PROJ
    return out


def kernel(x, rel_bias, norm1_w, w_in, moba_q_norm, moba_k_norm, gdn_conv_w, gdn_A_log, gdn_dt_bias, gdn_norm_w, rwkv_mu, rwkv_w0, rwkv_w_up, rwkv_a0, rwkv_a_up, rwkv_g_up, rwkv_k_k, rwkv_k_a, rwkv_r_k, rwkv_v0, rwkv_v_down, rwkv_v_up, rwkv_ln_w, rwkv_ln_b, mamba_conv_w, mamba_conv_b, mamba_dt_bias, mamba_A_log, mamba_D, mamba_norm_w, w_branch, w_out, norm2_w, ffn_w_in, ffn_w_down):
    bsz, seq, d = x.shape
    depth = w_in.shape[0]
    assert d == D_MODEL and (bsz * seq) % IN_PROJ_TILE_M == 0 and seq % MM_TILE_M == 0
    x2 = x.reshape(bsz * seq, d)
    row = lambda v: v.reshape(1, -1).astype(f32)

    hid = jnp.arange(MIX_W) // HEAD_DIM
    bd_ones = (hid[:, None] == hid[None, :]).astype(bf16)
    bd_mean = (bd_ones.astype(f32) / HEAD_DIM).astype(bf16)
    e_beta, e_ba, e_dt = _head_expand(SM_BETA), _head_expand(SM_BA), _head_expand(SM_DT)
    cm64, cm128 = _chunk_sum_matrix(GDN_CHUNK), _chunk_sum_matrix(SSD_CHUNK)
    assert GDN_CHUNK == RWKV_CHUNK
    tab = _moba_bias_tables(rel_bias)
    v_first = None
    for i in range(depth):
        proj, small = _in_proj(x2, row(norm1_w[i]), _pack_w_in(w_in[i]))

        qaug, kaug, v_a = _moba_prep(proj, bsz, seq, row(jnp.tile(moba_q_norm[i], N_HEADS)),
                                     row(jnp.tile(moba_k_norm[i], N_HEADS)), bd_mean)
        y_a = _moba_attn(qaug, kaug, v_a, tab)

        gdn_in = _gdn_prep(proj, small, bsz, seq, gdn_conv_w[i], _lane_vec(gdn_A_log[i], SM_BA),
                           _lane_vec(gdn_dt_bias[i], SM_BA), bd_ones, e_beta, e_ba, cm64)
        o_b = _gdn_chunk(*gdn_in, row(gdn_norm_w[i]))

        vres = None if i == 0 else (v_first, row(rwkv_v0[i - 1]), rwkv_v_down[i - 1].astype(bf16),
                                    rwkv_v_up[i - 1].astype(bf16))
        outs = _rwkv_prep(proj, bsz, seq, row(rwkv_mu[i]), row(rwkv_w0[i]), rwkv_w_up[i].astype(bf16),
                          row(rwkv_a0[i]), rwkv_a_up[i].astype(bf16), rwkv_g_up[i].astype(bf16),
                          row(rwkv_k_k[i]), row(rwkv_k_a[i]), row(rwkv_r_k[i]), bd_ones, cm64, vres)
        g_out, bonus = outs[8], outs[9]
        if i == 0:
            v_first = outs[10]
        wkv = _rwkv_chunk(*outs[:8])

        xdt, x_d, bc, acs = _ssd_prep(proj, small, bsz, seq, mamba_conv_w[i], row(mamba_conv_b[i]),
                                      _lane_vec(mamba_A_log[i], SM_DT), _lane_vec(mamba_dt_bias[i], SM_DT), e_dt, cm128)
        y_d = _ssd_chunk(xdt, x_d, bc, acs, jnp.broadcast_to(mamba_D[i][:, None], (N_HEADS, HEAD_DIM)).astype(f32))

        x2 = _merge(x2, proj, bsz, seq, y_a, o_b, wkv, bonus, g_out,
                    row(rwkv_ln_w[i]), row(rwkv_ln_b[i]), y_d, row(mamba_norm_w[i]),
                    w_branch[i].astype(bf16), w_out[i].astype(bf16))

        x2 = _ffn(x2, row(norm2_w[i]), ffn_w_in[i][:, :D_FF].astype(bf16), ffn_w_in[i][:, D_FF:].astype(bf16),
                  ffn_w_down[i].astype(bf16))
    return x2.reshape(bsz, seq, d)
```

```python
import functools
import math

import jax
import jax.numpy as jnp
from jax import lax
from jax.experimental import pallas as pl
from jax.experimental.pallas import tpu as pltpu

f32, bf16 = jnp.float32, jnp.bfloat16
HI = lax.Precision.HIGHEST

D_MODEL = 1024
N_HEADS = 8
HEAD_DIM = 64
MIX_W = N_HEADS * HEAD_DIM
RMS_EPS = 1e-6
L2_EPS = 1e-6
CONV_K = 4
MOBA_BLOCK = 256
MOBA_TOPK = 3
MOBA_PAIRS_PER_STEP = 2
MOBA_KV_PER_GROUP = 2
REL_BUCKETS = 32
REL_MAX_DIST = 128
GDN_CHUNK = 64
RWKV_CHUNK = 64
CHUNKS_PER_STEP = 4
RWKV_LN_EPS = 64e-5
SSM_STATE = 128
SSD_CHUNK = 128
SSD_CHUNKS_PER_STEP = 2
D_FF = 2816
FFN_HIDDEN_SPLITS = 2
NEG = -1e30
LOG2E = math.log2(math.e)

LANES = 128
BF16_SUBLANES = 16
VT_ROWS = HEAD_DIM + BF16_SUBLANES

OFF_A, OFF_BQKV, OFF_BZ, OFF_C, OFF_SMALL, OFF_DZ, OFF_DXBC, OFF_GATES = 0, 1536, 3072, 3584, 5376, 5632, 6144, 7168
N_PROJ = OFF_GATES + 4 * D_MODEL
SM_BETA, SM_BA, SM_DT = 0, 8, 16
W_A, W_BQKV, W_BZ, W_BBETA, W_BA, W_C, W_DZ, W_DXBC, W_DDT, W_GATES = 0, 1536, 3072, 3584, 3592, 3600, 5392, 5904, 6928, 6936

ROW_TILE = 512
CUMSUM_TILE = 256
MM_TILE_M = 512
IN_PROJ_TILE_M, IN_PROJ_TILE_N = 1024, 1024
HALO_ROWS = BF16_SUBLANES


def _cparams(sem, vmem_mb):
    return pltpu.CompilerParams(dimension_semantics=sem, vmem_limit_bytes=vmem_mb * 1024 * 1024)


def _mm(a, b):
    return jnp.dot(a.astype(bf16), b.astype(bf16), preferred_element_type=f32)


def _mm_nt(a, b):
    return lax.dot_general(a.astype(bf16), b.astype(bf16), (((1,), (1,)), ((), ())), preferred_element_type=f32)


def _mm_tn(a, b):
    return lax.dot_general(a.astype(bf16), b.astype(bf16), (((0,), (0,)), ((), ())), preferred_element_type=f32)


def _mm_hi(a, b):
    return jnp.dot(a, b, precision=HI, preferred_element_type=f32)


def _softplus(x):
    return jnp.maximum(x, 0.0) + jnp.log1p(jnp.exp(-jnp.abs(x)))


def _sigmoid(x):
    return jax.nn.sigmoid(x)


def _silu(x):
    return x * _sigmoid(x)


def _iota2(shape):
    return lax.broadcasted_iota(jnp.int32, shape, 0), lax.broadcasted_iota(jnp.int32, shape, 1)


def _split3(x):
    hi = x.astype(bf16)
    r1 = x - hi.astype(f32)
    mid = r1.astype(bf16)
    return hi, mid, (r1 - mid.astype(f32)).astype(bf16)


def _sel_left(m01, x):
    return sum(jnp.dot(m01, p, preferred_element_type=f32) for p in _split3(x))


def _sel_right(x, m01):
    return sum(jnp.dot(p, m01, preferred_element_type=f32) for p in _split3(x))


def _sel_right1(x, m01):
    return jnp.dot(x.astype(bf16), m01, preferred_element_type=f32)


def _unit_lower_inverses(l_list):
    n = l_list[0].shape[0]
    r, c = _iota2((n, n))
    eye = jnp.where(r == c, 1.0, 0.0)
    ts = [eye for _ in l_list]
    for ls in range(n.bit_length() - 1):
        m = ((r >> (ls + 1)) == (c >> (ls + 1))) & (((r >> ls) & 1) == 1) & (((c >> ls) & 1) == 0)
        lms = [jnp.where(m, l, 0.0) for l in l_list]
        if ls == 0:
            ts = [t - lm for t, lm in zip(ts, lms)]
        else:
            tl = [_mm(t, lm) for t, lm in zip(ts, lms)]
            ts = [t - _mm(x, t) for t, x in zip(ts, tl)]
    return ts


def _in_proj_kernel(x_ref, nw_ref, w_ref, o_ref, sm_ref, h_scr):
    j = pl.program_id(1)

    @pl.when(j == 0)
    def _():
        x = x_ref[...]
        y = x * lax.rsqrt(jnp.mean(x * x, axis=-1, keepdims=True) + RMS_EPS)
        h_scr[...] = (y * nw_ref[...]).astype(bf16)

    tn = o_ref.shape[1]
    w = w_ref[:, pl.ds(pl.multiple_of(j * tn, tn), tn)]
    acc = jnp.dot(h_scr[...], w, preferred_element_type=f32)
    o_ref[...] = acc.astype(o_ref.dtype)

    @pl.when(j == OFF_SMALL // tn)
    def _():
        sm_ref[...] = acc[:, OFF_SMALL % tn:OFF_SMALL % tn + LANES]


def _in_proj(x2, nw, w):
    t, d = x2.shape
    n = w.shape[1]
    tm, tn = IN_PROJ_TILE_M, IN_PROJ_TILE_N
    return pl.pallas_call(
        _in_proj_kernel, name="in_proj",
        grid=(t // tm, n // tn),
        in_specs=[pl.BlockSpec((tm, d), lambda i, j: (i, 0)),
                  pl.BlockSpec((1, d), lambda i, j: (0, 0)),
                  pl.BlockSpec((d, n), lambda i, j: (0, 0), pipeline_mode=pl.Buffered(1))],
        out_specs=[pl.BlockSpec((tm, tn), lambda i, j: (i, j)),
                   pl.BlockSpec((tm, LANES), lambda i, j: (i, 0))],
        out_shape=[jax.ShapeDtypeStruct((t, n), bf16), jax.ShapeDtypeStruct((t, LANES), f32)],
        scratch_shapes=[pltpu.VMEM((tm, d), bf16)],
        compiler_params=_cparams(("parallel", "arbitrary"), 48),
    )(x2, nw, w)


def _ffn_kernel(x_ref, nw_ref, wi_ref, wd_ref, o_ref):
    x = x_ref[...]
    h = (x * lax.rsqrt(jnp.mean(x * x, axis=-1, keepdims=True) + RMS_EPS) * nw_ref[...]).astype(bf16)
    n = wd_ref.shape[0]
    step = n // FFN_HIDDEN_SPLITS
    acc = x
    for c0 in range(0, n, step):
        g = jnp.dot(h, wi_ref[:, c0:c0 + step], preferred_element_type=f32)
        u = jnp.dot(h, wi_ref[:, n + c0:n + c0 + step], preferred_element_type=f32)
        acc = acc + jnp.dot((_silu(g) * u).astype(bf16), wd_ref[c0:c0 + step, :], preferred_element_type=f32)
    o_ref[...] = acc


def _ffn(x2, nw, wi, wd):
    t, d = x2.shape
    n = wd.shape[0]
    tm = MM_TILE_M
    resident = lambda shape: pl.BlockSpec(shape, lambda i: (0, 0), pipeline_mode=pl.Buffered(1))
    return pl.pallas_call(
        _ffn_kernel, name="ffn",
        grid=(t // tm,),
        in_specs=[pl.BlockSpec((tm, d), lambda i: (i, 0)),
                  pl.BlockSpec((1, d), lambda i: (0, 0)),
                  resident((d, 2 * n)), resident((n, d))],
        out_specs=pl.BlockSpec((tm, d), lambda i: (i, 0)),
        out_shape=jax.ShapeDtypeStruct((t, d), f32),
        compiler_params=_cparams(("parallel",), 48),
    )(x2, nw, wi, wd)


def _top3_bias(gate_t, n_past):
    row = lax.broadcasted_iota(jnp.int32, gate_t.shape, 0)
    g = jnp.where(row < n_past, gate_t, -jnp.inf)
    sel = jnp.zeros(gate_t.shape, jnp.bool_)
    for _ in range(MOBA_TOPK):
        m = jnp.max(g, axis=0, keepdims=True)
        idx = jnp.min(jnp.where(g == m, row, gate_t.shape[0]), axis=0, keepdims=True)
        pick = row == idx
        sel = sel | (pick & (m > -jnp.inf))
        g = jnp.where(pick, -jnp.inf, g)
    return jnp.where(sel, 0.0, NEG)


def _moba_prep_kernel(a_ref, qw_ref, kw_ref, bd_ref, qaugt_ref, kaug_ref, vaugt_ref, kmean_scr):
    i = pl.program_id(1)

    @pl.when(i == 0)
    def _():
        kmean_scr[...] = jnp.zeros_like(kmean_scr)

    a = a_ref[...].astype(f32)
    q, k, v = a[:, :MIX_W], a[:, MIX_W:2 * MIX_W], a[:, 2 * MIX_W:]
    bd = bd_ref[...]
    qn = q * lax.rsqrt(_sel_right1(q * q, bd) + RMS_EPS) * qw_ref[...]
    kn = k * lax.rsqrt(_sel_right1(k * k, bd) + RMS_EPS) * kw_ref[...]
    lane = lax.broadcasted_iota(jnp.int32, (MOBA_BLOCK, LANES), 1)
    onehot = jnp.where(lane == i, 1.0, 0.0).astype(bf16)
    ones_row = jnp.where(lax.broadcasted_iota(jnp.int32, (VT_ROWS - HEAD_DIM, MOBA_BLOCK), 0) == 0, 1.0, 0.0)
    kmean = kmean_scr[...]
    nbp = kmean.shape[0]
    dim = lax.broadcasted_iota(jnp.int32, (LANES, MOBA_BLOCK), 0)
    sel_pad = jnp.zeros((LANES - nbp, MOBA_BLOCK), f32)
    for p in range(N_HEADS // 2):
        sl = slice(p * LANES, (p + 1) * LANES)
        kaug_ref[0, p] = jnp.concatenate([kn[:, sl].astype(bf16), onehot], axis=-1)
        vt = v[:, sl].T
        qt = (qn[:, sl] * (HEAD_DIM ** -0.5 * LOG2E)).T
        for hh in range(2):
            vaugt_ref[0, 2 * p + hh] = jnp.concatenate(
                [vt[hh * HEAD_DIM:(hh + 1) * HEAD_DIM], ones_row], axis=0).astype(bf16)
            keep = (dim < HEAD_DIM) if hh == 0 else (dim >= HEAD_DIM)
            q2t = jnp.where(keep, qt, 0.0)
            gate_t = _mm_hi(kmean[:, sl], q2t)
            qaugt_ref[0, 2 * p + hh] = jnp.concatenate([q2t, _top3_bias(gate_t, i), sel_pad], axis=0).astype(bf16)
    kmean_scr[pl.ds(i, 1), :] = jnp.mean(kn, axis=0, keepdims=True)


def _moba_prep(proj, bsz, seq, qw, kw, bd_mean):
    nb = seq // MOBA_BLOCK
    nbp = -(-nb // 8) * 8
    assert nbp <= LANES
    return pl.pallas_call(
        _moba_prep_kernel, name="moba_prep",
        grid=(bsz, nb),
        in_specs=[pl.BlockSpec((MOBA_BLOCK, 3 * MIX_W), lambda b, i: (b * nb + i, OFF_A // (3 * MIX_W))),
                  pl.BlockSpec((1, MIX_W), lambda b, i: (0, 0)),
                  pl.BlockSpec((1, MIX_W), lambda b, i: (0, 0)),
                  pl.BlockSpec((MIX_W, MIX_W), lambda b, i: (0, 0))],
        out_specs=[pl.BlockSpec((1, N_HEADS, 2 * LANES, MOBA_BLOCK), lambda b, i: (b, 0, 0, i)),
                   pl.BlockSpec((1, N_HEADS // 2, MOBA_BLOCK, 2 * LANES), lambda b, i: (b, 0, i, 0)),
                   pl.BlockSpec((1, N_HEADS, VT_ROWS, MOBA_BLOCK), lambda b, i: (b, 0, 0, i))],
        out_shape=[jax.ShapeDtypeStruct((bsz, N_HEADS, 2 * LANES, seq), bf16),
                   jax.ShapeDtypeStruct((bsz, N_HEADS // 2, seq, 2 * LANES), bf16),
                   jax.ShapeDtypeStruct((bsz, N_HEADS, VT_ROWS, seq), bf16)],
        scratch_shapes=[pltpu.VMEM((nbp, MIX_W), f32)],
        compiler_params=_cparams(("parallel", "arbitrary"), 32),
    )(proj, qw, kw, bd_mean)


def _moba_attn_kernel(qaugt_ref, kaug_ref, vaugt_ref, tabt_ref, o_ref, sa_scr, sb_scr, acc_scr):
    i = pl.program_id(2)
    blk = MOBA_BLOCK
    pairs = range(MOBA_PAIRS_PER_STEP)
    hs = range(2 * MOBA_PAIRS_PER_STEP)
    key, qry = _iota2((blk, blk))
    mm = lambda a, b: jnp.dot(a, b, preferred_element_type=f32)
    cmax = lambda s: jnp.max(s, axis=0, keepdims=True)
    pv = lambda v, pe: jnp.dot(v, pe.astype(bf16), preferred_element_type=f32)
    kblk = lambda pp, j: kaug_ref[0, pp, pl.ds(pl.multiple_of(j * blk, blk), blk), :]
    vblk = lambda hh, j: vaugt_ref[0, hh, :, pl.ds(pl.multiple_of(j * blk, blk), blk)]

    n_far = jnp.maximum(i - 1, 0)
    nk = MOBA_KV_PER_GROUP
    n_groups = (n_far + nk - 1) // nk
    blocks_of = lambda g: [jnp.where(nk * g + a < n_far, nk * g + a, i) for a in range(nk)]

    def scores(g, buf):
        qt = [qaugt_ref[0, hh] for hh in hs]
        for a, j in enumerate(blocks_of(g)):
            ks = [kblk(pp, j) for pp in pairs]
            for hh in hs:
                buf[hh, a] = mm(ks[hh // 2], qt[hh])

    scores(0, sa_scr)

    k_own = [kblk(pp, i) for pp in pairs]
    k_adj = [kblk(pp, n_far) for pp in pairs]
    qt = [qaugt_ref[0, hh] for hh in hs]
    s_own = [jnp.where(qry >= key, mm(k_own[hh // 2][:, :LANES], qt[hh][:LANES]) + tabt_ref[hh // 2, hh % 2, 1], NEG)
             for hh in hs]
    s_adj = [mm(k_adj[hh // 2], qt[hh]) + tabt_ref[hh // 2, hh % 2, 0] for hh in hs]
    m = [jnp.maximum(cmax(s_own[hh]), cmax(s_adj[hh])) for hh in hs]
    for hh in hs:
        acc_scr[hh] = (pv(vblk(hh, i), jnp.exp2(s_own[hh] - m[hh]))
                       + pv(vblk(hh, n_far), jnp.exp2(s_adj[hh] - m[hh])))

    def consume(g, buf, m):
        js = blocks_of(g)
        m_new = []
        for hh in hs:
            s = [buf[hh, a] for a in range(nk)]
            mh = functools.reduce(jnp.maximum, [cmax(x) for x in s], m[hh])
            ah = jnp.exp2(m[hh] - mh) * acc_scr[hh]
            for a in range(nk):
                ah = ah + pv(vblk(hh, js[a]), jnp.exp2(s[a] - mh))
            acc_scr[hh] = ah
            m_new.append(mh)
        return tuple(m_new)

    def body(u, m):
        scores(2 * u + 1, sb_scr)
        m = consume(2 * u, sa_scr, m)
        scores(2 * u + 2, sa_scr)
        return consume(2 * u + 1, sb_scr, m)

    m = lax.fori_loop(0, n_groups // 2, body, tuple(m))
    lax.cond(n_groups % 2 == 1, lambda mm_: consume(n_groups - 1, sa_scr, mm_), lambda mm_: mm_, m)
    o_ref[0] = jnp.concatenate(
        [(acc_scr[hh][:HEAD_DIM] / acc_scr[hh][HEAD_DIM:HEAD_DIM + 1]).T for hh in hs], axis=-1)


def _moba_attn(qaugt, kaug, vaugt, tabt):
    bsz, _, _, seq = qaugt.shape
    nb = seq // MOBA_BLOCK
    pp = MOBA_PAIRS_PER_STEP
    return pl.pallas_call(
        _moba_attn_kernel, name="moba_attn",
        grid=(bsz, N_HEADS // (2 * pp), nb),
        in_specs=[pl.BlockSpec((1, 2 * pp, 2 * LANES, MOBA_BLOCK), lambda b, p, i: (b, p, 0, i)),
                  pl.BlockSpec((1, pp, seq, 2 * LANES), lambda b, p, i: (b, p, 0, 0)),
                  pl.BlockSpec((1, 2 * pp, VT_ROWS, seq), lambda b, p, i: (b, p, 0, 0)),
                  pl.BlockSpec((pp, 2, 2, MOBA_BLOCK, MOBA_BLOCK), lambda b, p, i: (p, 0, 0, 0, 0))],
        out_specs=pl.BlockSpec((1, MOBA_BLOCK, pp * LANES), lambda b, p, i: (b, i, p)),
        out_shape=jax.ShapeDtypeStruct((bsz, seq, MIX_W), f32),
        scratch_shapes=[pltpu.VMEM((2 * pp, MOBA_KV_PER_GROUP, MOBA_BLOCK, MOBA_BLOCK), f32)] * 2
        + [pltpu.VMEM((2 * pp, VT_ROWS, MOBA_BLOCK), f32)],
        compiler_params=_cparams(("parallel", "parallel", "arbitrary"), 56),
    )(qaugt, kaug, vaugt, tabt)


def _t5_bucket(dist):
    n = jnp.maximum(dist, 0)
    max_exact = REL_BUCKETS // 2
    nf = jnp.maximum(n, max_exact).astype(f32)
    large = max_exact + (jnp.log(nf / max_exact) / math.log(REL_MAX_DIST / max_exact)
                         * (REL_BUCKETS - max_exact)).astype(jnp.int32)
    large = jnp.minimum(large, REL_BUCKETS - 1)
    return jnp.where(n < max_exact, n, large)


def _moba_bias_kernel(vec_ref, o_ref):
    blk = MOBA_BLOCK
    t = pltpu.roll(jnp.broadcast_to(vec_ref[0] * LOG2E, (blk, 2 * blk)), 0, 1, stride=1, stride_axis=0)
    o_ref[0, 0] = t[:, blk:]
    o_ref[0, 1] = t[:, :blk]


def _moba_bias_tables(rel_bias):
    assert MOBA_BLOCK >= REL_MAX_DIST
    by_dist = rel_bias.astype(f32)[_t5_bucket(jnp.arange(2 * MOBA_BLOCK))]
    far = rel_bias.astype(f32)[_t5_bucket(jnp.array(2 * MOBA_BLOCK))]
    vec = (by_dist - far).T.reshape(N_HEADS, 1, 2 * MOBA_BLOCK)
    tab = pl.pallas_call(
        _moba_bias_kernel, name="moba_bias",
        grid=(N_HEADS,),
        in_specs=[pl.BlockSpec((1, 1, 2 * MOBA_BLOCK), lambda h: (h, 0, 0))],
        out_specs=pl.BlockSpec((1, 2, MOBA_BLOCK, MOBA_BLOCK), lambda h: (h, 0, 0, 0)),
        out_shape=jax.ShapeDtypeStruct((N_HEADS, 2, MOBA_BLOCK, MOBA_BLOCK), f32),
        compiler_params=_cparams(("parallel",), 16),
    )(vec)
    return tab.reshape(N_HEADS // 2, 2, 2, MOBA_BLOCK, MOBA_BLOCK)


def _causal_conv(x, halo, w_ref):
    ts, nh = x.shape[0], halo.shape[0]
    xe = jnp.concatenate([halo, x], axis=0)
    acc = x * w_ref[CONV_K - 1:CONV_K, :]
    for d in range(1, CONV_K):
        acc = acc + xe[nh - d:nh - d + ts] * w_ref[CONV_K - 1 - d:CONV_K - d, :]
    return acc


def _tile_and_halo(x_ref, halo_ref):
    halo = jnp.where(pl.program_id(1) == 0, 0.0, halo_ref[...].astype(f32))
    return x_ref[...].astype(f32), halo


def _write_heads(o_ref, val):
    for h in range(N_HEADS):
        o_ref[0, h] = val[:, h * HEAD_DIM:(h + 1) * HEAD_DIM].astype(o_ref.dtype)


def _read_heads(ref):
    return jnp.concatenate([ref[0, h] for h in range(N_HEADS)], axis=-1)


def _hm_spec(rows):
    return pl.BlockSpec((1, N_HEADS, rows, HEAD_DIM), lambda b, i: (b, 0, i, 0))


def _hm_shape(bsz, seq, dtype=f32):
    return jax.ShapeDtypeStruct((bsz, N_HEADS, seq, HEAD_DIM), dtype)


def _chunk_sum_matrix(chunk):
    r = jnp.arange(CUMSUM_TILE)[:, None]
    c = jnp.arange(CUMSUM_TILE)[None, :]
    same = (r // chunk) == (c // chunk)
    return jnp.concatenate([same & (r >= c), same], axis=0).astype(bf16)


def _chunk_sums(cm_ref, x):
    parts = [_sel_left(cm_ref[...], x[r:r + CUMSUM_TILE]) for r in range(0, x.shape[0], CUMSUM_TILE)]
    return (jnp.concatenate([p[:CUMSUM_TILE] for p in parts], axis=0),
            jnp.concatenate([p[CUMSUM_TILE:] for p in parts], axis=0))


def _row_spec(width, off, nt):
    return pl.BlockSpec((ROW_TILE, width), lambda b, i: (b * nt + i, off // width))


def _halo_spec(width, off, nt):
    per = ROW_TILE // HALO_ROWS
    return pl.BlockSpec((HALO_ROWS, width), lambda b, i: (jnp.maximum((b * nt + i) * per - 1, 0), off // width))


def _small_spec(nt):
    return pl.BlockSpec((ROW_TILE, LANES), lambda b, i: (b * nt + i, 0))


def _const_spec(shape):
    return pl.BlockSpec(shape, lambda b, i: (0,) * len(shape))


def _gdn_prep_kernel(x_ref, halo_ref, sm_ref, cw_ref, alog_ref, dtb_ref, bd_ref, eb_ref, ea_ref, cm_ref,
                     q_ref, k_ref, kb_ref, vb_ref, qd_ref, kbe_ref, kd_ref, gc_ref):
    qkv = _silu(_causal_conv(*_tile_and_halo(x_ref, halo_ref), cw_ref))
    q, k, v = qkv[:, :MIX_W], qkv[:, MIX_W:2 * MIX_W], qkv[:, 2 * MIX_W:]
    bd = bd_ref[...]
    q = q * lax.rsqrt(_sel_right1(q * q, bd) + L2_EPS) * HEAD_DIM ** -0.5
    k = k * lax.rsqrt(_sel_right1(k * k, bd) + L2_EPS)
    sm = sm_ref[...]
    beta = _sel_right1(_sigmoid(sm), eb_ref[...])
    g = -jnp.exp(alog_ref[...]) * _softplus(sm + dtb_ref[...])
    gc, g_end = _chunk_sums(cm_ref, g)
    eg = jnp.exp(_sel_right(gc, ea_ref[...]))
    e_rest = jnp.exp(_sel_right(g_end - gc, ea_ref[...]))
    kb = k * beta
    _write_heads(q_ref, q)
    _write_heads(k_ref, k)
    _write_heads(kb_ref, kb)
    _write_heads(vb_ref, v * beta)
    _write_heads(qd_ref, q * eg)
    _write_heads(kbe_ref, kb * eg)
    _write_heads(kd_ref, k * e_rest)
    gc_ref[...] = gc[:, SM_BA:SM_BA + N_HEADS]


def _gdn_prep(proj, small, bsz, seq, conv_w, alog128, dtb128, bd_ones, e_beta, e_ba, cm):
    nt = seq // ROW_TILE
    w3 = 3 * MIX_W
    return pl.pallas_call(
        _gdn_prep_kernel, name="gdn_prep",
        grid=(bsz, nt),
        in_specs=[_row_spec(w3, OFF_BQKV, nt), _halo_spec(w3, OFF_BQKV, nt), _small_spec(nt),
                  _const_spec((CONV_K, w3)), _const_spec((1, LANES)), _const_spec((1, LANES)),
                  _const_spec((MIX_W, MIX_W)), _const_spec((LANES, MIX_W)), _const_spec((LANES, MIX_W)),
                  _const_spec((2 * CUMSUM_TILE, CUMSUM_TILE))],
        out_specs=[_hm_spec(ROW_TILE)] * 7 + [pl.BlockSpec((ROW_TILE, N_HEADS), lambda b, i: (b * nt + i, 0))],
        out_shape=[_hm_shape(bsz, seq, bf16)] * 7 + [jax.ShapeDtypeStruct((bsz * seq, N_HEADS), f32)],
        compiler_params=_cparams(("parallel", "parallel"), 40),
    )(proj, proj, small, conv_w, alog128, dtb128, bd_ones, e_beta, e_ba, cm)


def _gdn_chunk_kernel(q_ref, k_ref, kb_ref, vb_ref, qd_ref, kbe_ref, kd_ref, gc_ref, gct_ref, nw_ref, o_ref, st_scr):
    @pl.when(pl.program_id(1) == 0)
    def _():
        st_scr[...] = jnp.zeros_like(st_scr)

    n = GDN_CHUNK
    hs = range(N_HEADS)
    ch = [(ci, h) for ci in range(CHUNKS_PER_STEP) for h in hs]
    blk = lambda ref, ci, h: ref[0, h, ci * n:(ci + 1) * n, :]
    r, c = _iota2((n, n))
    gc_all = [gc_ref[0, ci] for ci in range(CHUNKS_PER_STEP)]
    gct_all = [gct_ref[0, ci] for ci in range(CHUNKS_PER_STEP)]
    decay = {x: jnp.exp(jnp.where(r >= c, gc_all[x[0]][:, x[1]:x[1] + 1] - gct_all[x[0]][x[1]:x[1] + 1, :], NEG))
             for x in ch}
    gram = {x: _mm_nt(jnp.concatenate([blk(kb_ref, *x), blk(q_ref, *x)], axis=0), blk(k_ref, *x)) for x in ch}
    t = dict(zip(ch, _unit_lower_inverses([jnp.where(r > c, gram[x][:n] * decay[x], 0.0) for x in ch])))
    u = {x: _mm(t[x], blk(vb_ref, *x)) for x in ch}
    w = {x: _mm(t[x], blk(kbe_ref, *x)) for x in ch}
    a_in = {x: gram[x][n:] * decay[x] for x in ch}
    st = [st_scr[h] for h in hs]
    for ci in range(CHUNKS_PER_STEP):
        ws = [_mm(jnp.concatenate([w[ci, h].astype(bf16), blk(qd_ref, ci, h)], axis=0), st[h]) for h in hs]
        v_new = [u[ci, h] - ws[h][:n] for h in hs]
        o = [ws[h][n:] + _mm(a_in[ci, h], v_new[h]) for h in hs]
        upd = [_mm_tn(blk(kd_ref, ci, h), v_new[h]) for h in hs]
        for h in hs:
            on = o[h] * lax.rsqrt(jnp.mean(o[h] * o[h], axis=-1, keepdims=True) + RMS_EPS) * nw_ref[...]
            o_ref[ci * n:(ci + 1) * n, h * HEAD_DIM:(h + 1) * HEAD_DIM] = on
        st = [st[h] * jnp.exp(gc_all[ci][n - 1:n, h:h + 1]) + upd[h] for h in hs]
    for h in hs:
        st_scr[h] = st[h]


def _tok_spec(rows, nsteps):
    return pl.BlockSpec((rows, MIX_W), lambda b, i: (b * nsteps + i, 0))


def _gdn_chunk(q, k, kb, vb, qd, kbe, kd, gc, norm_w):
    bsz, _, seq, _ = q.shape
    n = GDN_CHUNK
    nc = seq // n
    per = CHUNKS_PER_STEP
    gc4 = gc.reshape(bsz, nc, n, N_HEADS)
    gct4 = jnp.swapaxes(gc4, 2, 3)
    return pl.pallas_call(
        _gdn_chunk_kernel, name="gdn_chunk",
        grid=(bsz, nc // per),
        in_specs=[_hm_spec(per * n)] * 7 + [pl.BlockSpec((1, per, n, N_HEADS), lambda b, i: (b, i, 0, 0)),
                                            pl.BlockSpec((1, per, N_HEADS, n), lambda b, i: (b, i, 0, 0)),
                                            _const_spec((1, HEAD_DIM))],
        out_specs=_tok_spec(per * n, nc // per),
        out_shape=jax.ShapeDtypeStruct((bsz * seq, MIX_W), f32),
        scratch_shapes=[pltpu.VMEM((N_HEADS, HEAD_DIM, HEAD_DIM), f32)],
        compiler_params=_cparams(("parallel", "arbitrary"), 32),
    )(q, k, kb, vb, qd, kbe, kd, gc4, gct4, norm_w)


def _rwkv_prep_kernel(has_vres, *refs):
    if has_vres:
        (c_ref, halo_ref, mu_ref, w0_ref, wup_ref, a0_ref, aup_ref, gup_ref, kk_ref, ka_ref, bd_ref,
         rk_ref, cm_ref, vf_ref, v0_ref, vdn_ref, vup_ref,
         rt_ref, at_ref, bt_ref, kt_ref, bp_ref, kp_ref, v_ref, pe_ref, gout_ref, bonus_ref) = refs
    else:
        (c_ref, halo_ref, mu_ref, w0_ref, wup_ref, a0_ref, aup_ref, gup_ref, kk_ref, ka_ref, bd_ref,
         rk_ref, cm_ref,
         rt_ref, at_ref, bt_ref, kt_ref, bp_ref, kp_ref, v_ref, pe_ref, gout_ref, bonus_ref, cv_ref) = refs
    c, halo = _tile_and_halo(c_ref, halo_ref)
    prev = jnp.concatenate([halo[HALO_ROWS - 1:], c[:-1]], axis=0)
    c = c + (prev - c) * mu_ref[...]
    c_r, c_k, c_v = c[:, :MIX_W], c[:, MIX_W:2 * MIX_W], c[:, 2 * MIX_W:3 * MIX_W]
    c_wd = c[:, 3 * MIX_W:3 * MIX_W + 64]
    c_ad = c[:, 3 * MIX_W + 64:3 * MIX_W + 128]
    c_gd = c[:, 3 * MIX_W + 128:]
    w_log = -_softplus(-(w0_ref[...] + _mm(jnp.tanh(c_wd), wup_ref[...]))) - 0.5
    a_in = _sigmoid(a0_ref[...] + _mm(c_ad, aup_ref[...]))
    gout_ref[...] = _mm(_sigmoid(c_gd), gup_ref[...])
    if has_vres:
        lam = _sigmoid(v0_ref[...] + _mm(_mm(c_v, vdn_ref[...]), vup_ref[...]))
        v_r = c_v + (vf_ref[...] - c_v) * lam
    else:
        v_r = c_v
        cv_ref[...] = c_v
    bd = bd_ref[...]
    kk = c_k * kk_ref[...]
    kk = kk * lax.rsqrt(_sel_right1(kk * kk, bd) + L2_EPS)
    k_r = c_k * (1.0 + (a_in - 1.0) * ka_ref[...])
    b = kk * a_in
    bonus_ref[...] = _sel_right1(c_r * k_r * rk_ref[...], bd) * v_r
    lc, lc_end = _chunk_sums(cm_ref, -jnp.exp(w_log))
    e_neg = jnp.exp(-lc)
    e_rest = jnp.exp(lc_end - lc)
    _write_heads(rt_ref, c_r * jnp.exp(lc))
    _write_heads(at_ref, -kk * jnp.exp(lc + jnp.exp(w_log)))
    _write_heads(bt_ref, b * e_neg)
    _write_heads(kt_ref, k_r * e_neg)
    _write_heads(bp_ref, b * e_rest)
    _write_heads(kp_ref, k_r * e_rest)
    _write_heads(v_ref, v_r)
    _write_heads(pe_ref, jnp.exp(lc_end))


def _rwkv_prep(proj, bsz, seq, mu, w0, w_up, a0, a_up, g_up, k_k, k_a, r_k, bd_ones, cm, vres):
    nt = seq // ROW_TILE
    wc = 3 * MIX_W + 256
    std = pl.BlockSpec((ROW_TILE, MIX_W), lambda b, i: (b * nt + i, 0))
    std_shape = jax.ShapeDtypeStruct((bsz * seq, MIX_W), f32)
    in_specs = [_row_spec(wc, OFF_C, nt), _halo_spec(wc, OFF_C, nt), _const_spec((1, wc)),
                _const_spec((1, MIX_W)), _const_spec((64, MIX_W)), _const_spec((1, MIX_W)), _const_spec((64, MIX_W)),
                _const_spec((128, MIX_W)), _const_spec((1, MIX_W)), _const_spec((1, MIX_W)), _const_spec((MIX_W, MIX_W)),
                _const_spec((1, MIX_W)), _const_spec((2 * CUMSUM_TILE, CUMSUM_TILE))]
    args = [proj, proj, mu, w0, w_up, a0, a_up, g_up, k_k, k_a, bd_ones, r_k, cm]
    out_specs = [_hm_spec(ROW_TILE)] * 8 + [std, std]
    out_shape = [_hm_shape(bsz, seq, bf16)] * 7 + [_hm_shape(bsz, seq), std_shape, std_shape]
    if vres is not None:
        v_first, v0, v_down, v_up = vres
        in_specs += [std, _const_spec((1, MIX_W)), _const_spec(v_down.shape), _const_spec(v_up.shape)]
        args += [v_first, v0, v_down, v_up]
    else:
        out_specs.append(std)
        out_shape.append(std_shape)
    return pl.pallas_call(
        functools.partial(_rwkv_prep_kernel, vres is not None), name="rwkv_prep",
        grid=(bsz, nt), in_specs=in_specs, out_specs=out_specs, out_shape=out_shape,
        compiler_params=_cparams(("parallel", "parallel"), 40),
    )(*args)


def _rwkv_chunk_kernel(rt_ref, at_ref, bt_ref, kt_ref, bp_ref, kp_ref, v_ref, pe_ref, o_ref, st_scr):
    @pl.when(pl.program_id(1) == 0)
    def _():
        st_scr[...] = jnp.zeros_like(st_scr)

    n = RWKV_CHUNK
    hs = range(N_HEADS)
    ch = [(ci, h) for ci in range(CHUNKS_PER_STEP) for h in hs]
    blk = lambda ref, ci, h: ref[0, h, ci * n:(ci + 1) * n, :]
    row, col = _iota2((2 * n, 2 * n))
    rr, cc = row & (n - 1), col & (n - 1)
    mask = rr + jnp.where(row < n, 0, 1) > cc
    lhs = {x: jnp.concatenate([blk(at_ref, *x), blk(rt_ref, *x)], axis=0) for x in ch}
    gm = {x: jnp.where(mask, _mm_nt(lhs[x], jnp.concatenate([blk(bt_ref, *x), blk(kt_ref, *x)], axis=0)), 0.0)
          for x in ch}
    t = dict(zip(ch, _unit_lower_inverses([-gm[x][:n, :n] for x in ch])))
    v = {x: blk(v_ref, *x) for x in ch}
    makv = {x: _mm(gm[x][:n], jnp.concatenate([jnp.zeros_like(v[x]), v[x]], axis=0)) for x in ch}
    st = [st_scr[h] for h in hs]
    for ci in range(CHUNKS_PER_STEP):
        ah = [_mm_nt(lhs[ci, h], st[h]) for h in hs]
        u = [_mm(t[ci, h], ah[h][:n] + makv[ci, h]) for h in hs]
        uv = [jnp.concatenate([u[h].astype(bf16), v[ci, h]], axis=0) for h in hs]
        o = [ah[h][n:] + _mm(gm[ci, h][n:], uv[h]) for h in hs]
        upd = [_mm_tn(uv[h], jnp.concatenate([blk(bp_ref, ci, h), blk(kp_ref, ci, h)], axis=0)) for h in hs]
        for h in hs:
            o_ref[0, h, ci * n:(ci + 1) * n, :] = o[h]
        st = [st[h] * pe_ref[0, h, ci * n:ci * n + 1, :] + upd[h] for h in hs]
    for h in hs:
        st_scr[h] = st[h]


def _rwkv_chunk(rt, at, bt, kt, bp, kp, v, pe):
    bsz, _, seq, _ = rt.shape
    n = RWKV_CHUNK * CHUNKS_PER_STEP
    return pl.pallas_call(
        _rwkv_chunk_kernel, name="rwkv_chunk",
        grid=(bsz, seq // n),
        in_specs=[_hm_spec(n)] * 8,
        out_specs=_hm_spec(n),
        out_shape=_hm_shape(bsz, seq),
        scratch_shapes=[pltpu.VMEM((N_HEADS, HEAD_DIM, HEAD_DIM), f32)],
        compiler_params=_cparams(("parallel", "arbitrary"), 32),
    )(rt, at, bt, kt, bp, kp, v, pe)


def _ssd_prep_kernel(x_ref, halo_ref, sm_ref, cw_ref, cb_ref, alog_ref, dtb_ref, edt_ref, cm_ref,
                     xdt_ref, x_out_ref, bc_ref, acs_ref):
    xbc = _silu(_causal_conv(*_tile_and_halo(x_ref, halo_ref), cw_ref) + cb_ref[...])
    m_x = xbc[:, :MIX_W]
    dt = _softplus(sm_ref[...] + dtb_ref[...])
    _write_heads(xdt_ref, m_x * _sel_right1(dt, edt_ref[...]))
    _write_heads(x_out_ref, m_x)
    bc_ref[...] = xbc[:, MIX_W:].astype(bc_ref.dtype)
    acs, _ = _chunk_sums(cm_ref, dt * -jnp.exp(alog_ref[...]))
    acs_ref[...] = acs[:, SM_DT:SM_DT + N_HEADS]


def _ssd_prep(proj, small, bsz, seq, conv_w, conv_b, alog128, dtb128, e_dt, cm):
    nt = seq // ROW_TILE
    wx = MIX_W + 4 * SSM_STATE
    return pl.pallas_call(
        _ssd_prep_kernel, name="ssd_prep",
        grid=(bsz, nt),
        in_specs=[_row_spec(wx, OFF_DXBC, nt), _halo_spec(wx, OFF_DXBC, nt), _small_spec(nt),
                  _const_spec((CONV_K, wx)), _const_spec((1, wx)), _const_spec((1, LANES)), _const_spec((1, LANES)),
                  _const_spec((LANES, MIX_W)), _const_spec((2 * CUMSUM_TILE, CUMSUM_TILE))],
        out_specs=[_hm_spec(ROW_TILE)] * 2 + [pl.BlockSpec((ROW_TILE, 4 * SSM_STATE), lambda b, i: (b * nt + i, 0)),
                                              pl.BlockSpec((ROW_TILE, N_HEADS), lambda b, i: (b * nt + i, 0))],
        out_shape=[_hm_shape(bsz, seq, bf16), _hm_shape(bsz, seq),
                   jax.ShapeDtypeStruct((bsz * seq, 4 * SSM_STATE), bf16),
                   jax.ShapeDtypeStruct((bsz * seq, N_HEADS), f32)],
        compiler_params=_cparams(("parallel", "parallel"), 40),
    )(proj, proj, small, conv_w, conv_b, alog128, dtb128, e_dt, cm)


def _ssd_chunk_kernel(xdt_ref, x_ref, bc_ref, a_ref, at_ref, dvec_ref, o_ref, st_scr):
    @pl.when(pl.program_id(1) == 0)
    def _():
        st_scr[...] = jnp.zeros_like(st_scr)

    n = SSD_CHUNK
    hs = range(N_HEADS)
    cis = range(SSD_CHUNKS_PER_STEP)
    heads_per_group = N_HEADS // 2
    grp = lambda h: h // heads_per_group
    ch = [(ci, h) for ci in cis for h in hs]
    rows = lambda ci: slice(ci * n, (ci + 1) * n)
    r, c = _iota2((n, n))
    b_g = {(ci, g): bc_ref[rows(ci), g * SSM_STATE:(g + 1) * SSM_STATE] for ci in cis for g in range(2)}
    c_g = {(ci, g): bc_ref[rows(ci), (2 + g) * SSM_STATE:(3 + g) * SSM_STATE] for ci in cis for g in range(2)}
    cb = {x: _mm_nt(c_g[x], b_g[x]) for x in b_g}
    ac = {(ci, h): a_ref[0, ci][:, h:h + 1] for ci, h in ch}
    a_last = {(ci, h): a_ref[0, ci][n - 1:n, h:h + 1] for ci, h in ch}
    lmat = {(ci, h): jnp.exp(jnp.where(r >= c, ac[ci, h] - at_ref[0, ci][h:h + 1, :], NEG)) for ci, h in ch}
    xg = {(ci, h): xdt_ref[0, h, rows(ci), :] for ci, h in ch}
    y_diag = {(ci, h): _mm(cb[ci, grp(h)] * lmat[ci, h], xg[ci, h]) for ci, h in ch}
    upd = {(ci, h): _mm_tn(b_g[ci, grp(h)].astype(f32) * jnp.exp(a_last[ci, h] - ac[ci, h]), xg[ci, h]) for ci, h in ch}
    c_in = {(ci, h): c_g[ci, grp(h)].astype(f32) * jnp.exp(ac[ci, h]) for ci, h in ch}
    st = [st_scr[h] for h in hs]
    for ci in cis:
        y_off = [_mm(c_in[ci, h], st[h]) for h in hs]
        for h in hs:
            o_ref[rows(ci), h * HEAD_DIM:(h + 1) * HEAD_DIM] = (
                y_diag[ci, h] + y_off[h] + x_ref[0, h, rows(ci), :] * dvec_ref[h:h + 1, :])
        st = [st[h] * jnp.exp(a_last[ci, h]) + upd[ci, h] for h in hs]
    for h in hs:
        st_scr[h] = st[h]


def _ssd_chunk(xdt, x, bc, acs, dvec):
    bsz, _, seq, _ = xdt.shape
    n = SSD_CHUNK
    nc = seq // n
    per = SSD_CHUNKS_PER_STEP
    a4 = acs.reshape(bsz, nc, n, N_HEADS)
    at4 = jnp.swapaxes(a4, 2, 3)
    return pl.pallas_call(
        _ssd_chunk_kernel, name="ssd_chunk",
        grid=(bsz, nc // per),
        in_specs=[_hm_spec(per * n)] * 2 + [pl.BlockSpec((per * n, 4 * SSM_STATE), lambda b, i: (b * (nc // per) + i, 0)),
                                            pl.BlockSpec((1, per, n, N_HEADS), lambda b, i: (b, i, 0, 0)),
                                            pl.BlockSpec((1, per, N_HEADS, n), lambda b, i: (b, i, 0, 0)),
                                            _const_spec((N_HEADS, HEAD_DIM))],
        out_specs=_tok_spec(per * n, nc // per),
        out_shape=jax.ShapeDtypeStruct((bsz * seq, MIX_W), f32),
        scratch_shapes=[pltpu.VMEM((N_HEADS, SSM_STATE, HEAD_DIM), f32)],
        compiler_params=_cparams(("parallel", "arbitrary"), 32),
    )(xdt, x, bc, a4, at4, dvec)


def _merge_kernel(x_ref, ya_ref, ob_ref, bz_ref, wkv_ref, bonus_ref, gout_ref, lnw_ref, lnb_ref,
                  yd_ref, dz_ref, mnw_ref, g0_ref, g1_ref, g2_ref, g3_ref, wb_ref, wo_ref, o_ref):
    def gated(n, y, g_ref):
        return _sigmoid(g_ref[...].astype(f32)) * _mm(y, wb_ref[n])

    acc = gated(0, ya_ref[0], g0_ref)
    acc = acc + gated(1, ob_ref[...] * _silu(bz_ref[...].astype(f32)), g1_ref)

    def wkv_head(h):
        w = wkv_ref[0, h]
        mu = jnp.mean(w, axis=-1, keepdims=True)
        var = jnp.mean(jnp.square(w - mu), axis=-1, keepdims=True)
        return (w - mu) * lax.rsqrt(var + RWKV_LN_EPS)

    wkv_ln = jnp.concatenate([wkv_head(h) for h in range(N_HEADS)], axis=-1)
    y_c = (wkv_ln * lnw_ref[...] + lnb_ref[...] + bonus_ref[...]) * gout_ref[...]
    acc = acc + gated(2, y_c, g2_ref)

    yz = yd_ref[...] * _silu(dz_ref[...].astype(f32))
    half = MIX_W // 2
    y_d = jnp.concatenate(
        [yz[:, s:s + half] * lax.rsqrt(jnp.mean(jnp.square(yz[:, s:s + half]), axis=-1, keepdims=True) + RMS_EPS)
         for s in (0, half)], axis=-1) * mnw_ref[...]
    acc = acc + gated(3, y_d, g3_ref)
    o_ref[...] = x_ref[...] + _mm(acc, wo_ref[...])


def _merge(x2, proj, bsz, seq, ya, ob, wkv, bonus, gout, lnw, lnb, yd, mnw, wb, wo):
    nt = seq // ROW_TILE
    std = lambda w: pl.BlockSpec((ROW_TILE, w), lambda b, i: (b * nt + i, 0))
    gate = lambda n: _row_spec(D_MODEL, OFF_GATES + n * D_MODEL, nt)
    return pl.pallas_call(
        _merge_kernel, name="merge",
        grid=(bsz, nt),
        in_specs=[std(D_MODEL), pl.BlockSpec((1, ROW_TILE, MIX_W), lambda b, i: (b, i, 0)),
                  std(MIX_W), _row_spec(MIX_W, OFF_BZ, nt),
                  _hm_spec(ROW_TILE), std(MIX_W), std(MIX_W), _const_spec((1, MIX_W)), _const_spec((1, MIX_W)),
                  std(MIX_W), _row_spec(MIX_W, OFF_DZ, nt), _const_spec((1, MIX_W)),
                  gate(0), gate(1), gate(2), gate(3),
                  _const_spec((4, MIX_W, D_MODEL)), _const_spec((D_MODEL, D_MODEL))],
        out_specs=std(D_MODEL),
        out_shape=jax.ShapeDtypeStruct((bsz * seq, D_MODEL), f32),
        compiler_params=_cparams(("parallel", "parallel"), 48),
    )(x2, ya, ob, proj, wkv, bonus, gout, lnw, lnb, yd, proj, mnw, proj, proj, proj, proj, wb, wo)


def _lane_vec(vals, off):
    return jnp.zeros((1, LANES), f32).at[0, off:off + vals.shape[0]].set(vals)


def _head_expand(off):
    n = jnp.arange(LANES)[:, None]
    c = jnp.arange(MIX_W)[None, :]
    return (n - off == c // HEAD_DIM).astype(bf16)


def _pack_w_in(w):
    w = w.astype(bf16)
    pad = lambda n: jnp.zeros((w.shape[0], n), w.dtype)
    cols = [w[:, W_A:W_BZ],
            w[:, W_BZ:W_BBETA],
            w[:, W_C:W_DZ],
            w[:, W_BBETA:W_C], w[:, W_DDT:W_GATES], pad(2 * LANES - 3 * N_HEADS),
            w[:, W_DZ:W_DXBC], w[:, W_DXBC:W_DDT], w[:, W_GATES:]]
    out = jnp.concatenate(cols, axis=1).astype(bf16)
    assert out.shape[1] == N_PROJ
    return out


def kernel(x, rel_bias, norm1_w, w_in, moba_q_norm, moba_k_norm, gdn_conv_w, gdn_A_log, gdn_dt_bias, gdn_norm_w, rwkv_mu, rwkv_w0, rwkv_w_up, rwkv_a0, rwkv_a_up, rwkv_g_up, rwkv_k_k, rwkv_k_a, rwkv_r_k, rwkv_v0, rwkv_v_down, rwkv_v_up, rwkv_ln_w, rwkv_ln_b, mamba_conv_w, mamba_conv_b, mamba_dt_bias, mamba_A_log, mamba_D, mamba_norm_w, w_branch, w_out, norm2_w, ffn_w_in, ffn_w_down):
    bsz, seq, d = x.shape
    depth = w_in.shape[0]
    assert d == D_MODEL and (bsz * seq) % IN_PROJ_TILE_M == 0 and seq % MM_TILE_M == 0
    x2 = x.reshape(bsz * seq, d)
    row = lambda v: v.reshape(1, -1).astype(f32)

    hid = jnp.arange(MIX_W) // HEAD_DIM
    bd_ones = (hid[:, None] == hid[None, :]).astype(bf16)
    bd_mean = (bd_ones.astype(f32) / HEAD_DIM).astype(bf16)
    e_beta, e_ba, e_dt = _head_expand(SM_BETA), _head_expand(SM_BA), _head_expand(SM_DT)
    cm64, cm128 = _chunk_sum_matrix(GDN_CHUNK), _chunk_sum_matrix(SSD_CHUNK)
    assert GDN_CHUNK == RWKV_CHUNK
    tab = _moba_bias_tables(rel_bias)
    v_first = None
    for i in range(depth):
        proj, small = _in_proj(x2, row(norm1_w[i]), _pack_w_in(w_in[i]))

        qaug, kaug, v_a = _moba_prep(proj, bsz, seq, row(jnp.tile(moba_q_norm[i], N_HEADS)),
                                     row(jnp.tile(moba_k_norm[i], N_HEADS)), bd_mean)
        y_a = _moba_attn(qaug, kaug, v_a, tab)

        gdn_in = _gdn_prep(proj, small, bsz, seq, gdn_conv_w[i], _lane_vec(gdn_A_log[i], SM_BA),
                           _lane_vec(gdn_dt_bias[i], SM_BA), bd_ones, e_beta, e_ba, cm64)
        o_b = _gdn_chunk(*gdn_in, row(gdn_norm_w[i]))

        vres = None if i == 0 else (v_first, row(rwkv_v0[i - 1]), rwkv_v_down[i - 1].astype(bf16),
                                    rwkv_v_up[i - 1].astype(bf16))
        outs = _rwkv_prep(proj, bsz, seq, row(rwkv_mu[i]), row(rwkv_w0[i]), rwkv_w_up[i].astype(bf16),
                          row(rwkv_a0[i]), rwkv_a_up[i].astype(bf16), rwkv_g_up[i].astype(bf16),
                          row(rwkv_k_k[i]), row(rwkv_k_a[i]), row(rwkv_r_k[i]), bd_ones, cm64, vres)
        g_out, bonus = outs[8], outs[9]
        if i == 0:
            v_first = outs[10]
        wkv = _rwkv_chunk(*outs[:8])

        xdt, x_d, bc, acs = _ssd_prep(proj, small, bsz, seq, mamba_conv_w[i], row(mamba_conv_b[i]),
                                      _lane_vec(mamba_A_log[i], SM_DT), _lane_vec(mamba_dt_bias[i], SM_DT), e_dt, cm128)
        y_d = _ssd_chunk(xdt, x_d, bc, acs, jnp.broadcast_to(mamba_D[i][:, None], (N_HEADS, HEAD_DIM)).astype(f32))

        x2 = _merge(x2, proj, bsz, seq, y_a, o_b, wkv, bonus, g_out,
                    row(rwkv_ln_w[i]), row(rwkv_ln_b[i]), y_d, row(mamba_norm_w[i]),
                    w_branch[i].astype(bf16), w_out[i].astype(bf16))

        x2 = _ffn(x2, row(norm2_w[i]), ffn_w_in[i].astype(bf16), ffn_w_down[i].astype(bf16))
    return x2.reshape(bsz, seq, d)
```

```python
import functools
import math

import jax
import jax.numpy as jnp
from jax import lax
from jax.experimental import pallas as pl
from jax.experimental.pallas import tpu as pltpu

f32, bf16 = jnp.float32, jnp.bfloat16
HI = lax.Precision.HIGHEST

D_MODEL = 1024
N_HEADS = 8
HEAD_DIM = 64
MIX_W = N_HEADS * HEAD_DIM
RMS_EPS = 1e-6
L2_EPS = 1e-6
CONV_K = 4
MOBA_BLOCK = 256
MOBA_TOPK = 3
MOBA_PAIRS_PER_STEP = 2
MOBA_KV_PER_GROUP = 2
REL_BUCKETS = 32
REL_MAX_DIST = 128
GDN_CHUNK = 64
RWKV_CHUNK = 64
CHUNKS_PER_STEP = 4
RWKV_LN_EPS = 64e-5
SSM_STATE = 128
SSD_CHUNK = 128
SSD_CHUNKS_PER_STEP = 2
FFN_HIDDEN_SPLITS = 2
NEG = -1e30
LOG2E = math.log2(math.e)

LANES = 128
BF16_SUBLANES = 16
VT_ROWS = HEAD_DIM + BF16_SUBLANES

OFF_A, OFF_BQKV, OFF_BZ, OFF_C, OFF_SMALL, OFF_DZ, OFF_DXBC, OFF_GATES = 0, 1536, 3072, 3584, 5376, 5632, 6144, 7168
N_PROJ = OFF_GATES + 4 * D_MODEL
SM_BETA, SM_BA, SM_DT = 0, 8, 16
W_A, W_BQKV, W_BZ, W_BBETA, W_BA, W_C, W_DZ, W_DXBC, W_DDT, W_GATES = 0, 1536, 3072, 3584, 3592, 3600, 5392, 5904, 6928, 6936

ROW_TILE = 512
CUMSUM_TILE = 256
MM_TILE_M = 512
IN_PROJ_TILE_M, IN_PROJ_TILE_N = 1024, 1024
HALO_ROWS = BF16_SUBLANES


def _cparams(sem, vmem_mb):
    return pltpu.CompilerParams(dimension_semantics=sem, vmem_limit_bytes=vmem_mb * 1024 * 1024)


def _mm(a, b):
    return jnp.dot(a.astype(bf16), b.astype(bf16), preferred_element_type=f32)


def _mm_nt(a, b):
    return lax.dot_general(a.astype(bf16), b.astype(bf16), (((1,), (1,)), ((), ())), preferred_element_type=f32)


def _mm_tn(a, b):
    return lax.dot_general(a.astype(bf16), b.astype(bf16), (((0,), (0,)), ((), ())), preferred_element_type=f32)


def _mm_hi(a, b):
    return jnp.dot(a, b, precision=HI, preferred_element_type=f32)


def _softplus(x):
    return jnp.maximum(x, 0.0) + jnp.log1p(jnp.exp(-jnp.abs(x)))


def _sigmoid(x):
    return jax.nn.sigmoid(x)


def _silu(x):
    return x * _sigmoid(x)


def _iota2(shape):
    return lax.broadcasted_iota(jnp.int32, shape, 0), lax.broadcasted_iota(jnp.int32, shape, 1)


def _split3(x):
    hi = x.astype(bf16)
    r1 = x - hi.astype(f32)
    mid = r1.astype(bf16)
    return hi, mid, (r1 - mid.astype(f32)).astype(bf16)


def _sel_left(m01, x):
    return sum(jnp.dot(m01, p, preferred_element_type=f32) for p in _split3(x))


def _sel_right(x, m01):
    return sum(jnp.dot(p, m01, preferred_element_type=f32) for p in _split3(x))


def _sel_right1(x, m01):
    return jnp.dot(x.astype(bf16), m01, preferred_element_type=f32)


def _unit_lower_inverses(l_list):
    n = l_list[0].shape[0]
    r, c = _iota2((n, n))
    eye = jnp.where(r == c, 1.0, 0.0)
    ts = [eye for _ in l_list]
    for ls in range(n.bit_length() - 1):
        m = ((r >> (ls + 1)) == (c >> (ls + 1))) & (((r >> ls) & 1) == 1) & (((c >> ls) & 1) == 0)
        lms = [jnp.where(m, l, 0.0) for l in l_list]
        if ls == 0:
            ts = [t - lm for t, lm in zip(ts, lms)]
        else:
            tl = [_mm(t, lm) for t, lm in zip(ts, lms)]
            ts = [t - _mm(x, t) for t, x in zip(ts, tl)]
    return ts


def _in_proj_kernel(x_ref, nw_ref, w_ref, o_ref, sm_ref, h_scr):
    j = pl.program_id(1)

    @pl.when(j == 0)
    def _():
        x = x_ref[...]
        y = x * lax.rsqrt(jnp.mean(x * x, axis=-1, keepdims=True) + RMS_EPS)
        h_scr[...] = (y * nw_ref[...]).astype(bf16)

    tn = o_ref.shape[1]
    w = w_ref[:, pl.ds(pl.multiple_of(j * tn, tn), tn)]
    acc = jnp.dot(h_scr[...], w, preferred_element_type=f32)
    o_ref[...] = acc.astype(o_ref.dtype)

    @pl.when(j == OFF_SMALL // tn)
    def _():
        sm_ref[...] = acc[:, OFF_SMALL % tn:OFF_SMALL % tn + LANES]


def _in_proj(x2, nw, w):
    t, d = x2.shape
    n = w.shape[1]
    tm, tn = IN_PROJ_TILE_M, IN_PROJ_TILE_N
    return pl.pallas_call(
        _in_proj_kernel, name="in_proj",
        grid=(t // tm, n // tn),
        in_specs=[pl.BlockSpec((tm, d), lambda i, j: (i, 0)),
                  pl.BlockSpec((1, d), lambda i, j: (0, 0)),
                  pl.BlockSpec((d, n), lambda i, j: (0, 0), pipeline_mode=pl.Buffered(1))],
        out_specs=[pl.BlockSpec((tm, tn), lambda i, j: (i, j)),
                   pl.BlockSpec((tm, LANES), lambda i, j: (i, 0))],
        out_shape=[jax.ShapeDtypeStruct((t, n), bf16), jax.ShapeDtypeStruct((t, LANES), f32)],
        scratch_shapes=[pltpu.VMEM((tm, d), bf16)],
        compiler_params=_cparams(("parallel", "arbitrary"), 48),
    )(x2, nw, w)


def _ffn_kernel(x_ref, nw_ref, wi_ref, wd_ref, o_ref):
    x = x_ref[...]
    h = (x * lax.rsqrt(jnp.mean(x * x, axis=-1, keepdims=True) + RMS_EPS) * nw_ref[...]).astype(bf16)
    n = wd_ref.shape[0]
    step = n // FFN_HIDDEN_SPLITS
    acc = x
    for c0 in range(0, n, step):
        g = jnp.dot(h, wi_ref[:, c0:c0 + step], preferred_element_type=f32)
        u = jnp.dot(h, wi_ref[:, n + c0:n + c0 + step], preferred_element_type=f32)
        acc = acc + jnp.dot((_silu(g) * u).astype(bf16), wd_ref[c0:c0 + step, :], preferred_element_type=f32)
    o_ref[...] = acc


def _ffn(x2, nw, wi, wd):
    t, d = x2.shape
    n = wd.shape[0]
    tm = MM_TILE_M
    resident = lambda shape: pl.BlockSpec(shape, lambda i: (0, 0), pipeline_mode=pl.Buffered(1))
    return pl.pallas_call(
        _ffn_kernel, name="ffn",
        grid=(t // tm,),
        in_specs=[pl.BlockSpec((tm, d), lambda i: (i, 0)),
                  pl.BlockSpec((1, d), lambda i: (0, 0)),
                  resident((d, 2 * n)), resident((n, d))],
        out_specs=pl.BlockSpec((tm, d), lambda i: (i, 0)),
        out_shape=jax.ShapeDtypeStruct((t, d), f32),
        compiler_params=_cparams(("parallel",), 48),
    )(x2, nw, wi, wd)


def _top3_bias(gate_t, n_past):
    row = lax.broadcasted_iota(jnp.int32, gate_t.shape, 0)
    g = jnp.where(row < n_past, gate_t, -jnp.inf)
    sel = jnp.zeros(gate_t.shape, jnp.bool_)
    for _ in range(MOBA_TOPK):
        m = jnp.max(g, axis=0, keepdims=True)
        idx = jnp.min(jnp.where(g == m, row, gate_t.shape[0]), axis=0, keepdims=True)
        pick = row == idx
        sel = sel | (pick & (m > -jnp.inf))
        g = jnp.where(pick, -jnp.inf, g)
    return jnp.where(sel, 0.0, NEG)


def _moba_prep_kernel(a_ref, qw_ref, kw_ref, bd_ref, qaugt_ref, kaug_ref, vaugt_ref, kmean_scr):
    i = pl.program_id(1)

    @pl.when(i == 0)
    def _():
        kmean_scr[...] = jnp.zeros_like(kmean_scr)

    a = a_ref[...].astype(f32)
    q, k, v = a[:, :MIX_W], a[:, MIX_W:2 * MIX_W], a[:, 2 * MIX_W:]
    bd = bd_ref[...]
    qn = q * lax.rsqrt(_sel_right1(q * q, bd) + RMS_EPS) * qw_ref[...]
    kn = k * lax.rsqrt(_sel_right1(k * k, bd) + RMS_EPS) * kw_ref[...]
    lane = lax.broadcasted_iota(jnp.int32, (MOBA_BLOCK, LANES), 1)
    onehot = jnp.where(lane == i, 1.0, 0.0).astype(bf16)
    ones_row = jnp.where(lax.broadcasted_iota(jnp.int32, (VT_ROWS - HEAD_DIM, MOBA_BLOCK), 0) == 0, 1.0, 0.0)
    kmean = kmean_scr[...]
    nbp = kmean.shape[0]
    dim = lax.broadcasted_iota(jnp.int32, (LANES, MOBA_BLOCK), 0)
    sel_pad = jnp.zeros((LANES - nbp, MOBA_BLOCK), f32)
    for p in range(N_HEADS // 2):
        sl = slice(p * LANES, (p + 1) * LANES)
        kaug_ref[0, p] = jnp.concatenate([kn[:, sl].astype(bf16), onehot], axis=-1)
        vt = v[:, sl].T
        qt = (qn[:, sl] * (HEAD_DIM ** -0.5 * LOG2E)).T
        for hh in range(2):
            vaugt_ref[0, 2 * p + hh] = jnp.concatenate(
                [vt[hh * HEAD_DIM:(hh + 1) * HEAD_DIM], ones_row], axis=0).astype(bf16)
            keep = (dim < HEAD_DIM) if hh == 0 else (dim >= HEAD_DIM)
            q2t = jnp.where(keep, qt, 0.0)
            gate_t = _mm_hi(kmean[:, sl], q2t)
            qaugt_ref[0, 2 * p + hh] = jnp.concatenate([q2t, _top3_bias(gate_t, i), sel_pad], axis=0).astype(bf16)
    kmean_scr[pl.ds(i, 1), :] = jnp.mean(kn, axis=0, keepdims=True)


def _moba_prep(proj, bsz, seq, qw, kw, bd_mean):
    nb = seq // MOBA_BLOCK
    nbp = -(-nb // 8) * 8
    assert nbp <= LANES
    return pl.pallas_call(
        _moba_prep_kernel, name="moba_prep",
        grid=(bsz, nb),
        in_specs=[pl.BlockSpec((MOBA_BLOCK, 3 * MIX_W), lambda b, i: (b * nb + i, OFF_A // (3 * MIX_W))),
                  pl.BlockSpec((1, MIX_W), lambda b, i: (0, 0)),
                  pl.BlockSpec((1, MIX_W), lambda b, i: (0, 0)),
                  pl.BlockSpec((MIX_W, MIX_W), lambda b, i: (0, 0))],
        out_specs=[pl.BlockSpec((1, N_HEADS, 2 * LANES, MOBA_BLOCK), lambda b, i: (b, 0, 0, i)),
                   pl.BlockSpec((1, N_HEADS // 2, MOBA_BLOCK, 2 * LANES), lambda b, i: (b, 0, i, 0)),
                   pl.BlockSpec((1, N_HEADS, VT_ROWS, MOBA_BLOCK), lambda b, i: (b, 0, 0, i))],
        out_shape=[jax.ShapeDtypeStruct((bsz, N_HEADS, 2 * LANES, seq), bf16),
                   jax.ShapeDtypeStruct((bsz, N_HEADS // 2, seq, 2 * LANES), bf16),
                   jax.ShapeDtypeStruct((bsz, N_HEADS, VT_ROWS, seq), bf16)],
        scratch_shapes=[pltpu.VMEM((nbp, MIX_W), f32)],
        compiler_params=_cparams(("parallel", "arbitrary"), 32),
    )(proj, qw, kw, bd_mean)


def _moba_attn_kernel(qaugt_ref, kaug_ref, vaugt_ref, tabt_ref, o_ref, sa_scr, sb_scr, acc_scr):
    i = pl.program_id(2)
    blk = MOBA_BLOCK
    pairs = range(MOBA_PAIRS_PER_STEP)
    hs = range(2 * MOBA_PAIRS_PER_STEP)
    key, qry = _iota2((blk, blk))
    mm = lambda a, b: jnp.dot(a, b, preferred_element_type=f32)
    cmax = lambda s: jnp.max(s, axis=0, keepdims=True)
    pv = lambda v, pe: jnp.dot(v, pe.astype(bf16), preferred_element_type=f32)
    kblk = lambda pp, j: kaug_ref[0, pp, pl.ds(pl.multiple_of(j * blk, blk), blk), :]
    vblk = lambda hh, j: vaugt_ref[0, hh, :, pl.ds(pl.multiple_of(j * blk, blk), blk)]

    n_far = jnp.maximum(i - 1, 0)
    nk = MOBA_KV_PER_GROUP
    n_groups = (n_far + nk - 1) // nk
    blocks_of = lambda g: [jnp.where(nk * g + a < n_far, nk * g + a, i) for a in range(nk)]

    def scores(g, buf):
        qt = [qaugt_ref[0, hh] for hh in hs]
        for a, j in enumerate(blocks_of(g)):
            ks = [kblk(pp, j) for pp in pairs]
            for hh in hs:
                buf[hh, a] = mm(ks[hh // 2], qt[hh])

    scores(0, sa_scr)

    k_own = [kblk(pp, i) for pp in pairs]
    k_adj = [kblk(pp, n_far) for pp in pairs]
    qt = [qaugt_ref[0, hh] for hh in hs]
    s_own = [jnp.where(qry >= key, mm(k_own[hh // 2][:, :LANES], qt[hh][:LANES]) + tabt_ref[hh // 2, hh % 2, 1], NEG)
             for hh in hs]
    s_adj = [mm(k_adj[hh // 2], qt[hh]) + tabt_ref[hh // 2, hh % 2, 0] for hh in hs]
    m = [jnp.maximum(cmax(s_own[hh]), cmax(s_adj[hh])) for hh in hs]
    for hh in hs:
        acc_scr[hh] = (pv(vblk(hh, i), jnp.exp2(s_own[hh] - m[hh]))
                       + pv(vblk(hh, n_far), jnp.exp2(s_adj[hh] - m[hh])))

    def consume(g, buf, m):
        js = blocks_of(g)
        m_new = []
        for hh in hs:
            s = [buf[hh, a] for a in range(nk)]
            mh = functools.reduce(jnp.maximum, [cmax(x) for x in s], m[hh])
            ah = jnp.exp2(m[hh] - mh) * acc_scr[hh]
            for a in range(nk):
                ah = ah + pv(vblk(hh, js[a]), jnp.exp2(s[a] - mh))
            acc_scr[hh] = ah
            m_new.append(mh)
        return tuple(m_new)

    def body(u, m):
        scores(2 * u + 1, sb_scr)
        m = consume(2 * u, sa_scr, m)
        scores(2 * u + 2, sa_scr)
        return consume(2 * u + 1, sb_scr, m)

    m = lax.fori_loop(0, n_groups // 2, body, tuple(m))
    lax.cond(n_groups % 2 == 1, lambda mm_: consume(n_groups - 1, sa_scr, mm_), lambda mm_: mm_, m)
    o_ref[0] = jnp.concatenate(
        [(acc_scr[hh][:HEAD_DIM] / acc_scr[hh][HEAD_DIM:HEAD_DIM + 1]).T for hh in hs], axis=-1)


def _moba_attn(qaugt, kaug, vaugt, tabt):
    bsz, _, _, seq = qaugt.shape
    nb = seq // MOBA_BLOCK
    pp = MOBA_PAIRS_PER_STEP
    return pl.pallas_call(
        _moba_attn_kernel, name="moba_attn",
        grid=(bsz, N_HEADS // (2 * pp), nb),
        in_specs=[pl.BlockSpec((1, 2 * pp, 2 * LANES, MOBA_BLOCK), lambda b, p, i: (b, p, 0, i)),
                  pl.BlockSpec((1, pp, seq, 2 * LANES), lambda b, p, i: (b, p, 0, 0)),
                  pl.BlockSpec((1, 2 * pp, VT_ROWS, seq), lambda b, p, i: (b, p, 0, 0)),
                  pl.BlockSpec((pp, 2, 2, MOBA_BLOCK, MOBA_BLOCK), lambda b, p, i: (p, 0, 0, 0, 0))],
        out_specs=pl.BlockSpec((1, MOBA_BLOCK, pp * LANES), lambda b, p, i: (b, i, p)),
        out_shape=jax.ShapeDtypeStruct((bsz, seq, MIX_W), f32),
        scratch_shapes=[pltpu.VMEM((2 * pp, MOBA_KV_PER_GROUP, MOBA_BLOCK, MOBA_BLOCK), f32)] * 2
        + [pltpu.VMEM((2 * pp, VT_ROWS, MOBA_BLOCK), f32)],
        compiler_params=_cparams(("parallel", "parallel", "arbitrary"), 56),
    )(qaugt, kaug, vaugt, tabt)


def _t5_bucket(dist):
    n = jnp.maximum(dist, 0)
    max_exact = REL_BUCKETS // 2
    nf = jnp.maximum(n, max_exact).astype(f32)
    large = max_exact + (jnp.log(nf / max_exact) / math.log(REL_MAX_DIST / max_exact)
                         * (REL_BUCKETS - max_exact)).astype(jnp.int32)
    large = jnp.minimum(large, REL_BUCKETS - 1)
    return jnp.where(n < max_exact, n, large)


def _moba_bias_kernel(vec_ref, o_ref):
    blk = MOBA_BLOCK
    t = pltpu.roll(jnp.broadcast_to(vec_ref[0] * LOG2E, (blk, 2 * blk)), 0, 1, stride=1, stride_axis=0)
    o_ref[0, 0] = t[:, blk:]
    o_ref[0, 1] = t[:, :blk]


def _moba_bias_tables(rel_bias):
    assert MOBA_BLOCK >= REL_MAX_DIST
    by_dist = rel_bias.astype(f32)[_t5_bucket(jnp.arange(2 * MOBA_BLOCK))]
    far = rel_bias.astype(f32)[_t5_bucket(jnp.array(2 * MOBA_BLOCK))]
    vec = (by_dist - far).T.reshape(N_HEADS, 1, 2 * MOBA_BLOCK)
    tab = pl.pallas_call(
        _moba_bias_kernel, name="moba_bias",
        grid=(N_HEADS,),
        in_specs=[pl.BlockSpec((1, 1, 2 * MOBA_BLOCK), lambda h: (h, 0, 0))],
        out_specs=pl.BlockSpec((1, 2, MOBA_BLOCK, MOBA_BLOCK), lambda h: (h, 0, 0, 0)),
        out_shape=jax.ShapeDtypeStruct((N_HEADS, 2, MOBA_BLOCK, MOBA_BLOCK), f32),
        compiler_params=_cparams(("parallel",), 16),
    )(vec)
    return tab.reshape(N_HEADS // 2, 2, 2, MOBA_BLOCK, MOBA_BLOCK)


def _causal_conv(x, halo, w_ref):
    ts, nh = x.shape[0], halo.shape[0]
    xe = jnp.concatenate([halo, x], axis=0)
    acc = x * w_ref[CONV_K - 1:CONV_K, :]
    for d in range(1, CONV_K):
        acc = acc + xe[nh - d:nh - d + ts] * w_ref[CONV_K - 1 - d:CONV_K - d, :]
    return acc


def _tile_and_halo(x_ref, halo_ref):
    halo = jnp.where(pl.program_id(1) == 0, 0.0, halo_ref[...].astype(f32))
    return x_ref[...].astype(f32), halo


def _tok_spec(rows, nsteps):
    return pl.BlockSpec((rows, MIX_W), lambda b, i: (b * nsteps + i, 0))


def _pair_masks(rows):
    lane = lax.broadcasted_iota(jnp.int32, (rows, LANES), 1)
    return lane < HEAD_DIM, lane >= HEAD_DIM


def _chunk_sum_matrix(chunk):
    r = jnp.arange(CUMSUM_TILE)[:, None]
    c = jnp.arange(CUMSUM_TILE)[None, :]
    same = (r // chunk) == (c // chunk)
    return jnp.concatenate([same & (r >= c), same], axis=0).astype(bf16)


def _chunk_sums(cm_ref, x):
    parts = [_sel_left(cm_ref[...], x[r:r + CUMSUM_TILE]) for r in range(0, x.shape[0], CUMSUM_TILE)]
    return (jnp.concatenate([p[:CUMSUM_TILE] for p in parts], axis=0),
            jnp.concatenate([p[CUMSUM_TILE:] for p in parts], axis=0))


def _row_spec(width, off, nt):
    return pl.BlockSpec((ROW_TILE, width), lambda b, i: (b * nt + i, off // width))


def _halo_spec(width, off, nt):
    per = ROW_TILE // HALO_ROWS
    return pl.BlockSpec((HALO_ROWS, width), lambda b, i: (jnp.maximum((b * nt + i) * per - 1, 0), off // width))


def _small_spec(nt):
    return pl.BlockSpec((ROW_TILE, LANES), lambda b, i: (b * nt + i, 0))


def _const_spec(shape):
    return pl.BlockSpec(shape, lambda b, i: (0,) * len(shape))


def _gdn_prep_kernel(x_ref, halo_ref, sm_ref, cw_ref, alog_ref, dtb_ref, bd_ref, eb_ref, ea_ref, cm_ref,
                     q_ref, k_ref, kb_ref, vb_ref, qd_ref, kbe_ref, kd_ref, gc_ref):
    qkv = _silu(_causal_conv(*_tile_and_halo(x_ref, halo_ref), cw_ref))
    q, k, v = qkv[:, :MIX_W], qkv[:, MIX_W:2 * MIX_W], qkv[:, 2 * MIX_W:]
    bd = bd_ref[...]
    q = q * lax.rsqrt(_sel_right1(q * q, bd) + L2_EPS) * HEAD_DIM ** -0.5
    k = k * lax.rsqrt(_sel_right1(k * k, bd) + L2_EPS)
    sm = sm_ref[...]
    beta = _sel_right1(_sigmoid(sm), eb_ref[...])
    g = -jnp.exp(alog_ref[...]) * _softplus(sm + dtb_ref[...])
    gc, g_end = _chunk_sums(cm_ref, g)
    eg = jnp.exp(_sel_right(gc, ea_ref[...]))
    e_rest = jnp.exp(_sel_right(g_end - gc, ea_ref[...]))
    kb = k * beta
    for ref, val in ((q_ref, q), (k_ref, k), (kb_ref, kb), (vb_ref, v * beta),
                     (qd_ref, q * eg), (kbe_ref, kb * eg), (kd_ref, k * e_rest)):
        ref[...] = val.astype(bf16)
    gc_ref[...] = gc[:, SM_BA:SM_BA + N_HEADS]


def _gdn_prep(proj, small, bsz, seq, conv_w, alog128, dtb128, bd_ones, e_beta, e_ba, cm):
    nt = seq // ROW_TILE
    w3 = 3 * MIX_W
    return pl.pallas_call(
        _gdn_prep_kernel, name="gdn_prep",
        grid=(bsz, nt),
        in_specs=[_row_spec(w3, OFF_BQKV, nt), _halo_spec(w3, OFF_BQKV, nt), _small_spec(nt),
                  _const_spec((CONV_K, w3)), _const_spec((1, LANES)), _const_spec((1, LANES)),
                  _const_spec((MIX_W, MIX_W)), _const_spec((LANES, MIX_W)), _const_spec((LANES, MIX_W)),
                  _const_spec((2 * CUMSUM_TILE, CUMSUM_TILE))],
        out_specs=[_tok_spec(ROW_TILE, nt)] * 7 + [pl.BlockSpec((ROW_TILE, N_HEADS), lambda b, i: (b * nt + i, 0))],
        out_shape=[jax.ShapeDtypeStruct((bsz * seq, MIX_W), bf16)] * 7 + [jax.ShapeDtypeStruct((bsz * seq, N_HEADS), f32)],
        compiler_params=_cparams(("parallel", "parallel"), 40),
    )(proj, proj, small, conv_w, alog128, dtb128, bd_ones, e_beta, e_ba, cm)


def _gdn_chunk_kernel(q_ref, k_ref, kb_ref, vb_ref, qd_ref, kbe_ref, kd_ref, gc_ref, gct_ref, nw_ref, bdm_ref,
                      o_ref, st_scr):
    @pl.when(pl.program_id(1) == 0)
    def _():
        st_scr[...] = jnp.zeros_like(st_scr)

    n = GDN_CHUNK
    cis, prs = range(CHUNKS_PER_STEP), range(N_HEADS // 2)
    tiles = [(ci, p) for ci in cis for p in prs]
    ch = [(ci, p, hh) for ci, p in tiles for hh in (0, 1)]
    blk = lambda ref, ci, p: ref[ci * n:(ci + 1) * n, p * LANES:(p + 1) * LANES]
    r, c = _iota2((n, n))
    row2, col2 = _iota2((2 * n, 2 * n))
    same_head = (row2 < n) == (col2 < n)
    hm2 = _pair_masks(2 * n)
    keep0 = _pair_masks(n)[0]
    pick = lambda x0, x1: jnp.where(keep0, x0, x1)
    zero = jnp.zeros((), bf16)
    gc_all = [gc_ref[0, ci] for ci in cis]
    gct_all = [gct_ref[0, ci] for ci in cis]
    gcol = lambda ci, p, hh: gc_all[ci][:, 2 * p + hh:2 * p + hh + 1]
    decay = {(ci, p, hh): jnp.exp(jnp.where(r >= c, gcol(ci, p, hh) - gct_all[ci][2 * p + hh:2 * p + hh + 1, :], NEG))
             for ci, p, hh in ch}
    lhs = {x: jnp.concatenate([blk(kb_ref, *x), blk(q_ref, *x)], axis=0) for x in tiles}
    gram = {(ci, p, hh): _mm_nt(jnp.where(hm2[hh], lhs[ci, p], zero), blk(k_ref, ci, p)) for ci, p, hh in ch}
    t = dict(zip(ch, _unit_lower_inverses([jnp.where(r > c, gram[x][:n] * decay[x], 0.0) for x in ch])))
    u = {(ci, p): pick(_mm(t[ci, p, 0], blk(vb_ref, ci, p)), _mm(t[ci, p, 1], blk(vb_ref, ci, p))) for ci, p in tiles}
    w = {(ci, p): pick(_mm(t[ci, p, 0], blk(kbe_ref, ci, p)), _mm(t[ci, p, 1], blk(kbe_ref, ci, p)))
         for ci, p in tiles}
    a_in = {x: gram[x][n:] * decay[x] for x in ch}
    st = [st_scr[p] for p in prs]
    for ci in cis:
        ws = [_mm(jnp.concatenate([w[ci, p].astype(bf16), blk(qd_ref, ci, p)], axis=0), st[p]) for p in prs]
        v_new = [u[ci, p] - ws[p][:n] for p in prs]
        o = [ws[p][n:] + pick(_mm(a_in[ci, p, 0], v_new[p]), _mm(a_in[ci, p, 1], v_new[p])) for p in prs]
        upd = [jnp.where(same_head, _mm_tn(blk(kd_ref, ci, p), v_new[p]), 0.0) for p in prs]
        for p in prs:
            ms = _sel_right1(o[p] * o[p], bdm_ref[...])
            o_ref[ci * n:(ci + 1) * n, p * LANES:(p + 1) * LANES] = o[p] * lax.rsqrt(ms + RMS_EPS) * nw_ref[...]
        g_end = [jnp.exp(jnp.where(keep0[0:1], gcol(ci, p, 0)[n - 1:n], gcol(ci, p, 1)[n - 1:n])) for p in prs]
        st = [st[p] * g_end[p] + upd[p] for p in prs]
    for p in prs:
        st_scr[p] = st[p]


def _gdn_chunk(bsz, seq, q, k, kb, vb, qd, kbe, kd, gc, norm_w2, bd_mean2):
    n = GDN_CHUNK
    nc = seq // n
    per = CHUNKS_PER_STEP
    gc4 = gc.reshape(bsz, nc, n, N_HEADS)
    gct4 = jnp.swapaxes(gc4, 2, 3)
    spec = _tok_spec(per * n, nc // per)
    return pl.pallas_call(
        _gdn_chunk_kernel, name="gdn_chunk",
        grid=(bsz, nc // per),
        in_specs=[spec] * 7 + [pl.BlockSpec((1, per, n, N_HEADS), lambda b, i: (b, i, 0, 0)),
                               pl.BlockSpec((1, per, N_HEADS, n), lambda b, i: (b, i, 0, 0)),
                               _const_spec((1, LANES)), _const_spec((LANES, LANES))],
        out_specs=spec,
        out_shape=jax.ShapeDtypeStruct((bsz * seq, MIX_W), f32),
        scratch_shapes=[pltpu.VMEM((N_HEADS // 2, 2 * HEAD_DIM, 2 * HEAD_DIM), f32)],
        compiler_params=_cparams(("parallel", "arbitrary"), 32),
    )(q, k, kb, vb, qd, kbe, kd, gc4, gct4, norm_w2, bd_mean2)


def _rwkv_prep_kernel(has_vres, *refs):
    if has_vres:
        (c_ref, halo_ref, mu_ref, w0_ref, wup_ref, a0_ref, aup_ref, gup_ref, kk_ref, ka_ref, bd_ref,
         rk_ref, cm_ref, vf_ref, v0_ref, vdn_ref, vup_ref,
         rt_ref, at_ref, bt_ref, kt_ref, v_ref, pe_ref, gout_ref, bonus_ref) = refs
    else:
        (c_ref, halo_ref, mu_ref, w0_ref, wup_ref, a0_ref, aup_ref, gup_ref, kk_ref, ka_ref, bd_ref,
         rk_ref, cm_ref,
         rt_ref, at_ref, bt_ref, kt_ref, v_ref, pe_ref, gout_ref, bonus_ref, cv_ref) = refs
    c, halo = _tile_and_halo(c_ref, halo_ref)
    prev = jnp.concatenate([halo[HALO_ROWS - 1:], c[:-1]], axis=0)
    c = c + (prev - c) * mu_ref[...]
    c_r, c_k, c_v = c[:, :MIX_W], c[:, MIX_W:2 * MIX_W], c[:, 2 * MIX_W:3 * MIX_W]
    c_wd = c[:, 3 * MIX_W:3 * MIX_W + 64]
    c_ad = c[:, 3 * MIX_W + 64:3 * MIX_W + 128]
    c_gd = c[:, 3 * MIX_W + 128:]
    w_log = -_softplus(-(w0_ref[...] + _mm(jnp.tanh(c_wd), wup_ref[...]))) - 0.5
    a_in = _sigmoid(a0_ref[...] + _mm(c_ad, aup_ref[...]))
    gout_ref[...] = _mm(_sigmoid(c_gd), gup_ref[...])
    if has_vres:
        lam = _sigmoid(v0_ref[...] + _mm(_mm(c_v, vdn_ref[...]), vup_ref[...]))
        v_r = c_v + (vf_ref[...] - c_v) * lam
    else:
        v_r = c_v
        cv_ref[...] = c_v
    bd = bd_ref[...]
    kk = c_k * kk_ref[...]
    kk = kk * lax.rsqrt(_sel_right1(kk * kk, bd) + L2_EPS)
    k_r = c_k * (1.0 + (a_in - 1.0) * ka_ref[...])
    b = kk * a_in
    bonus_ref[...] = _sel_right1(c_r * k_r * rk_ref[...], bd) * v_r
    lc, lc_end = _chunk_sums(cm_ref, -jnp.exp(w_log))
    e_neg = jnp.exp(-lc)
    rt_ref[...] = (c_r * jnp.exp(lc)).astype(bf16)
    at_ref[...] = (-kk * jnp.exp(lc + jnp.exp(w_log))).astype(bf16)
    bt_ref[...] = (b * e_neg).astype(bf16)
    kt_ref[...] = (k_r * e_neg).astype(bf16)
    v_ref[...] = v_r.astype(bf16)
    pe_ref[...] = jnp.exp(lc_end)


def _rwkv_prep(proj, bsz, seq, mu, w0, w_up, a0, a_up, g_up, k_k, k_a, r_k, bd_ones, cm, vres):
    nt = seq // ROW_TILE
    wc = 3 * MIX_W + 256
    std = pl.BlockSpec((ROW_TILE, MIX_W), lambda b, i: (b * nt + i, 0))
    std_shape = jax.ShapeDtypeStruct((bsz * seq, MIX_W), f32)
    in_specs = [_row_spec(wc, OFF_C, nt), _halo_spec(wc, OFF_C, nt), _const_spec((1, wc)),
                _const_spec((1, MIX_W)), _const_spec((64, MIX_W)), _const_spec((1, MIX_W)), _const_spec((64, MIX_W)),
                _const_spec((128, MIX_W)), _const_spec((1, MIX_W)), _const_spec((1, MIX_W)), _const_spec((MIX_W, MIX_W)),
                _const_spec((1, MIX_W)), _const_spec((2 * CUMSUM_TILE, CUMSUM_TILE))]
    args = [proj, proj, mu, w0, w_up, a0, a_up, g_up, k_k, k_a, bd_ones, r_k, cm]
    out_specs = [std] * 8
    out_shape = [jax.ShapeDtypeStruct((bsz * seq, MIX_W), bf16)] * 5 + [std_shape] * 3
    if vres is not None:
        v_first, v0, v_down, v_up = vres
        in_specs += [std, _const_spec((1, MIX_W)), _const_spec(v_down.shape), _const_spec(v_up.shape)]
        args += [v_first, v0, v_down, v_up]
    else:
        out_specs.append(std)
        out_shape.append(std_shape)
    return pl.pallas_call(
        functools.partial(_rwkv_prep_kernel, vres is not None), name="rwkv_prep",
        grid=(bsz, nt), in_specs=in_specs, out_specs=out_specs, out_shape=out_shape,
        compiler_params=_cparams(("parallel", "parallel"), 40),
    )(*args)


def _rwkv_chunk_kernel(rt_ref, at_ref, bt_ref, kt_ref, v_ref, pe_ref, o_ref, st_scr):
    @pl.when(pl.program_id(1) == 0)
    def _():
        st_scr[...] = jnp.zeros_like(st_scr)

    n = RWKV_CHUNK
    cis, prs = range(CHUNKS_PER_STEP), range(N_HEADS // 2)
    tiles = [(ci, p) for ci in cis for p in prs]
    ch = [(ci, p, hh) for ci, p in tiles for hh in (0, 1)]
    blk = lambda ref, ci, p: ref[ci * n:(ci + 1) * n, p * LANES:(p + 1) * LANES]
    row, col = _iota2((2 * n, 2 * n))
    rr, cc = row & (n - 1), col & (n - 1)
    mask = rr + jnp.where(row < n, 0, 1) > cc
    same_head = (row < n) == (col < n)
    hm2 = _pair_masks(2 * n)
    keep0 = _pair_masks(n)[0]
    pick = lambda x0, x1: jnp.where(keep0, x0, x1)
    zero = jnp.zeros((), bf16)
    lhs = {x: jnp.concatenate([blk(at_ref, *x), blk(rt_ref, *x)], axis=0) for x in tiles}
    rhs = {x: jnp.concatenate([blk(bt_ref, *x), blk(kt_ref, *x)], axis=0) for x in tiles}
    gm = {(ci, p, hh): jnp.where(mask, _mm_nt(jnp.where(hm2[hh], lhs[ci, p], zero), rhs[ci, p]), 0.0)
          for ci, p, hh in ch}
    t = dict(zip(ch, _unit_lower_inverses([-gm[x][:n, :n] for x in ch])))
    v = {x: blk(v_ref, *x) for x in tiles}
    zv = {x: jnp.concatenate([jnp.zeros_like(v[x]), v[x]], axis=0) for x in tiles}
    makv = {(ci, p): pick(_mm(gm[ci, p, 0][:n], zv[ci, p]), _mm(gm[ci, p, 1][:n], zv[ci, p])) for ci, p in tiles}
    st = [st_scr[p] for p in prs]
    for ci in cis:
        ah = [_mm_nt(lhs[ci, p], st[p]) for p in prs]
        rhs_u = [ah[p][:n] + makv[ci, p] for p in prs]
        u = [pick(_mm(t[ci, p, 0], rhs_u[p]), _mm(t[ci, p, 1], rhs_u[p])) for p in prs]
        uv = [jnp.concatenate([u[p].astype(bf16), v[ci, p]], axis=0) for p in prs]
        o = [ah[p][n:] + pick(_mm(gm[ci, p, 0][n:], uv[p]), _mm(gm[ci, p, 1][n:], uv[p])) for p in prs]
        pe = [pe_ref[ci * n:ci * n + 1, p * LANES:(p + 1) * LANES] for p in prs]
        upd = [_mm_tn(uv[p], rhs[ci, p].astype(f32) * pe[p]) for p in prs]
        for p in prs:
            o_ref[ci * n:(ci + 1) * n, p * LANES:(p + 1) * LANES] = o[p]
        st = [st[p] * pe[p] + jnp.where(same_head, upd[p], 0.0) for p in prs]
    for p in prs:
        st_scr[p] = st[p]


def _rwkv_chunk(bsz, seq, rt, at, bt, kt, v, pe):
    n = RWKV_CHUNK * CHUNKS_PER_STEP
    spec = _tok_spec(n, seq // n)
    return pl.pallas_call(
        _rwkv_chunk_kernel, name="rwkv_chunk",
        grid=(bsz, seq // n),
        in_specs=[spec] * 6,
        out_specs=spec,
        out_shape=jax.ShapeDtypeStruct((bsz * seq, MIX_W), f32),
        scratch_shapes=[pltpu.VMEM((N_HEADS // 2, 2 * HEAD_DIM, 2 * HEAD_DIM), f32)],
        compiler_params=_cparams(("parallel", "arbitrary"), 32),
    )(rt, at, bt, kt, v, pe)


def _ssd_prep_kernel(x_ref, halo_ref, sm_ref, cw_ref, cb_ref, alog_ref, dtb_ref, edt_ref, cm_ref,
                     xdt_ref, x_out_ref, bc_ref, acs_ref):
    xbc = _silu(_causal_conv(*_tile_and_halo(x_ref, halo_ref), cw_ref) + cb_ref[...])
    m_x = xbc[:, :MIX_W]
    dt = _softplus(sm_ref[...] + dtb_ref[...])
    xdt_ref[...] = (m_x * _sel_right1(dt, edt_ref[...])).astype(xdt_ref.dtype)
    x_out_ref[...] = m_x
    bc_ref[...] = xbc[:, MIX_W:].astype(bc_ref.dtype)
    acs, _ = _chunk_sums(cm_ref, dt * -jnp.exp(alog_ref[...]))
    acs_ref[...] = acs[:, SM_DT:SM_DT + N_HEADS]


def _ssd_prep(proj, small, bsz, seq, conv_w, conv_b, alog128, dtb128, e_dt, cm):
    nt = seq // ROW_TILE
    wx = MIX_W + 4 * SSM_STATE
    return pl.pallas_call(
        _ssd_prep_kernel, name="ssd_prep",
        grid=(bsz, nt),
        in_specs=[_row_spec(wx, OFF_DXBC, nt), _halo_spec(wx, OFF_DXBC, nt), _small_spec(nt),
                  _const_spec((CONV_K, wx)), _const_spec((1, wx)), _const_spec((1, LANES)), _const_spec((1, LANES)),
                  _const_spec((LANES, MIX_W)), _const_spec((2 * CUMSUM_TILE, CUMSUM_TILE))],
        out_specs=[_tok_spec(ROW_TILE, nt)] * 2 + [pl.BlockSpec((ROW_TILE, 4 * SSM_STATE), lambda b, i: (b * nt + i, 0)),
                                                   pl.BlockSpec((ROW_TILE, N_HEADS), lambda b, i: (b * nt + i, 0))],
        out_shape=[jax.ShapeDtypeStruct((bsz * seq, MIX_W), bf16), jax.ShapeDtypeStruct((bsz * seq, MIX_W), f32),
                   jax.ShapeDtypeStruct((bsz * seq, 4 * SSM_STATE), bf16),
                   jax.ShapeDtypeStruct((bsz * seq, N_HEADS), f32)],
        compiler_params=_cparams(("parallel", "parallel"), 40),
    )(proj, proj, small, conv_w, conv_b, alog128, dtb128, e_dt, cm)


def _ssd_chunk_kernel(xdt_ref, x_ref, bc_ref, a_ref, at_ref, dvec_ref, o_ref, st_scr):
    @pl.when(pl.program_id(1) == 0)
    def _():
        st_scr[...] = jnp.zeros_like(st_scr)

    n = SSD_CHUNK
    cis, prs = range(SSD_CHUNKS_PER_STEP), range(N_HEADS // 2)
    grp = lambda p: (2 * p) // (N_HEADS // 2)
    tiles = [(ci, p) for ci in cis for p in prs]
    ch = [(ci, p, hh) for ci, p in tiles for hh in (0, 1)]
    rows = lambda ci: slice(ci * n, (ci + 1) * n)
    lanes = lambda p: slice(p * LANES, (p + 1) * LANES)
    r, c = _iota2((n, n))
    keep0 = _pair_masks(n)[0]
    pick = lambda x0, x1: jnp.where(keep0, x0, x1)
    b_g = {(ci, g): bc_ref[rows(ci), g * SSM_STATE:(g + 1) * SSM_STATE] for ci in cis for g in range(2)}
    c_g = {(ci, g): bc_ref[rows(ci), (2 + g) * SSM_STATE:(3 + g) * SSM_STATE] for ci in cis for g in range(2)}
    cb = {x: _mm_nt(c_g[x], b_g[x]) for x in b_g}
    col = lambda ci, p, hh: a_ref[0, ci][:, 2 * p + hh:2 * p + hh + 1]
    ac = {x: col(*x) for x in ch}
    a_last = {x: col(*x)[n - 1:n] for x in ch}
    lmat = {(ci, p, hh): jnp.exp(jnp.where(r >= c, ac[ci, p, hh] - at_ref[0, ci][2 * p + hh:2 * p + hh + 1, :], NEG))
            for ci, p, hh in ch}
    xg = {(ci, p): xdt_ref[rows(ci), lanes(p)] for ci, p in tiles}
    y_diag = {(ci, p): pick(*[_mm(cb[ci, grp(p)] * lmat[ci, p, hh], xg[ci, p]) for hh in (0, 1)]) for ci, p in tiles}
    upd = {(ci, p): pick(*[_mm_tn(b_g[ci, grp(p)].astype(f32) * jnp.exp(a_last[ci, p, hh] - ac[ci, p, hh]), xg[ci, p])
                           for hh in (0, 1)]) for ci, p in tiles}
    c_in = {(ci, p, hh): c_g[ci, grp(p)].astype(f32) * jnp.exp(ac[ci, p, hh]) for ci, p, hh in ch}
    st = [st_scr[p] for p in prs]
    for ci in cis:
        y_off = [pick(_mm(c_in[ci, p, 0], st[p]), _mm(c_in[ci, p, 1], st[p])) for p in prs]
        for p in prs:
            o_ref[rows(ci), lanes(p)] = y_diag[ci, p] + y_off[p] + x_ref[rows(ci), lanes(p)] * dvec_ref[:, lanes(p)]
        st = [st[p] * jnp.exp(jnp.where(keep0[0:1], a_last[ci, p, 0], a_last[ci, p, 1])) + upd[ci, p] for p in prs]
    for p in prs:
        st_scr[p] = st[p]


def _ssd_chunk(bsz, seq, xdt, x, bc, acs, dvec):
    n = SSD_CHUNK
    nc = seq // n
    per = SSD_CHUNKS_PER_STEP
    a4 = acs.reshape(bsz, nc, n, N_HEADS)
    at4 = jnp.swapaxes(a4, 2, 3)
    return pl.pallas_call(
        _ssd_chunk_kernel, name="ssd_chunk",
        grid=(bsz, nc // per),
        in_specs=[_tok_spec(per * n, nc // per)] * 2
        + [pl.BlockSpec((per * n, 4 * SSM_STATE), lambda b, i: (b * (nc // per) + i, 0)),
           pl.BlockSpec((1, per, n, N_HEADS), lambda b, i: (b, i, 0, 0)),
           pl.BlockSpec((1, per, N_HEADS, n), lambda b, i: (b, i, 0, 0)),
           _const_spec((1, MIX_W))],
        out_specs=_tok_spec(per * n, nc // per),
        out_shape=jax.ShapeDtypeStruct((bsz * seq, MIX_W), f32),
        scratch_shapes=[pltpu.VMEM((N_HEADS // 2, SSM_STATE, 2 * HEAD_DIM), f32)],
        compiler_params=_cparams(("parallel", "arbitrary"), 32),
    )(xdt, x, bc, a4, at4, dvec)


def _merge_kernel(x_ref, ya_ref, ob_ref, bz_ref, wkv_ref, bonus_ref, gout_ref, lnw_ref, lnb_ref, bdm_ref,
                  yd_ref, dz_ref, mnw_ref, g0_ref, g1_ref, g2_ref, g3_ref, wb_ref, wo_ref, o_ref):
    def gated(n, y, g_ref):
        return _sigmoid(g_ref[...].astype(f32)) * _mm(y, wb_ref[n])

    acc = gated(0, ya_ref[0], g0_ref)
    acc = acc + gated(1, ob_ref[...] * _silu(bz_ref[...].astype(f32)), g1_ref)

    w = wkv_ref[...]
    d = w - _sel_right1(w, bdm_ref[...])
    wkv_ln = d * lax.rsqrt(_sel_right1(d * d, bdm_ref[...]) + RWKV_LN_EPS)
    y_c = (wkv_ln * lnw_ref[...] + lnb_ref[...] + bonus_ref[...]) * gout_ref[...]
    acc = acc + gated(2, y_c, g2_ref)

    yz = yd_ref[...] * _silu(dz_ref[...].astype(f32))
    half = MIX_W // 2
    y_d = jnp.concatenate(
        [yz[:, s:s + half] * lax.rsqrt(jnp.mean(jnp.square(yz[:, s:s + half]), axis=-1, keepdims=True) + RMS_EPS)
         for s in (0, half)], axis=-1) * mnw_ref[...]
    acc = acc + gated(3, y_d, g3_ref)
    o_ref[...] = x_ref[...] + _mm(acc, wo_ref[...])


def _merge(x2, proj, bsz, seq, ya, ob, wkv, bonus, gout, lnw, lnb, bd_mean, yd, mnw, wb, wo):
    nt = seq // ROW_TILE
    std = lambda w: pl.BlockSpec((ROW_TILE, w), lambda b, i: (b * nt + i, 0))
    gate = lambda n: _row_spec(D_MODEL, OFF_GATES + n * D_MODEL, nt)
    return pl.pallas_call(
        _merge_kernel, name="merge",
        grid=(bsz, nt),
        in_specs=[std(D_MODEL), pl.BlockSpec((1, ROW_TILE, MIX_W), lambda b, i: (b, i, 0)),
                  std(MIX_W), _row_spec(MIX_W, OFF_BZ, nt),
                  std(MIX_W), std(MIX_W), std(MIX_W), _const_spec((1, MIX_W)), _const_spec((1, MIX_W)),
                  _const_spec((MIX_W, MIX_W)),
                  std(MIX_W), _row_spec(MIX_W, OFF_DZ, nt), _const_spec((1, MIX_W)),
                  gate(0), gate(1), gate(2), gate(3),
                  _const_spec((4, MIX_W, D_MODEL)), _const_spec((D_MODEL, D_MODEL))],
        out_specs=std(D_MODEL),
        out_shape=jax.ShapeDtypeStruct((bsz * seq, D_MODEL), f32),
        compiler_params=_cparams(("parallel", "parallel"), 48),
    )(x2, ya, ob, proj, wkv, bonus, gout, lnw, lnb, bd_mean, yd, proj, mnw, proj, proj, proj, proj, wb, wo)


def _lane_vec(vals, off):
    return jnp.zeros((1, LANES), f32).at[0, off:off + vals.shape[0]].set(vals)


def _head_expand(off):
    n = jnp.arange(LANES)[:, None]
    c = jnp.arange(MIX_W)[None, :]
    return (n - off == c // HEAD_DIM).astype(bf16)


def _pack_w_in(w):
    w = w.astype(bf16)
    pad = lambda n: jnp.zeros((w.shape[0], n), w.dtype)
    cols = [w[:, W_A:W_BZ],
            w[:, W_BZ:W_BBETA],
            w[:, W_C:W_DZ],
            w[:, W_BBETA:W_C], w[:, W_DDT:W_GATES], pad(2 * LANES - 3 * N_HEADS),
            w[:, W_DZ:W_DXBC], w[:, W_DXBC:W_DDT], w[:, W_GATES:]]
    out = jnp.concatenate(cols, axis=1).astype(bf16)
    assert out.shape[1] == N_PROJ
    return out


def kernel(x, rel_bias, norm1_w, w_in, moba_q_norm, moba_k_norm, gdn_conv_w, gdn_A_log, gdn_dt_bias, gdn_norm_w, rwkv_mu, rwkv_w0, rwkv_w_up, rwkv_a0, rwkv_a_up, rwkv_g_up, rwkv_k_k, rwkv_k_a, rwkv_r_k, rwkv_v0, rwkv_v_down, rwkv_v_up, rwkv_ln_w, rwkv_ln_b, mamba_conv_w, mamba_conv_b, mamba_dt_bias, mamba_A_log, mamba_D, mamba_norm_w, w_branch, w_out, norm2_w, ffn_w_in, ffn_w_down):
    bsz, seq, d = x.shape
    depth = w_in.shape[0]
    assert d == D_MODEL and (bsz * seq) % IN_PROJ_TILE_M == 0 and seq % MM_TILE_M == 0
    x2 = x.reshape(bsz * seq, d)
    row = lambda v: v.reshape(1, -1).astype(f32)

    hid = jnp.arange(MIX_W) // HEAD_DIM
    bd_ones = (hid[:, None] == hid[None, :]).astype(bf16)
    bd_mean = (bd_ones.astype(f32) / HEAD_DIM).astype(bf16)
    e_beta, e_ba, e_dt = _head_expand(SM_BETA), _head_expand(SM_BA), _head_expand(SM_DT)
    cm64, cm128 = _chunk_sum_matrix(GDN_CHUNK), _chunk_sum_matrix(SSD_CHUNK)
    assert GDN_CHUNK == RWKV_CHUNK
    tab = _moba_bias_tables(rel_bias)
    v_first = None
    for i in range(depth):
        proj, small = _in_proj(x2, row(norm1_w[i]), _pack_w_in(w_in[i]))

        qaug, kaug, v_a = _moba_prep(proj, bsz, seq, row(jnp.tile(moba_q_norm[i], N_HEADS)),
                                     row(jnp.tile(moba_k_norm[i], N_HEADS)), bd_mean)
        y_a = _moba_attn(qaug, kaug, v_a, tab)

        gdn_in = _gdn_prep(proj, small, bsz, seq, gdn_conv_w[i], _lane_vec(gdn_A_log[i], SM_BA),
                           _lane_vec(gdn_dt_bias[i], SM_BA), bd_ones, e_beta, e_ba, cm64)
        o_b = _gdn_chunk(bsz, seq, *gdn_in, row(jnp.tile(gdn_norm_w[i], 2)), bd_mean[:LANES, :LANES])

        vres = None if i == 0 else (v_first, row(rwkv_v0[i - 1]), rwkv_v_down[i - 1].astype(bf16),
                                    rwkv_v_up[i - 1].astype(bf16))
        outs = _rwkv_prep(proj, bsz, seq, row(rwkv_mu[i]), row(rwkv_w0[i]), rwkv_w_up[i].astype(bf16),
                          row(rwkv_a0[i]), rwkv_a_up[i].astype(bf16), rwkv_g_up[i].astype(bf16),
                          row(rwkv_k_k[i]), row(rwkv_k_a[i]), row(rwkv_r_k[i]), bd_ones, cm64, vres)
        g_out, bonus = outs[6], outs[7]
        if i == 0:
            v_first = outs[8]
        wkv = _rwkv_chunk(bsz, seq, *outs[:6])

        xdt, x_d, bc, acs = _ssd_prep(proj, small, bsz, seq, mamba_conv_w[i], row(mamba_conv_b[i]),
                                      _lane_vec(mamba_A_log[i], SM_DT), _lane_vec(mamba_dt_bias[i], SM_DT), e_dt, cm128)
        y_d = _ssd_chunk(bsz, seq, xdt, x_d, bc, acs, row(jnp.repeat(mamba_D[i], HEAD_DIM)))

        x2 = _merge(x2, proj, bsz, seq, y_a, o_b, wkv, bonus, g_out,
                    row(rwkv_ln_w[i]), row(rwkv_ln_b[i]), bd_mean, y_d, row(mamba_norm_w[i]),
                    w_branch[i].astype(bf16), w_out[i].astype(bf16))

        x2 = _ffn(x2, row(norm2_w[i]), ffn_w_in[i].astype(bf16), ffn_w_down[i].astype(bf16))
    return x2.reshape(bsz, seq, d)
```

```python
import functools
import math

import jax
import jax.numpy as jnp
from jax import lax
from jax.experimental import pallas as pl
from jax.experimental.pallas import tpu as pltpu

f32, bf16 = jnp.float32, jnp.bfloat16
HI = lax.Precision.HIGHEST

D_MODEL = 1024
N_HEADS = 8
HEAD_DIM = 64
MIX_W = N_HEADS * HEAD_DIM
RMS_EPS = 1e-6
L2_EPS = 1e-6
CONV_K = 4
MOBA_BLOCK = 256
MOBA_TOPK = 3
MOBA_PAIRS_PER_STEP = 2
MOBA_KV_PER_GROUP = 2
REL_BUCKETS = 32
REL_MAX_DIST = 128
GDN_CHUNK = 64
RWKV_CHUNK = 64
CHUNKS_PER_STEP = 4
RWKV_LN_EPS = 64e-5
SSM_STATE = 128
SSD_CHUNK = 128
SSD_CHUNKS_PER_STEP = 4
FFN_HIDDEN_SPLITS = 2
NEG = -1e30
LOG2E = math.log2(math.e)

LANES = 128
BF16_SUBLANES = 16
VT_ROWS = HEAD_DIM + BF16_SUBLANES

OFF_A, OFF_BQKV, OFF_BZ, OFF_C, OFF_SMALL, OFF_DZ, OFF_DXBC, OFF_GATES = 0, 1536, 3072, 3584, 5376, 5632, 6144, 7168
N_PROJ = OFF_GATES + 4 * D_MODEL
SM_BETA, SM_BA, SM_DT = 0, 8, 16
W_A, W_BQKV, W_BZ, W_BBETA, W_BA, W_C, W_DZ, W_DXBC, W_DDT, W_GATES = 0, 1536, 3072, 3584, 3592, 3600, 5392, 5904, 6928, 6936

ROW_TILE = 512
CUMSUM_TILE = 256
MM_TILE_M = 512
IN_PROJ_TILE_M, IN_PROJ_TILE_N = 1024, 1024
HALO_ROWS = BF16_SUBLANES


def _cparams(sem, vmem_mb):
    return pltpu.CompilerParams(dimension_semantics=sem, vmem_limit_bytes=vmem_mb * 1024 * 1024)


def _mm(a, b):
    return jnp.dot(a.astype(bf16), b.astype(bf16), preferred_element_type=f32)


def _mm_nt(a, b):
    return lax.dot_general(a.astype(bf16), b.astype(bf16), (((1,), (1,)), ((), ())), preferred_element_type=f32)


def _mm_tn(a, b):
    return lax.dot_general(a.astype(bf16), b.astype(bf16), (((0,), (0,)), ((), ())), preferred_element_type=f32)


def _mm_hi(a, b):
    return jnp.dot(a, b, precision=HI, preferred_element_type=f32)


def _softplus(x):
    return jnp.maximum(x, 0.0) + jnp.log1p(jnp.exp(-jnp.abs(x)))


def _sigmoid(x):
    return jax.nn.sigmoid(x)


def _silu(x):
    return x * _sigmoid(x)


def _iota2(shape):
    return lax.broadcasted_iota(jnp.int32, shape, 0), lax.broadcasted_iota(jnp.int32, shape, 1)


def _split3(x):
    hi = x.astype(bf16)
    r1 = x - hi.astype(f32)
    mid = r1.astype(bf16)
    return hi, mid, (r1 - mid.astype(f32)).astype(bf16)


def _sel_left(m01, x):
    return sum(jnp.dot(m01, p, preferred_element_type=f32) for p in _split3(x))


def _sel_right(x, m01):
    return sum(jnp.dot(p, m01, preferred_element_type=f32) for p in _split3(x))


def _sel_right1(x, m01):
    return jnp.dot(x.astype(bf16), m01, preferred_element_type=f32)


def _unit_lower_inverses(l_list):
    n = l_list[0].shape[0]
    r, c = _iota2((n, n))
    eye = jnp.where(r == c, 1.0, 0.0)
    ts = [eye for _ in l_list]
    for ls in range(n.bit_length() - 1):
        m = ((r >> (ls + 1)) == (c >> (ls + 1))) & (((r >> ls) & 1) == 1) & (((c >> ls) & 1) == 0)
        lms = [jnp.where(m, l, 0.0) for l in l_list]
        if ls == 0:
            ts = [t - lm for t, lm in zip(ts, lms)]
        else:
            tl = [_mm(t, lm) for t, lm in zip(ts, lms)]
            ts = [t - _mm(x, t) for t, x in zip(ts, tl)]
    return ts


def _in_proj_kernel(x_ref, nw_ref, w_ref, o_ref, sm_ref, h_scr):
    j = pl.program_id(1)

    @pl.when(j == 0)
    def _():
        x = x_ref[...]
        y = x * lax.rsqrt(jnp.mean(x * x, axis=-1, keepdims=True) + RMS_EPS)
        h_scr[...] = (y * nw_ref[...]).astype(bf16)

    tn = o_ref.shape[1]
    w = w_ref[:, pl.ds(pl.multiple_of(j * tn, tn), tn)]
    acc = jnp.dot(h_scr[...], w, preferred_element_type=f32)
    o_ref[...] = acc.astype(o_ref.dtype)

    @pl.when(j == OFF_SMALL // tn)
    def _():
        sm_ref[...] = acc[:, OFF_SMALL % tn:OFF_SMALL % tn + LANES]


def _in_proj(x2, nw, w):
    t, d = x2.shape
    n = w.shape[1]
    tm, tn = IN_PROJ_TILE_M, IN_PROJ_TILE_N
    return pl.pallas_call(
        _in_proj_kernel, name="in_proj",
        grid=(t // tm, n // tn),
        in_specs=[pl.BlockSpec((tm, d), lambda i, j: (i, 0)),
                  pl.BlockSpec((1, d), lambda i, j: (0, 0)),
                  pl.BlockSpec((d, n), lambda i, j: (0, 0), pipeline_mode=pl.Buffered(1))],
        out_specs=[pl.BlockSpec((tm, tn), lambda i, j: (i, j)),
                   pl.BlockSpec((tm, LANES), lambda i, j: (i, 0))],
        out_shape=[jax.ShapeDtypeStruct((t, n), bf16), jax.ShapeDtypeStruct((t, LANES), f32)],
        scratch_shapes=[pltpu.VMEM((tm, d), bf16)],
        compiler_params=_cparams(("parallel", "arbitrary"), 48),
    )(x2, nw, w)


def _ffn_kernel(x_ref, nw_ref, wi_ref, wd_ref, o_ref):
    x = x_ref[...]
    h = (x * lax.rsqrt(jnp.mean(x * x, axis=-1, keepdims=True) + RMS_EPS) * nw_ref[...]).astype(bf16)
    n = wd_ref.shape[0]
    step = n // FFN_HIDDEN_SPLITS
    acc = x
    for c0 in range(0, n, step):
        g = jnp.dot(h, wi_ref[:, c0:c0 + step], preferred_element_type=f32)
        u = jnp.dot(h, wi_ref[:, n + c0:n + c0 + step], preferred_element_type=f32)
        acc = acc + jnp.dot((_silu(g) * u).astype(bf16), wd_ref[c0:c0 + step, :], preferred_element_type=f32)
    o_ref[...] = acc


def _ffn(x2, nw, wi, wd):
    t, d = x2.shape
    n = wd.shape[0]
    tm = MM_TILE_M
    resident = lambda shape: pl.BlockSpec(shape, lambda i: (0, 0), pipeline_mode=pl.Buffered(1))
    return pl.pallas_call(
        _ffn_kernel, name="ffn",
        grid=(t // tm,),
        in_specs=[pl.BlockSpec((tm, d), lambda i: (i, 0)),
                  pl.BlockSpec((1, d), lambda i: (0, 0)),
                  resident((d, 2 * n)), resident((n, d))],
        out_specs=pl.BlockSpec((tm, d), lambda i: (i, 0)),
        out_shape=jax.ShapeDtypeStruct((t, d), f32),
        compiler_params=_cparams(("parallel",), 48),
    )(x2, nw, wi, wd)


def _top3_bias(gate_t, n_past):
    row = lax.broadcasted_iota(jnp.int32, gate_t.shape, 0)
    g = jnp.where(row < n_past, gate_t, -jnp.inf)
    sel = jnp.zeros(gate_t.shape, jnp.bool_)
    for _ in range(MOBA_TOPK):
        m = jnp.max(g, axis=0, keepdims=True)
        idx = jnp.min(jnp.where(g == m, row, gate_t.shape[0]), axis=0, keepdims=True)
        pick = row == idx
        sel = sel | (pick & (m > -jnp.inf))
        g = jnp.where(pick, -jnp.inf, g)
    return jnp.where(sel, 0.0, NEG)


def _moba_prep_kernel(a_ref, qw_ref, kw_ref, bd_ref, qaugt_ref, kaug_ref, vaugt_ref, kmean_scr):
    i = pl.program_id(1)

    @pl.when(i == 0)
    def _():
        kmean_scr[...] = jnp.zeros_like(kmean_scr)

    a = a_ref[...].astype(f32)
    q, k, v = a[:, :MIX_W], a[:, MIX_W:2 * MIX_W], a[:, 2 * MIX_W:]
    bd = bd_ref[...]
    qn = q * lax.rsqrt(_sel_right1(q * q, bd) + RMS_EPS) * qw_ref[...]
    kn = k * lax.rsqrt(_sel_right1(k * k, bd) + RMS_EPS) * kw_ref[...]
    lane = lax.broadcasted_iota(jnp.int32, (MOBA_BLOCK, LANES), 1)
    onehot = jnp.where(lane == i, 1.0, 0.0).astype(bf16)
    ones_row = jnp.where(lax.broadcasted_iota(jnp.int32, (VT_ROWS - HEAD_DIM, MOBA_BLOCK), 0) == 0, 1.0, 0.0)
    kmean = kmean_scr[...]
    nbp = kmean.shape[0]
    dim = lax.broadcasted_iota(jnp.int32, (LANES, MOBA_BLOCK), 0)
    sel_pad = jnp.zeros((LANES - nbp, MOBA_BLOCK), f32)
    for p in range(N_HEADS // 2):
        sl = slice(p * LANES, (p + 1) * LANES)
        kaug_ref[0, p] = jnp.concatenate([kn[:, sl].astype(bf16), onehot], axis=-1)
        vt = v[:, sl].T
        qt = (qn[:, sl] * (HEAD_DIM ** -0.5 * LOG2E)).T
        for hh in range(2):
            vaugt_ref[0, 2 * p + hh] = jnp.concatenate(
                [vt[hh * HEAD_DIM:(hh + 1) * HEAD_DIM], ones_row], axis=0).astype(bf16)
            keep = (dim < HEAD_DIM) if hh == 0 else (dim >= HEAD_DIM)
            q2t = jnp.where(keep, qt, 0.0)
            gate_t = _mm_hi(kmean[:, sl], q2t)
            qaugt_ref[0, 2 * p + hh] = jnp.concatenate([q2t, _top3_bias(gate_t, i), sel_pad], axis=0).astype(bf16)
    kmean_scr[pl.ds(i, 1), :] = jnp.mean(kn, axis=0, keepdims=True)


def _moba_prep(proj, bsz, seq, qw, kw, bd_mean):
    nb = seq // MOBA_BLOCK
    nbp = -(-nb // 8) * 8
    assert nbp <= LANES
    return pl.pallas_call(
        _moba_prep_kernel, name="moba_prep",
        grid=(bsz, nb),
        in_specs=[pl.BlockSpec((MOBA_BLOCK, 3 * MIX_W), lambda b, i: (b * nb + i, OFF_A // (3 * MIX_W))),
                  pl.BlockSpec((1, MIX_W), lambda b, i: (0, 0)),
                  pl.BlockSpec((1, MIX_W), lambda b, i: (0, 0)),
                  pl.BlockSpec((MIX_W, MIX_W), lambda b, i: (0, 0))],
        out_specs=[pl.BlockSpec((1, N_HEADS, 2 * LANES, MOBA_BLOCK), lambda b, i: (b, 0, 0, i)),
                   pl.BlockSpec((1, N_HEADS // 2, MOBA_BLOCK, 2 * LANES), lambda b, i: (b, 0, i, 0)),
                   pl.BlockSpec((1, N_HEADS, VT_ROWS, MOBA_BLOCK), lambda b, i: (b, 0, 0, i))],
        out_shape=[jax.ShapeDtypeStruct((bsz, N_HEADS, 2 * LANES, seq), bf16),
                   jax.ShapeDtypeStruct((bsz, N_HEADS // 2, seq, 2 * LANES), bf16),
                   jax.ShapeDtypeStruct((bsz, N_HEADS, VT_ROWS, seq), bf16)],
        scratch_shapes=[pltpu.VMEM((nbp, MIX_W), f32)],
        compiler_params=_cparams(("parallel", "arbitrary"), 32),
    )(proj, qw, kw, bd_mean)


def _moba_attn_kernel(qaugt_ref, kaug_ref, vaugt_ref, tabt_ref, o_ref, sa_scr, sb_scr, acc_scr):
    i = pl.program_id(2)
    blk = MOBA_BLOCK
    pairs = range(MOBA_PAIRS_PER_STEP)
    hs = range(2 * MOBA_PAIRS_PER_STEP)
    key, qry = _iota2((blk, blk))
    mm = lambda a, b: jnp.dot(a, b, preferred_element_type=f32)
    cmax = lambda s: jnp.max(s, axis=0, keepdims=True)
    pv = lambda v, pe: jnp.dot(v, pe.astype(bf16), preferred_element_type=f32)
    kblk = lambda pp, j: kaug_ref[0, pp, pl.ds(pl.multiple_of(j * blk, blk), blk), :]
    vblk = lambda hh, j: vaugt_ref[0, hh, :, pl.ds(pl.multiple_of(j * blk, blk), blk)]

    n_far = jnp.maximum(i - 1, 0)
    nk = MOBA_KV_PER_GROUP
    n_groups = (n_far + nk - 1) // nk
    blocks_of = lambda g: [jnp.where(nk * g + a < n_far, nk * g + a, i) for a in range(nk)]

    def scores(g, buf):
        qt = [qaugt_ref[0, hh] for hh in hs]
        for a, j in enumerate(blocks_of(g)):
            ks = [kblk(pp, j) for pp in pairs]
            for hh in hs:
                buf[hh, a] = mm(ks[hh // 2], qt[hh])

    scores(0, sa_scr)

    k_own = [kblk(pp, i) for pp in pairs]
    k_adj = [kblk(pp, n_far) for pp in pairs]
    qt = [qaugt_ref[0, hh] for hh in hs]
    s_own = [jnp.where(qry >= key, mm(k_own[hh // 2][:, :LANES], qt[hh][:LANES]) + tabt_ref[hh // 2, hh % 2, 1], NEG)
             for hh in hs]
    s_adj = [mm(k_adj[hh // 2], qt[hh]) + tabt_ref[hh // 2, hh % 2, 0] for hh in hs]
    m = [jnp.maximum(cmax(s_own[hh]), cmax(s_adj[hh])) for hh in hs]
    for hh in hs:
        acc_scr[hh] = (pv(vblk(hh, i), jnp.exp2(s_own[hh] - m[hh]))
                       + pv(vblk(hh, n_far), jnp.exp2(s_adj[hh] - m[hh])))

    def consume(g, buf, m):
        js = blocks_of(g)
        m_new = []
        for hh in hs:
            s = [buf[hh, a] for a in range(nk)]
            mh = functools.reduce(jnp.maximum, [cmax(x) for x in s], m[hh])
            ah = jnp.exp2(m[hh] - mh) * acc_scr[hh]
            for a in range(nk):
                ah = ah + pv(vblk(hh, js[a]), jnp.exp2(s[a] - mh))
            acc_scr[hh] = ah
            m_new.append(mh)
        return tuple(m_new)

    def body(u, m):
        scores(2 * u + 1, sb_scr)
        m = consume(2 * u, sa_scr, m)
        scores(2 * u + 2, sa_scr)
        return consume(2 * u + 1, sb_scr, m)

    m = lax.fori_loop(0, n_groups // 2, body, tuple(m))
    lax.cond(n_groups % 2 == 1, lambda mm_: consume(n_groups - 1, sa_scr, mm_), lambda mm_: mm_, m)
    o_ref[0] = jnp.concatenate(
        [(acc_scr[hh][:HEAD_DIM] / acc_scr[hh][HEAD_DIM:HEAD_DIM + 1]).T for hh in hs], axis=-1)


def _moba_attn(qaugt, kaug, vaugt, tabt):
    bsz, _, _, seq = qaugt.shape
    nb = seq // MOBA_BLOCK
    pp = MOBA_PAIRS_PER_STEP
    return pl.pallas_call(
        _moba_attn_kernel, name="moba_attn",
        grid=(bsz, N_HEADS // (2 * pp), nb),
        in_specs=[pl.BlockSpec((1, 2 * pp, 2 * LANES, MOBA_BLOCK), lambda b, p, i: (b, p, 0, i)),
                  pl.BlockSpec((1, pp, seq, 2 * LANES), lambda b, p, i: (b, p, 0, 0)),
                  pl.BlockSpec((1, 2 * pp, VT_ROWS, seq), lambda b, p, i: (b, p, 0, 0)),
                  pl.BlockSpec((pp, 2, 2, MOBA_BLOCK, MOBA_BLOCK), lambda b, p, i: (p, 0, 0, 0, 0))],
        out_specs=pl.BlockSpec((1, MOBA_BLOCK, pp * LANES), lambda b, p, i: (b, i, p)),
        out_shape=jax.ShapeDtypeStruct((bsz, seq, MIX_W), f32),
        scratch_shapes=[pltpu.VMEM((2 * pp, MOBA_KV_PER_GROUP, MOBA_BLOCK, MOBA_BLOCK), f32)] * 2
        + [pltpu.VMEM((2 * pp, VT_ROWS, MOBA_BLOCK), f32)],
        compiler_params=_cparams(("parallel", "parallel", "arbitrary"), 56),
    )(qaugt, kaug, vaugt, tabt)


def _t5_bucket(dist):
    n = jnp.maximum(dist, 0)
    max_exact = REL_BUCKETS // 2
    nf = jnp.maximum(n, max_exact).astype(f32)
    large = max_exact + (jnp.log(nf / max_exact) / math.log(REL_MAX_DIST / max_exact)
                         * (REL_BUCKETS - max_exact)).astype(jnp.int32)
    large = jnp.minimum(large, REL_BUCKETS - 1)
    return jnp.where(n < max_exact, n, large)


def _moba_bias_kernel(vec_ref, o_ref):
    blk = MOBA_BLOCK
    t = pltpu.roll(jnp.broadcast_to(vec_ref[0] * LOG2E, (blk, 2 * blk)), 0, 1, stride=1, stride_axis=0)
    o_ref[0, 0] = t[:, blk:]
    o_ref[0, 1] = t[:, :blk]


def _moba_bias_tables(rel_bias):
    assert MOBA_BLOCK >= REL_MAX_DIST
    by_dist = rel_bias.astype(f32)[_t5_bucket(jnp.arange(2 * MOBA_BLOCK))]
    far = rel_bias.astype(f32)[_t5_bucket(jnp.array(2 * MOBA_BLOCK))]
    vec = (by_dist - far).T.reshape(N_HEADS, 1, 2 * MOBA_BLOCK)
    tab = pl.pallas_call(
        _moba_bias_kernel, name="moba_bias",
        grid=(N_HEADS,),
        in_specs=[pl.BlockSpec((1, 1, 2 * MOBA_BLOCK), lambda h: (h, 0, 0))],
        out_specs=pl.BlockSpec((1, 2, MOBA_BLOCK, MOBA_BLOCK), lambda h: (h, 0, 0, 0)),
        out_shape=jax.ShapeDtypeStruct((N_HEADS, 2, MOBA_BLOCK, MOBA_BLOCK), f32),
        compiler_params=_cparams(("parallel",), 16),
    )(vec)
    return tab.reshape(N_HEADS // 2, 2, 2, MOBA_BLOCK, MOBA_BLOCK)


def _causal_conv(x, halo, w_ref):
    ts, nh = x.shape[0], halo.shape[0]
    xe = jnp.concatenate([halo, x], axis=0)
    acc = x * w_ref[CONV_K - 1:CONV_K, :]
    for d in range(1, CONV_K):
        acc = acc + xe[nh - d:nh - d + ts] * w_ref[CONV_K - 1 - d:CONV_K - d, :]
    return acc


def _tile_and_halo(x_ref, halo_ref):
    halo = jnp.where(pl.program_id(1) == 0, 0.0, halo_ref[...].astype(f32))
    return x_ref[...].astype(f32), halo


def _tok_spec(rows, nsteps):
    return pl.BlockSpec((rows, MIX_W), lambda b, i: (b * nsteps + i, 0))


def _pair_masks(rows):
    lane = lax.broadcasted_iota(jnp.int32, (rows, LANES), 1)
    return lane < HEAD_DIM, lane >= HEAD_DIM


def _chunk_sum_matrix(chunk):
    r = jnp.arange(CUMSUM_TILE)[:, None]
    c = jnp.arange(CUMSUM_TILE)[None, :]
    same = (r // chunk) == (c // chunk)
    return jnp.concatenate([same & (r >= c), same], axis=0).astype(bf16)


def _chunk_sums(cm_ref, x):
    parts = [_sel_left(cm_ref[...], x[r:r + CUMSUM_TILE]) for r in range(0, x.shape[0], CUMSUM_TILE)]
    return (jnp.concatenate([p[:CUMSUM_TILE] for p in parts], axis=0),
            jnp.concatenate([p[CUMSUM_TILE:] for p in parts], axis=0))


def _row_spec(width, off, nt):
    return pl.BlockSpec((ROW_TILE, width), lambda b, i: (b * nt + i, off // width))


def _halo_spec(width, off, nt):
    per = ROW_TILE // HALO_ROWS
    return pl.BlockSpec((HALO_ROWS, width), lambda b, i: (jnp.maximum((b * nt + i) * per - 1, 0), off // width))


def _small_spec(nt):
    return pl.BlockSpec((ROW_TILE, LANES), lambda b, i: (b * nt + i, 0))


def _const_spec(shape):
    return pl.BlockSpec(shape, lambda b, i: (0,) * len(shape))


def _gdn_prep_kernel(x_ref, halo_ref, sm_ref, cw_ref, alog_ref, dtb_ref, bd_ref, eb_ref, ea_ref, cm_ref,
                     q_ref, k_ref, kb_ref, vb_ref, qd_ref, kbe_ref, kd_ref, gc_ref):
    qkv = _silu(_causal_conv(*_tile_and_halo(x_ref, halo_ref), cw_ref))
    q, k, v = qkv[:, :MIX_W], qkv[:, MIX_W:2 * MIX_W], qkv[:, 2 * MIX_W:]
    bd = bd_ref[...]
    q = q * lax.rsqrt(_sel_right1(q * q, bd) + L2_EPS) * HEAD_DIM ** -0.5
    k = k * lax.rsqrt(_sel_right1(k * k, bd) + L2_EPS)
    sm = sm_ref[...]
    beta = _sel_right1(_sigmoid(sm), eb_ref[...])
    g = -jnp.exp(alog_ref[...]) * _softplus(sm + dtb_ref[...])
    gc, g_end = _chunk_sums(cm_ref, g)
    eg = jnp.exp(_sel_right(gc, ea_ref[...]))
    e_rest = jnp.exp(_sel_right(g_end - gc, ea_ref[...]))
    kb = k * beta
    for ref, val in ((q_ref, q), (k_ref, k), (kb_ref, kb), (vb_ref, v * beta),
                     (qd_ref, q * eg), (kbe_ref, kb * eg), (kd_ref, k * e_rest)):
        ref[...] = val.astype(bf16)
    gc_ref[...] = gc[:, SM_BA:SM_BA + N_HEADS]


def _gdn_prep(proj, small, bsz, seq, conv_w, alog128, dtb128, bd_ones, e_beta, e_ba, cm):
    nt = seq // ROW_TILE
    w3 = 3 * MIX_W
    return pl.pallas_call(
        _gdn_prep_kernel, name="gdn_prep",
        grid=(bsz, nt),
        in_specs=[_row_spec(w3, OFF_BQKV, nt), _halo_spec(w3, OFF_BQKV, nt), _small_spec(nt),
                  _const_spec((CONV_K, w3)), _const_spec((1, LANES)), _const_spec((1, LANES)),
                  _const_spec((MIX_W, MIX_W)), _const_spec((LANES, MIX_W)), _const_spec((LANES, MIX_W)),
                  _const_spec((2 * CUMSUM_TILE, CUMSUM_TILE))],
        out_specs=[_tok_spec(ROW_TILE, nt)] * 7 + [pl.BlockSpec((ROW_TILE, N_HEADS), lambda b, i: (b * nt + i, 0))],
        out_shape=[jax.ShapeDtypeStruct((bsz * seq, MIX_W), bf16)] * 7 + [jax.ShapeDtypeStruct((bsz * seq, N_HEADS), f32)],
        compiler_params=_cparams(("parallel", "parallel"), 40),
    )(proj, proj, small, conv_w, alog128, dtb128, bd_ones, e_beta, e_ba, cm)


def _gdn_chunk_kernel(q_ref, k_ref, kb_ref, vb_ref, qd_ref, kbe_ref, kd_ref, gc_ref, gct_ref, nw_ref, bdm_ref,
                      o_ref, st_scr):
    @pl.when(pl.program_id(1) == 0)
    def _():
        st_scr[...] = jnp.zeros_like(st_scr)

    n = GDN_CHUNK
    cis, prs = range(CHUNKS_PER_STEP), range(N_HEADS // 2)
    tiles = [(ci, p) for ci in cis for p in prs]
    ch = [(ci, p, hh) for ci, p in tiles for hh in (0, 1)]
    blk = lambda ref, ci, p: ref[ci * n:(ci + 1) * n, p * LANES:(p + 1) * LANES]
    r, c = _iota2((n, n))
    row2, col2 = _iota2((2 * n, 2 * n))
    same_head = (row2 < n) == (col2 < n)
    hm2 = _pair_masks(2 * n)
    keep0 = _pair_masks(n)[0]
    pick = lambda x0, x1: jnp.where(keep0, x0, x1)
    zero = jnp.zeros((), bf16)
    gc_all = [gc_ref[0, ci] for ci in cis]
    gct_all = [gct_ref[0, ci] for ci in cis]
    gcol = lambda ci, p, hh: gc_all[ci][:, 2 * p + hh:2 * p + hh + 1]
    decay = {(ci, p, hh): jnp.exp(jnp.where(r >= c, gcol(ci, p, hh) - gct_all[ci][2 * p + hh:2 * p + hh + 1, :], NEG))
             for ci, p, hh in ch}
    lhs = {x: jnp.concatenate([blk(kb_ref, *x), blk(q_ref, *x)], axis=0) for x in tiles}
    gram = {(ci, p, hh): _mm_nt(jnp.where(hm2[hh], lhs[ci, p], zero), blk(k_ref, ci, p)) for ci, p, hh in ch}
    t = dict(zip(ch, _unit_lower_inverses([jnp.where(r > c, gram[x][:n] * decay[x], 0.0) for x in ch])))
    u = {(ci, p): pick(_mm(t[ci, p, 0], blk(vb_ref, ci, p)), _mm(t[ci, p, 1], blk(vb_ref, ci, p))) for ci, p in tiles}
    w = {(ci, p): pick(_mm(t[ci, p, 0], blk(kbe_ref, ci, p)), _mm(t[ci, p, 1], blk(kbe_ref, ci, p)))
         for ci, p in tiles}
    a_in = {x: gram[x][n:] * decay[x] for x in ch}
    st = [st_scr[p] for p in prs]
    for ci in cis:
        ws = [_mm(jnp.concatenate([w[ci, p].astype(bf16), blk(qd_ref, ci, p)], axis=0), st[p]) for p in prs]
        v_new = [u[ci, p] - ws[p][:n] for p in prs]
        o = [ws[p][n:] + pick(_mm(a_in[ci, p, 0], v_new[p]), _mm(a_in[ci, p, 1], v_new[p])) for p in prs]
        upd = [jnp.where(same_head, _mm_tn(blk(kd_ref, ci, p), v_new[p]), 0.0) for p in prs]
        for p in prs:
            ms = _sel_right1(o[p] * o[p], bdm_ref[...])
            o_ref[ci * n:(ci + 1) * n, p * LANES:(p + 1) * LANES] = o[p] * lax.rsqrt(ms + RMS_EPS) * nw_ref[...]
        g_end = [jnp.exp(jnp.where(keep0[0:1], gcol(ci, p, 0)[n - 1:n], gcol(ci, p, 1)[n - 1:n])) for p in prs]
        st = [st[p] * g_end[p] + upd[p] for p in prs]
    for p in prs:
        st_scr[p] = st[p]


def _gdn_chunk(bsz, seq, q, k, kb, vb, qd, kbe, kd, gc, norm_w2, bd_mean2):
    n = GDN_CHUNK
    nc = seq // n
    per = CHUNKS_PER_STEP
    gc4 = gc.reshape(bsz, nc, n, N_HEADS)
    gct4 = jnp.swapaxes(gc4, 2, 3)
    spec = _tok_spec(per * n, nc // per)
    return pl.pallas_call(
        _gdn_chunk_kernel, name="gdn_chunk",
        grid=(bsz, nc // per),
        in_specs=[spec] * 7 + [pl.BlockSpec((1, per, n, N_HEADS), lambda b, i: (b, i, 0, 0)),
                               pl.BlockSpec((1, per, N_HEADS, n), lambda b, i: (b, i, 0, 0)),
                               _const_spec((1, LANES)), _const_spec((LANES, LANES))],
        out_specs=spec,
        out_shape=jax.ShapeDtypeStruct((bsz * seq, MIX_W), f32),
        scratch_shapes=[pltpu.VMEM((N_HEADS // 2, 2 * HEAD_DIM, 2 * HEAD_DIM), f32)],
        compiler_params=_cparams(("parallel", "arbitrary"), 32),
    )(q, k, kb, vb, qd, kbe, kd, gc4, gct4, norm_w2, bd_mean2)


def _rwkv_prep_kernel(has_vres, *refs):
    if has_vres:
        (c_ref, halo_ref, mu_ref, w0_ref, wup_ref, a0_ref, aup_ref, gup_ref, kk_ref, ka_ref, bd_ref,
         rk_ref, cm_ref, vf_ref, v0_ref, vdn_ref, vup_ref,
         rt_ref, at_ref, bt_ref, kt_ref, v_ref, pe_ref, gout_ref, bonus_ref) = refs
    else:
        (c_ref, halo_ref, mu_ref, w0_ref, wup_ref, a0_ref, aup_ref, gup_ref, kk_ref, ka_ref, bd_ref,
         rk_ref, cm_ref,
         rt_ref, at_ref, bt_ref, kt_ref, v_ref, pe_ref, gout_ref, bonus_ref, cv_ref) = refs
    c, halo = _tile_and_halo(c_ref, halo_ref)
    prev = jnp.concatenate([halo[HALO_ROWS - 1:], c[:-1]], axis=0)
    c = c + (prev - c) * mu_ref[...]
    c_r, c_k, c_v = c[:, :MIX_W], c[:, MIX_W:2 * MIX_W], c[:, 2 * MIX_W:3 * MIX_W]
    c_wd = c[:, 3 * MIX_W:3 * MIX_W + 64]
    c_ad = c[:, 3 * MIX_W + 64:3 * MIX_W + 128]
    c_gd = c[:, 3 * MIX_W + 128:]
    w_log = -_softplus(-(w0_ref[...] + _mm(jnp.tanh(c_wd), wup_ref[...]))) - 0.5
    a_in = _sigmoid(a0_ref[...] + _mm(c_ad, aup_ref[...]))
    gout_ref[...] = _mm(_sigmoid(c_gd), gup_ref[...])
    if has_vres:
        lam = _sigmoid(v0_ref[...] + _mm(_mm(c_v, vdn_ref[...]), vup_ref[...]))
        v_r = c_v + (vf_ref[...] - c_v) * lam
    else:
        v_r = c_v
        cv_ref[...] = c_v
    bd = bd_ref[...]
    kk = c_k * kk_ref[...]
    kk = kk * lax.rsqrt(_sel_right1(kk * kk, bd) + L2_EPS)
    k_r = c_k * (1.0 + (a_in - 1.0) * ka_ref[...])
    b = kk * a_in
    bonus_ref[...] = _sel_right1(c_r * k_r * rk_ref[...], bd) * v_r
    lc, lc_end = _chunk_sums(cm_ref, -jnp.exp(w_log))
    e_neg = jnp.exp(-lc)
    rt_ref[...] = (c_r * jnp.exp(lc)).astype(bf16)
    at_ref[...] = (-kk * jnp.exp(lc + jnp.exp(w_log))).astype(bf16)
    bt_ref[...] = (b * e_neg).astype(bf16)
    kt_ref[...] = (k_r * e_neg).astype(bf16)
    v_ref[...] = v_r.astype(bf16)
    pe_ref[...] = jnp.exp(lc_end)


def _rwkv_prep(proj, bsz, seq, mu, w0, w_up, a0, a_up, g_up, k_k, k_a, r_k, bd_ones, cm, vres):
    nt = seq // ROW_TILE
    wc = 3 * MIX_W + 256
    std = pl.BlockSpec((ROW_TILE, MIX_W), lambda b, i: (b * nt + i, 0))
    std_shape = jax.ShapeDtypeStruct((bsz * seq, MIX_W), f32)
    in_specs = [_row_spec(wc, OFF_C, nt), _halo_spec(wc, OFF_C, nt), _const_spec((1, wc)),
                _const_spec((1, MIX_W)), _const_spec((64, MIX_W)), _const_spec((1, MIX_W)), _const_spec((64, MIX_W)),
                _const_spec((128, MIX_W)), _const_spec((1, MIX_W)), _const_spec((1, MIX_W)), _const_spec((MIX_W, MIX_W)),
                _const_spec((1, MIX_W)), _const_spec((2 * CUMSUM_TILE, CUMSUM_TILE))]
    args = [proj, proj, mu, w0, w_up, a0, a_up, g_up, k_k, k_a, bd_ones, r_k, cm]
    out_specs = [std] * 8
    out_shape = [jax.ShapeDtypeStruct((bsz * seq, MIX_W), bf16)] * 5 + [std_shape] * 3
    if vres is not None:
        v_first, v0, v_down, v_up = vres
        in_specs += [std, _const_spec((1, MIX_W)), _const_spec(v_down.shape), _const_spec(v_up.shape)]
        args += [v_first, v0, v_down, v_up]
    else:
        out_specs.append(std)
        out_shape.append(std_shape)
    return pl.pallas_call(
        functools.partial(_rwkv_prep_kernel, vres is not None), name="rwkv_prep",
        grid=(bsz, nt), in_specs=in_specs, out_specs=out_specs, out_shape=out_shape,
        compiler_params=_cparams(("parallel", "parallel"), 40),
    )(*args)


def _rwkv_chunk_kernel(rt_ref, at_ref, bt_ref, kt_ref, v_ref, pe_ref, o_ref, st_scr):
    @pl.when(pl.program_id(1) == 0)
    def _():
        st_scr[...] = jnp.zeros_like(st_scr)

    n = RWKV_CHUNK
    cis, prs = range(CHUNKS_PER_STEP), range(N_HEADS // 2)
    tiles = [(ci, p) for ci in cis for p in prs]
    ch = [(ci, p, hh) for ci, p in tiles for hh in (0, 1)]
    blk = lambda ref, ci, p: ref[ci * n:(ci + 1) * n, p * LANES:(p + 1) * LANES]
    row, col = _iota2((2 * n, 2 * n))
    rr, cc = row & (n - 1), col & (n - 1)
    mask = rr + jnp.where(row < n, 0, 1) > cc
    same_head = (row < n) == (col < n)
    hm2 = _pair_masks(2 * n)
    keep0 = _pair_masks(n)[0]
    pick = lambda x0, x1: jnp.where(keep0, x0, x1)
    zero = jnp.zeros((), bf16)
    lhs = {x: jnp.concatenate([blk(at_ref, *x), blk(rt_ref, *x)], axis=0) for x in tiles}
    rhs = {x: jnp.concatenate([blk(bt_ref, *x), blk(kt_ref, *x)], axis=0) for x in tiles}
    gm = {(ci, p, hh): jnp.where(mask, _mm_nt(jnp.where(hm2[hh], lhs[ci, p], zero), rhs[ci, p]), 0.0)
          for ci, p, hh in ch}
    t = dict(zip(ch, _unit_lower_inverses([-gm[x][:n, :n] for x in ch])))
    v = {x: blk(v_ref, *x) for x in tiles}
    zv = {x: jnp.concatenate([jnp.zeros_like(v[x]), v[x]], axis=0) for x in tiles}
    makv = {(ci, p): pick(_mm(gm[ci, p, 0][:n], zv[ci, p]), _mm(gm[ci, p, 1][:n], zv[ci, p])) for ci, p in tiles}
    st = [st_scr[p] for p in prs]
    for ci in cis:
        ah = [_mm_nt(lhs[ci, p], st[p]) for p in prs]
        rhs_u = [ah[p][:n] + makv[ci, p] for p in prs]
        u = [pick(_mm(t[ci, p, 0], rhs_u[p]), _mm(t[ci, p, 1], rhs_u[p])) for p in prs]
        uv = [jnp.concatenate([u[p].astype(bf16), v[ci, p]], axis=0) for p in prs]
        o = [ah[p][n:] + pick(_mm(gm[ci, p, 0][n:], uv[p]), _mm(gm[ci, p, 1][n:], uv[p])) for p in prs]
        pe = [pe_ref[ci * n:ci * n + 1, p * LANES:(p + 1) * LANES] for p in prs]
        upd = [_mm_tn(uv[p], rhs[ci, p].astype(f32) * pe[p]) for p in prs]
        for p in prs:
            o_ref[ci * n:(ci + 1) * n, p * LANES:(p + 1) * LANES] = o[p]
        st = [st[p] * pe[p] + jnp.where(same_head, upd[p], 0.0) for p in prs]
    for p in prs:
        st_scr[p] = st[p]


def _rwkv_chunk(bsz, seq, rt, at, bt, kt, v, pe):
    n = RWKV_CHUNK * CHUNKS_PER_STEP
    spec = _tok_spec(n, seq // n)
    return pl.pallas_call(
        _rwkv_chunk_kernel, name="rwkv_chunk",
        grid=(bsz, seq // n),
        in_specs=[spec] * 6,
        out_specs=spec,
        out_shape=jax.ShapeDtypeStruct((bsz * seq, MIX_W), f32),
        scratch_shapes=[pltpu.VMEM((N_HEADS // 2, 2 * HEAD_DIM, 2 * HEAD_DIM), f32)],
        compiler_params=_cparams(("parallel", "arbitrary"), 32),
    )(rt, at, bt, kt, v, pe)


def _ssd_prep_kernel(x_ref, halo_ref, sm_ref, cw_ref, cb_ref, alog_ref, dtb_ref, edt_ref, cm_ref,
                     xdt_ref, x_out_ref, bc_ref, acs_ref):
    xbc = _silu(_causal_conv(*_tile_and_halo(x_ref, halo_ref), cw_ref) + cb_ref[...])
    m_x = xbc[:, :MIX_W]
    dt = _softplus(sm_ref[...] + dtb_ref[...])
    xdt_ref[...] = (m_x * _sel_right1(dt, edt_ref[...])).astype(xdt_ref.dtype)
    x_out_ref[...] = m_x
    bc_ref[...] = xbc[:, MIX_W:].astype(bc_ref.dtype)
    acs, _ = _chunk_sums(cm_ref, dt * -jnp.exp(alog_ref[...]))
    acs_ref[...] = acs[:, SM_DT:SM_DT + N_HEADS]


def _ssd_prep(proj, small, bsz, seq, conv_w, conv_b, alog128, dtb128, e_dt, cm):
    nt = seq // ROW_TILE
    wx = MIX_W + 4 * SSM_STATE
    return pl.pallas_call(
        _ssd_prep_kernel, name="ssd_prep",
        grid=(bsz, nt),
        in_specs=[_row_spec(wx, OFF_DXBC, nt), _halo_spec(wx, OFF_DXBC, nt), _small_spec(nt),
                  _const_spec((CONV_K, wx)), _const_spec((1, wx)), _const_spec((1, LANES)), _const_spec((1, LANES)),
                  _const_spec((LANES, MIX_W)), _const_spec((2 * CUMSUM_TILE, CUMSUM_TILE))],
        out_specs=[_tok_spec(ROW_TILE, nt)] * 2 + [pl.BlockSpec((ROW_TILE, 4 * SSM_STATE), lambda b, i: (b * nt + i, 0)),
                                                   pl.BlockSpec((ROW_TILE, N_HEADS), lambda b, i: (b * nt + i, 0))],
        out_shape=[jax.ShapeDtypeStruct((bsz * seq, MIX_W), bf16), jax.ShapeDtypeStruct((bsz * seq, MIX_W), f32),
                   jax.ShapeDtypeStruct((bsz * seq, 4 * SSM_STATE), bf16),
                   jax.ShapeDtypeStruct((bsz * seq, N_HEADS), f32)],
        compiler_params=_cparams(("parallel", "parallel"), 40),
    )(proj, proj, small, conv_w, conv_b, alog128, dtb128, e_dt, cm)


def _ssd_chunk_kernel(xdt_ref, x_ref, bc_ref, a_ref, at_ref, dvec_ref, o_ref, st_scr):
    @pl.when(pl.program_id(1) == 0)
    def _():
        st_scr[...] = jnp.zeros_like(st_scr)

    n = SSD_CHUNK
    cis, prs = range(SSD_CHUNKS_PER_STEP), range(N_HEADS // 2)
    grp = lambda p: (2 * p) // (N_HEADS // 2)
    tiles = [(ci, p) for ci in cis for p in prs]
    ch = [(ci, p, hh) for ci, p in tiles for hh in (0, 1)]
    rows = lambda ci: slice(ci * n, (ci + 1) * n)
    lanes = lambda p: slice(p * LANES, (p + 1) * LANES)
    r, c = _iota2((n, n))
    keep0 = _pair_masks(n)[0]
    pick = lambda x0, x1: jnp.where(keep0, x0, x1)
    b_g = {(ci, g): bc_ref[rows(ci), g * SSM_STATE:(g + 1) * SSM_STATE] for ci in cis for g in range(2)}
    c_g = {(ci, g): bc_ref[rows(ci), (2 + g) * SSM_STATE:(3 + g) * SSM_STATE] for ci in cis for g in range(2)}
    cb = {x: _mm_nt(c_g[x], b_g[x]) for x in b_g}
    col = lambda ci, p, hh: a_ref[0, ci][:, 2 * p + hh:2 * p + hh + 1]
    ac = {x: col(*x) for x in ch}
    a_last = {x: col(*x)[n - 1:n] for x in ch}
    lmat = {(ci, p, hh): jnp.exp(jnp.where(r >= c, ac[ci, p, hh] - at_ref[0, ci][2 * p + hh:2 * p + hh + 1, :], NEG))
            for ci, p, hh in ch}
    xg = {(ci, p): xdt_ref[rows(ci), lanes(p)] for ci, p in tiles}
    y_diag = {(ci, p): pick(*[_mm(cb[ci, grp(p)] * lmat[ci, p, hh], xg[ci, p]) for hh in (0, 1)]) for ci, p in tiles}
    upd = {(ci, p): pick(*[_mm_tn(b_g[ci, grp(p)].astype(f32) * jnp.exp(a_last[ci, p, hh] - ac[ci, p, hh]), xg[ci, p])
                           for hh in (0, 1)]) for ci, p in tiles}
    c_in = {(ci, p, hh): c_g[ci, grp(p)].astype(f32) * jnp.exp(ac[ci, p, hh]) for ci, p, hh in ch}
    st = [st_scr[p] for p in prs]
    for ci in cis:
        y_off = [pick(_mm(c_in[ci, p, 0], st[p]), _mm(c_in[ci, p, 1], st[p])) for p in prs]
        for p in prs:
            o_ref[rows(ci), lanes(p)] = y_diag[ci, p] + y_off[p] + x_ref[rows(ci), lanes(p)] * dvec_ref[:, lanes(p)]
        st = [st[p] * jnp.exp(jnp.where(keep0[0:1], a_last[ci, p, 0], a_last[ci, p, 1])) + upd[ci, p] for p in prs]
    for p in prs:
        st_scr[p] = st[p]


def _ssd_chunk(bsz, seq, xdt, x, bc, acs, dvec):
    n = SSD_CHUNK
    nc = seq // n
    per = SSD_CHUNKS_PER_STEP
    a4 = acs.reshape(bsz, nc, n, N_HEADS)
    at4 = jnp.swapaxes(a4, 2, 3)
    return pl.pallas_call(
        _ssd_chunk_kernel, name="ssd_chunk",
        grid=(bsz, nc // per),
        in_specs=[_tok_spec(per * n, nc // per)] * 2
        + [pl.BlockSpec((per * n, 4 * SSM_STATE), lambda b, i: (b * (nc // per) + i, 0)),
           pl.BlockSpec((1, per, n, N_HEADS), lambda b, i: (b, i, 0, 0)),
           pl.BlockSpec((1, per, N_HEADS, n), lambda b, i: (b, i, 0, 0)),
           _const_spec((1, MIX_W))],
        out_specs=_tok_spec(per * n, nc // per),
        out_shape=jax.ShapeDtypeStruct((bsz * seq, MIX_W), f32),
        scratch_shapes=[pltpu.VMEM((N_HEADS // 2, SSM_STATE, 2 * HEAD_DIM), f32)],
        compiler_params=_cparams(("parallel", "arbitrary"), 32),
    )(xdt, x, bc, a4, at4, dvec)


def _merge_kernel(x_ref, ya_ref, ob_ref, bz_ref, wkv_ref, bonus_ref, gout_ref, lnw_ref, lnb_ref, bdm_ref,
                  yd_ref, dz_ref, mnw_ref, g0_ref, g1_ref, g2_ref, g3_ref, wb_ref, wo_ref, o_ref):
    def gated(n, y, g_ref):
        return _sigmoid(g_ref[...].astype(f32)) * _mm(y, wb_ref[n])

    acc = gated(0, ya_ref[0], g0_ref)
    acc = acc + gated(1, ob_ref[...] * _silu(bz_ref[...].astype(f32)), g1_ref)

    w = wkv_ref[...]
    d = w - _sel_right1(w, bdm_ref[...])
    wkv_ln = d * lax.rsqrt(_sel_right1(d * d, bdm_ref[...]) + RWKV_LN_EPS)
    y_c = (wkv_ln * lnw_ref[...] + lnb_ref[...] + bonus_ref[...]) * gout_ref[...]
    acc = acc + gated(2, y_c, g2_ref)

    yz = yd_ref[...] * _silu(dz_ref[...].astype(f32))
    half = MIX_W // 2
    y_d = jnp.concatenate(
        [yz[:, s:s + half] * lax.rsqrt(jnp.mean(jnp.square(yz[:, s:s + half]), axis=-1, keepdims=True) + RMS_EPS)
         for s in (0, half)], axis=-1) * mnw_ref[...]
    acc = acc + gated(3, y_d, g3_ref)
    o_ref[...] = x_ref[...] + _mm(acc, wo_ref[...])


def _merge(x2, proj, bsz, seq, ya, ob, wkv, bonus, gout, lnw, lnb, bd_mean, yd, mnw, wb, wo):
    nt = seq // ROW_TILE
    std = lambda w: pl.BlockSpec((ROW_TILE, w), lambda b, i: (b * nt + i, 0))
    gate = lambda n: _row_spec(D_MODEL, OFF_GATES + n * D_MODEL, nt)
    return pl.pallas_call(
        _merge_kernel, name="merge",
        grid=(bsz, nt),
        in_specs=[std(D_MODEL), pl.BlockSpec((1, ROW_TILE, MIX_W), lambda b, i: (b, i, 0)),
                  std(MIX_W), _row_spec(MIX_W, OFF_BZ, nt),
                  std(MIX_W), std(MIX_W), std(MIX_W), _const_spec((1, MIX_W)), _const_spec((1, MIX_W)),
                  _const_spec((MIX_W, MIX_W)),
                  std(MIX_W), _row_spec(MIX_W, OFF_DZ, nt), _const_spec((1, MIX_W)),
                  gate(0), gate(1), gate(2), gate(3),
                  _const_spec((4, MIX_W, D_MODEL)), _const_spec((D_MODEL, D_MODEL))],
        out_specs=std(D_MODEL),
        out_shape=jax.ShapeDtypeStruct((bsz * seq, D_MODEL), f32),
        compiler_params=_cparams(("parallel", "parallel"), 48),
    )(x2, ya, ob, proj, wkv, bonus, gout, lnw, lnb, bd_mean, yd, proj, mnw, proj, proj, proj, proj, wb, wo)


def _lane_vec(vals, off):
    return jnp.zeros((1, LANES), f32).at[0, off:off + vals.shape[0]].set(vals)


def _head_expand(off):
    n = jnp.arange(LANES)[:, None]
    c = jnp.arange(MIX_W)[None, :]
    return (n - off == c // HEAD_DIM).astype(bf16)


def _pack_w_in(w):
    pad = lambda n: jnp.zeros((w.shape[0], n), w.dtype)
    cols = [w[:, W_A:W_BZ],
            w[:, W_BZ:W_BBETA],
            w[:, W_C:W_DZ],
            w[:, W_BBETA:W_C], w[:, W_DDT:W_GATES], pad(2 * LANES - 3 * N_HEADS),
            w[:, W_DZ:W_DXBC], w[:, W_DXBC:W_DDT], w[:, W_GATES:]]
    out = jnp.concatenate(cols, axis=1).astype(bf16)
    assert out.shape[1] == N_PROJ
    return out


def kernel(x, rel_bias, norm1_w, w_in, moba_q_norm, moba_k_norm, gdn_conv_w, gdn_A_log, gdn_dt_bias, gdn_norm_w, rwkv_mu, rwkv_w0, rwkv_w_up, rwkv_a0, rwkv_a_up, rwkv_g_up, rwkv_k_k, rwkv_k_a, rwkv_r_k, rwkv_v0, rwkv_v_down, rwkv_v_up, rwkv_ln_w, rwkv_ln_b, mamba_conv_w, mamba_conv_b, mamba_dt_bias, mamba_A_log, mamba_D, mamba_norm_w, w_branch, w_out, norm2_w, ffn_w_in, ffn_w_down):
    bsz, seq, d = x.shape
    depth = w_in.shape[0]
    assert d == D_MODEL and (bsz * seq) % IN_PROJ_TILE_M == 0 and seq % MM_TILE_M == 0
    x2 = x.reshape(bsz * seq, d)
    row = lambda v: v.reshape(1, -1).astype(f32)

    hid = jnp.arange(MIX_W) // HEAD_DIM
    bd_ones = (hid[:, None] == hid[None, :]).astype(bf16)
    bd_mean = (bd_ones.astype(f32) / HEAD_DIM).astype(bf16)
    e_beta, e_ba, e_dt = _head_expand(SM_BETA), _head_expand(SM_BA), _head_expand(SM_DT)
    cm64, cm128 = _chunk_sum_matrix(GDN_CHUNK), _chunk_sum_matrix(SSD_CHUNK)
    assert GDN_CHUNK == RWKV_CHUNK
    tab = _moba_bias_tables(rel_bias)
    v_first = None
    for i in range(depth):
        proj, small = _in_proj(x2, row(norm1_w[i]), _pack_w_in(w_in[i]))

        qaug, kaug, v_a = _moba_prep(proj, bsz, seq, row(jnp.tile(moba_q_norm[i], N_HEADS)),
                                     row(jnp.tile(moba_k_norm[i], N_HEADS)), bd_mean)
        y_a = _moba_attn(qaug, kaug, v_a, tab)

        gdn_in = _gdn_prep(proj, small, bsz, seq, gdn_conv_w[i], _lane_vec(gdn_A_log[i], SM_BA),
                           _lane_vec(gdn_dt_bias[i], SM_BA), bd_ones, e_beta, e_ba, cm64)
        o_b = _gdn_chunk(bsz, seq, *gdn_in, row(jnp.tile(gdn_norm_w[i], 2)), bd_mean[:LANES, :LANES])

        vres = None if i == 0 else (v_first, row(rwkv_v0[i - 1]), rwkv_v_down[i - 1].astype(bf16),
                                    rwkv_v_up[i - 1].astype(bf16))
        outs = _rwkv_prep(proj, bsz, seq, row(rwkv_mu[i]), row(rwkv_w0[i]), rwkv_w_up[i].astype(bf16),
                          row(rwkv_a0[i]), rwkv_a_up[i].astype(bf16), rwkv_g_up[i].astype(bf16),
                          row(rwkv_k_k[i]), row(rwkv_k_a[i]), row(rwkv_r_k[i]), bd_ones, cm64, vres)
        g_out, bonus = outs[6], outs[7]
        if i == 0:
            v_first = outs[8]
        wkv = _rwkv_chunk(bsz, seq, *outs[:6])

        xdt, x_d, bc, acs = _ssd_prep(proj, small, bsz, seq, mamba_conv_w[i], row(mamba_conv_b[i]),
                                      _lane_vec(mamba_A_log[i], SM_DT), _lane_vec(mamba_dt_bias[i], SM_DT), e_dt, cm128)
        y_d = _ssd_chunk(bsz, seq, xdt, x_d, bc, acs, row(jnp.repeat(mamba_D[i], HEAD_DIM)))

        x2 = _merge(x2, proj, bsz, seq, y_a, o_b, wkv, bonus, g_out,
                    row(rwkv_ln_w[i]), row(rwkv_ln_b[i]), bd_mean, y_d, row(mamba_norm_w[i]),
                    w_branch[i].astype(bf16), w_out[i].astype(bf16))

        x2 = _ffn(x2, row(norm2_w[i]), ffn_w_in[i].astype(bf16), ffn_w_down[i].astype(bf16))
    return x2.reshape(bsz, seq, d)
```

```python
import functools
import math

import jax
import jax.numpy as jnp
from jax import lax
from jax.experimental import pallas as pl
from jax.experimental.pallas import tpu as pltpu

f32, bf16 = jnp.float32, jnp.bfloat16
HI = lax.Precision.HIGHEST

D_MODEL = 1024
N_HEADS = 8
HEAD_DIM = 64
MIX_W = N_HEADS * HEAD_DIM
RMS_EPS = 1e-6
L2_EPS = 1e-6
CONV_K = 4
MOBA_BLOCK = 256
MOBA_TOPK = 3
MOBA_PAIRS_PER_STEP = 4
MOBA_KV_PER_GROUP = 2
REL_BUCKETS = 32
REL_MAX_DIST = 128
GDN_CHUNK = 64
RWKV_CHUNK = 64
CHUNKS_PER_STEP = 4
RWKV_LN_EPS = 64e-5
SSM_STATE = 128
SSD_CHUNK = 128
SSD_CHUNKS_PER_STEP = 4
FFN_HIDDEN_SPLITS = 2
NEG = -1e30
LOG2E = math.log2(math.e)

LANES = 128
BF16_SUBLANES = 16
VT_ROWS = HEAD_DIM + BF16_SUBLANES

OFF_A, OFF_BQKV, OFF_BZ, OFF_C, OFF_SMALL, OFF_DZ, OFF_DXBC, OFF_GATES = 0, 1536, 3072, 3584, 5376, 5632, 6144, 7168
N_PROJ = OFF_GATES + 4 * D_MODEL
SM_BETA, SM_BA, SM_DT = 0, 8, 16
W_A, W_BQKV, W_BZ, W_BBETA, W_BA, W_C, W_DZ, W_DXBC, W_DDT, W_GATES = 0, 1536, 3072, 3584, 3592, 3600, 5392, 5904, 6928, 6936

ROW_TILE = 512
CUMSUM_TILE = 256
MM_TILE_M = 512
IN_PROJ_TILE_M, IN_PROJ_TILE_N = 1024, 1024
HALO_ROWS = BF16_SUBLANES


def _cparams(sem, vmem_mb):
    return pltpu.CompilerParams(dimension_semantics=sem, vmem_limit_bytes=vmem_mb * 1024 * 1024)


def _mm(a, b):
    return jnp.dot(a.astype(bf16), b.astype(bf16), preferred_element_type=f32)


def _mm_nt(a, b):
    return lax.dot_general(a.astype(bf16), b.astype(bf16), (((1,), (1,)), ((), ())), preferred_element_type=f32)


def _mm_tn(a, b):
    return lax.dot_general(a.astype(bf16), b.astype(bf16), (((0,), (0,)), ((), ())), preferred_element_type=f32)


def _mm_hi(a, b):
    return jnp.dot(a, b, precision=HI, preferred_element_type=f32)


def _softplus(x):
    return jnp.maximum(x, 0.0) + jnp.log1p(jnp.exp(-jnp.abs(x)))


def _sigmoid(x):
    return jax.nn.sigmoid(x)


def _silu(x):
    return x * _sigmoid(x)


def _iota2(shape):
    return lax.broadcasted_iota(jnp.int32, shape, 0), lax.broadcasted_iota(jnp.int32, shape, 1)


def _split3(x):
    hi = x.astype(bf16)
    r1 = x - hi.astype(f32)
    mid = r1.astype(bf16)
    return hi, mid, (r1 - mid.astype(f32)).astype(bf16)


def _sel_left(m01, x):
    return sum(jnp.dot(m01, p, preferred_element_type=f32) for p in _split3(x))


def _sel_right(x, m01):
    return sum(jnp.dot(p, m01, preferred_element_type=f32) for p in _split3(x))


def _sel_right1(x, m01):
    return jnp.dot(x.astype(bf16), m01, preferred_element_type=f32)


def _unit_lower_inverses(l_list):
    n = l_list[0].shape[0]
    r, c = _iota2((n, n))
    eye = jnp.where(r == c, 1.0, 0.0)
    ts = [eye for _ in l_list]
    for ls in range(n.bit_length() - 1):
        m = ((r >> (ls + 1)) == (c >> (ls + 1))) & (((r >> ls) & 1) == 1) & (((c >> ls) & 1) == 0)
        lms = [jnp.where(m, l, 0.0) for l in l_list]
        if ls == 0:
            ts = [t - lm for t, lm in zip(ts, lms)]
        else:
            tl = [_mm(t, lm) for t, lm in zip(ts, lms)]
            ts = [t - _mm(x, t) for t, x in zip(ts, tl)]
    return ts


def _in_proj_kernel(x_ref, nw_ref, w_ref, o_ref, sm_ref, h_scr):
    j = pl.program_id(1)

    @pl.when(j == 0)
    def _():
        x = x_ref[...]
        y = x * lax.rsqrt(jnp.mean(x * x, axis=-1, keepdims=True) + RMS_EPS)
        h_scr[...] = (y * nw_ref[...]).astype(bf16)

    tn = o_ref.shape[1]
    w = w_ref[:, pl.ds(pl.multiple_of(j * tn, tn), tn)]
    acc = jnp.dot(h_scr[...], w, preferred_element_type=f32)
    o_ref[...] = acc.astype(o_ref.dtype)

    @pl.when(j == OFF_SMALL // tn)
    def _():
        sm_ref[...] = acc[:, OFF_SMALL % tn:OFF_SMALL % tn + LANES]


def _in_proj(x2, nw, w):
    t, d = x2.shape
    n = w.shape[1]
    tm, tn = IN_PROJ_TILE_M, IN_PROJ_TILE_N
    return pl.pallas_call(
        _in_proj_kernel, name="in_proj",
        grid=(t // tm, n // tn),
        in_specs=[pl.BlockSpec((tm, d), lambda i, j: (i, 0)),
                  pl.BlockSpec((1, d), lambda i, j: (0, 0)),
                  pl.BlockSpec((d, n), lambda i, j: (0, 0), pipeline_mode=pl.Buffered(1))],
        out_specs=[pl.BlockSpec((tm, tn), lambda i, j: (i, j)),
                   pl.BlockSpec((tm, LANES), lambda i, j: (i, 0))],
        out_shape=[jax.ShapeDtypeStruct((t, n), bf16), jax.ShapeDtypeStruct((t, LANES), f32)],
        scratch_shapes=[pltpu.VMEM((tm, d), bf16)],
        compiler_params=_cparams(("parallel", "arbitrary"), 48),
    )(x2, nw, w)


def _ffn_kernel(x_ref, nw_ref, wi_ref, wd_ref, o_ref):
    x = x_ref[...]
    h = (x * lax.rsqrt(jnp.mean(x * x, axis=-1, keepdims=True) + RMS_EPS) * nw_ref[...]).astype(bf16)
    n = wd_ref.shape[0]
    step = n // FFN_HIDDEN_SPLITS
    acc = x
    for c0 in range(0, n, step):
        g = jnp.dot(h, wi_ref[:, c0:c0 + step], preferred_element_type=f32)
        u = jnp.dot(h, wi_ref[:, n + c0:n + c0 + step], preferred_element_type=f32)
        acc = acc + jnp.dot((_silu(g) * u).astype(bf16), wd_ref[c0:c0 + step, :], preferred_element_type=f32)
    o_ref[...] = acc


def _ffn(x2, nw, wi, wd):
    t, d = x2.shape
    n = wd.shape[0]
    tm = MM_TILE_M
    resident = lambda shape: pl.BlockSpec(shape, lambda i: (0, 0), pipeline_mode=pl.Buffered(1))
    return pl.pallas_call(
        _ffn_kernel, name="ffn",
        grid=(t // tm,),
        in_specs=[pl.BlockSpec((tm, d), lambda i: (i, 0)),
                  pl.BlockSpec((1, d), lambda i: (0, 0)),
                  resident((d, 2 * n)), resident((n, d))],
        out_specs=pl.BlockSpec((tm, d), lambda i: (i, 0)),
        out_shape=jax.ShapeDtypeStruct((t, d), f32),
        compiler_params=_cparams(("parallel",), 48),
    )(x2, nw, wi, wd)


def _top3_bias(gate_t, n_past):
    row = lax.broadcasted_iota(jnp.int32, gate_t.shape, 0)
    g = jnp.where(row < n_past, gate_t, -jnp.inf)
    sel = jnp.zeros(gate_t.shape, jnp.bool_)
    for _ in range(MOBA_TOPK):
        m = jnp.max(g, axis=0, keepdims=True)
        idx = jnp.min(jnp.where(g == m, row, gate_t.shape[0]), axis=0, keepdims=True)
        pick = row == idx
        sel = sel | (pick & (m > -jnp.inf))
        g = jnp.where(pick, -jnp.inf, g)
    return jnp.where(sel, 0.0, NEG)


def _moba_prep_kernel(a_ref, qw_ref, kw_ref, bd_ref, qaugt_ref, kaug_ref, vaugt_ref, kmean_scr):
    i = pl.program_id(1)

    @pl.when(i == 0)
    def _():
        kmean_scr[...] = jnp.zeros_like(kmean_scr)

    a = a_ref[...].astype(f32)
    q, k, v = a[:, :MIX_W], a[:, MIX_W:2 * MIX_W], a[:, 2 * MIX_W:]
    bd = bd_ref[...]
    qn = q * lax.rsqrt(_sel_right1(q * q, bd) + RMS_EPS) * qw_ref[...]
    kn = k * lax.rsqrt(_sel_right1(k * k, bd) + RMS_EPS) * kw_ref[...]
    lane = lax.broadcasted_iota(jnp.int32, (MOBA_BLOCK, LANES), 1)
    onehot = jnp.where(lane == i, 1.0, 0.0).astype(bf16)
    ones_row = jnp.where(lax.broadcasted_iota(jnp.int32, (VT_ROWS - HEAD_DIM, MOBA_BLOCK), 0) == 0, 1.0, 0.0)
    kmean = kmean_scr[...]
    nbp = kmean.shape[0]
    dim = lax.broadcasted_iota(jnp.int32, (LANES, MOBA_BLOCK), 0)
    sel_pad = jnp.zeros((LANES - nbp, MOBA_BLOCK), f32)
    for p in range(N_HEADS // 2):
        sl = slice(p * LANES, (p + 1) * LANES)
        kaug_ref[0, p] = jnp.concatenate([kn[:, sl].astype(bf16), onehot], axis=-1)
        vt = v[:, sl].T
        qt = (qn[:, sl] * (HEAD_DIM ** -0.5 * LOG2E)).T
        for hh in range(2):
            vaugt_ref[0, 2 * p + hh] = jnp.concatenate(
                [vt[hh * HEAD_DIM:(hh + 1) * HEAD_DIM], ones_row], axis=0).astype(bf16)
            keep = (dim < HEAD_DIM) if hh == 0 else (dim >= HEAD_DIM)
            q2t = jnp.where(keep, qt, 0.0)
            gate_t = _mm_hi(kmean[:, sl], q2t)
            qaugt_ref[0, 2 * p + hh] = jnp.concatenate([q2t, _top3_bias(gate_t, i), sel_pad], axis=0).astype(bf16)
    kmean_scr[pl.ds(i, 1), :] = jnp.mean(kn, axis=0, keepdims=True)


def _moba_prep(proj, bsz, seq, qw, kw, bd_mean):
    nb = seq // MOBA_BLOCK
    nbp = -(-nb // 8) * 8
    assert nbp <= LANES
    return pl.pallas_call(
        _moba_prep_kernel, name="moba_prep",
        grid=(bsz, nb),
        in_specs=[pl.BlockSpec((MOBA_BLOCK, 3 * MIX_W), lambda b, i: (b * nb + i, OFF_A // (3 * MIX_W))),
                  pl.BlockSpec((1, MIX_W), lambda b, i: (0, 0)),
                  pl.BlockSpec((1, MIX_W), lambda b, i: (0, 0)),
                  pl.BlockSpec((MIX_W, MIX_W), lambda b, i: (0, 0))],
        out_specs=[pl.BlockSpec((1, N_HEADS, 2 * LANES, MOBA_BLOCK), lambda b, i: (b, 0, 0, i)),
                   pl.BlockSpec((1, N_HEADS // 2, MOBA_BLOCK, 2 * LANES), lambda b, i: (b, 0, i, 0)),
                   pl.BlockSpec((1, N_HEADS, VT_ROWS, MOBA_BLOCK), lambda b, i: (b, 0, 0, i))],
        out_shape=[jax.ShapeDtypeStruct((bsz, N_HEADS, 2 * LANES, seq), bf16),
                   jax.ShapeDtypeStruct((bsz, N_HEADS // 2, seq, 2 * LANES), bf16),
                   jax.ShapeDtypeStruct((bsz, N_HEADS, VT_ROWS, seq), bf16)],
        scratch_shapes=[pltpu.VMEM((nbp, MIX_W), f32)],
        compiler_params=_cparams(("parallel", "arbitrary"), 32),
    )(proj, qw, kw, bd_mean)


def _moba_attn_kernel(qaugt_ref, kaug_ref, vaugt_ref, tabt_ref, o_ref, sa_scr, sb_scr, acc_scr):
    i = pl.program_id(2)
    blk = MOBA_BLOCK
    pairs = range(MOBA_PAIRS_PER_STEP)
    hs = range(2 * MOBA_PAIRS_PER_STEP)
    key, qry = _iota2((blk, blk))
    mm = lambda a, b: jnp.dot(a, b, preferred_element_type=f32)
    cmax = lambda s: jnp.max(s, axis=0, keepdims=True)
    pv = lambda v, pe: jnp.dot(v, pe.astype(bf16), preferred_element_type=f32)
    kblk = lambda pp, j: kaug_ref[0, pp, pl.ds(pl.multiple_of(j * blk, blk), blk), :]
    vblk = lambda hh, j: vaugt_ref[0, hh, :, pl.ds(pl.multiple_of(j * blk, blk), blk)]

    n_far = jnp.maximum(i - 1, 0)
    nk = MOBA_KV_PER_GROUP
    n_groups = (n_far + nk - 1) // nk
    blocks_of = lambda g: [jnp.where(nk * g + a < n_far, nk * g + a, i) for a in range(nk)]

    def scores(g, buf):
        qt = [qaugt_ref[0, hh] for hh in hs]
        for a, j in enumerate(blocks_of(g)):
            ks = [kblk(pp, j) for pp in pairs]
            for hh in hs:
                buf[hh, a] = mm(ks[hh // 2], qt[hh])

    scores(0, sa_scr)

    k_own = [kblk(pp, i) for pp in pairs]
    k_adj = [kblk(pp, n_far) for pp in pairs]
    qt = [qaugt_ref[0, hh] for hh in hs]
    s_own = [jnp.where(qry >= key, mm(k_own[hh // 2][:, :LANES], qt[hh][:LANES]) + tabt_ref[hh // 2, hh % 2, 1], NEG)
             for hh in hs]
    s_adj = [mm(k_adj[hh // 2], qt[hh]) + tabt_ref[hh // 2, hh % 2, 0] for hh in hs]
    m = [jnp.maximum(cmax(s_own[hh]), cmax(s_adj[hh])) for hh in hs]
    for hh in hs:
        acc_scr[hh] = (pv(vblk(hh, i), jnp.exp2(s_own[hh] - m[hh]))
                       + pv(vblk(hh, n_far), jnp.exp2(s_adj[hh] - m[hh])))

    def consume(g, buf, m):
        js = blocks_of(g)
        m_new = []
        for hh in hs:
            s = [buf[hh, a] for a in range(nk)]
            mh = functools.reduce(jnp.maximum, [cmax(x) for x in s], m[hh])
            ah = jnp.exp2(m[hh] - mh) * acc_scr[hh]
            for a in range(nk):
                ah = ah + pv(vblk(hh, js[a]), jnp.exp2(s[a] - mh))
            acc_scr[hh] = ah
            m_new.append(mh)
        return tuple(m_new)

    def body(u, m):
        scores(2 * u + 1, sb_scr)
        m = consume(2 * u, sa_scr, m)
        scores(2 * u + 2, sa_scr)
        return consume(2 * u + 1, sb_scr, m)

    m = lax.fori_loop(0, n_groups // 2, body, tuple(m))
    lax.cond(n_groups % 2 == 1, lambda mm_: consume(n_groups - 1, sa_scr, mm_), lambda mm_: mm_, m)
    o_ref[0] = jnp.concatenate(
        [(acc_scr[hh][:HEAD_DIM] / acc_scr[hh][HEAD_DIM:HEAD_DIM + 1]).T for hh in hs], axis=-1)


def _moba_attn(qaugt, kaug, vaugt, tabt):
    bsz, _, _, seq = qaugt.shape
    nb = seq // MOBA_BLOCK
    pp = MOBA_PAIRS_PER_STEP
    return pl.pallas_call(
        _moba_attn_kernel, name="moba_attn",
        grid=(bsz, N_HEADS // (2 * pp), nb),
        in_specs=[pl.BlockSpec((1, 2 * pp, 2 * LANES, MOBA_BLOCK), lambda b, p, i: (b, p, 0, i)),
                  pl.BlockSpec((1, pp, seq, 2 * LANES), lambda b, p, i: (b, p, 0, 0), pipeline_mode=pl.Buffered(1)),
                  pl.BlockSpec((1, 2 * pp, VT_ROWS, seq), lambda b, p, i: (b, p, 0, 0), pipeline_mode=pl.Buffered(1)),
                  pl.BlockSpec((pp, 2, 2, MOBA_BLOCK, MOBA_BLOCK), lambda b, p, i: (p, 0, 0, 0, 0),
                               pipeline_mode=pl.Buffered(1))],
        out_specs=pl.BlockSpec((1, MOBA_BLOCK, pp * LANES), lambda b, p, i: (b, i, p)),
        out_shape=jax.ShapeDtypeStruct((bsz, seq, MIX_W), f32),
        scratch_shapes=[pltpu.VMEM((2 * pp, MOBA_KV_PER_GROUP, MOBA_BLOCK, MOBA_BLOCK), f32)] * 2
        + [pltpu.VMEM((2 * pp, VT_ROWS, MOBA_BLOCK), f32)],
        compiler_params=_cparams(("parallel", "parallel", "arbitrary"), 56),
    )(qaugt, kaug, vaugt, tabt)


def _t5_bucket(dist):
    n = jnp.maximum(dist, 0)
    max_exact = REL_BUCKETS // 2
    nf = jnp.maximum(n, max_exact).astype(f32)
    large = max_exact + (jnp.log(nf / max_exact) / math.log(REL_MAX_DIST / max_exact)
                         * (REL_BUCKETS - max_exact)).astype(jnp.int32)
    large = jnp.minimum(large, REL_BUCKETS - 1)
    return jnp.where(n < max_exact, n, large)


def _moba_bias_kernel(vec_ref, o_ref):
    blk = MOBA_BLOCK
    t = pltpu.roll(jnp.broadcast_to(vec_ref[0] * LOG2E, (blk, 2 * blk)), 0, 1, stride=1, stride_axis=0)
    o_ref[0, 0] = t[:, blk:]
    o_ref[0, 1] = t[:, :blk]


def _moba_bias_tables(rel_bias):
    assert MOBA_BLOCK >= REL_MAX_DIST
    by_dist = rel_bias.astype(f32)[_t5_bucket(jnp.arange(2 * MOBA_BLOCK))]
    far = rel_bias.astype(f32)[_t5_bucket(jnp.array(2 * MOBA_BLOCK))]
    vec = (by_dist - far).T.reshape(N_HEADS, 1, 2 * MOBA_BLOCK)
    tab = pl.pallas_call(
        _moba_bias_kernel, name="moba_bias",
        grid=(N_HEADS,),
        in_specs=[pl.BlockSpec((1, 1, 2 * MOBA_BLOCK), lambda h: (h, 0, 0))],
        out_specs=pl.BlockSpec((1, 2, MOBA_BLOCK, MOBA_BLOCK), lambda h: (h, 0, 0, 0)),
        out_shape=jax.ShapeDtypeStruct((N_HEADS, 2, MOBA_BLOCK, MOBA_BLOCK), f32),
        compiler_params=_cparams(("parallel",), 16),
    )(vec)
    return tab.reshape(N_HEADS // 2, 2, 2, MOBA_BLOCK, MOBA_BLOCK)


def _causal_conv(x, halo, w_ref):
    ts, nh = x.shape[0], halo.shape[0]
    xe = jnp.concatenate([halo, x], axis=0)
    acc = x * w_ref[CONV_K - 1:CONV_K, :]
    for d in range(1, CONV_K):
        acc = acc + xe[nh - d:nh - d + ts] * w_ref[CONV_K - 1 - d:CONV_K - d, :]
    return acc


def _tile_and_halo(x_ref, halo_ref):
    halo = jnp.where(pl.program_id(1) == 0, 0.0, halo_ref[...].astype(f32))
    return x_ref[...].astype(f32), halo


def _tok_spec(rows, nsteps):
    return pl.BlockSpec((rows, MIX_W), lambda b, i: (b * nsteps + i, 0))


def _pair_masks(rows):
    lane = lax.broadcasted_iota(jnp.int32, (rows, LANES), 1)
    return lane < HEAD_DIM, lane >= HEAD_DIM


def _chunk_sum_matrix(chunk):
    r = jnp.arange(CUMSUM_TILE)[:, None]
    c = jnp.arange(CUMSUM_TILE)[None, :]
    same = (r // chunk) == (c // chunk)
    return jnp.concatenate([same & (r >= c), same], axis=0).astype(bf16)


def _chunk_sums(cm_ref, x):
    parts = [_sel_left(cm_ref[...], x[r:r + CUMSUM_TILE]) for r in range(0, x.shape[0], CUMSUM_TILE)]
    return (jnp.concatenate([p[:CUMSUM_TILE] for p in parts], axis=0),
            jnp.concatenate([p[CUMSUM_TILE:] for p in parts], axis=0))


def _row_spec(width, off, nt):
    return pl.BlockSpec((ROW_TILE, width), lambda b, i: (b * nt + i, off // width))


def _halo_spec(width, off, nt):
    per = ROW_TILE // HALO_ROWS
    return pl.BlockSpec((HALO_ROWS, width), lambda b, i: (jnp.maximum((b * nt + i) * per - 1, 0), off // width))


def _small_spec(nt):
    return pl.BlockSpec((ROW_TILE, LANES), lambda b, i: (b * nt + i, 0))


def _const_spec(shape):
    return pl.BlockSpec(shape, lambda b, i: (0,) * len(shape))


def _gdn_prep_kernel(x_ref, halo_ref, sm_ref, cw_ref, alog_ref, dtb_ref, bd_ref, eb_ref, ea_ref, cm_ref,
                     q_ref, k_ref, kb_ref, vb_ref, qd_ref, kbe_ref, kd_ref, gc_ref):
    qkv = _silu(_causal_conv(*_tile_and_halo(x_ref, halo_ref), cw_ref))
    q, k, v = qkv[:, :MIX_W], qkv[:, MIX_W:2 * MIX_W], qkv[:, 2 * MIX_W:]
    bd = bd_ref[...]
    q = q * lax.rsqrt(_sel_right1(q * q, bd) + L2_EPS) * HEAD_DIM ** -0.5
    k = k * lax.rsqrt(_sel_right1(k * k, bd) + L2_EPS)
    sm = sm_ref[...]
    beta = _sel_right1(_sigmoid(sm), eb_ref[...])
    g = -jnp.exp(alog_ref[...]) * _softplus(sm + dtb_ref[...])
    gc, g_end = _chunk_sums(cm_ref, g)
    eg = jnp.exp(_sel_right(gc, ea_ref[...]))
    e_rest = jnp.exp(_sel_right(g_end - gc, ea_ref[...]))
    kb = k * beta
    for ref, val in ((q_ref, q), (k_ref, k), (kb_ref, kb), (vb_ref, v * beta),
                     (qd_ref, q * eg), (kbe_ref, kb * eg), (kd_ref, k * e_rest)):
        ref[...] = val.astype(bf16)
    gc_ref[...] = gc[:, SM_BA:SM_BA + N_HEADS]


def _gdn_prep(proj, small, bsz, seq, conv_w, alog128, dtb128, bd_ones, e_beta, e_ba, cm):
    nt = seq // ROW_TILE
    w3 = 3 * MIX_W
    return pl.pallas_call(
        _gdn_prep_kernel, name="gdn_prep",
        grid=(bsz, nt),
        in_specs=[_row_spec(w3, OFF_BQKV, nt), _halo_spec(w3, OFF_BQKV, nt), _small_spec(nt),
                  _const_spec((CONV_K, w3)), _const_spec((1, LANES)), _const_spec((1, LANES)),
                  _const_spec((MIX_W, MIX_W)), _const_spec((LANES, MIX_W)), _const_spec((LANES, MIX_W)),
                  _const_spec((2 * CUMSUM_TILE, CUMSUM_TILE))],
        out_specs=[_tok_spec(ROW_TILE, nt)] * 7 + [pl.BlockSpec((ROW_TILE, N_HEADS), lambda b, i: (b * nt + i, 0))],
        out_shape=[jax.ShapeDtypeStruct((bsz * seq, MIX_W), bf16)] * 7 + [jax.ShapeDtypeStruct((bsz * seq, N_HEADS), f32)],
        compiler_params=_cparams(("parallel", "parallel"), 40),
    )(proj, proj, small, conv_w, alog128, dtb128, bd_ones, e_beta, e_ba, cm)


def _gdn_chunk_kernel(q_ref, k_ref, kb_ref, vb_ref, qd_ref, kbe_ref, kd_ref, gc_ref, gct_ref, nw_ref, bdm_ref,
                      o_ref, st_scr):
    @pl.when(pl.program_id(1) == 0)
    def _():
        st_scr[...] = jnp.zeros_like(st_scr)

    n = GDN_CHUNK
    cis, prs = range(CHUNKS_PER_STEP), range(N_HEADS // 2)
    tiles = [(ci, p) for ci in cis for p in prs]
    ch = [(ci, p, hh) for ci, p in tiles for hh in (0, 1)]
    blk = lambda ref, ci, p: ref[ci * n:(ci + 1) * n, p * LANES:(p + 1) * LANES]
    r, c = _iota2((n, n))
    row2, col2 = _iota2((2 * n, 2 * n))
    same_head = (row2 < n) == (col2 < n)
    hm2 = _pair_masks(2 * n)
    keep0 = _pair_masks(n)[0]
    pick = lambda x0, x1: jnp.where(keep0, x0, x1)
    zero = jnp.zeros((), bf16)
    gc_all = [gc_ref[0, ci] for ci in cis]
    gct_all = [gct_ref[0, ci] for ci in cis]
    gcol = lambda ci, p, hh: gc_all[ci][:, 2 * p + hh:2 * p + hh + 1]
    decay = {(ci, p, hh): jnp.exp(jnp.where(r >= c, gcol(ci, p, hh) - gct_all[ci][2 * p + hh:2 * p + hh + 1, :], NEG))
             for ci, p, hh in ch}
    lhs = {x: jnp.concatenate([blk(kb_ref, *x), blk(q_ref, *x)], axis=0) for x in tiles}
    gram = {(ci, p, hh): _mm_nt(jnp.where(hm2[hh], lhs[ci, p], zero), blk(k_ref, ci, p)) for ci, p, hh in ch}
    t = dict(zip(ch, _unit_lower_inverses([jnp.where(r > c, gram[x][:n] * decay[x], 0.0) for x in ch])))
    u = {(ci, p): pick(_mm(t[ci, p, 0], blk(vb_ref, ci, p)), _mm(t[ci, p, 1], blk(vb_ref, ci, p))) for ci, p in tiles}
    w = {(ci, p): pick(_mm(t[ci, p, 0], blk(kbe_ref, ci, p)), _mm(t[ci, p, 1], blk(kbe_ref, ci, p)))
         for ci, p in tiles}
    a_in = {x: gram[x][n:] * decay[x] for x in ch}
    st = [st_scr[p] for p in prs]
    for ci in cis:
        ws = [_mm(jnp.concatenate([w[ci, p].astype(bf16), blk(qd_ref, ci, p)], axis=0), st[p]) for p in prs]
        v_new = [u[ci, p] - ws[p][:n] for p in prs]
        o = [ws[p][n:] + pick(_mm(a_in[ci, p, 0], v_new[p]), _mm(a_in[ci, p, 1], v_new[p])) for p in prs]
        upd = [jnp.where(same_head, _mm_tn(blk(kd_ref, ci, p), v_new[p]), 0.0) for p in prs]
        for p in prs:
            ms = _sel_right1(o[p] * o[p], bdm_ref[...])
            o_ref[ci * n:(ci + 1) * n, p * LANES:(p + 1) * LANES] = o[p] * lax.rsqrt(ms + RMS_EPS) * nw_ref[...]
        g_end = [jnp.exp(jnp.where(keep0[0:1], gcol(ci, p, 0)[n - 1:n], gcol(ci, p, 1)[n - 1:n])) for p in prs]
        st = [st[p] * g_end[p] + upd[p] for p in prs]
    for p in prs:
        st_scr[p] = st[p]


def _gdn_chunk(bsz, seq, q, k, kb, vb, qd, kbe, kd, gc, norm_w2, bd_mean2):
    n = GDN_CHUNK
    nc = seq // n
    per = CHUNKS_PER_STEP
    gc4 = gc.reshape(bsz, nc, n, N_HEADS)
    gct4 = jnp.swapaxes(gc4, 2, 3)
    spec = _tok_spec(per * n, nc // per)
    return pl.pallas_call(
        _gdn_chunk_kernel, name="gdn_chunk",
        grid=(bsz, nc // per),
        in_specs=[spec] * 7 + [pl.BlockSpec((1, per, n, N_HEADS), lambda b, i: (b, i, 0, 0)),
                               pl.BlockSpec((1, per, N_HEADS, n), lambda b, i: (b, i, 0, 0)),
                               _const_spec((1, LANES)), _const_spec((LANES, LANES))],
        out_specs=spec,
        out_shape=jax.ShapeDtypeStruct((bsz * seq, MIX_W), f32),
        scratch_shapes=[pltpu.VMEM((N_HEADS // 2, 2 * HEAD_DIM, 2 * HEAD_DIM), f32)],
        compiler_params=_cparams(("parallel", "arbitrary"), 32),
    )(q, k, kb, vb, qd, kbe, kd, gc4, gct4, norm_w2, bd_mean2)


def _rwkv_prep_kernel(has_vres, *refs):
    if has_vres:
        (c_ref, halo_ref, mu_ref, w0_ref, wup_ref, a0_ref, aup_ref, gup_ref, kk_ref, ka_ref, bd_ref,
         rk_ref, cm_ref, vf_ref, v0_ref, vdn_ref, vup_ref,
         rt_ref, at_ref, bt_ref, kt_ref, v_ref, pe_ref, gout_ref, bonus_ref) = refs
    else:
        (c_ref, halo_ref, mu_ref, w0_ref, wup_ref, a0_ref, aup_ref, gup_ref, kk_ref, ka_ref, bd_ref,
         rk_ref, cm_ref,
         rt_ref, at_ref, bt_ref, kt_ref, v_ref, pe_ref, gout_ref, bonus_ref, cv_ref) = refs
    c, halo = _tile_and_halo(c_ref, halo_ref)
    prev = jnp.concatenate([halo[HALO_ROWS - 1:], c[:-1]], axis=0)
    c = c + (prev - c) * mu_ref[...]
    c_r, c_k, c_v = c[:, :MIX_W], c[:, MIX_W:2 * MIX_W], c[:, 2 * MIX_W:3 * MIX_W]
    c_wd = c[:, 3 * MIX_W:3 * MIX_W + 64]
    c_ad = c[:, 3 * MIX_W + 64:3 * MIX_W + 128]
    c_gd = c[:, 3 * MIX_W + 128:]
    w_log = -_softplus(-(w0_ref[...] + _mm(jnp.tanh(c_wd), wup_ref[...]))) - 0.5
    a_in = _sigmoid(a0_ref[...] + _mm(c_ad, aup_ref[...]))
    gout_ref[...] = _mm(_sigmoid(c_gd), gup_ref[...])
    if has_vres:
        lam = _sigmoid(v0_ref[...] + _mm(_mm(c_v, vdn_ref[...]), vup_ref[...]))
        v_r = c_v + (vf_ref[...] - c_v) * lam
    else:
        v_r = c_v
        cv_ref[...] = c_v
    bd = bd_ref[...]
    kk = c_k * kk_ref[...]
    kk = kk * lax.rsqrt(_sel_right1(kk * kk, bd) + L2_EPS)
    k_r = c_k * (1.0 + (a_in - 1.0) * ka_ref[...])
    b = kk * a_in
    bonus_ref[...] = _sel_right1(c_r * k_r * rk_ref[...], bd) * v_r
    lc, lc_end = _chunk_sums(cm_ref, -jnp.exp(w_log))
    e_neg = jnp.exp(-lc)
    rt_ref[...] = (c_r * jnp.exp(lc)).astype(bf16)
    at_ref[...] = (-kk * jnp.exp(lc + jnp.exp(w_log))).astype(bf16)
    bt_ref[...] = (b * e_neg).astype(bf16)
    kt_ref[...] = (k_r * e_neg).astype(bf16)
    v_ref[...] = v_r.astype(bf16)
    pe_ref[...] = jnp.exp(lc_end)


def _rwkv_prep(proj, bsz, seq, mu, w0, w_up, a0, a_up, g_up, k_k, k_a, r_k, bd_ones, cm, vres):
    nt = seq // ROW_TILE
    wc = 3 * MIX_W + 256
    std = pl.BlockSpec((ROW_TILE, MIX_W), lambda b, i: (b * nt + i, 0))
    std_shape = jax.ShapeDtypeStruct((bsz * seq, MIX_W), f32)
    in_specs = [_row_spec(wc, OFF_C, nt), _halo_spec(wc, OFF_C, nt), _const_spec((1, wc)),
                _const_spec((1, MIX_W)), _const_spec((64, MIX_W)), _const_spec((1, MIX_W)), _const_spec((64, MIX_W)),
                _const_spec((128, MIX_W)), _const_spec((1, MIX_W)), _const_spec((1, MIX_W)), _const_spec((MIX_W, MIX_W)),
                _const_spec((1, MIX_W)), _const_spec((2 * CUMSUM_TILE, CUMSUM_TILE))]
    args = [proj, proj, mu, w0, w_up, a0, a_up, g_up, k_k, k_a, bd_ones, r_k, cm]
    out_specs = [std] * 8
    out_shape = [jax.ShapeDtypeStruct((bsz * seq, MIX_W), bf16)] * 5 + [std_shape] * 3
    if vres is not None:
        v_first, v0, v_down, v_up = vres
        in_specs += [std, _const_spec((1, MIX_W)), _const_spec(v_down.shape), _const_spec(v_up.shape)]
        args += [v_first, v0, v_down, v_up]
    else:
        out_specs.append(std)
        out_shape.append(std_shape)
    return pl.pallas_call(
        functools.partial(_rwkv_prep_kernel, vres is not None), name="rwkv_prep",
        grid=(bsz, nt), in_specs=in_specs, out_specs=out_specs, out_shape=out_shape,
        compiler_params=_cparams(("parallel", "parallel"), 40),
    )(*args)


def _rwkv_chunk_kernel(rt_ref, at_ref, bt_ref, kt_ref, v_ref, pe_ref, o_ref, st_scr):
    @pl.when(pl.program_id(1) == 0)
    def _():
        st_scr[...] = jnp.zeros_like(st_scr)

    n = RWKV_CHUNK
    cis, prs = range(CHUNKS_PER_STEP), range(N_HEADS // 2)
    tiles = [(ci, p) for ci in cis for p in prs]
    ch = [(ci, p, hh) for ci, p in tiles for hh in (0, 1)]
    blk = lambda ref, ci, p: ref[ci * n:(ci + 1) * n, p * LANES:(p + 1) * LANES]
    row, col = _iota2((2 * n, 2 * n))
    rr, cc = row & (n - 1), col & (n - 1)
    mask = rr + jnp.where(row < n, 0, 1) > cc
    same_head = (row < n) == (col < n)
    hm2 = _pair_masks(2 * n)
    keep0 = _pair_masks(n)[0]
    pick = lambda x0, x1: jnp.where(keep0, x0, x1)
    zero = jnp.zeros((), bf16)
    lhs = {x: jnp.concatenate([blk(at_ref, *x), blk(rt_ref, *x)], axis=0) for x in tiles}
    rhs = {x: jnp.concatenate([blk(bt_ref, *x), blk(kt_ref, *x)], axis=0) for x in tiles}
    gm = {(ci, p, hh): jnp.where(mask, _mm_nt(jnp.where(hm2[hh], lhs[ci, p], zero), rhs[ci, p]), 0.0)
          for ci, p, hh in ch}
    t = dict(zip(ch, _unit_lower_inverses([-gm[x][:n, :n] for x in ch])))
    v = {x: blk(v_ref, *x) for x in tiles}
    zv = {x: jnp.concatenate([jnp.zeros_like(v[x]), v[x]], axis=0) for x in tiles}
    makv = {(ci, p): pick(_mm(gm[ci, p, 0][:n], zv[ci, p]), _mm(gm[ci, p, 1][:n], zv[ci, p])) for ci, p in tiles}
    st = [st_scr[p] for p in prs]
    for ci in cis:
        ah = [_mm_nt(lhs[ci, p], st[p]) for p in prs]
        rhs_u = [ah[p][:n] + makv[ci, p] for p in prs]
        u = [pick(_mm(t[ci, p, 0], rhs_u[p]), _mm(t[ci, p, 1], rhs_u[p])) for p in prs]
        uv = [jnp.concatenate([u[p].astype(bf16), v[ci, p]], axis=0) for p in prs]
        o = [ah[p][n:] + pick(_mm(gm[ci, p, 0][n:], uv[p]), _mm(gm[ci, p, 1][n:], uv[p])) for p in prs]
        pe = [pe_ref[ci * n:ci * n + 1, p * LANES:(p + 1) * LANES] for p in prs]
        upd = [_mm_tn(uv[p], rhs[ci, p].astype(f32) * pe[p]) for p in prs]
        for p in prs:
            o_ref[ci * n:(ci + 1) * n, p * LANES:(p + 1) * LANES] = o[p]
        st = [st[p] * pe[p] + jnp.where(same_head, upd[p], 0.0) for p in prs]
    for p in prs:
        st_scr[p] = st[p]


def _rwkv_chunk(bsz, seq, rt, at, bt, kt, v, pe):
    n = RWKV_CHUNK * CHUNKS_PER_STEP
    spec = _tok_spec(n, seq // n)
    return pl.pallas_call(
        _rwkv_chunk_kernel, name="rwkv_chunk",
        grid=(bsz, seq // n),
        in_specs=[spec] * 6,
        out_specs=spec,
        out_shape=jax.ShapeDtypeStruct((bsz * seq, MIX_W), f32),
        scratch_shapes=[pltpu.VMEM((N_HEADS // 2, 2 * HEAD_DIM, 2 * HEAD_DIM), f32)],
        compiler_params=_cparams(("parallel", "arbitrary"), 32),
    )(rt, at, bt, kt, v, pe)


def _ssd_prep_kernel(x_ref, halo_ref, sm_ref, cw_ref, cb_ref, alog_ref, dtb_ref, edt_ref, cm_ref,
                     xdt_ref, x_out_ref, bc_ref, acs_ref):
    xbc = _silu(_causal_conv(*_tile_and_halo(x_ref, halo_ref), cw_ref) + cb_ref[...])
    m_x = xbc[:, :MIX_W]
    dt = _softplus(sm_ref[...] + dtb_ref[...])
    xdt_ref[...] = (m_x * _sel_right1(dt, edt_ref[...])).astype(xdt_ref.dtype)
    x_out_ref[...] = m_x
    bc_ref[...] = xbc[:, MIX_W:].astype(bc_ref.dtype)
    acs, _ = _chunk_sums(cm_ref, dt * -jnp.exp(alog_ref[...]))
    acs_ref[...] = acs[:, SM_DT:SM_DT + N_HEADS]


def _ssd_prep(proj, small, bsz, seq, conv_w, conv_b, alog128, dtb128, e_dt, cm):
    nt = seq // ROW_TILE
    wx = MIX_W + 4 * SSM_STATE
    return pl.pallas_call(
        _ssd_prep_kernel, name="ssd_prep",
        grid=(bsz, nt),
        in_specs=[_row_spec(wx, OFF_DXBC, nt), _halo_spec(wx, OFF_DXBC, nt), _small_spec(nt),
                  _const_spec((CONV_K, wx)), _const_spec((1, wx)), _const_spec((1, LANES)), _const_spec((1, LANES)),
                  _const_spec((LANES, MIX_W)), _const_spec((2 * CUMSUM_TILE, CUMSUM_TILE))],
        out_specs=[_tok_spec(ROW_TILE, nt)] * 2 + [pl.BlockSpec((ROW_TILE, 4 * SSM_STATE), lambda b, i: (b * nt + i, 0)),
                                                   pl.BlockSpec((ROW_TILE, N_HEADS), lambda b, i: (b * nt + i, 0))],
        out_shape=[jax.ShapeDtypeStruct((bsz * seq, MIX_W), bf16), jax.ShapeDtypeStruct((bsz * seq, MIX_W), f32),
                   jax.ShapeDtypeStruct((bsz * seq, 4 * SSM_STATE), bf16),
                   jax.ShapeDtypeStruct((bsz * seq, N_HEADS), f32)],
        compiler_params=_cparams(("parallel", "parallel"), 40),
    )(proj, proj, small, conv_w, conv_b, alog128, dtb128, e_dt, cm)


def _ssd_chunk_kernel(xdt_ref, x_ref, bc_ref, a_ref, at_ref, dvec_ref, o_ref, st_scr):
    @pl.when(pl.program_id(1) == 0)
    def _():
        st_scr[...] = jnp.zeros_like(st_scr)

    n = SSD_CHUNK
    cis, prs = range(SSD_CHUNKS_PER_STEP), range(N_HEADS // 2)
    grp = lambda p: (2 * p) // (N_HEADS // 2)
    tiles = [(ci, p) for ci in cis for p in prs]
    ch = [(ci, p, hh) for ci, p in tiles for hh in (0, 1)]
    rows = lambda ci: slice(ci * n, (ci + 1) * n)
    lanes = lambda p: slice(p * LANES, (p + 1) * LANES)
    r, c = _iota2((n, n))
    keep0 = _pair_masks(n)[0]
    pick = lambda x0, x1: jnp.where(keep0, x0, x1)
    b_g = {(ci, g): bc_ref[rows(ci), g * SSM_STATE:(g + 1) * SSM_STATE] for ci in cis for g in range(2)}
    c_g = {(ci, g): bc_ref[rows(ci), (2 + g) * SSM_STATE:(3 + g) * SSM_STATE] for ci in cis for g in range(2)}
    cb = {x: _mm_nt(c_g[x], b_g[x]) for x in b_g}
    col = lambda ci, p, hh: a_ref[0, ci][:, 2 * p + hh:2 * p + hh + 1]
    ac = {x: col(*x) for x in ch}
    a_last = {x: col(*x)[n - 1:n] for x in ch}
    lmat = {(ci, p, hh): jnp.exp(jnp.where(r >= c, ac[ci, p, hh] - at_ref[0, ci][2 * p + hh:2 * p + hh + 1, :], NEG))
            for ci, p, hh in ch}
    xg = {(ci, p): xdt_ref[rows(ci), lanes(p)] for ci, p in tiles}
    y_diag = {(ci, p): pick(*[_mm(cb[ci, grp(p)] * lmat[ci, p, hh], xg[ci, p]) for hh in (0, 1)]) for ci, p in tiles}
    upd = {(ci, p): pick(*[_mm_tn(b_g[ci, grp(p)].astype(f32) * jnp.exp(a_last[ci, p, hh] - ac[ci, p, hh]), xg[ci, p])
                           for hh in (0, 1)]) for ci, p in tiles}
    c_in = {(ci, p, hh): c_g[ci, grp(p)].astype(f32) * jnp.exp(ac[ci, p, hh]) for ci, p, hh in ch}
    st = [st_scr[p] for p in prs]
    for ci in cis:
        y_off = [pick(_mm(c_in[ci, p, 0], st[p]), _mm(c_in[ci, p, 1], st[p])) for p in prs]
        for p in prs:
            o_ref[rows(ci), lanes(p)] = y_diag[ci, p] + y_off[p] + x_ref[rows(ci), lanes(p)] * dvec_ref[:, lanes(p)]
        st = [st[p] * jnp.exp(jnp.where(keep0[0:1], a_last[ci, p, 0], a_last[ci, p, 1])) + upd[ci, p] for p in prs]
    for p in prs:
        st_scr[p] = st[p]


def _ssd_chunk(bsz, seq, xdt, x, bc, acs, dvec):
    n = SSD_CHUNK
    nc = seq // n
    per = SSD_CHUNKS_PER_STEP
    a4 = acs.reshape(bsz, nc, n, N_HEADS)
    at4 = jnp.swapaxes(a4, 2, 3)
    return pl.pallas_call(
        _ssd_chunk_kernel, name="ssd_chunk",
        grid=(bsz, nc // per),
        in_specs=[_tok_spec(per * n, nc // per)] * 2
        + [pl.BlockSpec((per * n, 4 * SSM_STATE), lambda b, i: (b * (nc // per) + i, 0)),
           pl.BlockSpec((1, per, n, N_HEADS), lambda b, i: (b, i, 0, 0)),
           pl.BlockSpec((1, per, N_HEADS, n), lambda b, i: (b, i, 0, 0)),
           _const_spec((1, MIX_W))],
        out_specs=_tok_spec(per * n, nc // per),
        out_shape=jax.ShapeDtypeStruct((bsz * seq, MIX_W), f32),
        scratch_shapes=[pltpu.VMEM((N_HEADS // 2, SSM_STATE, 2 * HEAD_DIM), f32)],
        compiler_params=_cparams(("parallel", "arbitrary"), 32),
    )(xdt, x, bc, a4, at4, dvec)


def _merge_kernel(x_ref, ya_ref, ob_ref, bz_ref, wkv_ref, bonus_ref, gout_ref, lnw_ref, lnb_ref, bdm_ref,
                  yd_ref, dz_ref, mnw_ref, g0_ref, g1_ref, g2_ref, g3_ref, wb_ref, wo_ref, o_ref):
    def gated(n, y, g_ref):
        return _sigmoid(g_ref[...].astype(f32)) * _mm(y, wb_ref[n])

    acc = gated(0, ya_ref[0], g0_ref)
    acc = acc + gated(1, ob_ref[...] * _silu(bz_ref[...].astype(f32)), g1_ref)

    w = wkv_ref[...]
    d = w - _sel_right1(w, bdm_ref[...])
    wkv_ln = d * lax.rsqrt(_sel_right1(d * d, bdm_ref[...]) + RWKV_LN_EPS)
    y_c = (wkv_ln * lnw_ref[...] + lnb_ref[...] + bonus_ref[...]) * gout_ref[...]
    acc = acc + gated(2, y_c, g2_ref)

    yz = yd_ref[...] * _silu(dz_ref[...].astype(f32))
    half = MIX_W // 2
    y_d = jnp.concatenate(
        [yz[:, s:s + half] * lax.rsqrt(jnp.mean(jnp.square(yz[:, s:s + half]), axis=-1, keepdims=True) + RMS_EPS)
         for s in (0, half)], axis=-1) * mnw_ref[...]
    acc = acc + gated(3, y_d, g3_ref)
    o_ref[...] = x_ref[...] + _mm(acc, wo_ref[...])


def _merge(x2, proj, bsz, seq, ya, ob, wkv, bonus, gout, lnw, lnb, bd_mean, yd, mnw, wb, wo):
    nt = seq // ROW_TILE
    std = lambda w: pl.BlockSpec((ROW_TILE, w), lambda b, i: (b * nt + i, 0))
    gate = lambda n: _row_spec(D_MODEL, OFF_GATES + n * D_MODEL, nt)
    return pl.pallas_call(
        _merge_kernel, name="merge",
        grid=(bsz, nt),
        in_specs=[std(D_MODEL), pl.BlockSpec((1, ROW_TILE, MIX_W), lambda b, i: (b, i, 0)),
                  std(MIX_W), _row_spec(MIX_W, OFF_BZ, nt),
                  std(MIX_W), std(MIX_W), std(MIX_W), _const_spec((1, MIX_W)), _const_spec((1, MIX_W)),
                  _const_spec((MIX_W, MIX_W)),
                  std(MIX_W), _row_spec(MIX_W, OFF_DZ, nt), _const_spec((1, MIX_W)),
                  gate(0), gate(1), gate(2), gate(3),
                  _const_spec((4, MIX_W, D_MODEL)), _const_spec((D_MODEL, D_MODEL))],
        out_specs=std(D_MODEL),
        out_shape=jax.ShapeDtypeStruct((bsz * seq, D_MODEL), f32),
        compiler_params=_cparams(("parallel", "parallel"), 48),
    )(x2, ya, ob, proj, wkv, bonus, gout, lnw, lnb, bd_mean, yd, proj, mnw, proj, proj, proj, proj, wb, wo)


def _lane_vec(vals, off):
    return jnp.zeros((1, LANES), f32).at[0, off:off + vals.shape[0]].set(vals)


def _head_expand(off):
    n = jnp.arange(LANES)[:, None]
    c = jnp.arange(MIX_W)[None, :]
    return (n - off == c // HEAD_DIM).astype(bf16)


def _pack_w_in(w):
    pad = lambda n: jnp.zeros((w.shape[0], n), w.dtype)
    cols = [w[:, W_A:W_BZ],
            w[:, W_BZ:W_BBETA],
            w[:, W_C:W_DZ],
            w[:, W_BBETA:W_C], w[:, W_DDT:W_GATES], pad(2 * LANES - 3 * N_HEADS),
            w[:, W_DZ:W_DXBC], w[:, W_DXBC:W_DDT], w[:, W_GATES:]]
    out = jnp.concatenate(cols, axis=1).astype(bf16)
    assert out.shape[1] == N_PROJ
    return out


def kernel(x, rel_bias, norm1_w, w_in, moba_q_norm, moba_k_norm, gdn_conv_w, gdn_A_log, gdn_dt_bias, gdn_norm_w, rwkv_mu, rwkv_w0, rwkv_w_up, rwkv_a0, rwkv_a_up, rwkv_g_up, rwkv_k_k, rwkv_k_a, rwkv_r_k, rwkv_v0, rwkv_v_down, rwkv_v_up, rwkv_ln_w, rwkv_ln_b, mamba_conv_w, mamba_conv_b, mamba_dt_bias, mamba_A_log, mamba_D, mamba_norm_w, w_branch, w_out, norm2_w, ffn_w_in, ffn_w_down):
    bsz, seq, d = x.shape
    depth = w_in.shape[0]
    assert d == D_MODEL and (bsz * seq) % IN_PROJ_TILE_M == 0 and seq % MM_TILE_M == 0
    x2 = x.reshape(bsz * seq, d)
    row = lambda v: v.reshape(1, -1).astype(f32)

    hid = jnp.arange(MIX_W) // HEAD_DIM
    bd_ones = (hid[:, None] == hid[None, :]).astype(bf16)
    bd_mean = (bd_ones.astype(f32) / HEAD_DIM).astype(bf16)
    e_beta, e_ba, e_dt = _head_expand(SM_BETA), _head_expand(SM_BA), _head_expand(SM_DT)
    cm64, cm128 = _chunk_sum_matrix(GDN_CHUNK), _chunk_sum_matrix(SSD_CHUNK)
    assert GDN_CHUNK == RWKV_CHUNK
    tab = _moba_bias_tables(rel_bias)
    v_first = None
    for i in range(depth):
        proj, small = _in_proj(x2, row(norm1_w[i]), _pack_w_in(w_in[i]))

        qaug, kaug, v_a = _moba_prep(proj, bsz, seq, row(jnp.tile(moba_q_norm[i], N_HEADS)),
                                     row(jnp.tile(moba_k_norm[i], N_HEADS)), bd_mean)
        y_a = _moba_attn(qaug, kaug, v_a, tab)

        gdn_in = _gdn_prep(proj, small, bsz, seq, gdn_conv_w[i], _lane_vec(gdn_A_log[i], SM_BA),
                           _lane_vec(gdn_dt_bias[i], SM_BA), bd_ones, e_beta, e_ba, cm64)
        o_b = _gdn_chunk(bsz, seq, *gdn_in, row(jnp.tile(gdn_norm_w[i], 2)), bd_mean[:LANES, :LANES])

        vres = None if i == 0 else (v_first, row(rwkv_v0[i - 1]), rwkv_v_down[i - 1].astype(bf16),
                                    rwkv_v_up[i - 1].astype(bf16))
        outs = _rwkv_prep(proj, bsz, seq, row(rwkv_mu[i]), row(rwkv_w0[i]), rwkv_w_up[i].astype(bf16),
                          row(rwkv_a0[i]), rwkv_a_up[i].astype(bf16), rwkv_g_up[i].astype(bf16),
                          row(rwkv_k_k[i]), row(rwkv_k_a[i]), row(rwkv_r_k[i]), bd_ones, cm64, vres)
        g_out, bonus = outs[6], outs[7]
        if i == 0:
            v_first = outs[8]
        wkv = _rwkv_chunk(bsz, seq, *outs[:6])

        xdt, x_d, bc, acs = _ssd_prep(proj, small, bsz, seq, mamba_conv_w[i], row(mamba_conv_b[i]),
                                      _lane_vec(mamba_A_log[i], SM_DT), _lane_vec(mamba_dt_bias[i], SM_DT), e_dt, cm128)
        y_d = _ssd_chunk(bsz, seq, xdt, x_d, bc, acs, row(jnp.repeat(mamba_D[i], HEAD_DIM)))

        x2 = _merge(x2, proj, bsz, seq, y_a, o_b, wkv, bonus, g_out,
                    row(rwkv_ln_w[i]), row(rwkv_ln_b[i]), bd_mean, y_d, row(mamba_norm_w[i]),
                    w_branch[i].astype(bf16), w_out[i].astype(bf16))

        x2 = _ffn(x2, row(norm2_w[i]), ffn_w_in[i].astype(bf16), ffn_w_down[i].astype(bf16))
    return x2.reshape(bsz, seq, d)
```

```python
import functools
import math

import jax
import jax.numpy as jnp
from jax import lax
from jax.experimental import pallas as pl
from jax.experimental.pallas import tpu as pltpu

f32, bf16 = jnp.float32, jnp.bfloat16
HI = lax.Precision.HIGHEST

D_MODEL = 1024
N_HEADS = 8
HEAD_DIM = 64
MIX_W = N_HEADS * HEAD_DIM
RMS_EPS = 1e-6
L2_EPS = 1e-6
CONV_K = 4
MOBA_BLOCK = 256
MOBA_TOPK = 3
MOBA_PAIRS_PER_STEP = 4
MOBA_KV_PER_GROUP = 2
REL_BUCKETS = 32
REL_MAX_DIST = 128
GDN_CHUNK = 64
RWKV_CHUNK = 64
CHUNKS_PER_STEP = 4
RWKV_LN_EPS = 64e-5
SSM_STATE = 128
SSD_CHUNK = 128
SSD_CHUNKS_PER_STEP = 4
FFN_HIDDEN_SPLITS = 2
NEG = -1e30
LOG2E = math.log2(math.e)

LANES = 128
BF16_SUBLANES = 16
VT_ROWS = HEAD_DIM + BF16_SUBLANES

OFF_A, OFF_BQKV, OFF_BZ, OFF_C, OFF_SMALL, OFF_DZ, OFF_DXBC, OFF_GATES = 0, 1536, 3072, 3584, 5376, 5632, 6144, 7168
N_PROJ = OFF_GATES + 4 * D_MODEL
SM_BETA, SM_BA, SM_DT = 0, 8, 16
W_A, W_BQKV, W_BZ, W_BBETA, W_BA, W_C, W_DZ, W_DXBC, W_DDT, W_GATES = 0, 1536, 3072, 3584, 3592, 3600, 5392, 5904, 6928, 6936

ROW_TILE = 512
CUMSUM_TILE = 256
MM_TILE_M = 512
IN_PROJ_TILE_M, IN_PROJ_TILE_N = 1024, 1408
HALO_ROWS = BF16_SUBLANES


def _cparams(sem, vmem_mb):
    return pltpu.CompilerParams(dimension_semantics=sem, vmem_limit_bytes=vmem_mb * 1024 * 1024)


def _mm(a, b):
    return jnp.dot(a.astype(bf16), b.astype(bf16), preferred_element_type=f32)


def _mm_nt(a, b):
    return lax.dot_general(a.astype(bf16), b.astype(bf16), (((1,), (1,)), ((), ())), preferred_element_type=f32)


def _mm_tn(a, b):
    return lax.dot_general(a.astype(bf16), b.astype(bf16), (((0,), (0,)), ((), ())), preferred_element_type=f32)


def _mm_hi(a, b):
    return jnp.dot(a, b, precision=HI, preferred_element_type=f32)


def _softplus(x):
    return jnp.maximum(x, 0.0) + jnp.log1p(jnp.exp(-jnp.abs(x)))


def _sigmoid(x):
    return jax.nn.sigmoid(x)


def _silu(x):
    return x * _sigmoid(x)


def _iota2(shape):
    return lax.broadcasted_iota(jnp.int32, shape, 0), lax.broadcasted_iota(jnp.int32, shape, 1)


def _split3(x):
    hi = x.astype(bf16)
    r1 = x - hi.astype(f32)
    mid = r1.astype(bf16)
    return hi, mid, (r1 - mid.astype(f32)).astype(bf16)


def _sel_left(m01, x):
    return sum(jnp.dot(m01, p, preferred_element_type=f32) for p in _split3(x))


def _sel_right(x, m01):
    return sum(jnp.dot(p, m01, preferred_element_type=f32) for p in _split3(x))


def _sel_right1(x, m01):
    return jnp.dot(x.astype(bf16), m01, preferred_element_type=f32)


def _unit_lower_inverses(l_list):
    n = l_list[0].shape[0]
    r, c = _iota2((n, n))
    eye = jnp.where(r == c, 1.0, 0.0)
    ts = [eye for _ in l_list]
    for ls in range(n.bit_length() - 1):
        m = ((r >> (ls + 1)) == (c >> (ls + 1))) & (((r >> ls) & 1) == 1) & (((c >> ls) & 1) == 0)
        lms = [jnp.where(m, l, 0.0) for l in l_list]
        if ls == 0:
            ts = [t - lm for t, lm in zip(ts, lms)]
        else:
            tl = [_mm(t, lm) for t, lm in zip(ts, lms)]
            ts = [t - _mm(x, t) for t, x in zip(ts, tl)]
    return ts


def _in_proj_kernel(x_ref, nw_ref, w_ref, o_ref, sm_ref, h_scr):
    j = pl.program_id(1)

    @pl.when(j == 0)
    def _():
        x = x_ref[...]
        y = x * lax.rsqrt(jnp.mean(x * x, axis=-1, keepdims=True) + RMS_EPS)
        h_scr[...] = (y * nw_ref[...]).astype(bf16)

    tn = o_ref.shape[1]
    w = w_ref[:, pl.ds(pl.multiple_of(j * tn, tn), tn)]
    acc = jnp.dot(h_scr[...], w, preferred_element_type=f32)
    o_ref[...] = acc.astype(o_ref.dtype)

    @pl.when(j == OFF_SMALL // tn)
    def _():
        sm_ref[...] = acc[:, OFF_SMALL % tn:OFF_SMALL % tn + LANES]


def _in_proj(x2, nw, w):
    t, d = x2.shape
    n = w.shape[1]
    tm, tn = IN_PROJ_TILE_M, IN_PROJ_TILE_N
    return pl.pallas_call(
        _in_proj_kernel, name="in_proj",
        grid=(t // tm, n // tn),
        in_specs=[pl.BlockSpec((tm, d), lambda i, j: (i, 0)),
                  pl.BlockSpec((1, d), lambda i, j: (0, 0)),
                  pl.BlockSpec((d, n), lambda i, j: (0, 0), pipeline_mode=pl.Buffered(1))],
        out_specs=[pl.BlockSpec((tm, tn), lambda i, j: (i, j)),
                   pl.BlockSpec((tm, LANES), lambda i, j: (i, 0))],
        out_shape=[jax.ShapeDtypeStruct((t, n), bf16), jax.ShapeDtypeStruct((t, LANES), f32)],
        scratch_shapes=[pltpu.VMEM((tm, d), bf16)],
        compiler_params=_cparams(("parallel", "arbitrary"), 48),
    )(x2, nw, w)


def _ffn_kernel(x_ref, nw_ref, wi_ref, wd_ref, o_ref):
    x = x_ref[...]
    h = (x * lax.rsqrt(jnp.mean(x * x, axis=-1, keepdims=True) + RMS_EPS) * nw_ref[...]).astype(bf16)
    n = wd_ref.shape[0]
    step = n // FFN_HIDDEN_SPLITS
    acc = x
    for c0 in range(0, n, step):
        g = jnp.dot(h, wi_ref[:, c0:c0 + step], preferred_element_type=f32)
        u = jnp.dot(h, wi_ref[:, n + c0:n + c0 + step], preferred_element_type=f32)
        acc = acc + jnp.dot((_silu(g) * u).astype(bf16), wd_ref[c0:c0 + step, :], preferred_element_type=f32)
    o_ref[...] = acc


def _ffn(x2, nw, wi, wd):
    t, d = x2.shape
    n = wd.shape[0]
    tm = MM_TILE_M
    resident = lambda shape: pl.BlockSpec(shape, lambda i: (0, 0), pipeline_mode=pl.Buffered(1))
    return pl.pallas_call(
        _ffn_kernel, name="ffn",
        grid=(t // tm,),
        in_specs=[pl.BlockSpec((tm, d), lambda i: (i, 0)),
                  pl.BlockSpec((1, d), lambda i: (0, 0)),
                  resident((d, 2 * n)), resident((n, d))],
        out_specs=pl.BlockSpec((tm, d), lambda i: (i, 0)),
        out_shape=jax.ShapeDtypeStruct((t, d), f32),
        compiler_params=_cparams(("parallel",), 48),
    )(x2, nw, wi, wd)


def _top3_bias(gate_t, n_past):
    row = lax.broadcasted_iota(jnp.int32, gate_t.shape, 0)
    g = jnp.where(row < n_past, gate_t, -jnp.inf)
    sel = jnp.zeros(gate_t.shape, jnp.bool_)
    for _ in range(MOBA_TOPK):
        m = jnp.max(g, axis=0, keepdims=True)
        idx = jnp.min(jnp.where(g == m, row, gate_t.shape[0]), axis=0, keepdims=True)
        pick = row == idx
        sel = sel | (pick & (m > -jnp.inf))
        g = jnp.where(pick, -jnp.inf, g)
    return jnp.where(sel, 0.0, NEG)


def _moba_prep_kernel(a_ref, qw_ref, kw_ref, bd_ref, qaugt_ref, kaug_ref, vaugt_ref, kmean_scr):
    i = pl.program_id(1)

    @pl.when(i == 0)
    def _():
        kmean_scr[...] = jnp.zeros_like(kmean_scr)

    a = a_ref[...].astype(f32)
    q, k, v = a[:, :MIX_W], a[:, MIX_W:2 * MIX_W], a[:, 2 * MIX_W:]
    bd = bd_ref[...]
    qn = q * lax.rsqrt(_sel_right1(q * q, bd) + RMS_EPS) * qw_ref[...]
    kn = k * lax.rsqrt(_sel_right1(k * k, bd) + RMS_EPS) * kw_ref[...]
    lane = lax.broadcasted_iota(jnp.int32, (MOBA_BLOCK, LANES), 1)
    onehot = jnp.where(lane == i, 1.0, 0.0).astype(bf16)
    ones_row = jnp.where(lax.broadcasted_iota(jnp.int32, (VT_ROWS - HEAD_DIM, MOBA_BLOCK), 0) == 0, 1.0, 0.0)
    kmean = kmean_scr[...]
    nbp = kmean.shape[0]
    dim = lax.broadcasted_iota(jnp.int32, (LANES, MOBA_BLOCK), 0)
    sel_pad = jnp.zeros((LANES - nbp, MOBA_BLOCK), f32)
    for p in range(N_HEADS // 2):
        sl = slice(p * LANES, (p + 1) * LANES)
        kaug_ref[0, p] = jnp.concatenate([kn[:, sl].astype(bf16), onehot], axis=-1)
        vt = v[:, sl].T
        qt = (qn[:, sl] * (HEAD_DIM ** -0.5 * LOG2E)).T
        for hh in range(2):
            vaugt_ref[0, 2 * p + hh] = jnp.concatenate(
                [vt[hh * HEAD_DIM:(hh + 1) * HEAD_DIM], ones_row], axis=0).astype(bf16)
            keep = (dim < HEAD_DIM) if hh == 0 else (dim >= HEAD_DIM)
            q2t = jnp.where(keep, qt, 0.0)
            gate_t = _mm_hi(kmean[:, sl], q2t)
            qaugt_ref[0, 2 * p + hh] = jnp.concatenate([q2t, _top3_bias(gate_t, i), sel_pad], axis=0).astype(bf16)
    kmean_scr[pl.ds(i, 1), :] = jnp.mean(kn, axis=0, keepdims=True)


def _moba_prep(proj, bsz, seq, qw, kw, bd_mean):
    nb = seq // MOBA_BLOCK
    nbp = -(-nb // 8) * 8
    assert nbp <= LANES
    return pl.pallas_call(
        _moba_prep_kernel, name="moba_prep",
        grid=(bsz, nb),
        in_specs=[pl.BlockSpec((MOBA_BLOCK, 3 * MIX_W), lambda b, i: (b * nb + i, OFF_A // (3 * MIX_W))),
                  pl.BlockSpec((1, MIX_W), lambda b, i: (0, 0)),
                  pl.BlockSpec((1, MIX_W), lambda b, i: (0, 0)),
                  pl.BlockSpec((MIX_W, MIX_W), lambda b, i: (0, 0))],
        out_specs=[pl.BlockSpec((1, N_HEADS, 2 * LANES, MOBA_BLOCK), lambda b, i: (b, 0, 0, i)),
                   pl.BlockSpec((1, N_HEADS // 2, MOBA_BLOCK, 2 * LANES), lambda b, i: (b, 0, i, 0)),
                   pl.BlockSpec((1, N_HEADS, VT_ROWS, MOBA_BLOCK), lambda b, i: (b, 0, 0, i))],
        out_shape=[jax.ShapeDtypeStruct((bsz, N_HEADS, 2 * LANES, seq), bf16),
                   jax.ShapeDtypeStruct((bsz, N_HEADS // 2, seq, 2 * LANES), bf16),
                   jax.ShapeDtypeStruct((bsz, N_HEADS, VT_ROWS, seq), bf16)],
        scratch_shapes=[pltpu.VMEM((nbp, MIX_W), f32)],
        compiler_params=_cparams(("parallel", "arbitrary"), 32),
    )(proj, qw, kw, bd_mean)


def _moba_attn_kernel(qaugt_ref, kaug_ref, vaugt_ref, tabt_ref, o_ref, sa_scr, sb_scr, acc_scr):
    i = pl.program_id(2)
    blk = MOBA_BLOCK
    pairs = range(MOBA_PAIRS_PER_STEP)
    hs = range(2 * MOBA_PAIRS_PER_STEP)
    key, qry = _iota2((blk, blk))
    mm = lambda a, b: jnp.dot(a, b, preferred_element_type=f32)
    cmax = lambda s: jnp.max(s, axis=0, keepdims=True)
    pv = lambda v, pe: jnp.dot(v, pe.astype(bf16), preferred_element_type=f32)
    kblk = lambda pp, j: kaug_ref[0, pp, pl.ds(pl.multiple_of(j * blk, blk), blk), :]
    vblk = lambda hh, j: vaugt_ref[0, hh, :, pl.ds(pl.multiple_of(j * blk, blk), blk)]

    n_far = jnp.maximum(i - 1, 0)
    nk = MOBA_KV_PER_GROUP
    n_groups = (n_far + nk - 1) // nk
    blocks_of = lambda g: [jnp.where(nk * g + a < n_far, nk * g + a, i) for a in range(nk)]

    def scores(g, buf):
        qt = [qaugt_ref[0, hh] for hh in hs]
        for a, j in enumerate(blocks_of(g)):
            ks = [kblk(pp, j) for pp in pairs]
            for hh in hs:
                buf[hh, a] = mm(ks[hh // 2], qt[hh])

    scores(0, sa_scr)

    k_own = [kblk(pp, i) for pp in pairs]
    k_adj = [kblk(pp, n_far) for pp in pairs]
    qt = [qaugt_ref[0, hh] for hh in hs]
    s_own = [jnp.where(qry >= key, mm(k_own[hh // 2][:, :LANES], qt[hh][:LANES]) + tabt_ref[hh // 2, hh % 2, 1], NEG)
             for hh in hs]
    s_adj = [mm(k_adj[hh // 2], qt[hh]) + tabt_ref[hh // 2, hh % 2, 0] for hh in hs]
    m = [jnp.maximum(cmax(s_own[hh]), cmax(s_adj[hh])) for hh in hs]
    for hh in hs:
        acc_scr[hh] = (pv(vblk(hh, i), jnp.exp2(s_own[hh] - m[hh]))
                       + pv(vblk(hh, n_far), jnp.exp2(s_adj[hh] - m[hh])))

    def consume(g, buf, m):
        js = blocks_of(g)
        m_new = []
        for hh in hs:
            s = [buf[hh, a] for a in range(nk)]
            mh = functools.reduce(jnp.maximum, [cmax(x) for x in s], m[hh])
            ah = jnp.exp2(m[hh] - mh) * acc_scr[hh]
            for a in range(nk):
                ah = ah + pv(vblk(hh, js[a]), jnp.exp2(s[a] - mh))
            acc_scr[hh] = ah
            m_new.append(mh)
        return tuple(m_new)

    def body(u, m):
        scores(2 * u + 1, sb_scr)
        m = consume(2 * u, sa_scr, m)
        scores(2 * u + 2, sa_scr)
        return consume(2 * u + 1, sb_scr, m)

    m = lax.fori_loop(0, n_groups // 2, body, tuple(m))
    lax.cond(n_groups % 2 == 1, lambda mm_: consume(n_groups - 1, sa_scr, mm_), lambda mm_: mm_, m)
    o_ref[0] = jnp.concatenate(
        [(acc_scr[hh][:HEAD_DIM] / acc_scr[hh][HEAD_DIM:HEAD_DIM + 1]).T for hh in hs], axis=-1)


def _moba_attn(qaugt, kaug, vaugt, tabt):
    bsz, _, _, seq = qaugt.shape
    nb = seq // MOBA_BLOCK
    pp = MOBA_PAIRS_PER_STEP
    return pl.pallas_call(
        _moba_attn_kernel, name="moba_attn",
        grid=(bsz, N_HEADS // (2 * pp), nb),
        in_specs=[pl.BlockSpec((1, 2 * pp, 2 * LANES, MOBA_BLOCK), lambda b, p, i: (b, p, 0, i)),
                  pl.BlockSpec((1, pp, seq, 2 * LANES), lambda b, p, i: (b, p, 0, 0), pipeline_mode=pl.Buffered(1)),
                  pl.BlockSpec((1, 2 * pp, VT_ROWS, seq), lambda b, p, i: (b, p, 0, 0), pipeline_mode=pl.Buffered(1)),
                  pl.BlockSpec((pp, 2, 2, MOBA_BLOCK, MOBA_BLOCK), lambda b, p, i: (p, 0, 0, 0, 0),
                               pipeline_mode=pl.Buffered(1))],
        out_specs=pl.BlockSpec((1, MOBA_BLOCK, pp * LANES), lambda b, p, i: (b, i, p)),
        out_shape=jax.ShapeDtypeStruct((bsz, seq, MIX_W), f32),
        scratch_shapes=[pltpu.VMEM((2 * pp, MOBA_KV_PER_GROUP, MOBA_BLOCK, MOBA_BLOCK), f32)] * 2
        + [pltpu.VMEM((2 * pp, VT_ROWS, MOBA_BLOCK), f32)],
        compiler_params=_cparams(("parallel", "parallel", "arbitrary"), 56),
    )(qaugt, kaug, vaugt, tabt)


def _t5_bucket(dist):
    n = jnp.maximum(dist, 0)
    max_exact = REL_BUCKETS // 2
    nf = jnp.maximum(n, max_exact).astype(f32)
    large = max_exact + (jnp.log(nf / max_exact) / math.log(REL_MAX_DIST / max_exact)
                         * (REL_BUCKETS - max_exact)).astype(jnp.int32)
    large = jnp.minimum(large, REL_BUCKETS - 1)
    return jnp.where(n < max_exact, n, large)


def _moba_bias_kernel(vec_ref, o_ref):
    blk = MOBA_BLOCK
    t = pltpu.roll(jnp.broadcast_to(vec_ref[0] * LOG2E, (blk, 2 * blk)), 0, 1, stride=1, stride_axis=0)
    o_ref[0, 0] = t[:, blk:]
    o_ref[0, 1] = t[:, :blk]


def _moba_bias_tables(rel_bias):
    assert MOBA_BLOCK >= REL_MAX_DIST
    by_dist = rel_bias.astype(f32)[_t5_bucket(jnp.arange(2 * MOBA_BLOCK))]
    far = rel_bias.astype(f32)[_t5_bucket(jnp.array(2 * MOBA_BLOCK))]
    vec = (by_dist - far).T.reshape(N_HEADS, 1, 2 * MOBA_BLOCK)
    tab = pl.pallas_call(
        _moba_bias_kernel, name="moba_bias",
        grid=(N_HEADS,),
        in_specs=[pl.BlockSpec((1, 1, 2 * MOBA_BLOCK), lambda h: (h, 0, 0))],
        out_specs=pl.BlockSpec((1, 2, MOBA_BLOCK, MOBA_BLOCK), lambda h: (h, 0, 0, 0)),
        out_shape=jax.ShapeDtypeStruct((N_HEADS, 2, MOBA_BLOCK, MOBA_BLOCK), f32),
        compiler_params=_cparams(("parallel",), 16),
    )(vec)
    return tab.reshape(N_HEADS // 2, 2, 2, MOBA_BLOCK, MOBA_BLOCK)


def _causal_conv(x, halo, w_ref):
    ts, nh = x.shape[0], halo.shape[0]
    xe = jnp.concatenate([halo, x], axis=0)
    acc = x * w_ref[CONV_K - 1:CONV_K, :]
    for d in range(1, CONV_K):
        acc = acc + xe[nh - d:nh - d + ts] * w_ref[CONV_K - 1 - d:CONV_K - d, :]
    return acc


def _tile_and_halo(x_ref, halo_ref):
    halo = jnp.where(pl.program_id(1) == 0, 0.0, halo_ref[...].astype(f32))
    return x_ref[...].astype(f32), halo


def _tok_spec(rows, nsteps):
    return pl.BlockSpec((rows, MIX_W), lambda b, i: (b * nsteps + i, 0))


def _pair_masks(rows):
    lane = lax.broadcasted_iota(jnp.int32, (rows, LANES), 1)
    return lane < HEAD_DIM, lane >= HEAD_DIM


def _chunk_sum_matrix(chunk):
    r = jnp.arange(CUMSUM_TILE)[:, None]
    c = jnp.arange(CUMSUM_TILE)[None, :]
    same = (r // chunk) == (c // chunk)
    return jnp.concatenate([same & (r >= c), same], axis=0).astype(bf16)


def _chunk_sums(cm_ref, x):
    parts = [_sel_left(cm_ref[...], x[r:r + CUMSUM_TILE]) for r in range(0, x.shape[0], CUMSUM_TILE)]
    return (jnp.concatenate([p[:CUMSUM_TILE] for p in parts], axis=0),
            jnp.concatenate([p[CUMSUM_TILE:] for p in parts], axis=0))


def _row_spec(width, off, nt):
    return pl.BlockSpec((ROW_TILE, width), lambda b, i: (b * nt + i, off // width))


def _halo_spec(width, off, nt):
    per = ROW_TILE // HALO_ROWS
    return pl.BlockSpec((HALO_ROWS, width), lambda b, i: (jnp.maximum((b * nt + i) * per - 1, 0), off // width))


def _small_spec(nt):
    return pl.BlockSpec((ROW_TILE, LANES), lambda b, i: (b * nt + i, 0))


def _const_spec(shape):
    return pl.BlockSpec(shape, lambda b, i: (0,) * len(shape))


def _gdn_prep_kernel(x_ref, halo_ref, sm_ref, cw_ref, alog_ref, dtb_ref, bd_ref, eb_ref, ea_ref, cm_ref,
                     q_ref, k_ref, kb_ref, vb_ref, qd_ref, kbe_ref, kd_ref, gc_ref):
    qkv = _silu(_causal_conv(*_tile_and_halo(x_ref, halo_ref), cw_ref))
    q, k, v = qkv[:, :MIX_W], qkv[:, MIX_W:2 * MIX_W], qkv[:, 2 * MIX_W:]
    bd = bd_ref[...]
    q = q * lax.rsqrt(_sel_right1(q * q, bd) + L2_EPS) * HEAD_DIM ** -0.5
    k = k * lax.rsqrt(_sel_right1(k * k, bd) + L2_EPS)
    sm = sm_ref[...]
    beta = _sel_right1(_sigmoid(sm), eb_ref[...])
    g = -jnp.exp(alog_ref[...]) * _softplus(sm + dtb_ref[...])
    gc, g_end = _chunk_sums(cm_ref, g)
    eg = jnp.exp(_sel_right(gc, ea_ref[...]))
    e_rest = jnp.exp(_sel_right(g_end - gc, ea_ref[...]))
    kb = k * beta
    for ref, val in ((q_ref, q), (k_ref, k), (kb_ref, kb), (vb_ref, v * beta),
                     (qd_ref, q * eg), (kbe_ref, kb * eg), (kd_ref, k * e_rest)):
        ref[...] = val.astype(bf16)
    gc_ref[...] = gc[:, SM_BA:SM_BA + N_HEADS]


def _gdn_prep(proj, small, bsz, seq, conv_w, alog128, dtb128, bd_ones, e_beta, e_ba, cm):
    nt = seq // ROW_TILE
    w3 = 3 * MIX_W
    return pl.pallas_call(
        _gdn_prep_kernel, name="gdn_prep",
        grid=(bsz, nt),
        in_specs=[_row_spec(w3, OFF_BQKV, nt), _halo_spec(w3, OFF_BQKV, nt), _small_spec(nt),
                  _const_spec((CONV_K, w3)), _const_spec((1, LANES)), _const_spec((1, LANES)),
                  _const_spec((MIX_W, MIX_W)), _const_spec((LANES, MIX_W)), _const_spec((LANES, MIX_W)),
                  _const_spec((2 * CUMSUM_TILE, CUMSUM_TILE))],
        out_specs=[_tok_spec(ROW_TILE, nt)] * 7 + [pl.BlockSpec((ROW_TILE, N_HEADS), lambda b, i: (b * nt + i, 0))],
        out_shape=[jax.ShapeDtypeStruct((bsz * seq, MIX_W), bf16)] * 7 + [jax.ShapeDtypeStruct((bsz * seq, N_HEADS), f32)],
        compiler_params=_cparams(("parallel", "parallel"), 40),
    )(proj, proj, small, conv_w, alog128, dtb128, bd_ones, e_beta, e_ba, cm)


def _gdn_chunk_kernel(q_ref, k_ref, kb_ref, vb_ref, qd_ref, kbe_ref, kd_ref, gc_ref, gct_ref, nw_ref, bdm_ref,
                      o_ref, st_scr):
    @pl.when(pl.program_id(1) == 0)
    def _():
        st_scr[...] = jnp.zeros_like(st_scr)

    n = GDN_CHUNK
    cis, prs = range(CHUNKS_PER_STEP), range(N_HEADS // 2)
    tiles = [(ci, p) for ci in cis for p in prs]
    ch = [(ci, p, hh) for ci, p in tiles for hh in (0, 1)]
    blk = lambda ref, ci, p: ref[ci * n:(ci + 1) * n, p * LANES:(p + 1) * LANES]
    r, c = _iota2((n, n))
    row2, col2 = _iota2((2 * n, 2 * n))
    same_head = (row2 < n) == (col2 < n)
    hm2 = _pair_masks(2 * n)
    keep0 = _pair_masks(n)[0]
    pick = lambda x0, x1: jnp.where(keep0, x0, x1)
    zero = jnp.zeros((), bf16)
    gc_all = [gc_ref[0, ci] for ci in cis]
    gct_all = [gct_ref[0, ci] for ci in cis]
    gcol = lambda ci, p, hh: gc_all[ci][:, 2 * p + hh:2 * p + hh + 1]
    decay = {(ci, p, hh): jnp.exp(jnp.where(r >= c, gcol(ci, p, hh) - gct_all[ci][2 * p + hh:2 * p + hh + 1, :], NEG))
             for ci, p, hh in ch}
    lhs = {x: jnp.concatenate([blk(kb_ref, *x), blk(q_ref, *x)], axis=0) for x in tiles}
    gram = {(ci, p, hh): _mm_nt(jnp.where(hm2[hh], lhs[ci, p], zero), blk(k_ref, ci, p)) for ci, p, hh in ch}
    t = dict(zip(ch, _unit_lower_inverses([jnp.where(r > c, gram[x][:n] * decay[x], 0.0) for x in ch])))
    u = {(ci, p): pick(_mm(t[ci, p, 0], blk(vb_ref, ci, p)), _mm(t[ci, p, 1], blk(vb_ref, ci, p))) for ci, p in tiles}
    w = {(ci, p): pick(_mm(t[ci, p, 0], blk(kbe_ref, ci, p)), _mm(t[ci, p, 1], blk(kbe_ref, ci, p)))
         for ci, p in tiles}
    a_in = {x: gram[x][n:] * decay[x] for x in ch}
    st = [st_scr[p] for p in prs]
    for ci in cis:
        ws = [_mm(jnp.concatenate([w[ci, p].astype(bf16), blk(qd_ref, ci, p)], axis=0), st[p]) for p in prs]
        v_new = [u[ci, p] - ws[p][:n] for p in prs]
        o = [ws[p][n:] + pick(_mm(a_in[ci, p, 0], v_new[p]), _mm(a_in[ci, p, 1], v_new[p])) for p in prs]
        upd = [jnp.where(same_head, _mm_tn(blk(kd_ref, ci, p), v_new[p]), 0.0) for p in prs]
        for p in prs:
            ms = _sel_right1(o[p] * o[p], bdm_ref[...])
            o_ref[ci * n:(ci + 1) * n, p * LANES:(p + 1) * LANES] = o[p] * lax.rsqrt(ms + RMS_EPS) * nw_ref[...]
        g_end = [jnp.exp(jnp.where(keep0[0:1], gcol(ci, p, 0)[n - 1:n], gcol(ci, p, 1)[n - 1:n])) for p in prs]
        st = [st[p] * g_end[p] + upd[p] for p in prs]
    for p in prs:
        st_scr[p] = st[p]


def _gdn_chunk(bsz, seq, q, k, kb, vb, qd, kbe, kd, gc, norm_w2, bd_mean2):
    n = GDN_CHUNK
    nc = seq // n
    per = CHUNKS_PER_STEP
    gc4 = gc.reshape(bsz, nc, n, N_HEADS)
    gct4 = jnp.swapaxes(gc4, 2, 3)
    spec = _tok_spec(per * n, nc // per)
    return pl.pallas_call(
        _gdn_chunk_kernel, name="gdn_chunk",
        grid=(bsz, nc // per),
        in_specs=[spec] * 7 + [pl.BlockSpec((1, per, n, N_HEADS), lambda b, i: (b, i, 0, 0)),
                               pl.BlockSpec((1, per, N_HEADS, n), lambda b, i: (b, i, 0, 0)),
                               _const_spec((1, LANES)), _const_spec((LANES, LANES))],
        out_specs=spec,
        out_shape=jax.ShapeDtypeStruct((bsz * seq, MIX_W), f32),
        scratch_shapes=[pltpu.VMEM((N_HEADS // 2, 2 * HEAD_DIM, 2 * HEAD_DIM), f32)],
        compiler_params=_cparams(("parallel", "arbitrary"), 32),
    )(q, k, kb, vb, qd, kbe, kd, gc4, gct4, norm_w2, bd_mean2)


def _rwkv_prep_kernel(has_vres, *refs):
    if has_vres:
        (c_ref, halo_ref, mu_ref, w0_ref, wup_ref, a0_ref, aup_ref, gup_ref, kk_ref, ka_ref, bd_ref,
         rk_ref, cm_ref, vf_ref, v0_ref, vdn_ref, vup_ref,
         rt_ref, at_ref, bt_ref, kt_ref, v_ref, pe_ref, gout_ref, bonus_ref) = refs
    else:
        (c_ref, halo_ref, mu_ref, w0_ref, wup_ref, a0_ref, aup_ref, gup_ref, kk_ref, ka_ref, bd_ref,
         rk_ref, cm_ref,
         rt_ref, at_ref, bt_ref, kt_ref, v_ref, pe_ref, gout_ref, bonus_ref, cv_ref) = refs
    c, halo = _tile_and_halo(c_ref, halo_ref)
    prev = jnp.concatenate([halo[HALO_ROWS - 1:], c[:-1]], axis=0)
    c = c + (prev - c) * mu_ref[...]
    c_r, c_k, c_v = c[:, :MIX_W], c[:, MIX_W:2 * MIX_W], c[:, 2 * MIX_W:3 * MIX_W]
    c_wd = c[:, 3 * MIX_W:3 * MIX_W + 64]
    c_ad = c[:, 3 * MIX_W + 64:3 * MIX_W + 128]
    c_gd = c[:, 3 * MIX_W + 128:]
    w_log = -_softplus(-(w0_ref[...] + _mm(jnp.tanh(c_wd), wup_ref[...]))) - 0.5
    a_in = _sigmoid(a0_ref[...] + _mm(c_ad, aup_ref[...]))
    gout_ref[...] = _mm(_sigmoid(c_gd), gup_ref[...])
    if has_vres:
        lam = _sigmoid(v0_ref[...] + _mm(_mm(c_v, vdn_ref[...]), vup_ref[...]))
        v_r = c_v + (vf_ref[...] - c_v) * lam
    else:
        v_r = c_v
        cv_ref[...] = c_v
    bd = bd_ref[...]
    kk = c_k * kk_ref[...]
    kk = kk * lax.rsqrt(_sel_right1(kk * kk, bd) + L2_EPS)
    k_r = c_k * (1.0 + (a_in - 1.0) * ka_ref[...])
    b = kk * a_in
    bonus_ref[...] = _sel_right1(c_r * k_r * rk_ref[...], bd) * v_r
    lc, lc_end = _chunk_sums(cm_ref, -jnp.exp(w_log))
    e_neg = jnp.exp(-lc)
    rt_ref[...] = (c_r * jnp.exp(lc)).astype(bf16)
    at_ref[...] = (-kk * jnp.exp(lc + jnp.exp(w_log))).astype(bf16)
    bt_ref[...] = (b * e_neg).astype(bf16)
    kt_ref[...] = (k_r * e_neg).astype(bf16)
    v_ref[...] = v_r.astype(bf16)
    pe_ref[...] = jnp.exp(lc_end)


def _rwkv_prep(proj, bsz, seq, mu, w0, w_up, a0, a_up, g_up, k_k, k_a, r_k, bd_ones, cm, vres):
    nt = seq // ROW_TILE
    wc = 3 * MIX_W + 256
    std = pl.BlockSpec((ROW_TILE, MIX_W), lambda b, i: (b * nt + i, 0))
    std_shape = jax.ShapeDtypeStruct((bsz * seq, MIX_W), f32)
    in_specs = [_row_spec(wc, OFF_C, nt), _halo_spec(wc, OFF_C, nt), _const_spec((1, wc)),
                _const_spec((1, MIX_W)), _const_spec((64, MIX_W)), _const_spec((1, MIX_W)), _const_spec((64, MIX_W)),
                _const_spec((128, MIX_W)), _const_spec((1, MIX_W)), _const_spec((1, MIX_W)), _const_spec((MIX_W, MIX_W)),
                _const_spec((1, MIX_W)), _const_spec((2 * CUMSUM_TILE, CUMSUM_TILE))]
    args = [proj, proj, mu, w0, w_up, a0, a_up, g_up, k_k, k_a, bd_ones, r_k, cm]
    out_specs = [std] * 8
    out_shape = [jax.ShapeDtypeStruct((bsz * seq, MIX_W), bf16)] * 5 + [std_shape] * 3
    if vres is not None:
        v_first, v0, v_down, v_up = vres
        in_specs += [std, _const_spec((1, MIX_W)), _const_spec(v_down.shape), _const_spec(v_up.shape)]
        args += [v_first, v0, v_down, v_up]
    else:
        out_specs.append(std)
        out_shape.append(std_shape)
    return pl.pallas_call(
        functools.partial(_rwkv_prep_kernel, vres is not None), name="rwkv_prep",
        grid=(bsz, nt), in_specs=in_specs, out_specs=out_specs, out_shape=out_shape,
        compiler_params=_cparams(("parallel", "parallel"), 40),
    )(*args)


def _rwkv_chunk_kernel(rt_ref, at_ref, bt_ref, kt_ref, v_ref, pe_ref, o_ref, st_scr):
    @pl.when(pl.program_id(1) == 0)
    def _():
        st_scr[...] = jnp.zeros_like(st_scr)

    n = RWKV_CHUNK
    cis, prs = range(CHUNKS_PER_STEP), range(N_HEADS // 2)
    tiles = [(ci, p) for ci in cis for p in prs]
    ch = [(ci, p, hh) for ci, p in tiles for hh in (0, 1)]
    blk = lambda ref, ci, p: ref[ci * n:(ci + 1) * n, p * LANES:(p + 1) * LANES]
    row, col = _iota2((2 * n, 2 * n))
    rr, cc = row & (n - 1), col & (n - 1)
    mask = rr + jnp.where(row < n, 0, 1) > cc
    same_head = (row < n) == (col < n)
    hm2 = _pair_masks(2 * n)
    keep0 = _pair_masks(n)[0]
    pick = lambda x0, x1: jnp.where(keep0, x0, x1)
    zero = jnp.zeros((), bf16)
    lhs = {x: jnp.concatenate([blk(at_ref, *x), blk(rt_ref, *x)], axis=0) for x in tiles}
    rhs = {x: jnp.concatenate([blk(bt_ref, *x), blk(kt_ref, *x)], axis=0) for x in tiles}
    gm = {(ci, p, hh): jnp.where(mask, _mm_nt(jnp.where(hm2[hh], lhs[ci, p], zero), rhs[ci, p]), 0.0)
          for ci, p, hh in ch}
    t = dict(zip(ch, _unit_lower_inverses([-gm[x][:n, :n] for x in ch])))
    v = {x: blk(v_ref, *x) for x in tiles}
    zv = {x: jnp.concatenate([jnp.zeros_like(v[x]), v[x]], axis=0) for x in tiles}
    makv = {(ci, p): pick(_mm(gm[ci, p, 0][:n], zv[ci, p]), _mm(gm[ci, p, 1][:n], zv[ci, p])) for ci, p in tiles}
    st = [st_scr[p] for p in prs]
    for ci in cis:
        ah = [_mm_nt(lhs[ci, p], st[p]) for p in prs]
        rhs_u = [ah[p][:n] + makv[ci, p] for p in prs]
        u = [pick(_mm(t[ci, p, 0], rhs_u[p]), _mm(t[ci, p, 1], rhs_u[p])) for p in prs]
        uv = [jnp.concatenate([u[p].astype(bf16), v[ci, p]], axis=0) for p in prs]
        o = [ah[p][n:] + pick(_mm(gm[ci, p, 0][n:], uv[p]), _mm(gm[ci, p, 1][n:], uv[p])) for p in prs]
        pe = [pe_ref[ci * n:ci * n + 1, p * LANES:(p + 1) * LANES] for p in prs]
        upd = [_mm_tn(uv[p], rhs[ci, p].astype(f32) * pe[p]) for p in prs]
        for p in prs:
            o_ref[ci * n:(ci + 1) * n, p * LANES:(p + 1) * LANES] = o[p]
        st = [st[p] * pe[p] + jnp.where(same_head, upd[p], 0.0) for p in prs]
    for p in prs:
        st_scr[p] = st[p]


def _rwkv_chunk(bsz, seq, rt, at, bt, kt, v, pe):
    n = RWKV_CHUNK * CHUNKS_PER_STEP
    spec = _tok_spec(n, seq // n)
    return pl.pallas_call(
        _rwkv_chunk_kernel, name="rwkv_chunk",
        grid=(bsz, seq // n),
        in_specs=[spec] * 6,
        out_specs=spec,
        out_shape=jax.ShapeDtypeStruct((bsz * seq, MIX_W), f32),
        scratch_shapes=[pltpu.VMEM((N_HEADS // 2, 2 * HEAD_DIM, 2 * HEAD_DIM), f32)],
        compiler_params=_cparams(("parallel", "arbitrary"), 32),
    )(rt, at, bt, kt, v, pe)


def _ssd_prep_kernel(x_ref, halo_ref, sm_ref, cw_ref, cb_ref, alog_ref, dtb_ref, edt_ref, cm_ref,
                     xdt_ref, x_out_ref, bc_ref, acs_ref):
    xbc = _silu(_causal_conv(*_tile_and_halo(x_ref, halo_ref), cw_ref) + cb_ref[...])
    m_x = xbc[:, :MIX_W]
    dt = _softplus(sm_ref[...] + dtb_ref[...])
    xdt_ref[...] = (m_x * _sel_right1(dt, edt_ref[...])).astype(xdt_ref.dtype)
    x_out_ref[...] = m_x
    bc_ref[...] = xbc[:, MIX_W:].astype(bc_ref.dtype)
    acs, _ = _chunk_sums(cm_ref, dt * -jnp.exp(alog_ref[...]))
    acs_ref[...] = acs[:, SM_DT:SM_DT + N_HEADS]


def _ssd_prep(proj, small, bsz, seq, conv_w, conv_b, alog128, dtb128, e_dt, cm):
    nt = seq // ROW_TILE
    wx = MIX_W + 4 * SSM_STATE
    return pl.pallas_call(
        _ssd_prep_kernel, name="ssd_prep",
        grid=(bsz, nt),
        in_specs=[_row_spec(wx, OFF_DXBC, nt), _halo_spec(wx, OFF_DXBC, nt), _small_spec(nt),
                  _const_spec((CONV_K, wx)), _const_spec((1, wx)), _const_spec((1, LANES)), _const_spec((1, LANES)),
                  _const_spec((LANES, MIX_W)), _const_spec((2 * CUMSUM_TILE, CUMSUM_TILE))],
        out_specs=[_tok_spec(ROW_TILE, nt)] * 2 + [pl.BlockSpec((ROW_TILE, 4 * SSM_STATE), lambda b, i: (b * nt + i, 0)),
                                                   pl.BlockSpec((ROW_TILE, N_HEADS), lambda b, i: (b * nt + i, 0))],
        out_shape=[jax.ShapeDtypeStruct((bsz * seq, MIX_W), bf16), jax.ShapeDtypeStruct((bsz * seq, MIX_W), f32),
                   jax.ShapeDtypeStruct((bsz * seq, 4 * SSM_STATE), bf16),
                   jax.ShapeDtypeStruct((bsz * seq, N_HEADS), f32)],
        compiler_params=_cparams(("parallel", "parallel"), 40),
    )(proj, proj, small, conv_w, conv_b, alog128, dtb128, e_dt, cm)


def _ssd_chunk_kernel(xdt_ref, x_ref, bc_ref, a_ref, at_ref, dvec_ref, o_ref, st_scr):
    @pl.when(pl.program_id(1) == 0)
    def _():
        st_scr[...] = jnp.zeros_like(st_scr)

    n = SSD_CHUNK
    cis, prs = range(SSD_CHUNKS_PER_STEP), range(N_HEADS // 2)
    grp = lambda p: (2 * p) // (N_HEADS // 2)
    tiles = [(ci, p) for ci in cis for p in prs]
    ch = [(ci, p, hh) for ci, p in tiles for hh in (0, 1)]
    rows = lambda ci: slice(ci * n, (ci + 1) * n)
    lanes = lambda p: slice(p * LANES, (p + 1) * LANES)
    r, c = _iota2((n, n))
    keep0 = _pair_masks(n)[0]
    pick = lambda x0, x1: jnp.where(keep0, x0, x1)
    b_g = {(ci, g): bc_ref[rows(ci), g * SSM_STATE:(g + 1) * SSM_STATE] for ci in cis for g in range(2)}
    c_g = {(ci, g): bc_ref[rows(ci), (2 + g) * SSM_STATE:(3 + g) * SSM_STATE] for ci in cis for g in range(2)}
    cb = {x: _mm_nt(c_g[x], b_g[x]) for x in b_g}
    col = lambda ci, p, hh: a_ref[0, ci][:, 2 * p + hh:2 * p + hh + 1]
    ac = {x: col(*x) for x in ch}
    a_last = {x: col(*x)[n - 1:n] for x in ch}
    lmat = {(ci, p, hh): jnp.exp(jnp.where(r >= c, ac[ci, p, hh] - at_ref[0, ci][2 * p + hh:2 * p + hh + 1, :], NEG))
            for ci, p, hh in ch}
    xg = {(ci, p): xdt_ref[rows(ci), lanes(p)] for ci, p in tiles}
    y_diag = {(ci, p): pick(*[_mm(cb[ci, grp(p)] * lmat[ci, p, hh], xg[ci, p]) for hh in (0, 1)]) for ci, p in tiles}
    upd = {(ci, p): pick(*[_mm_tn(b_g[ci, grp(p)].astype(f32) * jnp.exp(a_last[ci, p, hh] - ac[ci, p, hh]), xg[ci, p])
                           for hh in (0, 1)]) for ci, p in tiles}
    c_in = {(ci, p, hh): c_g[ci, grp(p)].astype(f32) * jnp.exp(ac[ci, p, hh]) for ci, p, hh in ch}
    st = [st_scr[p] for p in prs]
    for ci in cis:
        y_off = [pick(_mm(c_in[ci, p, 0], st[p]), _mm(c_in[ci, p, 1], st[p])) for p in prs]
        for p in prs:
            o_ref[rows(ci), lanes(p)] = y_diag[ci, p] + y_off[p] + x_ref[rows(ci), lanes(p)] * dvec_ref[:, lanes(p)]
        st = [st[p] * jnp.exp(jnp.where(keep0[0:1], a_last[ci, p, 0], a_last[ci, p, 1])) + upd[ci, p] for p in prs]
    for p in prs:
        st_scr[p] = st[p]


def _ssd_chunk(bsz, seq, xdt, x, bc, acs, dvec):
    n = SSD_CHUNK
    nc = seq // n
    per = SSD_CHUNKS_PER_STEP
    a4 = acs.reshape(bsz, nc, n, N_HEADS)
    at4 = jnp.swapaxes(a4, 2, 3)
    return pl.pallas_call(
        _ssd_chunk_kernel, name="ssd_chunk",
        grid=(bsz, nc // per),
        in_specs=[_tok_spec(per * n, nc // per)] * 2
        + [pl.BlockSpec((per * n, 4 * SSM_STATE), lambda b, i: (b * (nc // per) + i, 0)),
           pl.BlockSpec((1, per, n, N_HEADS), lambda b, i: (b, i, 0, 0)),
           pl.BlockSpec((1, per, N_HEADS, n), lambda b, i: (b, i, 0, 0)),
           _const_spec((1, MIX_W))],
        out_specs=_tok_spec(per * n, nc // per),
        out_shape=jax.ShapeDtypeStruct((bsz * seq, MIX_W), f32),
        scratch_shapes=[pltpu.VMEM((N_HEADS // 2, SSM_STATE, 2 * HEAD_DIM), f32)],
        compiler_params=_cparams(("parallel", "arbitrary"), 32),
    )(xdt, x, bc, a4, at4, dvec)


def _merge_kernel(x_ref, ya_ref, ob_ref, bz_ref, wkv_ref, bonus_ref, gout_ref, lnw_ref, lnb_ref, bdm_ref,
                  yd_ref, dz_ref, mnw_ref, g0_ref, g1_ref, g2_ref, g3_ref, wb_ref, wo_ref, o_ref):
    def gated(n, y, g_ref):
        return _sigmoid(g_ref[...].astype(f32)) * _mm(y, wb_ref[n])

    acc = gated(0, ya_ref[0], g0_ref)
    acc = acc + gated(1, ob_ref[...] * _silu(bz_ref[...].astype(f32)), g1_ref)

    w = wkv_ref[...]
    d = w - _sel_right1(w, bdm_ref[...])
    wkv_ln = d * lax.rsqrt(_sel_right1(d * d, bdm_ref[...]) + RWKV_LN_EPS)
    y_c = (wkv_ln * lnw_ref[...] + lnb_ref[...] + bonus_ref[...]) * gout_ref[...]
    acc = acc + gated(2, y_c, g2_ref)

    yz = yd_ref[...] * _silu(dz_ref[...].astype(f32))
    half = MIX_W // 2
    y_d = jnp.concatenate(
        [yz[:, s:s + half] * lax.rsqrt(jnp.mean(jnp.square(yz[:, s:s + half]), axis=-1, keepdims=True) + RMS_EPS)
         for s in (0, half)], axis=-1) * mnw_ref[...]
    acc = acc + gated(3, y_d, g3_ref)
    o_ref[...] = x_ref[...] + _mm(acc, wo_ref[...])


def _merge(x2, proj, bsz, seq, ya, ob, wkv, bonus, gout, lnw, lnb, bd_mean, yd, mnw, wb, wo):
    nt = seq // ROW_TILE
    std = lambda w: pl.BlockSpec((ROW_TILE, w), lambda b, i: (b * nt + i, 0))
    gate = lambda n: _row_spec(D_MODEL, OFF_GATES + n * D_MODEL, nt)
    return pl.pallas_call(
        _merge_kernel, name="merge",
        grid=(bsz, nt),
        in_specs=[std(D_MODEL), pl.BlockSpec((1, ROW_TILE, MIX_W), lambda b, i: (b, i, 0)),
                  std(MIX_W), _row_spec(MIX_W, OFF_BZ, nt),
                  std(MIX_W), std(MIX_W), std(MIX_W), _const_spec((1, MIX_W)), _const_spec((1, MIX_W)),
                  _const_spec((MIX_W, MIX_W)),
                  std(MIX_W), _row_spec(MIX_W, OFF_DZ, nt), _const_spec((1, MIX_W)),
                  gate(0), gate(1), gate(2), gate(3),
                  _const_spec((4, MIX_W, D_MODEL)), _const_spec((D_MODEL, D_MODEL))],
        out_specs=std(D_MODEL),
        out_shape=jax.ShapeDtypeStruct((bsz * seq, D_MODEL), f32),
        compiler_params=_cparams(("parallel", "parallel"), 48),
    )(x2, ya, ob, proj, wkv, bonus, gout, lnw, lnb, bd_mean, yd, proj, mnw, proj, proj, proj, proj, wb, wo)


def _lane_vec(vals, off):
    return jnp.zeros((1, LANES), f32).at[0, off:off + vals.shape[0]].set(vals)


def _head_expand(off):
    n = jnp.arange(LANES)[:, None]
    c = jnp.arange(MIX_W)[None, :]
    return (n - off == c // HEAD_DIM).astype(bf16)


def _pack_w_in(w):
    pad = lambda n: jnp.zeros((w.shape[0], n), w.dtype)
    cols = [w[:, W_A:W_BZ],
            w[:, W_BZ:W_BBETA],
            w[:, W_C:W_DZ],
            w[:, W_BBETA:W_C], w[:, W_DDT:W_GATES], pad(2 * LANES - 3 * N_HEADS),
            w[:, W_DZ:W_DXBC], w[:, W_DXBC:W_DDT], w[:, W_GATES:]]
    out = jnp.concatenate(cols, axis=1).astype(bf16)
    assert out.shape[1] == N_PROJ
    return out


def kernel(x, rel_bias, norm1_w, w_in, moba_q_norm, moba_k_norm, gdn_conv_w, gdn_A_log, gdn_dt_bias, gdn_norm_w, rwkv_mu, rwkv_w0, rwkv_w_up, rwkv_a0, rwkv_a_up, rwkv_g_up, rwkv_k_k, rwkv_k_a, rwkv_r_k, rwkv_v0, rwkv_v_down, rwkv_v_up, rwkv_ln_w, rwkv_ln_b, mamba_conv_w, mamba_conv_b, mamba_dt_bias, mamba_A_log, mamba_D, mamba_norm_w, w_branch, w_out, norm2_w, ffn_w_in, ffn_w_down):
    bsz, seq, d = x.shape
    depth = w_in.shape[0]
    assert d == D_MODEL and (bsz * seq) % IN_PROJ_TILE_M == 0 and seq % MM_TILE_M == 0
    x2 = x.reshape(bsz * seq, d)
    row = lambda v: v.reshape(1, -1).astype(f32)

    hid = jnp.arange(MIX_W) // HEAD_DIM
    bd_ones = (hid[:, None] == hid[None, :]).astype(bf16)
    bd_mean = (bd_ones.astype(f32) / HEAD_DIM).astype(bf16)
    e_beta, e_ba, e_dt = _head_expand(SM_BETA), _head_expand(SM_BA), _head_expand(SM_DT)
    cm64, cm128 = _chunk_sum_matrix(GDN_CHUNK), _chunk_sum_matrix(SSD_CHUNK)
    assert GDN_CHUNK == RWKV_CHUNK
    tab = _moba_bias_tables(rel_bias)
    v_first = None
    for i in range(depth):
        proj, small = _in_proj(x2, row(norm1_w[i]), _pack_w_in(w_in[i]))

        qaug, kaug, v_a = _moba_prep(proj, bsz, seq, row(jnp.tile(moba_q_norm[i], N_HEADS)),
                                     row(jnp.tile(moba_k_norm[i], N_HEADS)), bd_mean)
        y_a = _moba_attn(qaug, kaug, v_a, tab)

        gdn_in = _gdn_prep(proj, small, bsz, seq, gdn_conv_w[i], _lane_vec(gdn_A_log[i], SM_BA),
                           _lane_vec(gdn_dt_bias[i], SM_BA), bd_ones, e_beta, e_ba, cm64)
        o_b = _gdn_chunk(bsz, seq, *gdn_in, row(jnp.tile(gdn_norm_w[i], 2)), bd_mean[:LANES, :LANES])

        vres = None if i == 0 else (v_first, row(rwkv_v0[i - 1]), rwkv_v_down[i - 1].astype(bf16),
                                    rwkv_v_up[i - 1].astype(bf16))
        outs = _rwkv_prep(proj, bsz, seq, row(rwkv_mu[i]), row(rwkv_w0[i]), rwkv_w_up[i].astype(bf16),
                          row(rwkv_a0[i]), rwkv_a_up[i].astype(bf16), rwkv_g_up[i].astype(bf16),
                          row(rwkv_k_k[i]), row(rwkv_k_a[i]), row(rwkv_r_k[i]), bd_ones, cm64, vres)
        g_out, bonus = outs[6], outs[7]
        if i == 0:
            v_first = outs[8]
        wkv = _rwkv_chunk(bsz, seq, *outs[:6])

        xdt, x_d, bc, acs = _ssd_prep(proj, small, bsz, seq, mamba_conv_w[i], row(mamba_conv_b[i]),
                                      _lane_vec(mamba_A_log[i], SM_DT), _lane_vec(mamba_dt_bias[i], SM_DT), e_dt, cm128)
        y_d = _ssd_chunk(bsz, seq, xdt, x_d, bc, acs, row(jnp.repeat(mamba_D[i], HEAD_DIM)))

        x2 = _merge(x2, proj, bsz, seq, y_a, o_b, wkv, bonus, g_out,
                    row(rwkv_ln_w[i]), row(rwkv_ln_b[i]), bd_mean, y_d, row(mamba_norm_w[i]),
                    w_branch[i].astype(bf16), w_out[i].astype(bf16))

        x2 = _ffn(x2, row(norm2_w[i]), ffn_w_in[i].astype(bf16), ffn_w_down[i].astype(bf16))
    return x2.reshape(bsz, seq, d)
```

```python
import functools
import math

import jax
import jax.numpy as jnp
from jax import lax
from jax.experimental import pallas as pl
from jax.experimental.pallas import tpu as pltpu

f32, bf16 = jnp.float32, jnp.bfloat16
HI = lax.Precision.HIGHEST

D_MODEL = 1024
N_HEADS = 8
HEAD_DIM = 64
MIX_W = N_HEADS * HEAD_DIM
RMS_EPS = 1e-6
L2_EPS = 1e-6
CONV_K = 4
MOBA_BLOCK = 256
MOBA_TOPK = 3
MOBA_PAIRS_PER_STEP = 4
MOBA_KV_PER_GROUP = 2
REL_BUCKETS = 32
REL_MAX_DIST = 128
GDN_CHUNK = 64
RWKV_CHUNK = 64
CHUNKS_PER_STEP = 4
RWKV_LN_EPS = 64e-5
SSM_STATE = 128
SSD_CHUNK = 128
SSD_CHUNKS_PER_STEP = 4
FFN_HIDDEN_SPLITS = 2
NEG = -1e30
LOG2E = math.log2(math.e)

LANES = 128
BF16_SUBLANES = 16
VT_ROWS = HEAD_DIM + BF16_SUBLANES

OFF_A, OFF_BQKV, OFF_BZ, OFF_C, OFF_SMALL, OFF_DZ, OFF_DXBC, OFF_GATES = 0, 1536, 3072, 3584, 5376, 5632, 6144, 7168
N_PROJ = OFF_GATES + 4 * D_MODEL
SM_BETA, SM_BA, SM_DT = 0, 8, 16
W_A, W_BQKV, W_BZ, W_BBETA, W_BA, W_C, W_DZ, W_DXBC, W_DDT, W_GATES = 0, 1536, 3072, 3584, 3592, 3600, 5392, 5904, 6928, 6936

ROW_TILE = 512
CUMSUM_TILE = 256
MM_TILE_M = 512
IN_PROJ_TILE_M, IN_PROJ_TILE_N = 1024, 512
HALO_ROWS = BF16_SUBLANES


def _cparams(sem, vmem_mb):
    return pltpu.CompilerParams(dimension_semantics=sem, vmem_limit_bytes=vmem_mb * 1024 * 1024)


def _mm(a, b):
    return jnp.dot(a.astype(bf16), b.astype(bf16), preferred_element_type=f32)


def _mm_nt(a, b):
    return lax.dot_general(a.astype(bf16), b.astype(bf16), (((1,), (1,)), ((), ())), preferred_element_type=f32)


def _mm_tn(a, b):
    return lax.dot_general(a.astype(bf16), b.astype(bf16), (((0,), (0,)), ((), ())), preferred_element_type=f32)


def _mm_hi(a, b):
    return jnp.dot(a, b, precision=HI, preferred_element_type=f32)


def _softplus(x):
    return jnp.maximum(x, 0.0) + jnp.log1p(jnp.exp(-jnp.abs(x)))


def _sigmoid(x):
    return jax.nn.sigmoid(x)


def _silu(x):
    return x * _sigmoid(x)


def _iota2(shape):
    return lax.broadcasted_iota(jnp.int32, shape, 0), lax.broadcasted_iota(jnp.int32, shape, 1)


def _split3(x):
    hi = x.astype(bf16)
    r1 = x - hi.astype(f32)
    mid = r1.astype(bf16)
    return hi, mid, (r1 - mid.astype(f32)).astype(bf16)


def _sel_left(m01, x):
    return sum(jnp.dot(m01, p, preferred_element_type=f32) for p in _split3(x))


def _sel_right(x, m01):
    return sum(jnp.dot(p, m01, preferred_element_type=f32) for p in _split3(x))


def _sel_right1(x, m01):
    return jnp.dot(x.astype(bf16), m01, preferred_element_type=f32)


def _unit_lower_inverses(l_list):
    n = l_list[0].shape[0]
    r, c = _iota2((n, n))
    eye = jnp.where(r == c, 1.0, 0.0)
    ts = [eye for _ in l_list]
    for ls in range(n.bit_length() - 1):
        m = ((r >> (ls + 1)) == (c >> (ls + 1))) & (((r >> ls) & 1) == 1) & (((c >> ls) & 1) == 0)
        lms = [jnp.where(m, l, 0.0) for l in l_list]
        if ls == 0:
            ts = [t - lm for t, lm in zip(ts, lms)]
        else:
            tl = [_mm(t, lm) for t, lm in zip(ts, lms)]
            ts = [t - _mm(x, t) for t, x in zip(ts, tl)]
    return ts


def _in_proj_kernel(x_ref, nw_ref, w_ref, o_ref, sm_ref, h_scr):
    j = pl.program_id(1)

    @pl.when(j == 0)
    def _():
        x = x_ref[...]
        y = x * lax.rsqrt(jnp.mean(x * x, axis=-1, keepdims=True) + RMS_EPS)
        h_scr[...] = (y * nw_ref[...]).astype(bf16)

    tn = o_ref.shape[1]
    w = w_ref[:, pl.ds(pl.multiple_of(j * tn, tn), tn)]
    acc = jnp.dot(h_scr[...], w, preferred_element_type=f32)
    o_ref[...] = acc.astype(o_ref.dtype)

    @pl.when(j == OFF_SMALL // tn)
    def _():
        sm_ref[...] = acc[:, OFF_SMALL % tn:OFF_SMALL % tn + LANES]


def _in_proj(x2, nw, w):
    t, d = x2.shape
    n = w.shape[1]
    tm, tn = IN_PROJ_TILE_M, IN_PROJ_TILE_N
    return pl.pallas_call(
        _in_proj_kernel, name="in_proj",
        grid=(t // tm, n // tn),
        in_specs=[pl.BlockSpec((tm, d), lambda i, j: (i, 0)),
                  pl.BlockSpec((1, d), lambda i, j: (0, 0)),
                  pl.BlockSpec((d, n), lambda i, j: (0, 0), pipeline_mode=pl.Buffered(1))],
        out_specs=[pl.BlockSpec((tm, tn), lambda i, j: (i, j)),
                   pl.BlockSpec((tm, LANES), lambda i, j: (i, 0))],
        out_shape=[jax.ShapeDtypeStruct((t, n), bf16), jax.ShapeDtypeStruct((t, LANES), f32)],
        scratch_shapes=[pltpu.VMEM((tm, d), bf16)],
        compiler_params=_cparams(("parallel", "arbitrary"), 48),
    )(x2, nw, w)


def _ffn_kernel(x_ref, nw_ref, wi_ref, wd_ref, o_ref):
    x = x_ref[...]
    h = (x * lax.rsqrt(jnp.mean(x * x, axis=-1, keepdims=True) + RMS_EPS) * nw_ref[...]).astype(bf16)
    n = wd_ref.shape[0]
    step = n // FFN_HIDDEN_SPLITS
    acc = x
    for c0 in range(0, n, step):
        g = jnp.dot(h, wi_ref[:, c0:c0 + step], preferred_element_type=f32)
        u = jnp.dot(h, wi_ref[:, n + c0:n + c0 + step], preferred_element_type=f32)
        acc = acc + jnp.dot((_silu(g) * u).astype(bf16), wd_ref[c0:c0 + step, :], preferred_element_type=f32)
    o_ref[...] = acc


def _ffn(x2, nw, wi, wd):
    t, d = x2.shape
    n = wd.shape[0]
    tm = MM_TILE_M
    resident = lambda shape: pl.BlockSpec(shape, lambda i: (0, 0), pipeline_mode=pl.Buffered(1))
    return pl.pallas_call(
        _ffn_kernel, name="ffn",
        grid=(t // tm,),
        in_specs=[pl.BlockSpec((tm, d), lambda i: (i, 0)),
                  pl.BlockSpec((1, d), lambda i: (0, 0)),
                  resident((d, 2 * n)), resident((n, d))],
        out_specs=pl.BlockSpec((tm, d), lambda i: (i, 0)),
        out_shape=jax.ShapeDtypeStruct((t, d), f32),
        compiler_params=_cparams(("parallel",), 48),
    )(x2, nw, wi, wd)


def _top3_bias(gate_t, n_past):
    row = lax.broadcasted_iota(jnp.int32, gate_t.shape, 0)
    g = jnp.where(row < n_past, gate_t, -jnp.inf)
    sel = jnp.zeros(gate_t.shape, jnp.bool_)
    for _ in range(MOBA_TOPK):
        m = jnp.max(g, axis=0, keepdims=True)
        idx = jnp.min(jnp.where(g == m, row, gate_t.shape[0]), axis=0, keepdims=True)
        pick = row == idx
        sel = sel | (pick & (m > -jnp.inf))
        g = jnp.where(pick, -jnp.inf, g)
    return jnp.where(sel, 0.0, NEG)


def _moba_prep_kernel(a_ref, qw_ref, kw_ref, bd_ref, qaugt_ref, kaug_ref, vaugt_ref, kmean_scr):
    i = pl.program_id(1)

    @pl.when(i == 0)
    def _():
        kmean_scr[...] = jnp.zeros_like(kmean_scr)

    a = a_ref[...].astype(f32)
    q, k, v = a[:, :MIX_W], a[:, MIX_W:2 * MIX_W], a[:, 2 * MIX_W:]
    bd = bd_ref[...]
    qn = q * lax.rsqrt(_sel_right1(q * q, bd) + RMS_EPS) * qw_ref[...]
    kn = k * lax.rsqrt(_sel_right1(k * k, bd) + RMS_EPS) * kw_ref[...]
    lane = lax.broadcasted_iota(jnp.int32, (MOBA_BLOCK, LANES), 1)
    onehot = jnp.where(lane == i, 1.0, 0.0).astype(bf16)
    ones_row = jnp.where(lax.broadcasted_iota(jnp.int32, (VT_ROWS - HEAD_DIM, MOBA_BLOCK), 0) == 0, 1.0, 0.0)
    kmean = kmean_scr[...]
    nbp = kmean.shape[0]
    dim = lax.broadcasted_iota(jnp.int32, (LANES, MOBA_BLOCK), 0)
    sel_pad = jnp.zeros((LANES - nbp, MOBA_BLOCK), f32)
    for p in range(N_HEADS // 2):
        sl = slice(p * LANES, (p + 1) * LANES)
        kaug_ref[0, p] = jnp.concatenate([kn[:, sl].astype(bf16), onehot], axis=-1)
        vt = v[:, sl].T
        qt = (qn[:, sl] * (HEAD_DIM ** -0.5 * LOG2E)).T
        for hh in range(2):
            vaugt_ref[0, 2 * p + hh] = jnp.concatenate(
                [vt[hh * HEAD_DIM:(hh + 1) * HEAD_DIM], ones_row], axis=0).astype(bf16)
            keep = (dim < HEAD_DIM) if hh == 0 else (dim >= HEAD_DIM)
            q2t = jnp.where(keep, qt, 0.0)
            gate_t = _mm_hi(kmean[:, sl], q2t)
            qaugt_ref[0, 2 * p + hh] = jnp.concatenate([q2t, _top3_bias(gate_t, i), sel_pad], axis=0).astype(bf16)
    kmean_scr[pl.ds(i, 1), :] = jnp.mean(kn, axis=0, keepdims=True)


def _moba_prep(proj, bsz, seq, qw, kw, bd_mean):
    nb = seq // MOBA_BLOCK
    nbp = -(-nb // 8) * 8
    assert nbp <= LANES
    return pl.pallas_call(
        _moba_prep_kernel, name="moba_prep",
        grid=(bsz, nb),
        in_specs=[pl.BlockSpec((MOBA_BLOCK, 3 * MIX_W), lambda b, i: (b * nb + i, OFF_A // (3 * MIX_W))),
                  pl.BlockSpec((1, MIX_W), lambda b, i: (0, 0)),
                  pl.BlockSpec((1, MIX_W), lambda b, i: (0, 0)),
                  pl.BlockSpec((MIX_W, MIX_W), lambda b, i: (0, 0))],
        out_specs=[pl.BlockSpec((1, N_HEADS, 2 * LANES, MOBA_BLOCK), lambda b, i: (b, 0, 0, i)),
                   pl.BlockSpec((1, N_HEADS // 2, MOBA_BLOCK, 2 * LANES), lambda b, i: (b, 0, i, 0)),
                   pl.BlockSpec((1, N_HEADS, VT_ROWS, MOBA_BLOCK), lambda b, i: (b, 0, 0, i))],
        out_shape=[jax.ShapeDtypeStruct((bsz, N_HEADS, 2 * LANES, seq), bf16),
                   jax.ShapeDtypeStruct((bsz, N_HEADS // 2, seq, 2 * LANES), bf16),
                   jax.ShapeDtypeStruct((bsz, N_HEADS, VT_ROWS, seq), bf16)],
        scratch_shapes=[pltpu.VMEM((nbp, MIX_W), f32)],
        compiler_params=_cparams(("parallel", "arbitrary"), 32),
    )(proj, qw, kw, bd_mean)


def _moba_attn_kernel(qaugt_ref, kaug_ref, vaugt_ref, tabt_ref, o_ref, sa_scr, sb_scr, acc_scr):
    i = pl.program_id(2)
    blk = MOBA_BLOCK
    pairs = range(MOBA_PAIRS_PER_STEP)
    hs = range(2 * MOBA_PAIRS_PER_STEP)
    key, qry = _iota2((blk, blk))
    mm = lambda a, b: jnp.dot(a, b, preferred_element_type=f32)
    cmax = lambda s: jnp.max(s, axis=0, keepdims=True)
    pv = lambda v, pe: jnp.dot(v, pe.astype(bf16), preferred_element_type=f32)
    kblk = lambda pp, j: kaug_ref[0, pp, pl.ds(pl.multiple_of(j * blk, blk), blk), :]
    vblk = lambda hh, j: vaugt_ref[0, hh, :, pl.ds(pl.multiple_of(j * blk, blk), blk)]

    n_far = jnp.maximum(i - 1, 0)
    nk = MOBA_KV_PER_GROUP
    n_groups = (n_far + nk - 1) // nk
    blocks_of = lambda g: [jnp.where(nk * g + a < n_far, nk * g + a, i) for a in range(nk)]

    def scores(g, buf):
        qt = [qaugt_ref[0, hh] for hh in hs]
        for a, j in enumerate(blocks_of(g)):
            ks = [kblk(pp, j) for pp in pairs]
            for hh in hs:
                buf[hh, a] = mm(ks[hh // 2], qt[hh])

    scores(0, sa_scr)

    k_own = [kblk(pp, i) for pp in pairs]
    k_adj = [kblk(pp, n_far) for pp in pairs]
    qt = [qaugt_ref[0, hh] for hh in hs]
    s_own = [jnp.where(qry >= key, mm(k_own[hh // 2][:, :LANES], qt[hh][:LANES]) + tabt_ref[hh // 2, hh % 2, 1], NEG)
             for hh in hs]
    s_adj = [mm(k_adj[hh // 2], qt[hh]) + tabt_ref[hh // 2, hh % 2, 0] for hh in hs]
    m = [jnp.maximum(cmax(s_own[hh]), cmax(s_adj[hh])) for hh in hs]
    for hh in hs:
        acc_scr[hh] = (pv(vblk(hh, i), jnp.exp2(s_own[hh] - m[hh]))
                       + pv(vblk(hh, n_far), jnp.exp2(s_adj[hh] - m[hh])))

    def consume(g, buf, m):
        js = blocks_of(g)
        m_new = []
        for hh in hs:
            s = [buf[hh, a] for a in range(nk)]
            mh = functools.reduce(jnp.maximum, [cmax(x) for x in s], m[hh])
            ah = jnp.exp2(m[hh] - mh) * acc_scr[hh]
            for a in range(nk):
                ah = ah + pv(vblk(hh, js[a]), jnp.exp2(s[a] - mh))
            acc_scr[hh] = ah
            m_new.append(mh)
        return tuple(m_new)

    def body(u, m):
        scores(2 * u + 1, sb_scr)
        m = consume(2 * u, sa_scr, m)
        scores(2 * u + 2, sa_scr)
        return consume(2 * u + 1, sb_scr, m)

    m = lax.fori_loop(0, n_groups // 2, body, tuple(m))
    lax.cond(n_groups % 2 == 1, lambda mm_: consume(n_groups - 1, sa_scr, mm_), lambda mm_: mm_, m)
    o_ref[0] = jnp.concatenate(
        [(acc_scr[hh][:HEAD_DIM] / acc_scr[hh][HEAD_DIM:HEAD_DIM + 1]).T for hh in hs], axis=-1)


def _moba_attn(qaugt, kaug, vaugt, tabt):
    bsz, _, _, seq = qaugt.shape
    nb = seq // MOBA_BLOCK
    pp = MOBA_PAIRS_PER_STEP
    return pl.pallas_call(
        _moba_attn_kernel, name="moba_attn",
        grid=(bsz, N_HEADS // (2 * pp), nb),
        in_specs=[pl.BlockSpec((1, 2 * pp, 2 * LANES, MOBA_BLOCK), lambda b, p, i: (b, p, 0, i)),
                  pl.BlockSpec((1, pp, seq, 2 * LANES), lambda b, p, i: (b, p, 0, 0), pipeline_mode=pl.Buffered(1)),
                  pl.BlockSpec((1, 2 * pp, VT_ROWS, seq), lambda b, p, i: (b, p, 0, 0), pipeline_mode=pl.Buffered(1)),
                  pl.BlockSpec((pp, 2, 2, MOBA_BLOCK, MOBA_BLOCK), lambda b, p, i: (p, 0, 0, 0, 0),
                               pipeline_mode=pl.Buffered(1))],
        out_specs=pl.BlockSpec((1, MOBA_BLOCK, pp * LANES), lambda b, p, i: (b, i, p)),
        out_shape=jax.ShapeDtypeStruct((bsz, seq, MIX_W), f32),
        scratch_shapes=[pltpu.VMEM((2 * pp, MOBA_KV_PER_GROUP, MOBA_BLOCK, MOBA_BLOCK), f32)] * 2
        + [pltpu.VMEM((2 * pp, VT_ROWS, MOBA_BLOCK), f32)],
        compiler_params=_cparams(("parallel", "parallel", "arbitrary"), 56),
    )(qaugt, kaug, vaugt, tabt)


def _t5_bucket(dist):
    n = jnp.maximum(dist, 0)
    max_exact = REL_BUCKETS // 2
    nf = jnp.maximum(n, max_exact).astype(f32)
    large = max_exact + (jnp.log(nf / max_exact) / math.log(REL_MAX_DIST / max_exact)
                         * (REL_BUCKETS - max_exact)).astype(jnp.int32)
    large = jnp.minimum(large, REL_BUCKETS - 1)
    return jnp.where(n < max_exact, n, large)


def _moba_bias_kernel(vec_ref, o_ref):
    blk = MOBA_BLOCK
    t = pltpu.roll(jnp.broadcast_to(vec_ref[0] * LOG2E, (blk, 2 * blk)), 0, 1, stride=1, stride_axis=0)
    o_ref[0, 0] = t[:, blk:]
    o_ref[0, 1] = t[:, :blk]


def _moba_bias_tables(rel_bias):
    assert MOBA_BLOCK >= REL_MAX_DIST
    by_dist = rel_bias.astype(f32)[_t5_bucket(jnp.arange(2 * MOBA_BLOCK))]
    far = rel_bias.astype(f32)[_t5_bucket(jnp.array(2 * MOBA_BLOCK))]
    vec = (by_dist - far).T.reshape(N_HEADS, 1, 2 * MOBA_BLOCK)
    tab = pl.pallas_call(
        _moba_bias_kernel, name="moba_bias",
        grid=(N_HEADS,),
        in_specs=[pl.BlockSpec((1, 1, 2 * MOBA_BLOCK), lambda h: (h, 0, 0))],
        out_specs=pl.BlockSpec((1, 2, MOBA_BLOCK, MOBA_BLOCK), lambda h: (h, 0, 0, 0)),
        out_shape=jax.ShapeDtypeStruct((N_HEADS, 2, MOBA_BLOCK, MOBA_BLOCK), f32),
        compiler_params=_cparams(("parallel",), 16),
    )(vec)
    return tab.reshape(N_HEADS // 2, 2, 2, MOBA_BLOCK, MOBA_BLOCK)


def _causal_conv(x, halo, w_ref):
    ts, nh = x.shape[0], halo.shape[0]
    xe = jnp.concatenate([halo, x], axis=0)
    acc = x * w_ref[CONV_K - 1:CONV_K, :]
    for d in range(1, CONV_K):
        acc = acc + xe[nh - d:nh - d + ts] * w_ref[CONV_K - 1 - d:CONV_K - d, :]
    return acc


def _tile_and_halo(x_ref, halo_ref):
    halo = jnp.where(pl.program_id(1) == 0, 0.0, halo_ref[...].astype(f32))
    return x_ref[...].astype(f32), halo


def _tok_spec(rows, nsteps):
    return pl.BlockSpec((rows, MIX_W), lambda b, i: (b * nsteps + i, 0))


def _pair_masks(rows):
    lane = lax.broadcasted_iota(jnp.int32, (rows, LANES), 1)
    return lane < HEAD_DIM, lane >= HEAD_DIM


def _chunk_sum_matrix(chunk):
    r = jnp.arange(CUMSUM_TILE)[:, None]
    c = jnp.arange(CUMSUM_TILE)[None, :]
    same = (r // chunk) == (c // chunk)
    return jnp.concatenate([same & (r >= c), same], axis=0).astype(bf16)


def _chunk_sums(cm_ref, x):
    parts = [_sel_left(cm_ref[...], x[r:r + CUMSUM_TILE]) for r in range(0, x.shape[0], CUMSUM_TILE)]
    return (jnp.concatenate([p[:CUMSUM_TILE] for p in parts], axis=0),
            jnp.concatenate([p[CUMSUM_TILE:] for p in parts], axis=0))


def _row_spec(width, off, nt):
    return pl.BlockSpec((ROW_TILE, width), lambda b, i: (b * nt + i, off // width))


def _halo_spec(width, off, nt):
    per = ROW_TILE // HALO_ROWS
    return pl.BlockSpec((HALO_ROWS, width), lambda b, i: (jnp.maximum((b * nt + i) * per - 1, 0), off // width))


def _small_spec(nt):
    return pl.BlockSpec((ROW_TILE, LANES), lambda b, i: (b * nt + i, 0))


def _const_spec(shape):
    return pl.BlockSpec(shape, lambda b, i: (0,) * len(shape))


def _gdn_prep_kernel(x_ref, halo_ref, sm_ref, cw_ref, alog_ref, dtb_ref, bd_ref, eb_ref, ea_ref, cm_ref,
                     q_ref, k_ref, kb_ref, vb_ref, qd_ref, kbe_ref, kd_ref, gc_ref):
    qkv = _silu(_causal_conv(*_tile_and_halo(x_ref, halo_ref), cw_ref))
    q, k, v = qkv[:, :MIX_W], qkv[:, MIX_W:2 * MIX_W], qkv[:, 2 * MIX_W:]
    bd = bd_ref[...]
    q = q * lax.rsqrt(_sel_right1(q * q, bd) + L2_EPS) * HEAD_DIM ** -0.5
    k = k * lax.rsqrt(_sel_right1(k * k, bd) + L2_EPS)
    sm = sm_ref[...]
    beta = _sel_right1(_sigmoid(sm), eb_ref[...])
    g = -jnp.exp(alog_ref[...]) * _softplus(sm + dtb_ref[...])
    gc, g_end = _chunk_sums(cm_ref, g)
    eg = jnp.exp(_sel_right(gc, ea_ref[...]))
    e_rest = jnp.exp(_sel_right(g_end - gc, ea_ref[...]))
    kb = k * beta
    for ref, val in ((q_ref, q), (k_ref, k), (kb_ref, kb), (vb_ref, v * beta),
                     (qd_ref, q * eg), (kbe_ref, kb * eg), (kd_ref, k * e_rest)):
        ref[...] = val.astype(bf16)
    gc_ref[...] = gc[:, SM_BA:SM_BA + N_HEADS]


def _gdn_prep(proj, small, bsz, seq, conv_w, alog128, dtb128, bd_ones, e_beta, e_ba, cm):
    nt = seq // ROW_TILE
    w3 = 3 * MIX_W
    return pl.pallas_call(
        _gdn_prep_kernel, name="gdn_prep",
        grid=(bsz, nt),
        in_specs=[_row_spec(w3, OFF_BQKV, nt), _halo_spec(w3, OFF_BQKV, nt), _small_spec(nt),
                  _const_spec((CONV_K, w3)), _const_spec((1, LANES)), _const_spec((1, LANES)),
                  _const_spec((MIX_W, MIX_W)), _const_spec((LANES, MIX_W)), _const_spec((LANES, MIX_W)),
                  _const_spec((2 * CUMSUM_TILE, CUMSUM_TILE))],
        out_specs=[_tok_spec(ROW_TILE, nt)] * 7 + [pl.BlockSpec((ROW_TILE, N_HEADS), lambda b, i: (b * nt + i, 0))],
        out_shape=[jax.ShapeDtypeStruct((bsz * seq, MIX_W), bf16)] * 7 + [jax.ShapeDtypeStruct((bsz * seq, N_HEADS), f32)],
        compiler_params=_cparams(("parallel", "parallel"), 40),
    )(proj, proj, small, conv_w, alog128, dtb128, bd_ones, e_beta, e_ba, cm)


def _gdn_chunk_kernel(q_ref, k_ref, kb_ref, vb_ref, qd_ref, kbe_ref, kd_ref, gc_ref, gct_ref, nw_ref, bdm_ref,
                      o_ref, st_scr):
    @pl.when(pl.program_id(1) == 0)
    def _():
        st_scr[...] = jnp.zeros_like(st_scr)

    n = GDN_CHUNK
    cis, prs = range(CHUNKS_PER_STEP), range(N_HEADS // 2)
    tiles = [(ci, p) for ci in cis for p in prs]
    ch = [(ci, p, hh) for ci, p in tiles for hh in (0, 1)]
    blk = lambda ref, ci, p: ref[ci * n:(ci + 1) * n, p * LANES:(p + 1) * LANES]
    r, c = _iota2((n, n))
    row2, col2 = _iota2((2 * n, 2 * n))
    same_head = (row2 < n) == (col2 < n)
    hm2 = _pair_masks(2 * n)
    keep0 = _pair_masks(n)[0]
    pick = lambda x0, x1: jnp.where(keep0, x0, x1)
    zero = jnp.zeros((), bf16)
    gc_all = [gc_ref[0, ci] for ci in cis]
    gct_all = [gct_ref[0, ci] for ci in cis]
    gcol = lambda ci, p, hh: gc_all[ci][:, 2 * p + hh:2 * p + hh + 1]
    decay = {(ci, p, hh): jnp.exp(jnp.where(r >= c, gcol(ci, p, hh) - gct_all[ci][2 * p + hh:2 * p + hh + 1, :], NEG))
             for ci, p, hh in ch}
    lhs = {x: jnp.concatenate([blk(kb_ref, *x), blk(q_ref, *x)], axis=0) for x in tiles}
    gram = {(ci, p, hh): _mm_nt(jnp.where(hm2[hh], lhs[ci, p], zero), blk(k_ref, ci, p)) for ci, p, hh in ch}
    t = dict(zip(ch, _unit_lower_inverses([jnp.where(r > c, gram[x][:n] * decay[x], 0.0) for x in ch])))
    u = {(ci, p): pick(_mm(t[ci, p, 0], blk(vb_ref, ci, p)), _mm(t[ci, p, 1], blk(vb_ref, ci, p))) for ci, p in tiles}
    w = {(ci, p): pick(_mm(t[ci, p, 0], blk(kbe_ref, ci, p)), _mm(t[ci, p, 1], blk(kbe_ref, ci, p)))
         for ci, p in tiles}
    a_in = {x: gram[x][n:] * decay[x] for x in ch}
    st = [st_scr[p] for p in prs]
    for ci in cis:
        ws = [_mm(jnp.concatenate([w[ci, p].astype(bf16), blk(qd_ref, ci, p)], axis=0), st[p]) for p in prs]
        v_new = [u[ci, p] - ws[p][:n] for p in prs]
        o = [ws[p][n:] + pick(_mm(a_in[ci, p, 0], v_new[p]), _mm(a_in[ci, p, 1], v_new[p])) for p in prs]
        upd = [jnp.where(same_head, _mm_tn(blk(kd_ref, ci, p), v_new[p]), 0.0) for p in prs]
        for p in prs:
            ms = _sel_right1(o[p] * o[p], bdm_ref[...])
            o_ref[ci * n:(ci + 1) * n, p * LANES:(p + 1) * LANES] = o[p] * lax.rsqrt(ms + RMS_EPS) * nw_ref[...]
        g_end = [jnp.exp(jnp.where(keep0[0:1], gcol(ci, p, 0)[n - 1:n], gcol(ci, p, 1)[n - 1:n])) for p in prs]
        st = [st[p] * g_end[p] + upd[p] for p in prs]
    for p in prs:
        st_scr[p] = st[p]


def _gdn_chunk(bsz, seq, q, k, kb, vb, qd, kbe, kd, gc, norm_w2, bd_mean2):
    n = GDN_CHUNK
    nc = seq // n
    per = CHUNKS_PER_STEP
    gc4 = gc.reshape(bsz, nc, n, N_HEADS)
    gct4 = jnp.swapaxes(gc4, 2, 3)
    spec = _tok_spec(per * n, nc // per)
    return pl.pallas_call(
        _gdn_chunk_kernel, name="gdn_chunk",
        grid=(bsz, nc // per),
        in_specs=[spec] * 7 + [pl.BlockSpec((1, per, n, N_HEADS), lambda b, i: (b, i, 0, 0)),
                               pl.BlockSpec((1, per, N_HEADS, n), lambda b, i: (b, i, 0, 0)),
                               _const_spec((1, LANES)), _const_spec((LANES, LANES))],
        out_specs=spec,
        out_shape=jax.ShapeDtypeStruct((bsz * seq, MIX_W), f32),
        scratch_shapes=[pltpu.VMEM((N_HEADS // 2, 2 * HEAD_DIM, 2 * HEAD_DIM), f32)],
        compiler_params=_cparams(("parallel", "arbitrary"), 32),
    )(q, k, kb, vb, qd, kbe, kd, gc4, gct4, norm_w2, bd_mean2)


def _rwkv_prep_kernel(has_vres, *refs):
    if has_vres:
        (c_ref, halo_ref, mu_ref, w0_ref, wup_ref, a0_ref, aup_ref, gup_ref, kk_ref, ka_ref, bd_ref,
         rk_ref, cm_ref, vf_ref, v0_ref, vdn_ref, vup_ref,
         rt_ref, at_ref, bt_ref, kt_ref, v_ref, pe_ref, gout_ref, bonus_ref) = refs
    else:
        (c_ref, halo_ref, mu_ref, w0_ref, wup_ref, a0_ref, aup_ref, gup_ref, kk_ref, ka_ref, bd_ref,
         rk_ref, cm_ref,
         rt_ref, at_ref, bt_ref, kt_ref, v_ref, pe_ref, gout_ref, bonus_ref, cv_ref) = refs
    c, halo = _tile_and_halo(c_ref, halo_ref)
    prev = jnp.concatenate([halo[HALO_ROWS - 1:], c[:-1]], axis=0)
    c = c + (prev - c) * mu_ref[...]
    c_r, c_k, c_v = c[:, :MIX_W], c[:, MIX_W:2 * MIX_W], c[:, 2 * MIX_W:3 * MIX_W]
    c_wd = c[:, 3 * MIX_W:3 * MIX_W + 64]
    c_ad = c[:, 3 * MIX_W + 64:3 * MIX_W + 128]
    c_gd = c[:, 3 * MIX_W + 128:]
    w_log = -_softplus(-(w0_ref[...] + _mm(jnp.tanh(c_wd), wup_ref[...]))) - 0.5
    a_in = _sigmoid(a0_ref[...] + _mm(c_ad, aup_ref[...]))
    gout_ref[...] = _mm(_sigmoid(c_gd), gup_ref[...])
    if has_vres:
        lam = _sigmoid(v0_ref[...] + _mm(_mm(c_v, vdn_ref[...]), vup_ref[...]))
        v_r = c_v + (vf_ref[...] - c_v) * lam
    else:
        v_r = c_v
        cv_ref[...] = c_v
    bd = bd_ref[...]
    kk = c_k * kk_ref[...]
    kk = kk * lax.rsqrt(_sel_right1(kk * kk, bd) + L2_EPS)
    k_r = c_k * (1.0 + (a_in - 1.0) * ka_ref[...])
    b = kk * a_in
    bonus_ref[...] = _sel_right1(c_r * k_r * rk_ref[...], bd) * v_r
    lc, lc_end = _chunk_sums(cm_ref, -jnp.exp(w_log))
    e_neg = jnp.exp(-lc)
    rt_ref[...] = (c_r * jnp.exp(lc)).astype(bf16)
    at_ref[...] = (-kk * jnp.exp(lc + jnp.exp(w_log))).astype(bf16)
    bt_ref[...] = (b * e_neg).astype(bf16)
    kt_ref[...] = (k_r * e_neg).astype(bf16)
    v_ref[...] = v_r.astype(bf16)
    pe_ref[...] = jnp.exp(lc_end)


def _rwkv_prep(proj, bsz, seq, mu, w0, w_up, a0, a_up, g_up, k_k, k_a, r_k, bd_ones, cm, vres):
    nt = seq // ROW_TILE
    wc = 3 * MIX_W + 256
    std = pl.BlockSpec((ROW_TILE, MIX_W), lambda b, i: (b * nt + i, 0))
    std_shape = jax.ShapeDtypeStruct((bsz * seq, MIX_W), f32)
    in_specs = [_row_spec(wc, OFF_C, nt), _halo_spec(wc, OFF_C, nt), _const_spec((1, wc)),
                _const_spec((1, MIX_W)), _const_spec((64, MIX_W)), _const_spec((1, MIX_W)), _const_spec((64, MIX_W)),
                _const_spec((128, MIX_W)), _const_spec((1, MIX_W)), _const_spec((1, MIX_W)), _const_spec((MIX_W, MIX_W)),
                _const_spec((1, MIX_W)), _const_spec((2 * CUMSUM_TILE, CUMSUM_TILE))]
    args = [proj, proj, mu, w0, w_up, a0, a_up, g_up, k_k, k_a, bd_ones, r_k, cm]
    out_specs = [std] * 8
    out_shape = [jax.ShapeDtypeStruct((bsz * seq, MIX_W), bf16)] * 5 + [std_shape] * 3
    if vres is not None:
        v_first, v0, v_down, v_up = vres
        in_specs += [std, _const_spec((1, MIX_W)), _const_spec(v_down.shape), _const_spec(v_up.shape)]
        args += [v_first, v0, v_down, v_up]
    else:
        out_specs.append(std)
        out_shape.append(std_shape)
    return pl.pallas_call(
        functools.partial(_rwkv_prep_kernel, vres is not None), name="rwkv_prep",
        grid=(bsz, nt), in_specs=in_specs, out_specs=out_specs, out_shape=out_shape,
        compiler_params=_cparams(("parallel", "parallel"), 40),
    )(*args)


def _rwkv_chunk_kernel(rt_ref, at_ref, bt_ref, kt_ref, v_ref, pe_ref, o_ref, st_scr):
    @pl.when(pl.program_id(1) == 0)
    def _():
        st_scr[...] = jnp.zeros_like(st_scr)

    n = RWKV_CHUNK
    cis, prs = range(CHUNKS_PER_STEP), range(N_HEADS // 2)
    tiles = [(ci, p) for ci in cis for p in prs]
    ch = [(ci, p, hh) for ci, p in tiles for hh in (0, 1)]
    blk = lambda ref, ci, p: ref[ci * n:(ci + 1) * n, p * LANES:(p + 1) * LANES]
    row, col = _iota2((2 * n, 2 * n))
    rr, cc = row & (n - 1), col & (n - 1)
    mask = rr + jnp.where(row < n, 0, 1) > cc
    same_head = (row < n) == (col < n)
    hm2 = _pair_masks(2 * n)
    keep0 = _pair_masks(n)[0]
    pick = lambda x0, x1: jnp.where(keep0, x0, x1)
    zero = jnp.zeros((), bf16)
    lhs = {x: jnp.concatenate([blk(at_ref, *x), blk(rt_ref, *x)], axis=0) for x in tiles}
    rhs = {x: jnp.concatenate([blk(bt_ref, *x), blk(kt_ref, *x)], axis=0) for x in tiles}
    gm = {(ci, p, hh): jnp.where(mask, _mm_nt(jnp.where(hm2[hh], lhs[ci, p], zero), rhs[ci, p]), 0.0)
          for ci, p, hh in ch}
    t = dict(zip(ch, _unit_lower_inverses([-gm[x][:n, :n] for x in ch])))
    v = {x: blk(v_ref, *x) for x in tiles}
    zv = {x: jnp.concatenate([jnp.zeros_like(v[x]), v[x]], axis=0) for x in tiles}
    makv = {(ci, p): pick(_mm(gm[ci, p, 0][:n], zv[ci, p]), _mm(gm[ci, p, 1][:n], zv[ci, p])) for ci, p in tiles}
    st = [st_scr[p] for p in prs]
    for ci in cis:
        ah = [_mm_nt(lhs[ci, p], st[p]) for p in prs]
        rhs_u = [ah[p][:n] + makv[ci, p] for p in prs]
        u = [pick(_mm(t[ci, p, 0], rhs_u[p]), _mm(t[ci, p, 1], rhs_u[p])) for p in prs]
        uv = [jnp.concatenate([u[p].astype(bf16), v[ci, p]], axis=0) for p in prs]
        o = [ah[p][n:] + pick(_mm(gm[ci, p, 0][n:], uv[p]), _mm(gm[ci, p, 1][n:], uv[p])) for p in prs]
        pe = [pe_ref[ci * n:ci * n + 1, p * LANES:(p + 1) * LANES] for p in prs]
        upd = [_mm_tn(uv[p], rhs[ci, p].astype(f32) * pe[p]) for p in prs]
        for p in prs:
            o_ref[ci * n:(ci + 1) * n, p * LANES:(p + 1) * LANES] = o[p]
        st = [st[p] * pe[p] + jnp.where(same_head, upd[p], 0.0) for p in prs]
    for p in prs:
        st_scr[p] = st[p]


def _rwkv_chunk(bsz, seq, rt, at, bt, kt, v, pe):
    n = RWKV_CHUNK * CHUNKS_PER_STEP
    spec = _tok_spec(n, seq // n)
    return pl.pallas_call(
        _rwkv_chunk_kernel, name="rwkv_chunk",
        grid=(bsz, seq // n),
        in_specs=[spec] * 6,
        out_specs=spec,
        out_shape=jax.ShapeDtypeStruct((bsz * seq, MIX_W), f32),
        scratch_shapes=[pltpu.VMEM((N_HEADS // 2, 2 * HEAD_DIM, 2 * HEAD_DIM), f32)],
        compiler_params=_cparams(("parallel", "arbitrary"), 32),
    )(rt, at, bt, kt, v, pe)


def _ssd_prep_kernel(x_ref, halo_ref, sm_ref, cw_ref, cb_ref, alog_ref, dtb_ref, edt_ref, cm_ref,
                     xdt_ref, x_out_ref, bc_ref, acs_ref):
    xbc = _silu(_causal_conv(*_tile_and_halo(x_ref, halo_ref), cw_ref) + cb_ref[...])
    m_x = xbc[:, :MIX_W]
    dt = _softplus(sm_ref[...] + dtb_ref[...])
    xdt_ref[...] = (m_x * _sel_right1(dt, edt_ref[...])).astype(xdt_ref.dtype)
    x_out_ref[...] = m_x
    bc_ref[...] = xbc[:, MIX_W:].astype(bc_ref.dtype)
    acs, _ = _chunk_sums(cm_ref, dt * -jnp.exp(alog_ref[...]))
    acs_ref[...] = acs[:, SM_DT:SM_DT + N_HEADS]


def _ssd_prep(proj, small, bsz, seq, conv_w, conv_b, alog128, dtb128, e_dt, cm):
    nt = seq // ROW_TILE
    wx = MIX_W + 4 * SSM_STATE
    return pl.pallas_call(
        _ssd_prep_kernel, name="ssd_prep",
        grid=(bsz, nt),
        in_specs=[_row_spec(wx, OFF_DXBC, nt), _halo_spec(wx, OFF_DXBC, nt), _small_spec(nt),
                  _const_spec((CONV_K, wx)), _const_spec((1, wx)), _const_spec((1, LANES)), _const_spec((1, LANES)),
                  _const_spec((LANES, MIX_W)), _const_spec((2 * CUMSUM_TILE, CUMSUM_TILE))],
        out_specs=[_tok_spec(ROW_TILE, nt)] * 2 + [pl.BlockSpec((ROW_TILE, 4 * SSM_STATE), lambda b, i: (b * nt + i, 0)),
                                                   pl.BlockSpec((ROW_TILE, N_HEADS), lambda b, i: (b * nt + i, 0))],
        out_shape=[jax.ShapeDtypeStruct((bsz * seq, MIX_W), bf16), jax.ShapeDtypeStruct((bsz * seq, MIX_W), f32),
                   jax.ShapeDtypeStruct((bsz * seq, 4 * SSM_STATE), bf16),
                   jax.ShapeDtypeStruct((bsz * seq, N_HEADS), f32)],
        compiler_params=_cparams(("parallel", "parallel"), 40),
    )(proj, proj, small, conv_w, conv_b, alog128, dtb128, e_dt, cm)


def _ssd_chunk_kernel(xdt_ref, x_ref, bc_ref, a_ref, at_ref, dvec_ref, o_ref, st_scr):
    @pl.when(pl.program_id(1) == 0)
    def _():
        st_scr[...] = jnp.zeros_like(st_scr)

    n = SSD_CHUNK
    cis, prs = range(SSD_CHUNKS_PER_STEP), range(N_HEADS // 2)
    grp = lambda p: (2 * p) // (N_HEADS // 2)
    tiles = [(ci, p) for ci in cis for p in prs]
    ch = [(ci, p, hh) for ci, p in tiles for hh in (0, 1)]
    rows = lambda ci: slice(ci * n, (ci + 1) * n)
    lanes = lambda p: slice(p * LANES, (p + 1) * LANES)
    r, c = _iota2((n, n))
    keep0 = _pair_masks(n)[0]
    pick = lambda x0, x1: jnp.where(keep0, x0, x1)
    b_g = {(ci, g): bc_ref[rows(ci), g * SSM_STATE:(g + 1) * SSM_STATE] for ci in cis for g in range(2)}
    c_g = {(ci, g): bc_ref[rows(ci), (2 + g) * SSM_STATE:(3 + g) * SSM_STATE] for ci in cis for g in range(2)}
    cb = {x: _mm_nt(c_g[x], b_g[x]) for x in b_g}
    col = lambda ci, p, hh: a_ref[0, ci][:, 2 * p + hh:2 * p + hh + 1]
    ac = {x: col(*x) for x in ch}
    a_last = {x: col(*x)[n - 1:n] for x in ch}
    lmat = {(ci, p, hh): jnp.exp(jnp.where(r >= c, ac[ci, p, hh] - at_ref[0, ci][2 * p + hh:2 * p + hh + 1, :], NEG))
            for ci, p, hh in ch}
    xg = {(ci, p): xdt_ref[rows(ci), lanes(p)] for ci, p in tiles}
    y_diag = {(ci, p): pick(*[_mm(cb[ci, grp(p)] * lmat[ci, p, hh], xg[ci, p]) for hh in (0, 1)]) for ci, p in tiles}
    upd = {(ci, p): pick(*[_mm_tn(b_g[ci, grp(p)].astype(f32) * jnp.exp(a_last[ci, p, hh] - ac[ci, p, hh]), xg[ci, p])
                           for hh in (0, 1)]) for ci, p in tiles}
    c_in = {(ci, p, hh): c_g[ci, grp(p)].astype(f32) * jnp.exp(ac[ci, p, hh]) for ci, p, hh in ch}
    st = [st_scr[p] for p in prs]
    for ci in cis:
        y_off = [pick(_mm(c_in[ci, p, 0], st[p]), _mm(c_in[ci, p, 1], st[p])) for p in prs]
        for p in prs:
            o_ref[rows(ci), lanes(p)] = y_diag[ci, p] + y_off[p] + x_ref[rows(ci), lanes(p)] * dvec_ref[:, lanes(p)]
        st = [st[p] * jnp.exp(jnp.where(keep0[0:1], a_last[ci, p, 0], a_last[ci, p, 1])) + upd[ci, p] for p in prs]
    for p in prs:
        st_scr[p] = st[p]


def _ssd_chunk(bsz, seq, xdt, x, bc, acs, dvec):
    n = SSD_CHUNK
    nc = seq // n
    per = SSD_CHUNKS_PER_STEP
    a4 = acs.reshape(bsz, nc, n, N_HEADS)
    at4 = jnp.swapaxes(a4, 2, 3)
    return pl.pallas_call(
        _ssd_chunk_kernel, name="ssd_chunk",
        grid=(bsz, nc // per),
        in_specs=[_tok_spec(per * n, nc // per)] * 2
        + [pl.BlockSpec((per * n, 4 * SSM_STATE), lambda b, i: (b * (nc // per) + i, 0)),
           pl.BlockSpec((1, per, n, N_HEADS), lambda b, i: (b, i, 0, 0)),
           pl.BlockSpec((1, per, N_HEADS, n), lambda b, i: (b, i, 0, 0)),
           _const_spec((1, MIX_W))],
        out_specs=_tok_spec(per * n, nc // per),
        out_shape=jax.ShapeDtypeStruct((bsz * seq, MIX_W), f32),
        scratch_shapes=[pltpu.VMEM((N_HEADS // 2, SSM_STATE, 2 * HEAD_DIM), f32)],
        compiler_params=_cparams(("parallel", "arbitrary"), 32),
    )(xdt, x, bc, a4, at4, dvec)


def _merge_kernel(x_ref, ya_ref, ob_ref, bz_ref, wkv_ref, bonus_ref, gout_ref, lnw_ref, lnb_ref, bdm_ref,
                  yd_ref, dz_ref, mnw_ref, g0_ref, g1_ref, g2_ref, g3_ref, wb_ref, wo_ref, o_ref):
    def gated(n, y, g_ref):
        return _sigmoid(g_ref[...].astype(f32)) * _mm(y, wb_ref[n])

    acc = gated(0, ya_ref[0], g0_ref)
    acc = acc + gated(1, ob_ref[...] * _silu(bz_ref[...].astype(f32)), g1_ref)

    w = wkv_ref[...]
    d = w - _sel_right1(w, bdm_ref[...])
    wkv_ln = d * lax.rsqrt(_sel_right1(d * d, bdm_ref[...]) + RWKV_LN_EPS)
    y_c = (wkv_ln * lnw_ref[...] + lnb_ref[...] + bonus_ref[...]) * gout_ref[...]
    acc = acc + gated(2, y_c, g2_ref)

    yz = yd_ref[...] * _silu(dz_ref[...].astype(f32))
    half = MIX_W // 2
    y_d = jnp.concatenate(
        [yz[:, s:s + half] * lax.rsqrt(jnp.mean(jnp.square(yz[:, s:s + half]), axis=-1, keepdims=True) + RMS_EPS)
         for s in (0, half)], axis=-1) * mnw_ref[...]
    acc = acc + gated(3, y_d, g3_ref)
    o_ref[...] = x_ref[...] + _mm(acc, wo_ref[...])


def _merge(x2, proj, bsz, seq, ya, ob, wkv, bonus, gout, lnw, lnb, bd_mean, yd, mnw, wb, wo):
    nt = seq // ROW_TILE
    std = lambda w: pl.BlockSpec((ROW_TILE, w), lambda b, i: (b * nt + i, 0))
    gate = lambda n: _row_spec(D_MODEL, OFF_GATES + n * D_MODEL, nt)
    return pl.pallas_call(
        _merge_kernel, name="merge",
        grid=(bsz, nt),
        in_specs=[std(D_MODEL), pl.BlockSpec((1, ROW_TILE, MIX_W), lambda b, i: (b, i, 0)),
                  std(MIX_W), _row_spec(MIX_W, OFF_BZ, nt),
                  std(MIX_W), std(MIX_W), std(MIX_W), _const_spec((1, MIX_W)), _const_spec((1, MIX_W)),
                  _const_spec((MIX_W, MIX_W)),
                  std(MIX_W), _row_spec(MIX_W, OFF_DZ, nt), _const_spec((1, MIX_W)),
                  gate(0), gate(1), gate(2), gate(3),
                  _const_spec((4, MIX_W, D_MODEL)), _const_spec((D_MODEL, D_MODEL))],
        out_specs=std(D_MODEL),
        out_shape=jax.ShapeDtypeStruct((bsz * seq, D_MODEL), f32),
        compiler_params=_cparams(("parallel", "parallel"), 48),
    )(x2, ya, ob, proj, wkv, bonus, gout, lnw, lnb, bd_mean, yd, proj, mnw, proj, proj, proj, proj, wb, wo)


def _lane_vec(vals, off):
    return jnp.zeros((1, LANES), f32).at[0, off:off + vals.shape[0]].set(vals)


def _head_expand(off):
    n = jnp.arange(LANES)[:, None]
    c = jnp.arange(MIX_W)[None, :]
    return (n - off == c // HEAD_DIM).astype(bf16)


def _pack_w_in(w):
    pad = lambda n: jnp.zeros((w.shape[0], n), w.dtype)
    cols = [w[:, W_A:W_BZ],
            w[:, W_BZ:W_BBETA],
            w[:, W_C:W_DZ],
            w[:, W_BBETA:W_C], w[:, W_DDT:W_GATES], pad(2 * LANES - 3 * N_HEADS),
            w[:, W_DZ:W_DXBC], w[:, W_DXBC:W_DDT], w[:, W_GATES:]]
    out = jnp.concatenate(cols, axis=1).astype(bf16)
    assert out.shape[1] == N_PROJ
    return out


def kernel(x, rel_bias, norm1_w, w_in, moba_q_norm, moba_k_norm, gdn_conv_w, gdn_A_log, gdn_dt_bias, gdn_norm_w, rwkv_mu, rwkv_w0, rwkv_w_up, rwkv_a0, rwkv_a_up, rwkv_g_up, rwkv_k_k, rwkv_k_a, rwkv_r_k, rwkv_v0, rwkv_v_down, rwkv_v_up, rwkv_ln_w, rwkv_ln_b, mamba_conv_w, mamba_conv_b, mamba_dt_bias, mamba_A_log, mamba_D, mamba_norm_w, w_branch, w_out, norm2_w, ffn_w_in, ffn_w_down):
    bsz, seq, d = x.shape
    depth = w_in.shape[0]
    assert d == D_MODEL and (bsz * seq) % IN_PROJ_TILE_M == 0 and seq % MM_TILE_M == 0
    x2 = x.reshape(bsz * seq, d)
    row = lambda v: v.reshape(1, -1).astype(f32)

    hid = jnp.arange(MIX_W) // HEAD_DIM
    bd_ones = (hid[:, None] == hid[None, :]).astype(bf16)
    bd_mean = (bd_ones.astype(f32) / HEAD_DIM).astype(bf16)
    e_beta, e_ba, e_dt = _head_expand(SM_BETA), _head_expand(SM_BA), _head_expand(SM_DT)
    cm64, cm128 = _chunk_sum_matrix(GDN_CHUNK), _chunk_sum_matrix(SSD_CHUNK)
    assert GDN_CHUNK == RWKV_CHUNK
    tab = _moba_bias_tables(rel_bias)
    v_first = None
    for i in range(depth):
        proj, small = _in_proj(x2, row(norm1_w[i]), _pack_w_in(w_in[i]))

        qaug, kaug, v_a = _moba_prep(proj, bsz, seq, row(jnp.tile(moba_q_norm[i], N_HEADS)),
                                     row(jnp.tile(moba_k_norm[i], N_HEADS)), bd_mean)
        y_a = _moba_attn(qaug, kaug, v_a, tab)

        gdn_in = _gdn_prep(proj, small, bsz, seq, gdn_conv_w[i], _lane_vec(gdn_A_log[i], SM_BA),
                           _lane_vec(gdn_dt_bias[i], SM_BA), bd_ones, e_beta, e_ba, cm64)
        o_b = _gdn_chunk(bsz, seq, *gdn_in, row(jnp.tile(gdn_norm_w[i], 2)), bd_mean[:LANES, :LANES])

        vres = None if i == 0 else (v_first, row(rwkv_v0[i - 1]), rwkv_v_down[i - 1].astype(bf16),
                                    rwkv_v_up[i - 1].astype(bf16))
        outs = _rwkv_prep(proj, bsz, seq, row(rwkv_mu[i]), row(rwkv_w0[i]), rwkv_w_up[i].astype(bf16),
                          row(rwkv_a0[i]), rwkv_a_up[i].astype(bf16), rwkv_g_up[i].astype(bf16),
                          row(rwkv_k_k[i]), row(rwkv_k_a[i]), row(rwkv_r_k[i]), bd_ones, cm64, vres)
        g_out, bonus = outs[6], outs[7]
        if i == 0:
            v_first = outs[8]
        wkv = _rwkv_chunk(bsz, seq, *outs[:6])

        xdt, x_d, bc, acs = _ssd_prep(proj, small, bsz, seq, mamba_conv_w[i], row(mamba_conv_b[i]),
                                      _lane_vec(mamba_A_log[i], SM_DT), _lane_vec(mamba_dt_bias[i], SM_DT), e_dt, cm128)
        y_d = _ssd_chunk(bsz, seq, xdt, x_d, bc, acs, row(jnp.repeat(mamba_D[i], HEAD_DIM)))

        x2 = _merge(x2, proj, bsz, seq, y_a, o_b, wkv, bonus, g_out,
                    row(rwkv_ln_w[i]), row(rwkv_ln_b[i]), bd_mean, y_d, row(mamba_norm_w[i]),
                    w_branch[i].astype(bf16), w_out[i].astype(bf16))

        x2 = _ffn(x2, row(norm2_w[i]), ffn_w_in[i].astype(bf16), ffn_w_down[i].astype(bf16))
    return x2.reshape(bsz, seq, d)
```

```python
import functools
import math

import jax
import jax.numpy as jnp
from jax import lax
from jax.experimental import pallas as pl
from jax.experimental.pallas import tpu as pltpu

f32, bf16 = jnp.float32, jnp.bfloat16
HI = lax.Precision.HIGHEST

D_MODEL = 1024
N_HEADS = 8
HEAD_DIM = 64
MIX_W = N_HEADS * HEAD_DIM
RMS_EPS = 1e-6
L2_EPS = 1e-6
CONV_K = 4
MOBA_BLOCK = 256
MOBA_TOPK = 3
MOBA_PAIRS_PER_STEP = 4
MOBA_KV_PER_GROUP = 2
REL_BUCKETS = 32
REL_MAX_DIST = 128
GDN_CHUNK = 64
RWKV_CHUNK = 64
CHUNKS_PER_STEP = 4
RWKV_LN_EPS = 64e-5
SSM_STATE = 128
SSD_CHUNK = 128
SSD_CHUNKS_PER_STEP = 4
FFN_HIDDEN_SPLITS = 1
NEG = -1e30
LOG2E = math.log2(math.e)

LANES = 128
BF16_SUBLANES = 16
VT_ROWS = HEAD_DIM + BF16_SUBLANES

OFF_A, OFF_BQKV, OFF_BZ, OFF_C, OFF_SMALL, OFF_DZ, OFF_DXBC, OFF_GATES = 0, 1536, 3072, 3584, 5376, 5632, 6144, 7168
N_PROJ = OFF_GATES + 4 * D_MODEL
SM_BETA, SM_BA, SM_DT = 0, 8, 16
W_A, W_BQKV, W_BZ, W_BBETA, W_BA, W_C, W_DZ, W_DXBC, W_DDT, W_GATES = 0, 1536, 3072, 3584, 3592, 3600, 5392, 5904, 6928, 6936

ROW_TILE = 512
CUMSUM_TILE = 256
MM_TILE_M = 512
IN_PROJ_TILE_M, IN_PROJ_TILE_N = 1024, 1024
HALO_ROWS = BF16_SUBLANES


def _cparams(sem, vmem_mb):
    return pltpu.CompilerParams(dimension_semantics=sem, vmem_limit_bytes=vmem_mb * 1024 * 1024)


def _mm(a, b):
    return jnp.dot(a.astype(bf16), b.astype(bf16), preferred_element_type=f32)


def _mm_nt(a, b):
    return lax.dot_general(a.astype(bf16), b.astype(bf16), (((1,), (1,)), ((), ())), preferred_element_type=f32)


def _mm_tn(a, b):
    return lax.dot_general(a.astype(bf16), b.astype(bf16), (((0,), (0,)), ((), ())), preferred_element_type=f32)


def _mm_hi(a, b):
    return jnp.dot(a, b, precision=HI, preferred_element_type=f32)


def _softplus(x):
    return jnp.maximum(x, 0.0) + jnp.log1p(jnp.exp(-jnp.abs(x)))


def _sigmoid(x):
    return jax.nn.sigmoid(x)


def _silu(x):
    return x * _sigmoid(x)


def _iota2(shape):
    return lax.broadcasted_iota(jnp.int32, shape, 0), lax.broadcasted_iota(jnp.int32, shape, 1)


def _split3(x):
    hi = x.astype(bf16)
    r1 = x - hi.astype(f32)
    mid = r1.astype(bf16)
    return hi, mid, (r1 - mid.astype(f32)).astype(bf16)


def _sel_left(m01, x):
    return sum(jnp.dot(m01, p, preferred_element_type=f32) for p in _split3(x))


def _sel_right(x, m01):
    return sum(jnp.dot(p, m01, preferred_element_type=f32) for p in _split3(x))


def _sel_right1(x, m01):
    return jnp.dot(x.astype(bf16), m01, preferred_element_type=f32)


def _unit_lower_inverses(l_list):
    n = l_list[0].shape[0]
    r, c = _iota2((n, n))
    eye = jnp.where(r == c, 1.0, 0.0)
    ts = [eye for _ in l_list]
    for ls in range(n.bit_length() - 1):
        m = ((r >> (ls + 1)) == (c >> (ls + 1))) & (((r >> ls) & 1) == 1) & (((c >> ls) & 1) == 0)
        lms = [jnp.where(m, l, 0.0) for l in l_list]
        if ls == 0:
            ts = [t - lm for t, lm in zip(ts, lms)]
        else:
            tl = [_mm(t, lm) for t, lm in zip(ts, lms)]
            ts = [t - _mm(x, t) for t, x in zip(ts, tl)]
    return ts


def _in_proj_kernel(x_ref, nw_ref, w_ref, o_ref, sm_ref, h_scr):
    j = pl.program_id(1)

    @pl.when(j == 0)
    def _():
        x = x_ref[...]
        y = x * lax.rsqrt(jnp.mean(x * x, axis=-1, keepdims=True) + RMS_EPS)
        h_scr[...] = (y * nw_ref[...]).astype(bf16)

    tn = o_ref.shape[1]
    w = w_ref[:, pl.ds(pl.multiple_of(j * tn, tn), tn)]
    acc = jnp.dot(h_scr[...], w, preferred_element_type=f32)
    o_ref[...] = acc.astype(o_ref.dtype)

    @pl.when(j == OFF_SMALL // tn)
    def _():
        sm_ref[...] = acc[:, OFF_SMALL % tn:OFF_SMALL % tn + LANES]


def _in_proj(x2, nw, w):
    t, d = x2.shape
    n = w.shape[1]
    tm, tn = IN_PROJ_TILE_M, IN_PROJ_TILE_N
    return pl.pallas_call(
        _in_proj_kernel, name="in_proj",
        grid=(t // tm, n // tn),
        in_specs=[pl.BlockSpec((tm, d), lambda i, j: (i, 0)),
                  pl.BlockSpec((1, d), lambda i, j: (0, 0)),
                  pl.BlockSpec((d, n), lambda i, j: (0, 0), pipeline_mode=pl.Buffered(1))],
        out_specs=[pl.BlockSpec((tm, tn), lambda i, j: (i, j)),
                   pl.BlockSpec((tm, LANES), lambda i, j: (i, 0))],
        out_shape=[jax.ShapeDtypeStruct((t, n), bf16), jax.ShapeDtypeStruct((t, LANES), f32)],
        scratch_shapes=[pltpu.VMEM((tm, d), bf16)],
        compiler_params=_cparams(("parallel", "arbitrary"), 48),
    )(x2, nw, w)


def _ffn_kernel(x_ref, nw_ref, wi_ref, wd_ref, o_ref):
    x = x_ref[...]
    h = (x * lax.rsqrt(jnp.mean(x * x, axis=-1, keepdims=True) + RMS_EPS) * nw_ref[...]).astype(bf16)
    n = wd_ref.shape[0]
    step = n // FFN_HIDDEN_SPLITS
    acc = x
    for c0 in range(0, n, step):
        g = jnp.dot(h, wi_ref[:, c0:c0 + step], preferred_element_type=f32)
        u = jnp.dot(h, wi_ref[:, n + c0:n + c0 + step], preferred_element_type=f32)
        acc = acc + jnp.dot((_silu(g) * u).astype(bf16), wd_ref[c0:c0 + step, :], preferred_element_type=f32)
    o_ref[...] = acc


def _ffn(x2, nw, wi, wd):
    t, d = x2.shape
    n = wd.shape[0]
    tm = MM_TILE_M
    resident = lambda shape: pl.BlockSpec(shape, lambda i: (0, 0), pipeline_mode=pl.Buffered(1))
    return pl.pallas_call(
        _ffn_kernel, name="ffn",
        grid=(t // tm,),
        in_specs=[pl.BlockSpec((tm, d), lambda i: (i, 0)),
                  pl.BlockSpec((1, d), lambda i: (0, 0)),
                  resident((d, 2 * n)), resident((n, d))],
        out_specs=pl.BlockSpec((tm, d), lambda i: (i, 0)),
        out_shape=jax.ShapeDtypeStruct((t, d), f32),
        compiler_params=_cparams(("parallel",), 48),
    )(x2, nw, wi, wd)


def _top3_bias(gate_t, n_past):
    row = lax.broadcasted_iota(jnp.int32, gate_t.shape, 0)
    g = jnp.where(row < n_past, gate_t, -jnp.inf)
    sel = jnp.zeros(gate_t.shape, jnp.bool_)
    for _ in range(MOBA_TOPK):
        m = jnp.max(g, axis=0, keepdims=True)
        idx = jnp.min(jnp.where(g == m, row, gate_t.shape[0]), axis=0, keepdims=True)
        pick = row == idx
        sel = sel | (pick & (m > -jnp.inf))
        g = jnp.where(pick, -jnp.inf, g)
    return jnp.where(sel, 0.0, NEG)


def _moba_prep_kernel(a_ref, qw_ref, kw_ref, bd_ref, qaugt_ref, kaug_ref, vaugt_ref, kmean_scr):
    i = pl.program_id(1)

    @pl.when(i == 0)
    def _():
        kmean_scr[...] = jnp.zeros_like(kmean_scr)

    a = a_ref[...].astype(f32)
    q, k, v = a[:, :MIX_W], a[:, MIX_W:2 * MIX_W], a[:, 2 * MIX_W:]
    bd = bd_ref[...]
    qn = q * lax.rsqrt(_sel_right1(q * q, bd) + RMS_EPS) * qw_ref[...]
    kn = k * lax.rsqrt(_sel_right1(k * k, bd) + RMS_EPS) * kw_ref[...]
    lane = lax.broadcasted_iota(jnp.int32, (MOBA_BLOCK, LANES), 1)
    onehot = jnp.where(lane == i, 1.0, 0.0).astype(bf16)
    ones_row = jnp.where(lax.broadcasted_iota(jnp.int32, (VT_ROWS - HEAD_DIM, MOBA_BLOCK), 0) == 0, 1.0, 0.0)
    kmean = kmean_scr[...]
    nbp = kmean.shape[0]
    dim = lax.broadcasted_iota(jnp.int32, (LANES, MOBA_BLOCK), 0)
    sel_pad = jnp.zeros((LANES - nbp, MOBA_BLOCK), f32)
    for p in range(N_HEADS // 2):
        sl = slice(p * LANES, (p + 1) * LANES)
        kaug_ref[0, p] = jnp.concatenate([kn[:, sl].astype(bf16), onehot], axis=-1)
        vt = v[:, sl].T
        qt = (qn[:, sl] * (HEAD_DIM ** -0.5 * LOG2E)).T
        for hh in range(2):
            vaugt_ref[0, 2 * p + hh] = jnp.concatenate(
                [vt[hh * HEAD_DIM:(hh + 1) * HEAD_DIM], ones_row], axis=0).astype(bf16)
            keep = (dim < HEAD_DIM) if hh == 0 else (dim >= HEAD_DIM)
            q2t = jnp.where(keep, qt, 0.0)
            gate_t = _mm_hi(kmean[:, sl], q2t)
            qaugt_ref[0, 2 * p + hh] = jnp.concatenate([q2t, _top3_bias(gate_t, i), sel_pad], axis=0).astype(bf16)
    kmean_scr[pl.ds(i, 1), :] = jnp.mean(kn, axis=0, keepdims=True)


def _moba_prep(proj, bsz, seq, qw, kw, bd_mean):
    nb = seq // MOBA_BLOCK
    nbp = -(-nb // 8) * 8
    assert nbp <= LANES
    return pl.pallas_call(
        _moba_prep_kernel, name="moba_prep",
        grid=(bsz, nb),
        in_specs=[pl.BlockSpec((MOBA_BLOCK, 3 * MIX_W), lambda b, i: (b * nb + i, OFF_A // (3 * MIX_W))),
                  pl.BlockSpec((1, MIX_W), lambda b, i: (0, 0)),
                  pl.BlockSpec((1, MIX_W), lambda b, i: (0, 0)),
                  pl.BlockSpec((MIX_W, MIX_W), lambda b, i: (0, 0))],
        out_specs=[pl.BlockSpec((1, N_HEADS, 2 * LANES, MOBA_BLOCK), lambda b, i: (b, 0, 0, i)),
                   pl.BlockSpec((1, N_HEADS // 2, MOBA_BLOCK, 2 * LANES), lambda b, i: (b, 0, i, 0)),
                   pl.BlockSpec((1, N_HEADS, VT_ROWS, MOBA_BLOCK), lambda b, i: (b, 0, 0, i))],
        out_shape=[jax.ShapeDtypeStruct((bsz, N_HEADS, 2 * LANES, seq), bf16),
                   jax.ShapeDtypeStruct((bsz, N_HEADS // 2, seq, 2 * LANES), bf16),
                   jax.ShapeDtypeStruct((bsz, N_HEADS, VT_ROWS, seq), bf16)],
        scratch_shapes=[pltpu.VMEM((nbp, MIX_W), f32)],
        compiler_params=_cparams(("parallel", "arbitrary"), 32),
    )(proj, qw, kw, bd_mean)


def _moba_attn_kernel(qaugt_ref, kaug_ref, vaugt_ref, tabt_ref, o_ref, sa_scr, sb_scr, acc_scr):
    i = pl.program_id(2)
    blk = MOBA_BLOCK
    pairs = range(MOBA_PAIRS_PER_STEP)
    hs = range(2 * MOBA_PAIRS_PER_STEP)
    key, qry = _iota2((blk, blk))
    mm = lambda a, b: jnp.dot(a, b, preferred_element_type=f32)
    cmax = lambda s: jnp.max(s, axis=0, keepdims=True)
    pv = lambda v, pe: jnp.dot(v, pe.astype(bf16), preferred_element_type=f32)
    kblk = lambda pp, j: kaug_ref[0, pp, pl.ds(pl.multiple_of(j * blk, blk), blk), :]
    vblk = lambda hh, j: vaugt_ref[0, hh, :, pl.ds(pl.multiple_of(j * blk, blk), blk)]

    n_far = jnp.maximum(i - 1, 0)
    nk = MOBA_KV_PER_GROUP
    n_groups = (n_far + nk - 1) // nk
    blocks_of = lambda g: [jnp.where(nk * g + a < n_far, nk * g + a, i) for a in range(nk)]

    def scores(g, buf):
        qt = [qaugt_ref[0, hh] for hh in hs]
        for a, j in enumerate(blocks_of(g)):
            ks = [kblk(pp, j) for pp in pairs]
            for hh in hs:
                buf[hh, a] = mm(ks[hh // 2], qt[hh])

    scores(0, sa_scr)

    k_own = [kblk(pp, i) for pp in pairs]
    k_adj = [kblk(pp, n_far) for pp in pairs]
    qt = [qaugt_ref[0, hh] for hh in hs]
    s_own = [jnp.where(qry >= key, mm(k_own[hh // 2][:, :LANES], qt[hh][:LANES]) + tabt_ref[hh // 2, hh % 2, 1], NEG)
             for hh in hs]
    s_adj = [mm(k_adj[hh // 2], qt[hh]) + tabt_ref[hh // 2, hh % 2, 0] for hh in hs]
    m = [jnp.maximum(cmax(s_own[hh]), cmax(s_adj[hh])) for hh in hs]
    for hh in hs:
        acc_scr[hh] = (pv(vblk(hh, i), jnp.exp2(s_own[hh] - m[hh]))
                       + pv(vblk(hh, n_far), jnp.exp2(s_adj[hh] - m[hh])))

    def consume(g, buf, m):
        js = blocks_of(g)
        m_new = []
        for hh in hs:
            s = [buf[hh, a] for a in range(nk)]
            mh = functools.reduce(jnp.maximum, [cmax(x) for x in s], m[hh])
            ah = jnp.exp2(m[hh] - mh) * acc_scr[hh]
            for a in range(nk):
                ah = ah + pv(vblk(hh, js[a]), jnp.exp2(s[a] - mh))
            acc_scr[hh] = ah
            m_new.append(mh)
        return tuple(m_new)

    def body(u, m):
        scores(2 * u + 1, sb_scr)
        m = consume(2 * u, sa_scr, m)
        scores(2 * u + 2, sa_scr)
        return consume(2 * u + 1, sb_scr, m)

    m = lax.fori_loop(0, n_groups // 2, body, tuple(m))
    lax.cond(n_groups % 2 == 1, lambda mm_: consume(n_groups - 1, sa_scr, mm_), lambda mm_: mm_, m)
    o_ref[0] = jnp.concatenate(
        [(acc_scr[hh][:HEAD_DIM] / acc_scr[hh][HEAD_DIM:HEAD_DIM + 1]).T for hh in hs], axis=-1)


def _moba_attn(qaugt, kaug, vaugt, tabt):
    bsz, _, _, seq = qaugt.shape
    nb = seq // MOBA_BLOCK
    pp = MOBA_PAIRS_PER_STEP
    return pl.pallas_call(
        _moba_attn_kernel, name="moba_attn",
        grid=(bsz, N_HEADS // (2 * pp), nb),
        in_specs=[pl.BlockSpec((1, 2 * pp, 2 * LANES, MOBA_BLOCK), lambda b, p, i: (b, p, 0, i)),
                  pl.BlockSpec((1, pp, seq, 2 * LANES), lambda b, p, i: (b, p, 0, 0), pipeline_mode=pl.Buffered(1)),
                  pl.BlockSpec((1, 2 * pp, VT_ROWS, seq), lambda b, p, i: (b, p, 0, 0), pipeline_mode=pl.Buffered(1)),
                  pl.BlockSpec((pp, 2, 2, MOBA_BLOCK, MOBA_BLOCK), lambda b, p, i: (p, 0, 0, 0, 0),
                               pipeline_mode=pl.Buffered(1))],
        out_specs=pl.BlockSpec((1, MOBA_BLOCK, pp * LANES), lambda b, p, i: (b, i, p)),
        out_shape=jax.ShapeDtypeStruct((bsz, seq, MIX_W), f32),
        scratch_shapes=[pltpu.VMEM((2 * pp, MOBA_KV_PER_GROUP, MOBA_BLOCK, MOBA_BLOCK), f32)] * 2
        + [pltpu.VMEM((2 * pp, VT_ROWS, MOBA_BLOCK), f32)],
        compiler_params=_cparams(("parallel", "parallel", "arbitrary"), 56),
    )(qaugt, kaug, vaugt, tabt)


def _t5_bucket(dist):
    n = jnp.maximum(dist, 0)
    max_exact = REL_BUCKETS // 2
    nf = jnp.maximum(n, max_exact).astype(f32)
    large = max_exact + (jnp.log(nf / max_exact) / math.log(REL_MAX_DIST / max_exact)
                         * (REL_BUCKETS - max_exact)).astype(jnp.int32)
    large = jnp.minimum(large, REL_BUCKETS - 1)
    return jnp.where(n < max_exact, n, large)


def _moba_bias_kernel(vec_ref, o_ref):
    blk = MOBA_BLOCK
    t = pltpu.roll(jnp.broadcast_to(vec_ref[0] * LOG2E, (blk, 2 * blk)), 0, 1, stride=1, stride_axis=0)
    o_ref[0, 0] = t[:, blk:]
    o_ref[0, 1] = t[:, :blk]


def _moba_bias_tables(rel_bias):
    assert MOBA_BLOCK >= REL_MAX_DIST
    by_dist = rel_bias.astype(f32)[_t5_bucket(jnp.arange(2 * MOBA_BLOCK))]
    far = rel_bias.astype(f32)[_t5_bucket(jnp.array(2 * MOBA_BLOCK))]
    vec = (by_dist - far).T.reshape(N_HEADS, 1, 2 * MOBA_BLOCK)
    tab = pl.pallas_call(
        _moba_bias_kernel, name="moba_bias",
        grid=(N_HEADS,),
        in_specs=[pl.BlockSpec((1, 1, 2 * MOBA_BLOCK), lambda h: (h, 0, 0))],
        out_specs=pl.BlockSpec((1, 2, MOBA_BLOCK, MOBA_BLOCK), lambda h: (h, 0, 0, 0)),
        out_shape=jax.ShapeDtypeStruct((N_HEADS, 2, MOBA_BLOCK, MOBA_BLOCK), f32),
        compiler_params=_cparams(("parallel",), 16),
    )(vec)
    return tab.reshape(N_HEADS // 2, 2, 2, MOBA_BLOCK, MOBA_BLOCK)


def _causal_conv(x, halo, w_ref):
    ts, nh = x.shape[0], halo.shape[0]
    xe = jnp.concatenate([halo, x], axis=0)
    acc = x * w_ref[CONV_K - 1:CONV_K, :]
    for d in range(1, CONV_K):
        acc = acc + xe[nh - d:nh - d + ts] * w_ref[CONV_K - 1 - d:CONV_K - d, :]
    return acc


def _tile_and_halo(x_ref, halo_ref):
    halo = jnp.where(pl.program_id(1) == 0, 0.0, halo_ref[...].astype(f32))
    return x_ref[...].astype(f32), halo


def _tok_spec(rows, nsteps):
    return pl.BlockSpec((rows, MIX_W), lambda b, i: (b * nsteps + i, 0))


def _pair_masks(rows):
    lane = lax.broadcasted_iota(jnp.int32, (rows, LANES), 1)
    return lane < HEAD_DIM, lane >= HEAD_DIM


def _chunk_sum_matrix(chunk):
    r = jnp.arange(CUMSUM_TILE)[:, None]
    c = jnp.arange(CUMSUM_TILE)[None, :]
    same = (r // chunk) == (c // chunk)
    return jnp.concatenate([same & (r >= c), same], axis=0).astype(bf16)


def _chunk_sums(cm_ref, x):
    parts = [_sel_left(cm_ref[...], x[r:r + CUMSUM_TILE]) for r in range(0, x.shape[0], CUMSUM_TILE)]
    return (jnp.concatenate([p[:CUMSUM_TILE] for p in parts], axis=0),
            jnp.concatenate([p[CUMSUM_TILE:] for p in parts], axis=0))


def _row_spec(width, off, nt):
    return pl.BlockSpec((ROW_TILE, width), lambda b, i: (b * nt + i, off // width))


def _halo_spec(width, off, nt):
    per = ROW_TILE // HALO_ROWS
    return pl.BlockSpec((HALO_ROWS, width), lambda b, i: (jnp.maximum((b * nt + i) * per - 1, 0), off // width))


def _small_spec(nt):
    return pl.BlockSpec((ROW_TILE, LANES), lambda b, i: (b * nt + i, 0))


def _const_spec(shape):
    return pl.BlockSpec(shape, lambda b, i: (0,) * len(shape))


def _gdn_prep_kernel(x_ref, halo_ref, sm_ref, cw_ref, alog_ref, dtb_ref, bd_ref, eb_ref, ea_ref, cm_ref,
                     q_ref, k_ref, kb_ref, vb_ref, qd_ref, kbe_ref, kd_ref, gc_ref):
    qkv = _silu(_causal_conv(*_tile_and_halo(x_ref, halo_ref), cw_ref))
    q, k, v = qkv[:, :MIX_W], qkv[:, MIX_W:2 * MIX_W], qkv[:, 2 * MIX_W:]
    bd = bd_ref[...]
    q = q * lax.rsqrt(_sel_right1(q * q, bd) + L2_EPS) * HEAD_DIM ** -0.5
    k = k * lax.rsqrt(_sel_right1(k * k, bd) + L2_EPS)
    sm = sm_ref[...]
    beta = _sel_right1(_sigmoid(sm), eb_ref[...])
    g = -jnp.exp(alog_ref[...]) * _softplus(sm + dtb_ref[...])
    gc, g_end = _chunk_sums(cm_ref, g)
    eg = jnp.exp(_sel_right(gc, ea_ref[...]))
    e_rest = jnp.exp(_sel_right(g_end - gc, ea_ref[...]))
    kb = k * beta
    for ref, val in ((q_ref, q), (k_ref, k), (kb_ref, kb), (vb_ref, v * beta),
                     (qd_ref, q * eg), (kbe_ref, kb * eg), (kd_ref, k * e_rest)):
        ref[...] = val.astype(bf16)
    gc_ref[...] = gc[:, SM_BA:SM_BA + N_HEADS]


def _gdn_prep(proj, small, bsz, seq, conv_w, alog128, dtb128, bd_ones, e_beta, e_ba, cm):
    nt = seq // ROW_TILE
    w3 = 3 * MIX_W
    return pl.pallas_call(
        _gdn_prep_kernel, name="gdn_prep",
        grid=(bsz, nt),
        in_specs=[_row_spec(w3, OFF_BQKV, nt), _halo_spec(w3, OFF_BQKV, nt), _small_spec(nt),
                  _const_spec((CONV_K, w3)), _const_spec((1, LANES)), _const_spec((1, LANES)),
                  _const_spec((MIX_W, MIX_W)), _const_spec((LANES, MIX_W)), _const_spec((LANES, MIX_W)),
                  _const_spec((2 * CUMSUM_TILE, CUMSUM_TILE))],
        out_specs=[_tok_spec(ROW_TILE, nt)] * 7 + [pl.BlockSpec((ROW_TILE, N_HEADS), lambda b, i: (b * nt + i, 0))],
        out_shape=[jax.ShapeDtypeStruct((bsz * seq, MIX_W), bf16)] * 7 + [jax.ShapeDtypeStruct((bsz * seq, N_HEADS), f32)],
        compiler_params=_cparams(("parallel", "parallel"), 40),
    )(proj, proj, small, conv_w, alog128, dtb128, bd_ones, e_beta, e_ba, cm)


def _gdn_chunk_kernel(q_ref, k_ref, kb_ref, vb_ref, qd_ref, kbe_ref, kd_ref, gc_ref, gct_ref, nw_ref, bdm_ref,
                      o_ref, st_scr):
    @pl.when(pl.program_id(1) == 0)
    def _():
        st_scr[...] = jnp.zeros_like(st_scr)

    n = GDN_CHUNK
    cis, prs = range(CHUNKS_PER_STEP), range(N_HEADS // 2)
    tiles = [(ci, p) for ci in cis for p in prs]
    ch = [(ci, p, hh) for ci, p in tiles for hh in (0, 1)]
    blk = lambda ref, ci, p: ref[ci * n:(ci + 1) * n, p * LANES:(p + 1) * LANES]
    r, c = _iota2((n, n))
    row2, col2 = _iota2((2 * n, 2 * n))
    same_head = (row2 < n) == (col2 < n)
    hm2 = _pair_masks(2 * n)
    keep0 = _pair_masks(n)[0]
    pick = lambda x0, x1: jnp.where(keep0, x0, x1)
    zero = jnp.zeros((), bf16)
    gc_all = [gc_ref[0, ci] for ci in cis]
    gct_all = [gct_ref[0, ci] for ci in cis]
    gcol = lambda ci, p, hh: gc_all[ci][:, 2 * p + hh:2 * p + hh + 1]
    decay = {(ci, p, hh): jnp.exp(jnp.where(r >= c, gcol(ci, p, hh) - gct_all[ci][2 * p + hh:2 * p + hh + 1, :], NEG))
             for ci, p, hh in ch}
    lhs = {x: jnp.concatenate([blk(kb_ref, *x), blk(q_ref, *x)], axis=0) for x in tiles}
    gram = {(ci, p, hh): _mm_nt(jnp.where(hm2[hh], lhs[ci, p], zero), blk(k_ref, ci, p)) for ci, p, hh in ch}
    t = dict(zip(ch, _unit_lower_inverses([jnp.where(r > c, gram[x][:n] * decay[x], 0.0) for x in ch])))
    u = {(ci, p): pick(_mm(t[ci, p, 0], blk(vb_ref, ci, p)), _mm(t[ci, p, 1], blk(vb_ref, ci, p))) for ci, p in tiles}
    w = {(ci, p): pick(_mm(t[ci, p, 0], blk(kbe_ref, ci, p)), _mm(t[ci, p, 1], blk(kbe_ref, ci, p)))
         for ci, p in tiles}
    a_in = {x: gram[x][n:] * decay[x] for x in ch}
    st = [st_scr[p] for p in prs]
    for ci in cis:
        ws = [_mm(jnp.concatenate([w[ci, p].astype(bf16), blk(qd_ref, ci, p)], axis=0), st[p]) for p in prs]
        v_new = [u[ci, p] - ws[p][:n] for p in prs]
        o = [ws[p][n:] + pick(_mm(a_in[ci, p, 0], v_new[p]), _mm(a_in[ci, p, 1], v_new[p])) for p in prs]
        upd = [jnp.where(same_head, _mm_tn(blk(kd_ref, ci, p), v_new[p]), 0.0) for p in prs]
        for p in prs:
            ms = _sel_right1(o[p] * o[p], bdm_ref[...])
            o_ref[ci * n:(ci + 1) * n, p * LANES:(p + 1) * LANES] = o[p] * lax.rsqrt(ms + RMS_EPS) * nw_ref[...]
        g_end = [jnp.exp(jnp.where(keep0[0:1], gcol(ci, p, 0)[n - 1:n], gcol(ci, p, 1)[n - 1:n])) for p in prs]
        st = [st[p] * g_end[p] + upd[p] for p in prs]
    for p in prs:
        st_scr[p] = st[p]


def _gdn_chunk(bsz, seq, q, k, kb, vb, qd, kbe, kd, gc, norm_w2, bd_mean2):
    n = GDN_CHUNK
    nc = seq // n
    per = CHUNKS_PER_STEP
    gc4 = gc.reshape(bsz, nc, n, N_HEADS)
    gct4 = jnp.swapaxes(gc4, 2, 3)
    spec = _tok_spec(per * n, nc // per)
    return pl.pallas_call(
        _gdn_chunk_kernel, name="gdn_chunk",
        grid=(bsz, nc // per),
        in_specs=[spec] * 7 + [pl.BlockSpec((1, per, n, N_HEADS), lambda b, i: (b, i, 0, 0)),
                               pl.BlockSpec((1, per, N_HEADS, n), lambda b, i: (b, i, 0, 0)),
                               _const_spec((1, LANES)), _const_spec((LANES, LANES))],
        out_specs=spec,
        out_shape=jax.ShapeDtypeStruct((bsz * seq, MIX_W), f32),
        scratch_shapes=[pltpu.VMEM((N_HEADS // 2, 2 * HEAD_DIM, 2 * HEAD_DIM), f32)],
        compiler_params=_cparams(("parallel", "arbitrary"), 32),
    )(q, k, kb, vb, qd, kbe, kd, gc4, gct4, norm_w2, bd_mean2)


def _rwkv_prep_kernel(has_vres, *refs):
    if has_vres:
        (c_ref, halo_ref, mu_ref, w0_ref, wup_ref, a0_ref, aup_ref, gup_ref, kk_ref, ka_ref, bd_ref,
         rk_ref, cm_ref, vf_ref, v0_ref, vdn_ref, vup_ref,
         rt_ref, at_ref, bt_ref, kt_ref, v_ref, pe_ref, gout_ref, bonus_ref) = refs
    else:
        (c_ref, halo_ref, mu_ref, w0_ref, wup_ref, a0_ref, aup_ref, gup_ref, kk_ref, ka_ref, bd_ref,
         rk_ref, cm_ref,
         rt_ref, at_ref, bt_ref, kt_ref, v_ref, pe_ref, gout_ref, bonus_ref, cv_ref) = refs
    c, halo = _tile_and_halo(c_ref, halo_ref)
    prev = jnp.concatenate([halo[HALO_ROWS - 1:], c[:-1]], axis=0)
    c = c + (prev - c) * mu_ref[...]
    c_r, c_k, c_v = c[:, :MIX_W], c[:, MIX_W:2 * MIX_W], c[:, 2 * MIX_W:3 * MIX_W]
    c_wd = c[:, 3 * MIX_W:3 * MIX_W + 64]
    c_ad = c[:, 3 * MIX_W + 64:3 * MIX_W + 128]
    c_gd = c[:, 3 * MIX_W + 128:]
    w_log = -_softplus(-(w0_ref[...] + _mm(jnp.tanh(c_wd), wup_ref[...]))) - 0.5
    a_in = _sigmoid(a0_ref[...] + _mm(c_ad, aup_ref[...]))
    gout_ref[...] = _mm(_sigmoid(c_gd), gup_ref[...])
    if has_vres:
        lam = _sigmoid(v0_ref[...] + _mm(_mm(c_v, vdn_ref[...]), vup_ref[...]))
        v_r = c_v + (vf_ref[...] - c_v) * lam
    else:
        v_r = c_v
        cv_ref[...] = c_v
    bd = bd_ref[...]
    kk = c_k * kk_ref[...]
    kk = kk * lax.rsqrt(_sel_right1(kk * kk, bd) + L2_EPS)
    k_r = c_k * (1.0 + (a_in - 1.0) * ka_ref[...])
    b = kk * a_in
    bonus_ref[...] = _sel_right1(c_r * k_r * rk_ref[...], bd) * v_r
    lc, lc_end = _chunk_sums(cm_ref, -jnp.exp(w_log))
    e_neg = jnp.exp(-lc)
    rt_ref[...] = (c_r * jnp.exp(lc)).astype(bf16)
    at_ref[...] = (-kk * jnp.exp(lc + jnp.exp(w_log))).astype(bf16)
    bt_ref[...] = (b * e_neg).astype(bf16)
    kt_ref[...] = (k_r * e_neg).astype(bf16)
    v_ref[...] = v_r.astype(bf16)
    pe_ref[...] = jnp.exp(lc_end)


def _rwkv_prep(proj, bsz, seq, mu, w0, w_up, a0, a_up, g_up, k_k, k_a, r_k, bd_ones, cm, vres):
    nt = seq // ROW_TILE
    wc = 3 * MIX_W + 256
    std = pl.BlockSpec((ROW_TILE, MIX_W), lambda b, i: (b * nt + i, 0))
    std_shape = jax.ShapeDtypeStruct((bsz * seq, MIX_W), f32)
    in_specs = [_row_spec(wc, OFF_C, nt), _halo_spec(wc, OFF_C, nt), _const_spec((1, wc)),
                _const_spec((1, MIX_W)), _const_spec((64, MIX_W)), _const_spec((1, MIX_W)), _const_spec((64, MIX_W)),
                _const_spec((128, MIX_W)), _const_spec((1, MIX_W)), _const_spec((1, MIX_W)), _const_spec((MIX_W, MIX_W)),
                _const_spec((1, MIX_W)), _const_spec((2 * CUMSUM_TILE, CUMSUM_TILE))]
    args = [proj, proj, mu, w0, w_up, a0, a_up, g_up, k_k, k_a, bd_ones, r_k, cm]
    out_specs = [std] * 8
    out_shape = [jax.ShapeDtypeStruct((bsz * seq, MIX_W), bf16)] * 5 + [std_shape] * 3
    if vres is not None:
        v_first, v0, v_down, v_up = vres
        in_specs += [std, _const_spec((1, MIX_W)), _const_spec(v_down.shape), _const_spec(v_up.shape)]
        args += [v_first, v0, v_down, v_up]
    else:
        out_specs.append(std)
        out_shape.append(std_shape)
    return pl.pallas_call(
        functools.partial(_rwkv_prep_kernel, vres is not None), name="rwkv_prep",
        grid=(bsz, nt), in_specs=in_specs, out_specs=out_specs, out_shape=out_shape,
        compiler_params=_cparams(("parallel", "parallel"), 40),
    )(*args)


def _rwkv_chunk_kernel(rt_ref, at_ref, bt_ref, kt_ref, v_ref, pe_ref, o_ref, st_scr):
    @pl.when(pl.program_id(1) == 0)
    def _():
        st_scr[...] = jnp.zeros_like(st_scr)

    n = RWKV_CHUNK
    cis, prs = range(CHUNKS_PER_STEP), range(N_HEADS // 2)
    tiles = [(ci, p) for ci in cis for p in prs]
    ch = [(ci, p, hh) for ci, p in tiles for hh in (0, 1)]
    blk = lambda ref, ci, p: ref[ci * n:(ci + 1) * n, p * LANES:(p + 1) * LANES]
    row, col = _iota2((2 * n, 2 * n))
    rr, cc = row & (n - 1), col & (n - 1)
    mask = rr + jnp.where(row < n, 0, 1) > cc
    same_head = (row < n) == (col < n)
    hm2 = _pair_masks(2 * n)
    keep0 = _pair_masks(n)[0]
    pick = lambda x0, x1: jnp.where(keep0, x0, x1)
    zero = jnp.zeros((), bf16)
    lhs = {x: jnp.concatenate([blk(at_ref, *x), blk(rt_ref, *x)], axis=0) for x in tiles}
    rhs = {x: jnp.concatenate([blk(bt_ref, *x), blk(kt_ref, *x)], axis=0) for x in tiles}
    gm = {(ci, p, hh): jnp.where(mask, _mm_nt(jnp.where(hm2[hh], lhs[ci, p], zero), rhs[ci, p]), 0.0)
          for ci, p, hh in ch}
    t = dict(zip(ch, _unit_lower_inverses([-gm[x][:n, :n] for x in ch])))
    v = {x: blk(v_ref, *x) for x in tiles}
    zv = {x: jnp.concatenate([jnp.zeros_like(v[x]), v[x]], axis=0) for x in tiles}
    makv = {(ci, p): pick(_mm(gm[ci, p, 0][:n], zv[ci, p]), _mm(gm[ci, p, 1][:n], zv[ci, p])) for ci, p in tiles}
    st = [st_scr[p] for p in prs]
    for ci in cis:
        ah = [_mm_nt(lhs[ci, p], st[p]) for p in prs]
        rhs_u = [ah[p][:n] + makv[ci, p] for p in prs]
        u = [pick(_mm(t[ci, p, 0], rhs_u[p]), _mm(t[ci, p, 1], rhs_u[p])) for p in prs]
        uv = [jnp.concatenate([u[p].astype(bf16), v[ci, p]], axis=0) for p in prs]
        o = [ah[p][n:] + pick(_mm(gm[ci, p, 0][n:], uv[p]), _mm(gm[ci, p, 1][n:], uv[p])) for p in prs]
        pe = [pe_ref[ci * n:ci * n + 1, p * LANES:(p + 1) * LANES] for p in prs]
        upd = [_mm_tn(uv[p], rhs[ci, p].astype(f32) * pe[p]) for p in prs]
        for p in prs:
            o_ref[ci * n:(ci + 1) * n, p * LANES:(p + 1) * LANES] = o[p]
        st = [st[p] * pe[p] + jnp.where(same_head, upd[p], 0.0) for p in prs]
    for p in prs:
        st_scr[p] = st[p]


def _rwkv_chunk(bsz, seq, rt, at, bt, kt, v, pe):
    n = RWKV_CHUNK * CHUNKS_PER_STEP
    spec = _tok_spec(n, seq // n)
    return pl.pallas_call(
        _rwkv_chunk_kernel, name="rwkv_chunk",
        grid=(bsz, seq // n),
        in_specs=[spec] * 6,
        out_specs=spec,
        out_shape=jax.ShapeDtypeStruct((bsz * seq, MIX_W), f32),
        scratch_shapes=[pltpu.VMEM((N_HEADS // 2, 2 * HEAD_DIM, 2 * HEAD_DIM), f32)],
        compiler_params=_cparams(("parallel", "arbitrary"), 32),
    )(rt, at, bt, kt, v, pe)


def _ssd_prep_kernel(x_ref, halo_ref, sm_ref, cw_ref, cb_ref, alog_ref, dtb_ref, edt_ref, cm_ref,
                     xdt_ref, x_out_ref, bc_ref, acs_ref):
    xbc = _silu(_causal_conv(*_tile_and_halo(x_ref, halo_ref), cw_ref) + cb_ref[...])
    m_x = xbc[:, :MIX_W]
    dt = _softplus(sm_ref[...] + dtb_ref[...])
    xdt_ref[...] = (m_x * _sel_right1(dt, edt_ref[...])).astype(xdt_ref.dtype)
    x_out_ref[...] = m_x
    bc_ref[...] = xbc[:, MIX_W:].astype(bc_ref.dtype)
    acs, _ = _chunk_sums(cm_ref, dt * -jnp.exp(alog_ref[...]))
    acs_ref[...] = acs[:, SM_DT:SM_DT + N_HEADS]


def _ssd_prep(proj, small, bsz, seq, conv_w, conv_b, alog128, dtb128, e_dt, cm):
    nt = seq // ROW_TILE
    wx = MIX_W + 4 * SSM_STATE
    return pl.pallas_call(
        _ssd_prep_kernel, name="ssd_prep",
        grid=(bsz, nt),
        in_specs=[_row_spec(wx, OFF_DXBC, nt), _halo_spec(wx, OFF_DXBC, nt), _small_spec(nt),
                  _const_spec((CONV_K, wx)), _const_spec((1, wx)), _const_spec((1, LANES)), _const_spec((1, LANES)),
                  _const_spec((LANES, MIX_W)), _const_spec((2 * CUMSUM_TILE, CUMSUM_TILE))],
        out_specs=[_tok_spec(ROW_TILE, nt)] * 2 + [pl.BlockSpec((ROW_TILE, 4 * SSM_STATE), lambda b, i: (b * nt + i, 0)),
                                                   pl.BlockSpec((ROW_TILE, N_HEADS), lambda b, i: (b * nt + i, 0))],
        out_shape=[jax.ShapeDtypeStruct((bsz * seq, MIX_W), bf16), jax.ShapeDtypeStruct((bsz * seq, MIX_W), f32),
                   jax.ShapeDtypeStruct((bsz * seq, 4 * SSM_STATE), bf16),
                   jax.ShapeDtypeStruct((bsz * seq, N_HEADS), f32)],
        compiler_params=_cparams(("parallel", "parallel"), 40),
    )(proj, proj, small, conv_w, conv_b, alog128, dtb128, e_dt, cm)


def _ssd_chunk_kernel(xdt_ref, x_ref, bc_ref, a_ref, at_ref, dvec_ref, o_ref, st_scr):
    @pl.when(pl.program_id(1) == 0)
    def _():
        st_scr[...] = jnp.zeros_like(st_scr)

    n = SSD_CHUNK
    cis, prs = range(SSD_CHUNKS_PER_STEP), range(N_HEADS // 2)
    grp = lambda p: (2 * p) // (N_HEADS // 2)
    tiles = [(ci, p) for ci in cis for p in prs]
    ch = [(ci, p, hh) for ci, p in tiles for hh in (0, 1)]
    rows = lambda ci: slice(ci * n, (ci + 1) * n)
    lanes = lambda p: slice(p * LANES, (p + 1) * LANES)
    r, c = _iota2((n, n))
    keep0 = _pair_masks(n)[0]
    pick = lambda x0, x1: jnp.where(keep0, x0, x1)
    b_g = {(ci, g): bc_ref[rows(ci), g * SSM_STATE:(g + 1) * SSM_STATE] for ci in cis for g in range(2)}
    c_g = {(ci, g): bc_ref[rows(ci), (2 + g) * SSM_STATE:(3 + g) * SSM_STATE] for ci in cis for g in range(2)}
    cb = {x: _mm_nt(c_g[x], b_g[x]) for x in b_g}
    col = lambda ci, p, hh: a_ref[0, ci][:, 2 * p + hh:2 * p + hh + 1]
    ac = {x: col(*x) for x in ch}
    a_last = {x: col(*x)[n - 1:n] for x in ch}
    lmat = {(ci, p, hh): jnp.exp(jnp.where(r >= c, ac[ci, p, hh] - at_ref[0, ci][2 * p + hh:2 * p + hh + 1, :], NEG))
            for ci, p, hh in ch}
    xg = {(ci, p): xdt_ref[rows(ci), lanes(p)] for ci, p in tiles}
    y_diag = {(ci, p): pick(*[_mm(cb[ci, grp(p)] * lmat[ci, p, hh], xg[ci, p]) for hh in (0, 1)]) for ci, p in tiles}
    upd = {(ci, p): pick(*[_mm_tn(b_g[ci, grp(p)].astype(f32) * jnp.exp(a_last[ci, p, hh] - ac[ci, p, hh]), xg[ci, p])
                           for hh in (0, 1)]) for ci, p in tiles}
    c_in = {(ci, p, hh): c_g[ci, grp(p)].astype(f32) * jnp.exp(ac[ci, p, hh]) for ci, p, hh in ch}
    st = [st_scr[p] for p in prs]
    for ci in cis:
        y_off = [pick(_mm(c_in[ci, p, 0], st[p]), _mm(c_in[ci, p, 1], st[p])) for p in prs]
        for p in prs:
            o_ref[rows(ci), lanes(p)] = y_diag[ci, p] + y_off[p] + x_ref[rows(ci), lanes(p)] * dvec_ref[:, lanes(p)]
        st = [st[p] * jnp.exp(jnp.where(keep0[0:1], a_last[ci, p, 0], a_last[ci, p, 1])) + upd[ci, p] for p in prs]
    for p in prs:
        st_scr[p] = st[p]


def _ssd_chunk(bsz, seq, xdt, x, bc, acs, dvec):
    n = SSD_CHUNK
    nc = seq // n
    per = SSD_CHUNKS_PER_STEP
    a4 = acs.reshape(bsz, nc, n, N_HEADS)
    at4 = jnp.swapaxes(a4, 2, 3)
    return pl.pallas_call(
        _ssd_chunk_kernel, name="ssd_chunk",
        grid=(bsz, nc // per),
        in_specs=[_tok_spec(per * n, nc // per)] * 2
        + [pl.BlockSpec((per * n, 4 * SSM_STATE), lambda b, i: (b * (nc // per) + i, 0)),
           pl.BlockSpec((1, per, n, N_HEADS), lambda b, i: (b, i, 0, 0)),
           pl.BlockSpec((1, per, N_HEADS, n), lambda b, i: (b, i, 0, 0)),
           _const_spec((1, MIX_W))],
        out_specs=_tok_spec(per * n, nc // per),
        out_shape=jax.ShapeDtypeStruct((bsz * seq, MIX_W), f32),
        scratch_shapes=[pltpu.VMEM((N_HEADS // 2, SSM_STATE, 2 * HEAD_DIM), f32)],
        compiler_params=_cparams(("parallel", "arbitrary"), 32),
    )(xdt, x, bc, a4, at4, dvec)


def _merge_kernel(x_ref, ya_ref, ob_ref, bz_ref, wkv_ref, bonus_ref, gout_ref, lnw_ref, lnb_ref, bdm_ref,
                  yd_ref, dz_ref, mnw_ref, g0_ref, g1_ref, g2_ref, g3_ref, wb_ref, wo_ref, o_ref):
    def gated(n, y, g_ref):
        return _sigmoid(g_ref[...].astype(f32)) * _mm(y, wb_ref[n])

    acc = gated(0, ya_ref[0], g0_ref)
    acc = acc + gated(1, ob_ref[...] * _silu(bz_ref[...].astype(f32)), g1_ref)

    w = wkv_ref[...]
    d = w - _sel_right1(w, bdm_ref[...])
    wkv_ln = d * lax.rsqrt(_sel_right1(d * d, bdm_ref[...]) + RWKV_LN_EPS)
    y_c = (wkv_ln * lnw_ref[...] + lnb_ref[...] + bonus_ref[...]) * gout_ref[...]
    acc = acc + gated(2, y_c, g2_ref)

    yz = yd_ref[...] * _silu(dz_ref[...].astype(f32))
    half = MIX_W // 2
    y_d = jnp.concatenate(
        [yz[:, s:s + half] * lax.rsqrt(jnp.mean(jnp.square(yz[:, s:s + half]), axis=-1, keepdims=True) + RMS_EPS)
         for s in (0, half)], axis=-1) * mnw_ref[...]
    acc = acc + gated(3, y_d, g3_ref)
    o_ref[...] = x_ref[...] + _mm(acc, wo_ref[...])


def _merge(x2, proj, bsz, seq, ya, ob, wkv, bonus, gout, lnw, lnb, bd_mean, yd, mnw, wb, wo):
    nt = seq // ROW_TILE
    std = lambda w: pl.BlockSpec((ROW_TILE, w), lambda b, i: (b * nt + i, 0))
    gate = lambda n: _row_spec(D_MODEL, OFF_GATES + n * D_MODEL, nt)
    return pl.pallas_call(
        _merge_kernel, name="merge",
        grid=(bsz, nt),
        in_specs=[std(D_MODEL), pl.BlockSpec((1, ROW_TILE, MIX_W), lambda b, i: (b, i, 0)),
                  std(MIX_W), _row_spec(MIX_W, OFF_BZ, nt),
                  std(MIX_W), std(MIX_W), std(MIX_W), _const_spec((1, MIX_W)), _const_spec((1, MIX_W)),
                  _const_spec((MIX_W, MIX_W)),
                  std(MIX_W), _row_spec(MIX_W, OFF_DZ, nt), _const_spec((1, MIX_W)),
                  gate(0), gate(1), gate(2), gate(3),
                  _const_spec((4, MIX_W, D_MODEL)), _const_spec((D_MODEL, D_MODEL))],
        out_specs=std(D_MODEL),
        out_shape=jax.ShapeDtypeStruct((bsz * seq, D_MODEL), f32),
        compiler_params=_cparams(("parallel", "parallel"), 48),
    )(x2, ya, ob, proj, wkv, bonus, gout, lnw, lnb, bd_mean, yd, proj, mnw, proj, proj, proj, proj, wb, wo)


def _lane_vec(vals, off):
    return jnp.zeros((1, LANES), f32).at[0, off:off + vals.shape[0]].set(vals)


def _head_expand(off):
    n = jnp.arange(LANES)[:, None]
    c = jnp.arange(MIX_W)[None, :]
    return (n - off == c // HEAD_DIM).astype(bf16)


def _pack_w_in(w):
    pad = lambda n: jnp.zeros((w.shape[0], n), w.dtype)
    cols = [w[:, W_A:W_BZ],
            w[:, W_BZ:W_BBETA],
            w[:, W_C:W_DZ],
            w[:, W_BBETA:W_C], w[:, W_DDT:W_GATES], pad(2 * LANES - 3 * N_HEADS),
            w[:, W_DZ:W_DXBC], w[:, W_DXBC:W_DDT], w[:, W_GATES:]]
    out = jnp.concatenate(cols, axis=1).astype(bf16)
    assert out.shape[1] == N_PROJ
    return out


def kernel(x, rel_bias, norm1_w, w_in, moba_q_norm, moba_k_norm, gdn_conv_w, gdn_A_log, gdn_dt_bias, gdn_norm_w, rwkv_mu, rwkv_w0, rwkv_w_up, rwkv_a0, rwkv_a_up, rwkv_g_up, rwkv_k_k, rwkv_k_a, rwkv_r_k, rwkv_v0, rwkv_v_down, rwkv_v_up, rwkv_ln_w, rwkv_ln_b, mamba_conv_w, mamba_conv_b, mamba_dt_bias, mamba_A_log, mamba_D, mamba_norm_w, w_branch, w_out, norm2_w, ffn_w_in, ffn_w_down):
    bsz, seq, d = x.shape
    depth = w_in.shape[0]
    assert d == D_MODEL and (bsz * seq) % IN_PROJ_TILE_M == 0 and seq % MM_TILE_M == 0
    x2 = x.reshape(bsz * seq, d)
    row = lambda v: v.reshape(1, -1).astype(f32)

    hid = jnp.arange(MIX_W) // HEAD_DIM
    bd_ones = (hid[:, None] == hid[None, :]).astype(bf16)
    bd_mean = (bd_ones.astype(f32) / HEAD_DIM).astype(bf16)
    e_beta, e_ba, e_dt = _head_expand(SM_BETA), _head_expand(SM_BA), _head_expand(SM_DT)
    cm64, cm128 = _chunk_sum_matrix(GDN_CHUNK), _chunk_sum_matrix(SSD_CHUNK)
    assert GDN_CHUNK == RWKV_CHUNK
    tab = _moba_bias_tables(rel_bias)
    v_first = None
    for i in range(depth):
        proj, small = _in_proj(x2, row(norm1_w[i]), _pack_w_in(w_in[i]))

        qaug, kaug, v_a = _moba_prep(proj, bsz, seq, row(jnp.tile(moba_q_norm[i], N_HEADS)),
                                     row(jnp.tile(moba_k_norm[i], N_HEADS)), bd_mean)
        y_a = _moba_attn(qaug, kaug, v_a, tab)

        gdn_in = _gdn_prep(proj, small, bsz, seq, gdn_conv_w[i], _lane_vec(gdn_A_log[i], SM_BA),
                           _lane_vec(gdn_dt_bias[i], SM_BA), bd_ones, e_beta, e_ba, cm64)
        o_b = _gdn_chunk(bsz, seq, *gdn_in, row(jnp.tile(gdn_norm_w[i], 2)), bd_mean[:LANES, :LANES])

        vres = None if i == 0 else (v_first, row(rwkv_v0[i - 1]), rwkv_v_down[i - 1].astype(bf16),
                                    rwkv_v_up[i - 1].astype(bf16))
        outs = _rwkv_prep(proj, bsz, seq, row(rwkv_mu[i]), row(rwkv_w0[i]), rwkv_w_up[i].astype(bf16),
                          row(rwkv_a0[i]), rwkv_a_up[i].astype(bf16), rwkv_g_up[i].astype(bf16),
                          row(rwkv_k_k[i]), row(rwkv_k_a[i]), row(rwkv_r_k[i]), bd_ones, cm64, vres)
        g_out, bonus = outs[6], outs[7]
        if i == 0:
            v_first = outs[8]
        wkv = _rwkv_chunk(bsz, seq, *outs[:6])

        xdt, x_d, bc, acs = _ssd_prep(proj, small, bsz, seq, mamba_conv_w[i], row(mamba_conv_b[i]),
                                      _lane_vec(mamba_A_log[i], SM_DT), _lane_vec(mamba_dt_bias[i], SM_DT), e_dt, cm128)
        y_d = _ssd_chunk(bsz, seq, xdt, x_d, bc, acs, row(jnp.repeat(mamba_D[i], HEAD_DIM)))

        x2 = _merge(x2, proj, bsz, seq, y_a, o_b, wkv, bonus, g_out,
                    row(rwkv_ln_w[i]), row(rwkv_ln_b[i]), bd_mean, y_d, row(mamba_norm_w[i]),
                    w_branch[i].astype(bf16), w_out[i].astype(bf16))

        x2 = _ffn(x2, row(norm2_w[i]), ffn_w_in[i].astype(bf16), ffn_w_down[i].astype(bf16))
    return x2.reshape(bsz, seq, d)
```
